```python
import jax
import jax.numpy as jnp
from jax import lax
import numpy as np

D_MODEL = 2048
BATCH = 4
SEQ = 2048
DEPTH = 4
DEC_BATCH = 128
DEC_SEQ = 1
PAST_LEN = 16384
PAGE_SIZE = 128

N_MIXERS = 4
W_BRANCH = D_MODEL // N_MIXERS
D_MIX = N_MIXERS * W_BRANCH

RWKV_HEAD = 64
RWKV_H = W_BRANCH // RWKV_HEAD
RWKV_DECAY_RANK = 64
RWKV_A_RANK = 64
RWKV_PROJ = 3 * W_BRANCH + RWKV_DECAY_RANK + RWKV_A_RANK
RWKV_GN_EPS = 64e-5

GLA_H = 4
GLA_DK = W_BRANCH // (2 * GLA_H)
GLA_DV = W_BRANCH // GLA_H
GLA_KW = GLA_H * GLA_DK
GLA_GATE_RANK = 16
GLA_GATE_NORM = 16.0
GLA_PROJ = 2 * GLA_KW + W_BRANCH + GLA_GATE_RANK

HGRN_DK = 128
HGRN_H = W_BRANCH // HGRN_DK
HGRN_DV = W_BRANCH // HGRN_H
HGRN_PROJ = 3 * W_BRANCH
LB_FLOOR = 1e-30

LRU_BLOCKS = 8
LRU_BLOCK = W_BRANCH // LRU_BLOCKS
LRU_C = 8.0
CONV_W = 4

SPLIT_GLA = RWKV_PROJ
SPLIT_HGRN = SPLIT_GLA + GLA_PROJ
SPLIT_LRU = SPLIT_HGRN + HGRN_PROJ
SPLIT_GATE = SPLIT_LRU + W_BRANCH
PROJ_IN = SPLIT_GATE + D_MIX

CHUNK = 32
EPS = 1e-6

kernel_name = 'hybrid_rwkv7_gla_hgrn2_rglru_step'


def _rms(x, g):
    xf = x.astype(jnp.float32)
    return xf * lax.rsqrt(jnp.mean(xf * xf, axis=-1, keepdims=True) + EPS) * g.astype(jnp.float32)


def _head_rms(o, g):
    B, L, H, d = o.shape
    o = o * lax.rsqrt(jnp.mean(o * o, axis=-1, keepdims=True) + EPS)
    return o.reshape(B, L, H * d) * g.astype(jnp.float32)


def _chunked_gla(q, k, v, g, S0):
    B, L, H, Dk = q.shape
    Dv = v.shape[-1]
    C = min(CHUNK, L)
    n = -(-L // C)
    pad = n * C - L

    def blocks(t):
        t = jnp.pad(t, ((0, 0), (0, pad), (0, 0), (0, 0)))
        return jnp.moveaxis(t.reshape(B, n, C, H, t.shape[-1]), 1, 0)

    causal = jnp.tril(jnp.ones((C, C), dtype=bool))[None, :, :, None, None]

    def step(S, blk):
        qc, kc, vc, gc = blk
        b = jnp.cumsum(gc, axis=1)
        diff = b[:, :, None] - b[:, None, :]
        decay = jnp.where(causal, jnp.exp(jnp.where(causal, diff, 0.0)), 0.0)
        att = jnp.einsum('bihd,bjhd,bijhd->bhij', qc, kc, decay)
        o = jnp.einsum('bhij,bjhv->bihv', att, vc) + jnp.einsum('bihd,bhdv->bihv', qc * jnp.exp(b), S)
        b_last = b[:, -1]
        S = S * jnp.exp(b_last)[..., None] + jnp.einsum('bjhd,bjhv->bhdv', kc * jnp.exp(b_last[:, None] - b), vc)
        return S, o

    S, o = lax.scan(step, S0, (blocks(q), blocks(k), blocks(v), blocks(g)))
    o = jnp.moveaxis(o, 0, 1).reshape(B, n * C, H, Dv)[:, :L]
    return o, S


def _rwkv7(pr, prev, S0, l, p):
    B, L, _ = pr.shape
    pr = pr.astype(jnp.float32)
    prev_rows = jnp.concatenate([prev[:, None, :].astype(jnp.float32), pr[:, :-1]], axis=1)
    xm = pr + (prev_rows - pr) * p['rwkv_mu'][l]
    r, k, v, w_lo, a_lo = jnp.split(
        xm, [W_BRANCH, 2 * W_BRANCH, 3 * W_BRANCH, 3 * W_BRANCH + RWKV_DECAY_RANK], axis=-1)
    w_raw = -jax.nn.softplus(-(p['rwkv_w0'][l] + jnp.tanh(w_lo) @ p['rwkv_w_up'][l])) - 0.5
    logw = -jnp.exp(w_raw)
    a = jax.nn.sigmoid(p['rwkv_a0'][l] + a_lo @ p['rwkv_a_up'][l])

    def hd(t):
        return t.reshape(B, L, RWKV_H, RWKV_HEAD)

    r, k, v, logw, a = hd(r), hd(k), hd(v), hd(logw), hd(a)
    kk = k * p['rwkv_k_k'][l].reshape(RWKV_H, RWKV_HEAD)
    kk = kk / jnp.maximum(jnp.sqrt(jnp.sum(kk * kk, axis=-1, keepdims=True)), 1e-12)
    k = k * (1.0 + (a - 1.0) * p['rwkv_k_a'][l].reshape(RWKV_H, RWKV_HEAD))

    def step(S, inp):
        r_t, w_t, k_t, v_t, kk_t, a_t = inp
        sk = jnp.einsum('bhvk,bhk->bhv', S, kk_t)
        S = (S * jnp.exp(w_t)[:, :, None, :]
             - jnp.einsum('bhv,bhk->bhvk', sk, kk_t * a_t)
             + jnp.einsum('bhv,bhk->bhvk', v_t, k_t))
        return S, jnp.einsum('bhvk,bhk->bhv', S, r_t)

    seq = tuple(jnp.moveaxis(t, 1, 0) for t in (r, logw, k, v, kk, a))
    S, o = lax.scan(step, S0.astype(jnp.float32), seq)
    o = jnp.moveaxis(o, 0, 1)
    mu = jnp.mean(o, axis=-1, keepdims=True)
    var = jnp.mean(jnp.square(o - mu), axis=-1, keepdims=True)
    o = ((o - mu) * lax.rsqrt(var + RWKV_GN_EPS)).reshape(B, L, W_BRANCH)
    o = o * p['rwkv_gn_g'][l] + p['rwkv_gn_b'][l]
    bonus = jnp.sum(r * k * p['rwkv_r_k'][l], axis=-1, keepdims=True) * v
    out = o + bonus.reshape(B, L, W_BRANCH)
    return out, pr[:, -1].astype(prev.dtype), S.astype(S0.dtype)


def _gla(pg, S0, l, p):
    B, L, _ = pg.shape
    pg = pg.astype(jnp.float32)
    q, k, v, g_lo = jnp.split(pg, [GLA_KW, 2 * GLA_KW, 2 * GLA_KW + W_BRANCH], axis=-1)
    g = jax.nn.log_sigmoid(g_lo @ p['gla_gk_up'][l] + p['gla_gk_b'][l]) / GLA_GATE_NORM
    q = q * (GLA_DK ** -0.5)
    hk = (B, L, GLA_H, GLA_DK)
    o, S = _chunked_gla(q.reshape(hk), k.reshape(hk), v.reshape(B, L, GLA_H, GLA_DV),
                        g.reshape(hk), S0.astype(jnp.float32))
    return _head_rms(o, p['gla_gn_g'][l]), S.astype(S0.dtype)


def _hgrn2(ph, S0, l, p):
    B, L, _ = ph.shape
    ph = ph.astype(jnp.float32)
    q, f_lo, i = jnp.split(ph, [W_BRANCH, 2 * W_BRANCH], axis=-1)
    lb = p['hgrn_lb'][l]
    logf = jnp.logaddexp(jnp.log(jnp.maximum(lb, LB_FLOOR)), jnp.log1p(-lb) + jax.nn.log_sigmoid(f_lo))
    k = (1.0 - lb) * jax.nn.sigmoid(-f_lo)
    hk = (B, L, HGRN_H, HGRN_DK)
    o, S = _chunked_gla(q.reshape(hk), k.reshape(hk), i.reshape(B, L, HGRN_H, HGRN_DV),
                        logf.reshape(hk), S0.astype(jnp.float32))
    return _head_rms(o, p['hgrn_gn_g'][l]), S.astype(S0.dtype)


def _rglru(pl, buf, h0, l, p):
    B, L, _ = pl.shape
    xp = jnp.concatenate([buf.astype(jnp.float32), pl.astype(jnp.float32)], axis=1)
    w = p['lru_conv_w'][l]
    y = p['lru_conv_b'][l] + xp[:, 0:L] * w[0]
    for j in range(1, CONV_W):
        y = y + xp[:, j:j + L] * w[j]
    new_buf = xp[:, -(CONV_W - 1):]
    yb = y.reshape(B, L, LRU_BLOCKS, LRU_BLOCK)
    r = jax.nn.sigmoid(jnp.einsum('blhi,hij->blhj', yb, p['lru_w_a'][l])
                       + p['lru_b_a'][l].reshape(LRU_BLOCKS, LRU_BLOCK))
    ig = jax.nn.sigmoid(jnp.einsum('blhi,hij->blhj', yb, p['lru_w_x'][l])
                        + p['lru_b_x'][l].reshape(LRU_BLOCKS, LRU_BLOCK))
    log_a = -LRU_C * r * jax.nn.softplus(-p['lru_lambda'][l]).reshape(LRU_BLOCKS, LRU_BLOCK)
    a = jnp.exp(log_a)
    b = jnp.sqrt(-jnp.expm1(2.0 * log_a)) * (ig * yb)
    b = b.at[:, 0].add(a[:, 0] * h0.astype(jnp.float32).reshape(B, LRU_BLOCKS, LRU_BLOCK))

    def comb(left, right):
        return (left[0] * right[0], right[0] * left[1] + right[1])

    _, h = lax.associative_scan(comb, (a, b), axis=1)
    h = h.reshape(B, L, W_BRANCH)
    return h, new_buf.astype(buf.dtype), h[:, -1].astype(h0.dtype)


def _layer(l, x, c, st, p):
    s_shift, s_wkv, s_gla, s_hgrn, s_conv, s_h = st
    dt = x.dtype
    mod = jnp.dot(jax.nn.silu(c.astype(jnp.float32)), p['w_ada'][l]) + p['b_ada'][l]
    shift, scale, gate = jnp.split(mod[:, None, :], 3, axis=-1)
    h = (_rms(x, p['norm_g'][l]) * (1.0 + scale) + shift).astype(dt)
    proj = jnp.einsum('bld,dp->blp', h, p['w_in'][l])
    pr, pg, ph, pl, pz = jnp.split(proj, [SPLIT_GLA, SPLIT_HGRN, SPLIT_LRU, SPLIT_GATE], axis=-1)
    o_a, n_shift, n_wkv = _rwkv7(pr, s_shift, s_wkv, l, p)
    o_b, n_gla = _gla(pg, s_gla, l, p)
    o_c, n_hgrn = _hgrn2(ph, s_hgrn, l, p)
    o_d, n_conv, n_h = _rglru(pl, s_conv, s_h, l, p)
    o = jnp.concatenate([o_a, o_b, o_c, o_d], axis=-1) * jax.nn.silu(pz.astype(jnp.float32))
    y = jnp.einsum('blm,md->bld', o.astype(dt), p['w_out'][l])
    x = (x.astype(jnp.float32) + gate * y.astype(jnp.float32)).astype(dt)
    return x, (n_shift, n_wkv, n_gla, n_hgrn, n_conv, n_h)


def _trunk(x, c, states, p):
    new = []
    for l in range(DEPTH):
        x, ns = _layer(l, x, c, tuple(s[l] for s in states), p)
        new.append(ns)
    y = _rms(x, p['final_g']).astype(x.dtype)
    stacked = tuple(jnp.stack([ns[i] for ns in new], axis=0) for i in range(6))
    return y, stacked


def _hgrn_lower_bounds(logits):
    sm = jax.nn.softmax(logits.astype(jnp.float32), axis=0)
    return jnp.cumsum(sm, axis=0) - sm[0]


def setup_inputs(seed: int = 0) -> dict:
    key = jax.random.key(seed)
    ks = jax.random.split(key, 48)

    def nrm(i, shape, s):
        return s * jax.random.normal(ks[i], shape, jnp.float32)

    u = jax.random.uniform(ks[40], (DEPTH, W_BRANCH), jnp.float32, minval=0.9, maxval=0.999)
    base = u ** (1.0 / LRU_C)
    return {
        'x_prompt': nrm(0, (BATCH, SEQ, D_MODEL), 1.0),
        'x_sample': nrm(1, (DEC_BATCH, DEC_SEQ, D_MODEL), 1.0),
        'c_prompt': nrm(2, (BATCH, D_MODEL), 1.0),
        'c_sample': nrm(3, (DEC_BATCH, D_MODEL), 1.0),
        'state_rwkv_shift': nrm(4, (DEPTH, DEC_BATCH, RWKV_PROJ), 1.0),
        'state_rwkv_wkv': nrm(5, (DEPTH, DEC_BATCH, RWKV_H, RWKV_HEAD, RWKV_HEAD), 0.3),
        'state_gla': nrm(6, (DEPTH, DEC_BATCH, GLA_H, GLA_DK, GLA_DV), 0.3),
        'state_hgrn': nrm(7, (DEPTH, DEC_BATCH, HGRN_H, HGRN_DK, HGRN_DV), 0.3),
        'state_lru_conv': nrm(8, (DEPTH, DEC_BATCH, CONV_W - 1, W_BRANCH), 1.0),
        'state_lru_h': nrm(9, (DEPTH, DEC_BATCH, W_BRANCH), 0.5),
        'norm_g': 1.0 + nrm(10, (DEPTH, D_MODEL), 0.02),
        'w_ada': nrm(11, (DEPTH, D_MODEL, 3 * D_MODEL), 0.3 * D_MODEL ** -0.5),
        'b_ada': nrm(12, (DEPTH, 3 * D_MODEL), 0.01),
        'w_in': nrm(13, (DEPTH, D_MODEL, PROJ_IN), D_MODEL ** -0.5),
        'w_out': nrm(14, (DEPTH, D_MIX, D_MODEL), D_MIX ** -0.5),
        'rwkv_mu': jax.random.uniform(ks[15], (DEPTH, RWKV_PROJ), jnp.float32),
        'rwkv_w0': jax.random.uniform(ks[16], (DEPTH, W_BRANCH), jnp.float32, minval=-5.0, maxval=-0.5),
        'rwkv_w_up': nrm(17, (DEPTH, RWKV_DECAY_RANK, W_BRANCH), 0.5 * RWKV_DECAY_RANK ** -0.5),
        'rwkv_a0': nrm(18, (DEPTH, W_BRANCH), 0.1),
        'rwkv_a_up': nrm(19, (DEPTH, RWKV_A_RANK, W_BRANCH), 0.5 * RWKV_A_RANK ** -0.5),
        'rwkv_k_k': 0.85 + nrm(20, (DEPTH, W_BRANCH), 0.1),
        'rwkv_k_a': 1.0 + nrm(21, (DEPTH, W_BRANCH), 0.1),
        'rwkv_r_k': nrm(22, (DEPTH, RWKV_H, RWKV_HEAD), 0.1),
        'rwkv_gn_g': 1.0 + nrm(23, (DEPTH, W_BRANCH), 0.02),
        'rwkv_gn_b': nrm(24, (DEPTH, W_BRANCH), 0.01),
        'gla_gk_up': nrm(25, (DEPTH, GLA_GATE_RANK, GLA_KW), GLA_GATE_RANK ** -0.5),
        'gla_gk_b': nrm(26, (DEPTH, GLA_KW), 0.1),
        'gla_gn_g': 1.0 + nrm(27, (DEPTH, W_BRANCH), 0.02),
        'hgrn_lb_logits': nrm(28, (DEPTH, W_BRANCH), 0.1),
        'hgrn_gn_g': 1.0 + nrm(29, (DEPTH, W_BRANCH), 0.02),
        'lru_conv_w': nrm(30, (DEPTH, CONV_W, W_BRANCH), CONV_W ** -0.5),
        'lru_conv_b': nrm(31, (DEPTH, W_BRANCH), 0.01),
        'lru_w_a': nrm(32, (DEPTH, LRU_BLOCKS, LRU_BLOCK, LRU_BLOCK), LRU_BLOCK ** -0.5),
        'lru_b_a': nrm(33, (DEPTH, W_BRANCH), 0.01),
        'lru_w_x': nrm(34, (DEPTH, LRU_BLOCKS, LRU_BLOCK, LRU_BLOCK), LRU_BLOCK ** -0.5),
        'lru_b_x': nrm(35, (DEPTH, W_BRANCH), 0.01),
        'lru_lambda': jnp.log(base) - jnp.log1p(-base),
        'final_g': 1.0 + nrm(41, (D_MODEL,), 0.02),
    }


def reference(x_prompt, x_sample, c_prompt, c_sample, state_rwkv_shift, state_rwkv_wkv, state_gla,
              state_hgrn, state_lru_conv, state_lru_h, norm_g, w_ada, b_ada, w_in, w_out, rwkv_mu,
              rwkv_w0, rwkv_w_up, rwkv_a0, rwkv_a_up, rwkv_k_k, rwkv_k_a, rwkv_r_k, rwkv_gn_g, rwkv_gn_b,
              gla_gk_up, gla_gk_b, gla_gn_g, hgrn_lb_logits, hgrn_gn_g, lru_conv_w, lru_conv_b, lru_w_a,
              lru_b_a, lru_w_x, lru_b_x, lru_lambda, final_g):
    p = {
        'norm_g': norm_g, 'w_ada': w_ada, 'b_ada': b_ada, 'w_in': w_in, 'w_out': w_out,
        'rwkv_mu': rwkv_mu, 'rwkv_w0': rwkv_w0, 'rwkv_w_up': rwkv_w_up, 'rwkv_a0': rwkv_a0,
        'rwkv_a_up': rwkv_a_up, 'rwkv_k_k': rwkv_k_k, 'rwkv_k_a': rwkv_k_a, 'rwkv_r_k': rwkv_r_k,
        'rwkv_gn_g': rwkv_gn_g, 'rwkv_gn_b': rwkv_gn_b,
        'gla_gk_up': gla_gk_up, 'gla_gk_b': gla_gk_b, 'gla_gn_g': gla_gn_g,
        'hgrn_lb': _hgrn_lower_bounds(hgrn_lb_logits), 'hgrn_gn_g': hgrn_gn_g,
        'lru_conv_w': lru_conv_w, 'lru_conv_b': lru_conv_b, 'lru_w_a': lru_w_a, 'lru_b_a': lru_b_a,
        'lru_w_x': lru_w_x, 'lru_b_x': lru_b_x, 'lru_lambda': lru_lambda, 'final_g': final_g,
    }
    bp = x_prompt.shape[0]
    dt = x_prompt.dtype
    zero_states = (
        jnp.zeros((DEPTH, bp, RWKV_PROJ), dt),
        jnp.zeros((DEPTH, bp, RWKV_H, RWKV_HEAD, RWKV_HEAD), dt),
        jnp.zeros((DEPTH, bp, GLA_H, GLA_DK, GLA_DV), dt),
        jnp.zeros((DEPTH, bp, HGRN_H, HGRN_DK, HGRN_DV), dt),
        jnp.zeros((DEPTH, bp, CONV_W - 1, W_BRANCH), dt),
        jnp.zeros((DEPTH, bp, W_BRANCH), dt),
    )
    y_prompt, (p_shift, p_wkv, p_gla, p_hgrn, p_conv, p_h) = _trunk(x_prompt, c_prompt, zero_states, p)
    sample_states = (state_rwkv_shift, state_rwkv_wkv, state_gla, state_hgrn, state_lru_conv, state_lru_h)
    y_sample, (s_shift, s_wkv, s_gla, s_hgrn, s_conv, s_h) = _trunk(x_sample, c_sample, sample_states, p)
    return (y_prompt, y_sample, p_shift, p_wkv, p_gla, p_hgrn, p_conv, p_h,
            s_shift, s_wkv, s_gla, s_hgrn, s_conv, s_h)
```

```python
import functools

import jax
import jax.numpy as jnp
from jax import lax
from jax.experimental import pallas as pl
from jax.experimental.pallas import tpu as pltpu

f32 = jnp.float32
bf16 = jnp.bfloat16

D_MODEL = 2048
DEPTH = 4
W_BRANCH = 512
EPS = 1e-6

RWKV_H = 8
RWKV_HEAD = 64
RWKV_LORA = 64
RWKV_PROJ = 3 * W_BRANCH + 2 * RWKV_LORA
RWKV_GN_EPS = 64e-5

GLA_H = 4
GLA_DK = 64
GLA_DV = 128
GLA_KW = GLA_H * GLA_DK
GLA_GATE_RANK = 16
GLA_GATE_NORM = 16.0

HGRN_H = 4
HGRN_DK = 128
HGRN_DV = 128
LB_FLOOR = 1e-30

LRU_BLOCKS = 8
LRU_BLOCK = 64
LRU_C = 8.0
CONV_W = 4

CHUNK = 32

C_R, C_K, C_V = 0, 512, 1024
C_GLA_V = 1536
C_HQ, C_HF, C_HI = 2048, 2560, 3072
C_LRU = 3584
C_GATE = 4096
C_GLA_Q, C_GLA_K = 6144, 6400
C_RWKV_LO = 6656
C_GLA_LO = 6784
PROJ_P = 6912

LANE = 128
VMEM_LIMIT = 56 * 1024 * 1024

HIGHEST = lax.Precision.HIGHEST


def _cparams(sem):
    return pltpu.CompilerParams(dimension_semantics=sem, vmem_limit_bytes=VMEM_LIMIT)


def _dotf(a, b):
    return jnp.dot(a, b, preferred_element_type=f32, precision=HIGHEST)


def _dot_nt(a, b):
    return lax.dot_general(a, b, (((1,), (1,)), ((), ())), preferred_element_type=f32, precision=HIGHEST)


def _dot_tn(a, b):
    return lax.dot_general(a, b, (((0,), (0,)), ((), ())), preferred_element_type=f32, precision=HIGHEST)


def _split3(x):
    hi = x.astype(bf16)
    r1 = x - hi.astype(f32)
    mid = r1.astype(bf16)
    lo = (r1 - mid.astype(f32)).astype(bf16)
    return hi, mid, lo


def _seg_sum(x, ones):
    hi, mid, lo = _split3(x)
    d = functools.partial(jnp.dot, preferred_element_type=f32)
    return d(hi, ones) + d(mid, ones) + d(lo, ones)


def _ones_seg(ones, x):
    hi, mid, lo = _split3(x)
    d = functools.partial(jnp.dot, preferred_element_type=f32)
    return d(ones, hi) + d(ones, mid) + d(ones, lo)


def _sigmoid(x):
    return jax.nn.sigmoid(x)


def _softplus(x):
    return jnp.maximum(x, 0.0) + jnp.log1p(jnp.exp(-jnp.abs(x)))


def _log_sigmoid(x):
    return -_softplus(-x)


def _ada_kernel(c_ref, w_ref, b_ref, o_ref):
    c = c_ref[...]
    s = (c * _sigmoid(c)).astype(bf16)
    o_ref[...] = jnp.dot(s, w_ref[...].astype(bf16), preferred_element_type=f32) + b_ref[...]


def _ada(c_all, w_ada, b_ada):
    rows = c_all.shape[0]
    tn = 512
    n = w_ada.shape[2]
    return pl.pallas_call(
        _ada_kernel,
        grid=(DEPTH, n // tn),
        in_specs=[
            pl.BlockSpec((rows, D_MODEL), lambda l, j: (0, 0)),
            pl.BlockSpec((None, D_MODEL, tn), lambda l, j: (l, 0, j)),
            pl.BlockSpec((None, 1, tn), lambda l, j: (l, 0, j)),
        ],
        out_specs=pl.BlockSpec((None, rows, tn), lambda l, j: (l, 0, j)),
        out_shape=jax.ShapeDtypeStruct((DEPTH, rows, n), f32),
        compiler_params=_cparams(("parallel", "parallel")),
        name="ada_mod",
    )(c_all, w_ada, b_ada.reshape(DEPTH, 1, n))


def _inproj_kernel(x_ref, g_ref, sc_ref, sh_ref, w_ref, o_ref, h_ref):
    @pl.when(pl.program_id(1) == 0)
    def _():
        x = x_ref[...]
        ms = jnp.mean(x * x, axis=-1, keepdims=True)
        h = x * lax.rsqrt(ms + EPS) * g_ref[...]
        h = h * (1.0 + sc_ref[...]) + sh_ref[...]
        h_ref[...] = h.astype(bf16)

    o_ref[...] = jnp.dot(h_ref[...], w_ref[...], preferred_element_type=f32)


def _mod_spec(l, which, per_seq_tiles, tm):
    if per_seq_tiles is None:
        return pl.BlockSpec((None, tm, D_MODEL), lambda i, *_: (l, 0, which))
    return pl.BlockSpec((None, 1, D_MODEL), lambda i, *_: (i // per_seq_tiles * 3 + which, 0, 0))


def _inproj(x2, norm_g3, mod, w_in_p, l, per_seq_tiles, tm):
    t = x2.shape[0]
    tn = 768
    return pl.pallas_call(
        _inproj_kernel,
        grid=(t // tm, PROJ_P // tn),
        in_specs=[
            pl.BlockSpec((tm, D_MODEL), lambda i, j: (i, 0)),
            pl.BlockSpec((None, 1, D_MODEL), lambda i, j: (l, 0, 0)),
            _mod_spec(l, 1, per_seq_tiles, tm),
            _mod_spec(l, 0, per_seq_tiles, tm),
            pl.BlockSpec((None, D_MODEL, tn), lambda i, j: (l, 0, j)),
        ],
        out_specs=pl.BlockSpec((tm, tn), lambda i, j: (i, j)),
        out_shape=jax.ShapeDtypeStruct((t, PROJ_P), f32),
        scratch_shapes=[pltpu.VMEM((tm, D_MODEL), bf16)],
        compiler_params=_cparams(("parallel", "arbitrary")),
        name="in_proj",
    )(x2, norm_g3, mod, mod, w_in_p)


def _outproj_kernel(oa_ref, ob_ref, oc_ref, od_ref, pz_ref, x_ref, gate_ref, w_ref, fg_ref, o_ref, *, final):
    z = pz_ref[...]
    o = jnp.concatenate([oa_ref[...], ob_ref[...], oc_ref[...], od_ref[...]], axis=1)
    o = o * (z * _sigmoid(z))
    y = jnp.dot(o.astype(bf16), w_ref[...], preferred_element_type=f32)
    xn = x_ref[...] + gate_ref[...] * y
    if final:
        ms = jnp.mean(xn * xn, axis=-1, keepdims=True)
        xn = xn * lax.rsqrt(ms + EPS) * fg_ref[...]
    o_ref[...] = xn


def _outproj(outs, proj2, x2, mod, w_out_b, final_g2, l, per_seq_tiles, tm, final):
    t = x2.shape[0]
    mix_spec = pl.BlockSpec((tm, W_BRANCH), lambda i: (i, 0))
    return pl.pallas_call(
        functools.partial(_outproj_kernel, final=final),
        grid=(t // tm,),
        in_specs=[
            mix_spec, mix_spec, mix_spec, mix_spec,
            pl.BlockSpec((tm, D_MODEL), lambda i: (i, C_GATE // D_MODEL)),
            pl.BlockSpec((tm, D_MODEL), lambda i: (i, 0)),
            _mod_spec(l, 2, per_seq_tiles, tm),
            pl.BlockSpec((None, D_MODEL, D_MODEL), lambda i: (l, 0, 0)),
            pl.BlockSpec((1, D_MODEL), lambda i: (0, 0)),
        ],
        out_specs=pl.BlockSpec((tm, D_MODEL), lambda i: (i, 0)),
        out_shape=jax.ShapeDtypeStruct((t, D_MODEL), f32),
        compiler_params=_cparams(("parallel",)),
        name="out_proj",
    )(*outs, proj2, x2, mod, w_out_b, final_g2)


def _rwkv_prologue(r, k, v, lo, pr, pk, pv, plo, mu_rkv, mu_lo, wup, pvec, bo):
    w0, a0, k_k, k_a, r_k = (pvec[i:i + 1, :] for i in range(5))
    xr = r + (pr - r) * mu_rkv[:, 0:512]
    xk = k + (pk - k) * mu_rkv[:, 512:1024]
    xv = v + (pv - v) * mu_rkv[:, 1024:1536]
    xlo = lo + (plo - lo) * mu_lo
    lane = lax.broadcasted_iota(jnp.int32, xlo.shape, 1)
    act = jnp.where(lane < RWKV_LORA, jnp.tanh(xlo), xlo)
    up = _dotf(act, wup)
    w_raw = -_softplus(-(w0 + up[:, 0:512])) - 0.5
    ew = jnp.exp(-jnp.exp(w_raw))
    a = _sigmoid(a0 + up[:, 512:1024])
    kk = xk * k_k
    kk = kk / jnp.maximum(jnp.sqrt(_seg_sum(kk * kk, bo)), 1e-12)
    kh = xk * (1.0 + (a - 1.0) * k_a)
    alp = kk * a
    return dict(
        kap=kk, ew=ew, alp=alp, kh=kh, vv=xv,
        wr=ew * xr,
        ar=_seg_sum(alp * xr, bo),
        kr=_seg_sum(kh * xr, bo),
        bonus=_seg_sum(xr * kh * r_k, bo) * xv,
    )


_RWKV_STEP_KEYS = ("kap", "ew", "alp", "kh", "vv", "wr", "ar", "kr")


def _rwkv_step(s, kap, ew, alp, kh, vv, wr, ar, kr, bo, idt):
    x = jnp.concatenate([s * kap, s * wr, idt * vv], axis=0)
    red = _seg_sum(x, bo)
    sk, swr, vcol = red[0:64], red[64:128], red[128:192]
    s_new = s * ew - sk * alp + vcol * kh
    ocol = swr - sk * ar + vcol * kr
    o_row = jnp.sum(ocol * idt, axis=0, keepdims=True)
    return s_new, o_row


def _rwkv_epilogue(o, bonus, pvec, bo):
    gn_g, gn_b = pvec[5:6, :], pvec[6:7, :]
    mu = _seg_sum(o, bo) * (1.0 / RWKV_HEAD)
    d = o - mu
    var = _seg_sum(d * d, bo) * (1.0 / RWKV_HEAD)
    return d * lax.rsqrt(var + RWKV_GN_EPS) * gn_g + gn_b + bonus


def _rwkv_seq_kernel(r_ref, k_ref, v_ref, lo_ref, prkv_ref, plo_ref, sin_ref, mu_rkv_ref, mu_lo_ref, wup_ref,
                     pvec_ref, bo_ref, idt_ref, o_ref, sout_ref,
                     s_sc, crkv_sc, clo_sc, kap_sc, ew_sc, alp_sc, kh_sc, vv_sc, wr_sc, ar_sc, kr_sc, oraw_sc,
                     bonus_sc, *, g_seqs, lb):
    tb = pl.program_id(1)

    @pl.when(tb == 0)
    def _():
        s_sc[...] = sin_ref[...]
        crkv_sc[...] = prkv_ref[...]
        clo_sc[...] = plo_ref[...]

    bo = bo_ref[...]
    idt = idt_ref[...]
    pvec = pvec_ref[...]
    step_sc = dict(kap=kap_sc, ew=ew_sc, alp=alp_sc, kh=kh_sc, vv=vv_sc, wr=wr_sc, ar=ar_sc, kr=kr_sc)

    row0 = lax.broadcasted_iota(jnp.int32, (lb, 1), 0) == 0
    for g in range(g_seqs):
        cur = [r_ref[g], k_ref[g], v_ref[g]]
        lo = lo_ref[g]
        carry = crkv_sc[g]
        prev = [jnp.where(row0, carry[:, i * 512:(i + 1) * 512], pltpu.roll(c, 1, 0)) for i, c in enumerate(cur)]
        plo = jnp.where(row0, clo_sc[g], pltpu.roll(lo, 1, 0))
        res = _rwkv_prologue(cur[0], cur[1], cur[2], lo, prev[0], prev[1], prev[2], plo,
                             mu_rkv_ref[...], mu_lo_ref[...], wup_ref[...], pvec, bo)
        for key in _RWKV_STEP_KEYS:
            step_sc[key][g] = res[key]
        bonus_sc[g] = res["bonus"]
        for i, c in enumerate(cur):
            crkv_sc[g, :, i * 512:(i + 1) * 512] = c[lb - 1:lb, :]
        clo_sc[g] = lo[lb - 1:lb, :]

    def body(t, carry):
        for g in range(g_seqs):
            rows = [step_sc[key][g, pl.ds(t, 1), :] for key in _RWKV_STEP_KEYS]
            s_new, o_row = _rwkv_step(s_sc[g], *rows, bo, idt)
            s_sc[g] = s_new
            oraw_sc[g, pl.ds(t, 1), :] = o_row
        return carry

    lax.fori_loop(0, lb, body, 0)

    for g in range(g_seqs):
        o_ref[g] = _rwkv_epilogue(oraw_sc[g], bonus_sc[g], pvec, bo)

    @pl.when(tb == pl.num_programs(1) - 1)
    def _():
        sout_ref[...] = s_sc[...]


def _rwkv_tok_kernel(r_ref, k_ref, v_ref, lo_ref, prkv_ref, plo_ref, sin_ref, mu_rkv_ref, mu_lo_ref, wup_ref,
                     pvec_ref, bo_ref, idt_ref, o_ref, sout_ref, *, g_rows):
    bo = bo_ref[...]
    idt = idt_ref[...]
    pvec = pvec_ref[...]
    prkv = prkv_ref[...]
    res = _rwkv_prologue(r_ref[...], k_ref[...], v_ref[...], lo_ref[...],
                         prkv[:, 0:512], prkv[:, 512:1024], prkv[:, 1024:1536], plo_ref[...],
                         mu_rkv_ref[...], mu_lo_ref[...], wup_ref[...], pvec, bo)
    o_rows = []
    for g in range(g_rows):
        rows = [res[key][g:g + 1, :] for key in _RWKV_STEP_KEYS]
        s_new, o_row = _rwkv_step(sin_ref[g], *rows, bo, idt)
        sout_ref[g] = s_new
        o_rows.append(o_row)
    o_ref[...] = _rwkv_epilogue(jnp.concatenate(o_rows, axis=0), res["bonus"], pvec, bo)


def _rwkv_consts():
    i = jnp.arange(W_BRANCH)
    bo = (i[:, None] // RWKV_HEAD == i[None, :] // RWKV_HEAD).astype(bf16)
    idt = (jnp.arange(RWKV_HEAD)[:, None] == (i[None, :] % RWKV_HEAD)).astype(f32)
    return bo, idt


def _rwkv_weight_specs(l):
    def cs(shape):
        return pl.BlockSpec((None,) + shape, lambda *_: (l,) + (0,) * len(shape))

    def const(shape):
        return pl.BlockSpec(shape, lambda *_: (0,) * len(shape))

    return [cs((1, 1536)), cs((1, LANE)), cs((LANE, 1024)), cs((8, W_BRANCH)),
            const((W_BRANCH, W_BRANCH)), const((RWKV_HEAD, W_BRANCH))]


def _rwkv_seq(proj3, prev_rkv, prev_lo, s_in, wts, l, lb):
    b, seq_len, _ = proj3.shape
    g = b
    nt = seq_len // lb

    def col(width, off):
        return pl.BlockSpec((g, lb, width), lambda i, t: (i, t, off // width))

    tok_sc = pltpu.VMEM((g, lb, W_BRANCH), f32)
    out = pl.pallas_call(
        functools.partial(_rwkv_seq_kernel, g_seqs=g, lb=lb),
        grid=(b // g, nt),
        in_specs=[
            col(512, C_R), col(512, C_K), col(512, C_V), col(LANE, C_RWKV_LO),
            pl.BlockSpec((g, 1, 1536), lambda i, t: (i, 0, 0)),
            pl.BlockSpec((g, 1, LANE), lambda i, t: (i, 0, 0)),
            pl.BlockSpec((g, RWKV_HEAD, W_BRANCH), lambda i, t: (i, 0, 0)),
        ] + _rwkv_weight_specs(l),
        out_specs=[
            pl.BlockSpec((g, lb, W_BRANCH), lambda i, t: (i, t, 0)),
            pl.BlockSpec((g, RWKV_HEAD, W_BRANCH), lambda i, t: (i, 0, 0)),
        ],
        out_shape=[
            jax.ShapeDtypeStruct((b, seq_len, W_BRANCH), f32),
            jax.ShapeDtypeStruct((b, RWKV_HEAD, W_BRANCH), f32),
        ],
        scratch_shapes=[
            pltpu.VMEM((g, RWKV_HEAD, W_BRANCH), f32),
            pltpu.VMEM((g, 1, 1536), f32),
            pltpu.VMEM((g, 1, LANE), f32),
        ] + [tok_sc] * 10,
        compiler_params=_cparams(("parallel", "arbitrary")),
        name="rwkv_seq",
    )(proj3, proj3, proj3, proj3, prev_rkv, prev_lo, s_in, *wts)
    return out


def _rwkv_tok(proj2, prev_rkv, prev_lo, s_in, wts, l, g):
    b = proj2.shape[0]

    def col(width, off):
        return pl.BlockSpec((g, width), lambda i: (i, off // width))

    return pl.pallas_call(
        functools.partial(_rwkv_tok_kernel, g_rows=g),
        grid=(b // g,),
        in_specs=[
            col(512, C_R), col(512, C_K), col(512, C_V), col(LANE, C_RWKV_LO),
            pl.BlockSpec((g, 1536), lambda i: (i, 0)),
            pl.BlockSpec((g, LANE), lambda i: (i, 0)),
            pl.BlockSpec((g, RWKV_HEAD, W_BRANCH), lambda i: (i, 0, 0)),
        ] + _rwkv_weight_specs(l),
        out_specs=[
            pl.BlockSpec((g, W_BRANCH), lambda i: (i, 0)),
            pl.BlockSpec((g, RWKV_HEAD, W_BRANCH), lambda i: (i, 0, 0)),
        ],
        out_shape=[
            jax.ShapeDtypeStruct((b, W_BRANCH), f32),
            jax.ShapeDtypeStruct((b, RWKV_HEAD, W_BRANCH), f32),
        ],
        compiler_params=_cparams(("parallel",)),
        name="rwkv_tok",
    )(proj2, proj2, proj2, proj2, prev_rkv, prev_lo, s_in, *wts)


def _gla_inputs(q_ref, k_ref, v_ref, lo_ref, gup_ref, gb_ref):
    q = q_ref[...] * (GLA_DK ** -0.5)
    z = _dotf(lo_ref[...], gup_ref[...]) + gb_ref[...]
    g = _log_sigmoid(z) * (1.0 / GLA_GATE_NORM)
    return q, k_ref[...], v_ref[...], g


def _hgrn_lb(logits, l):
    m = jnp.max(logits, axis=0, keepdims=True)
    e = jnp.exp(logits - m)
    sm = e / jnp.sum(e, axis=0, keepdims=True)
    lb = jnp.zeros_like(sm[0:1, :])
    for i in range(1, l + 1):
        lb = lb + sm[i:i + 1, :]
    return lb


def _hgrn_inputs(q_ref, f_ref, i_ref, logits_ref, l):
    lb = _hgrn_lb(logits_ref[...], l)
    f_lo = f_ref[...]
    a = jnp.log(jnp.maximum(lb, LB_FLOOR))
    b = jnp.log1p(-lb) + _log_sigmoid(f_lo)
    logf = jnp.maximum(a, b) + jnp.log1p(jnp.exp(-jnp.abs(a - b)))
    k = (1.0 - lb) * _sigmoid(-f_lo)
    return q_ref[...], k, i_ref[...], logf


def _chunk_core(q, k, v, g, lt_ref, bokv_ref, bdm_ref, bon_ref, gn_ref, st_sc, kpad, bpad, vpad, lb):
    bc = _ones_seg(lt_ref[...], g)
    kpad[CHUNK:CHUNK + lb, :] = k
    bpad[CHUNK:CHUNK + lb, :] = bc
    vpad[CHUNK:CHUNK + lb, :] = v
    rowc = lax.broadcasted_iota(jnp.int32, (lb, 1), 0) & (CHUNK - 1)
    bokv = bokv_ref[...]

    o = jnp.zeros((lb, W_BRANCH), f32)
    for d in range(CHUNK):
        start = CHUNK - d
        ks = kpad[start:start + lb, :]
        bs = bpad[start:start + lb, :]
        vs = vpad[start:start + lb, :]
        m = rowc >= d
        diff = jnp.where(m, bc - bs, 0.0)
        z = jnp.where(m, q * ks * jnp.exp(diff), 0.0)
        o = o + _seg_sum(z, bokv) * vs

    bdm = bdm_ref[...]
    st = st_sc[...]
    outs = []
    for c in range(lb // CHUNK):
        sl = slice(c * CHUNK, (c + 1) * CHUNK)
        bcc = bc[sl]
        blast = bcc[CHUNK - 1:CHUNK, :]
        qe = q[sl] * jnp.exp(bcc)
        outs.append(o[sl] + _dot_nt(qe, st))
        ke = k[sl] * jnp.exp(blast - bcc)
        st = st * jnp.exp(blast) + _dot_tn(v[sl], ke) * bdm
    st_sc[...] = st
    o = jnp.concatenate(outs, axis=0)
    ms = _seg_sum(o * o, bon_ref[...]) * (1.0 / LANE)
    return o * lax.rsqrt(ms + EPS) * gn_ref[...]


def _chunk_seq_kernel(*refs, kind, l, lb):
    n_in = 6 if kind == "gla" else 4
    ins = refs[:n_in]
    lt_ref, bokv_ref, bdm_ref, bon_ref, gn_ref, o_ref, stout_ref, st_sc, kpad, bpad, vpad = refs[n_in:]
    tb = pl.program_id(1)

    @pl.when(tb == 0)
    def _():
        st_sc[...] = jnp.zeros_like(st_sc)
        kpad[0:CHUNK, :] = jnp.zeros((CHUNK, kpad.shape[1]), f32)
        bpad[0:CHUNK, :] = jnp.zeros((CHUNK, bpad.shape[1]), f32)
        vpad[0:CHUNK, :] = jnp.zeros((CHUNK, vpad.shape[1]), f32)

    if kind == "gla":
        q, k, v, g = _gla_inputs(*ins)
    else:
        q, k, v, g = _hgrn_inputs(*ins, l)
    o_ref[...] = _chunk_core(q, k, v, g, lt_ref, bokv_ref, bdm_ref, bon_ref, gn_ref, st_sc, kpad, bpad, vpad, lb)

    @pl.when(tb == pl.num_programs(1) - 1)
    def _():
        stout_ref[...] = st_sc[...]


def _chunk_consts(h, dk, lb):
    f = h * dk
    i = jnp.arange(lb)
    lt = ((i[:, None] // CHUNK == i[None, :] // CHUNK) & (i[:, None] >= i[None, :])).astype(bf16)
    fi = jnp.arange(f)
    oi = jnp.arange(W_BRANCH)
    bokv = (fi[:, None] // dk == oi[None, :] // LANE).astype(bf16)
    bdm = (oi[:, None] // LANE == fi[None, :] // dk).astype(f32)
    bon = (oi[:, None] // LANE == oi[None, :] // LANE).astype(bf16)
    return lt, bokv, bdm, bon


def _chunk_seq(kind, proj3, extra, gn3, l, lb):
    b, seq_len, _ = proj3.shape
    h, dk = (GLA_H, GLA_DK) if kind == "gla" else (HGRN_H, HGRN_DK)
    f = h * dk
    nt = seq_len // lb

    def col(width, off):
        return pl.BlockSpec((None, lb, width), lambda i, t: (i, t, off // width))

    def const(shape):
        return pl.BlockSpec(shape, lambda *_: (0,) * len(shape))

    def layer(shape):
        return pl.BlockSpec((None,) + shape, lambda *_: (l,) + (0,) * len(shape))

    if kind == "gla":
        gup, gb = extra
        in_specs = [col(GLA_KW, C_GLA_Q), col(GLA_KW, C_GLA_K), col(W_BRANCH, C_GLA_V), col(LANE, C_GLA_LO),
                    layer((LANE, GLA_KW)), layer((1, GLA_KW))]
        args = [proj3, proj3, proj3, proj3, gup, gb]
    else:
        (logits,) = extra
        in_specs = [col(W_BRANCH, C_HQ), col(W_BRANCH, C_HF), col(W_BRANCH, C_HI), const((DEPTH, W_BRANCH))]
        args = [proj3, proj3, proj3, logits]
    consts = _chunk_consts(h, dk, lb)
    in_specs += [const(c.shape) for c in consts] + [layer((1, W_BRANCH))]
    return pl.pallas_call(
        functools.partial(_chunk_seq_kernel, kind=kind, l=l, lb=lb),
        grid=(b, nt),
        in_specs=in_specs,
        out_specs=[
            pl.BlockSpec((None, lb, W_BRANCH), lambda i, t: (i, t, 0)),
            pl.BlockSpec((None, W_BRANCH, f), lambda i, t: (i, 0, 0)),
        ],
        out_shape=[
            jax.ShapeDtypeStruct((b, seq_len, W_BRANCH), f32),
            jax.ShapeDtypeStruct((b, W_BRANCH, f), f32),
        ],
        scratch_shapes=[
            pltpu.VMEM((W_BRANCH, f), f32),
            pltpu.VMEM((CHUNK + lb, f), f32),
            pltpu.VMEM((CHUNK + lb, f), f32),
            pltpu.VMEM((CHUNK + lb, W_BRANCH), f32),
        ],
        compiler_params=_cparams(("parallel", "arbitrary")),
        name=kind + "_seq",
    )(*args, *consts, gn3)


def _state_from_transposed(st, h, dk):
    blocks = [st[:, i * LANE:(i + 1) * LANE, i * dk:(i + 1) * dk] for i in range(h)]
    return jnp.swapaxes(jnp.stack(blocks, axis=1), 2, 3)


def _block_diag_rows(x, g, dk):
    row = lax.broadcasted_iota(jnp.int32, x.shape, 0)
    lane = lax.broadcasted_iota(jnp.int32, x.shape, 1)
    per = LANE // dk
    pieces = []
    for j in range(g // per):
        keep = row == (j * per + lane // dk)
        pieces.append(jnp.where(keep, x, 0.0))
    return jnp.concatenate(pieces, axis=1)


def _chunk_tok_kernel(*refs, kind, l, g):
    if kind == "gla":
        q_ref, k_ref, v_ref, lo_ref, gup_ref, gb_ref, s_ref, gn_ref, o_ref, sout_ref = refs
        odd = (pl.program_id(1) % 2) == 1
        lane = lax.broadcasted_iota(jnp.int32, (g, LANE), 1)
        own = (lane >= GLA_DK) == odd

        def pick(x):
            return jnp.where(own, x, pltpu.roll(x, GLA_DK, 1))

        q = pick(q_ref[...]) * (GLA_DK ** -0.5)
        k = pick(k_ref[...])
        z = _dotf(lo_ref[...], gup_ref[...]) + gb_ref[...]
        dec = jnp.exp(pick(_log_sigmoid(z) * (1.0 / GLA_GATE_NORM)))
        v = v_ref[...]
        dk = GLA_DK
    else:
        q_ref, f_ref, i_ref, logits_ref, s_ref, gn_ref, o_ref, sout_ref = refs
        q, k, v, logf = _hgrn_inputs(q_ref, f_ref, i_ref, logits_ref, l)
        dec = jnp.exp(logf)
        dk = HGRN_DK
    s = s_ref[...].reshape(g * dk, LANE)
    ones = jnp.ones((g, LANE), f32)
    dcol = _dot_tn(_block_diag_rows(dec, g, dk), ones)
    s_new = s * dcol + _dot_tn(_block_diag_rows(k, g, dk), v)
    o = _dotf(_block_diag_rows(q, g, dk), s_new)
    ms = jnp.mean(o * o, axis=-1, keepdims=True)
    o_ref[...] = o * lax.rsqrt(ms + EPS) * gn_ref[...]
    sout_ref[...] = s_new.reshape(g, dk, LANE)


def _chunk_tok(kind, proj2, extra, gn3, s_in, l, g):
    b = proj2.shape[0]
    h, dk = (GLA_H, GLA_DK) if kind == "gla" else (HGRN_H, HGRN_DK)

    def head_col(off):
        per = LANE // dk
        return pl.BlockSpec((g, LANE), lambda i, j: (i, off // LANE + j // per))

    def const(shape):
        return pl.BlockSpec(shape, lambda *_: (0,) * len(shape))

    if kind == "gla":
        gup, gb = extra
        in_specs = [head_col(C_GLA_Q), head_col(C_GLA_K),
                    pl.BlockSpec((g, LANE), lambda i, j: (i, C_GLA_V // LANE + j)),
                    pl.BlockSpec((g, LANE), lambda i, j: (i, C_GLA_LO // LANE)),
                    pl.BlockSpec((None, LANE, LANE), lambda i, j: (l, 0, j // 2)),
                    pl.BlockSpec((None, 1, LANE), lambda i, j: (l, 0, j // 2))]
        args = [proj2, proj2, proj2, proj2, gup, gb]
    else:
        (logits,) = extra
        in_specs = [head_col(C_HQ), head_col(C_HF), head_col(C_HI),
                    pl.BlockSpec((DEPTH, LANE), lambda i, j: (0, j))]
        args = [proj2, proj2, proj2, logits]
    in_specs += [pl.BlockSpec((g, None, dk, LANE), lambda i, j: (i, j, 0, 0)),
                 pl.BlockSpec((None, 1, LANE), lambda i, j: (l, 0, j))]
    return pl.pallas_call(
        functools.partial(_chunk_tok_kernel, kind=kind, l=l, g=g),
        grid=(b // g, h),
        in_specs=in_specs,
        out_specs=[
            pl.BlockSpec((g, LANE), lambda i, j: (i, j)),
            pl.BlockSpec((g, None, dk, LANE), lambda i, j: (i, j, 0, 0)),
        ],
        out_shape=[
            jax.ShapeDtypeStruct((b, W_BRANCH), f32),
            jax.ShapeDtypeStruct((b, h, dk, LANE), f32),
        ],
        compiler_params=_cparams(("parallel", "parallel")),
        name=kind + "_tok",
    )(*args, s_in, gn3)


def _lru_gates(y, wbd, pvec):
    b_a, b_x, lam = pvec[1:2, :], pvec[2:3, :], pvec[3:4, :]
    gates = _dotf(y, wbd)
    r = _sigmoid(gates[:, 0:W_BRANCH] + b_a)
    ig = _sigmoid(gates[:, W_BRANCH:2 * W_BRANCH] + b_x)
    log_a = -LRU_C * r * _softplus(-lam)
    a = jnp.exp(log_a)
    one_m_a2 = -jnp.tanh(log_a) * (jnp.exp(2.0 * log_a) + 1.0)
    b = jnp.sqrt(one_m_a2) * (ig * y)
    return a, b


def _lru_seq_kernel(x_ref, cw_ref, pvec_ref, wbd_ref, o_ref, xpad, hcar, a_sc, b_sc, *, lb):
    tb = pl.program_id(1)

    @pl.when(tb == 0)
    def _():
        xpad[0:8, :] = jnp.zeros((8, W_BRANCH), f32)
        hcar[...] = jnp.zeros_like(hcar)

    x = x_ref[...]
    xpad[8:8 + lb, :] = x
    cw = cw_ref[...]
    pvec = pvec_ref[...]
    y = pvec[0:1, :] + x * cw[3:4, :]
    for j in range(CONV_W - 1):
        y = y + xpad[5 + j:5 + j + lb, :] * cw[j:j + 1, :]
    a, b = _lru_gates(y, wbd_ref[...], pvec)
    a_sc[...] = a
    b_sc[...] = b

    def body(t, h):
        h = a_sc[pl.ds(t, 1), :] * h + b_sc[pl.ds(t, 1), :]
        o_ref[pl.ds(t, 1), :] = h
        return h

    hcar[...] = lax.fori_loop(0, lb, body, hcar[...], unroll=8)
    xpad[5:8, :] = x[lb - 3:lb, :]


def _lru_seq(proj3, cw, pvec, wbd, l, lb):
    b, seq_len, _ = proj3.shape

    def layer(shape):
        return pl.BlockSpec((None,) + shape, lambda *_: (l,) + (0,) * len(shape))

    return pl.pallas_call(
        functools.partial(_lru_seq_kernel, lb=lb),
        grid=(b, seq_len // lb),
        in_specs=[
            pl.BlockSpec((None, lb, W_BRANCH), lambda i, t: (i, t, C_LRU // W_BRANCH)),
            layer((CONV_W, W_BRANCH)), layer((8, W_BRANCH)), layer((W_BRANCH, 2 * W_BRANCH)),
        ],
        out_specs=pl.BlockSpec((None, lb, W_BRANCH), lambda i, t: (i, t, 0)),
        out_shape=jax.ShapeDtypeStruct((b, seq_len, W_BRANCH), f32),
        scratch_shapes=[
            pltpu.VMEM((8 + lb, W_BRANCH), f32),
            pltpu.VMEM((1, W_BRANCH), f32),
            pltpu.VMEM((lb, W_BRANCH), f32),
            pltpu.VMEM((lb, W_BRANCH), f32),
        ],
        compiler_params=_cparams(("parallel", "arbitrary")),
        name="lru_seq",
    )(proj3, cw, pvec, wbd)


def _lru_tok_kernel(x_ref, b0_ref, b1_ref, b2_ref, h0_ref, cw_ref, pvec_ref, wbd_ref, o_ref):
    cw = cw_ref[...]
    pvec = pvec_ref[...]
    y = (pvec[0:1, :] + b0_ref[...] * cw[0:1, :] + b1_ref[...] * cw[1:2, :] + b2_ref[...] * cw[2:3, :]
         + x_ref[...] * cw[3:4, :])
    a, b = _lru_gates(y, wbd_ref[...], pvec)
    o_ref[...] = a * h0_ref[...] + b


def _lru_tok(proj2, bufs, h0, cw, pvec, wbd, l):
    b = proj2.shape[0]

    def layer(shape):
        return pl.BlockSpec((None,) + shape, lambda *_: (l,) + (0,) * len(shape))

    row = pl.BlockSpec((b, W_BRANCH), lambda i: (0, 0))
    return pl.pallas_call(
        _lru_tok_kernel,
        grid=(1,),
        in_specs=[pl.BlockSpec((b, W_BRANCH), lambda i: (0, C_LRU // W_BRANCH)), row, row, row, row,
                  layer((CONV_W, W_BRANCH)), layer((8, W_BRANCH)), layer((W_BRANCH, 2 * W_BRANCH))],
        out_specs=row,
        out_shape=jax.ShapeDtypeStruct((b, W_BRANCH), f32),
        compiler_params=_cparams(("arbitrary",)),
        name="lru_tok",
    )(proj2, *bufs, h0, cw, pvec, wbd)


def _prep_weights(p):
    w = p["w_in"]
    pad = jnp.zeros((DEPTH, D_MODEL, LANE - GLA_GATE_RANK), w.dtype)
    w_in_p = jnp.concatenate([
        w[:, :, 0:1536], w[:, :, 2176:2688], w[:, :, 2704:4240], w[:, :, 4240:4752], w[:, :, 4752:6800],
        w[:, :, 1664:2176], w[:, :, 1536:1664], w[:, :, 2688:2704], pad], axis=2).astype(bf16)
    zl = jnp.zeros((DEPTH, RWKV_LORA, W_BRANCH), f32)
    wup = jnp.concatenate([jnp.concatenate([p["rwkv_w_up"], zl], axis=2),
                           jnp.concatenate([zl, p["rwkv_a_up"]], axis=2)], axis=1)
    zrow = jnp.zeros((DEPTH, W_BRANCH), f32)
    rwkv_vec = jnp.stack([p["rwkv_w0"], p["rwkv_a0"], p["rwkv_k_k"], p["rwkv_k_a"],
                          p["rwkv_r_k"].reshape(DEPTH, W_BRANCH), p["rwkv_gn_g"], p["rwkv_gn_b"], zrow], axis=1)
    mu = p["rwkv_mu"]
    gup = jnp.concatenate([p["gla_gk_up"], jnp.zeros((DEPTH, LANE - GLA_GATE_RANK, GLA_KW), f32)], axis=1)
    eye = jnp.eye(LRU_BLOCKS, dtype=f32)

    def bd(wb):
        return jnp.einsum("lhij,hg->lhigj", wb, eye).reshape(DEPTH, W_BRANCH, W_BRANCH)

    lru_vec = jnp.stack([p["lru_conv_b"], p["lru_b_a"], p["lru_b_x"], p["lru_lambda"],
                         zrow, zrow, zrow, zrow], axis=1)
    return dict(
        w_in_p=w_in_p,
        w_out_b=p["w_out"].astype(bf16),
        norm_g3=p["norm_g"].reshape(DEPTH, 1, D_MODEL),
        final_g2=p["final_g"].reshape(1, D_MODEL),
        rwkv=(mu[:, None, 0:1536], mu[:, None, 1536:1664], wup, rwkv_vec) + _rwkv_consts(),
        gla=(gup, p["gla_gk_b"].reshape(DEPTH, 1, GLA_KW)),
        gla_gn=p["gla_gn_g"].reshape(DEPTH, 1, W_BRANCH),
        hgrn=(p["hgrn_lb_logits"],),
        hgrn_gn=p["hgrn_gn_g"].reshape(DEPTH, 1, W_BRANCH),
        lru=(p["lru_conv_w"], lru_vec, jnp.concatenate([bd(p["lru_w_a"]), bd(p["lru_w_x"])], axis=2)),
    )


def _shift_state(proj_last):
    return jnp.concatenate([proj_last[..., 0:1536], proj_last[..., C_RWKV_LO:C_RWKV_LO + LANE]], axis=-1)


def _rwkv_state_in(s):
    b = s.shape[0]
    return jnp.transpose(s, (0, 2, 1, 3)).reshape(b, RWKV_HEAD, W_BRANCH)


def _rwkv_state_out(s):
    b = s.shape[0]
    return jnp.transpose(s.reshape(b, RWKV_HEAD, RWKV_H, RWKV_HEAD), (0, 2, 1, 3))


def _trunk_seq(x, mod, wts, lb):
    b, seq_len, _ = x.shape
    t = b * seq_len
    tm = min(512, seq_len)
    per_seq = seq_len // tm
    mod3 = mod.reshape(DEPTH * b * 3, 1, D_MODEL)
    x2 = x.reshape(t, D_MODEL)
    zeros = functools.partial(jnp.zeros, dtype=f32)
    new = []
    for l in range(DEPTH):
        mod_l = mod3[l * b * 3:(l + 1) * b * 3]
        proj2 = _inproj(x2, wts["norm_g3"], mod_l, wts["w_in_p"], l, per_seq, tm)
        proj3 = proj2.reshape(b, seq_len, PROJ_P)
        o_a, s_wkv = _rwkv_seq(proj3, zeros((b, 1, 1536)), zeros((b, 1, LANE)),
                               zeros((b, RWKV_HEAD, W_BRANCH)), wts["rwkv"], l, min(lb, 128))
        o_b, st_gla = _chunk_seq("gla", proj3, wts["gla"], wts["gla_gn"], l, lb)
        o_c, st_hgrn = _chunk_seq("hgrn", proj3, wts["hgrn"], wts["hgrn_gn"], l, lb)
        o_d = _lru_seq(proj3, *wts["lru"], l, lb)
        outs = [o.reshape(t, W_BRANCH) for o in (o_a, o_b, o_c, o_d)]
        tmo = min(256, seq_len)
        x2 = _outproj(outs, proj2, x2, mod_l, wts["w_out_b"], wts["final_g2"], l, seq_len // tmo, tmo,
                      final=(l == DEPTH - 1))
        last = proj3[:, seq_len - 1]
        new.append((
            _shift_state(last),
            _rwkv_state_out(s_wkv),
            _state_from_transposed(st_gla, GLA_H, GLA_DK),
            _state_from_transposed(st_hgrn, HGRN_H, HGRN_DK),
            proj3[:, seq_len - (CONV_W - 1):, C_LRU:C_LRU + W_BRANCH],
            o_d[:, seq_len - 1],
        ))
    return x2.reshape(b, seq_len, D_MODEL), tuple(jnp.stack([n[i] for n in new], axis=0) for i in range(6))


def _trunk_tok(x, mod, states, wts):
    b = x.shape[0]
    s_shift, s_wkv, s_gla, s_hgrn, s_conv, s_h = states
    x2 = x.reshape(b, D_MODEL)
    g = 8
    new = []
    for l in range(DEPTH):
        proj2 = _inproj(x2, wts["norm_g3"], mod, wts["w_in_p"], l, None, b)
        o_a, n_wkv = _rwkv_tok(proj2, s_shift[l][:, 0:1536], s_shift[l][:, 1536:1664],
                               _rwkv_state_in(s_wkv[l]), wts["rwkv"], l, g)
        o_b, n_gla = _chunk_tok("gla", proj2, wts["gla"], wts["gla_gn"], s_gla[l], l, g)
        o_c, n_hgrn = _chunk_tok("hgrn", proj2, wts["hgrn"], wts["hgrn_gn"], s_hgrn[l], l, g)
        conv = s_conv[l]
        o_d = _lru_tok(proj2, [conv[:, 0], conv[:, 1], conv[:, 2]], s_h[l], *wts["lru"], l)
        x2 = _outproj([o_a, o_b, o_c, o_d], proj2, x2, mod, wts["w_out_b"], wts["final_g2"], l, None, b,
                      final=(l == DEPTH - 1))
        n_conv = jnp.concatenate([conv[:, 1:], proj2[:, None, C_LRU:C_LRU + W_BRANCH]], axis=1)
        new.append((_shift_state(proj2), _rwkv_state_out(n_wkv), n_gla, n_hgrn, n_conv, o_d))
    return x2.reshape(b, 1, D_MODEL), tuple(jnp.stack([n[i] for n in new], axis=0) for i in range(6))


def kernel(x_prompt, x_sample, c_prompt, c_sample, state_rwkv_shift, state_rwkv_wkv, state_gla, state_hgrn, state_lru_conv, state_lru_h, norm_g, w_ada, b_ada, w_in, w_out, rwkv_mu, rwkv_w0, rwkv_w_up, rwkv_a0, rwkv_a_up, rwkv_k_k, rwkv_k_a, rwkv_r_k, rwkv_gn_g, rwkv_gn_b, gla_gk_up, gla_gk_b, gla_gn_g, hgrn_lb_logits, hgrn_gn_g, lru_conv_w, lru_conv_b, lru_w_a, lru_b_a, lru_w_x, lru_b_x, lru_lambda, final_g):
    p = dict(norm_g=norm_g, w_in=w_in, w_out=w_out, rwkv_mu=rwkv_mu, rwkv_w0=rwkv_w0, rwkv_w_up=rwkv_w_up,
             rwkv_a0=rwkv_a0, rwkv_a_up=rwkv_a_up, rwkv_k_k=rwkv_k_k, rwkv_k_a=rwkv_k_a, rwkv_r_k=rwkv_r_k,
             rwkv_gn_g=rwkv_gn_g, rwkv_gn_b=rwkv_gn_b, gla_gk_up=gla_gk_up, gla_gk_b=gla_gk_b,
             gla_gn_g=gla_gn_g, hgrn_lb_logits=hgrn_lb_logits, hgrn_gn_g=hgrn_gn_g, lru_conv_w=lru_conv_w,
             lru_conv_b=lru_conv_b, lru_w_a=lru_w_a, lru_b_a=lru_b_a, lru_w_x=lru_w_x, lru_b_x=lru_b_x,
             lru_lambda=lru_lambda, final_g=final_g)
    wts = _prep_weights(p)
    bp = x_prompt.shape[0]
    bs = x_sample.shape[0]
    pad_rows = (-bp) % 8
    c_all = jnp.concatenate([c_prompt, jnp.zeros((pad_rows, D_MODEL), f32), c_sample], axis=0)
    mod = _ada(c_all, w_ada, b_ada)
    mod_p = mod[:, 0:bp]
    mod_s = mod[:, bp + pad_rows:bp + pad_rows + bs]
    y_p, st_p = _trunk_seq(x_prompt, mod_p, wts, 256)
    states = (state_rwkv_shift, state_rwkv_wkv, state_gla, state_hgrn, state_lru_conv, state_lru_h)
    y_s, st_s = _trunk_tok(x_sample, mod_s, states, wts)
    return (y_p, y_s) + st_p + st_s
```

```python
import functools

import jax
import jax.numpy as jnp
from jax import lax
from jax.experimental import pallas as pl
from jax.experimental.pallas import tpu as pltpu

f32 = jnp.float32
bf16 = jnp.bfloat16

D_MODEL = 2048
DEPTH = 4
W_BRANCH = 512
EPS = 1e-6

RWKV_H = 8
RWKV_HEAD = 64
RWKV_LORA = 64
RWKV_PROJ = 3 * W_BRANCH + 2 * RWKV_LORA
RWKV_GN_EPS = 64e-5

GLA_H = 4
GLA_DK = 64
GLA_DV = 128
GLA_KW = GLA_H * GLA_DK
GLA_GATE_RANK = 16
GLA_GATE_NORM = 16.0

HGRN_H = 4
HGRN_DK = 128
HGRN_DV = 128
LB_FLOOR = 1e-30

LRU_BLOCKS = 8
LRU_BLOCK = 64
LRU_C = 8.0
CONV_W = 4

CHUNK = 32

C_R, C_K, C_V = 0, 512, 1024
C_GLA_V = 1536
C_HQ, C_HF, C_HI = 2048, 2560, 3072
C_LRU = 3584
C_GATE = 4096
C_GLA_Q, C_GLA_K = 6144, 6400
C_RWKV_LO = 6656
C_GLA_LO = 6784
PROJ_P = 6912

LANE = 128
MXU_TILE = 256
VMEM_LIMIT = 56 * 1024 * 1024

HIGHEST = lax.Precision.HIGHEST


def _cparams(sem):
    return pltpu.CompilerParams(dimension_semantics=sem, vmem_limit_bytes=VMEM_LIMIT)


def _dotf(a, b):
    return jnp.dot(a, b, preferred_element_type=f32, precision=HIGHEST)


def _prec(a):
    return HIGHEST if a.dtype == f32 else None


def _dot_nt(a, b):
    return lax.dot_general(a, b, (((1,), (1,)), ((), ())), preferred_element_type=f32, precision=_prec(a))


def _dot_tn(a, b):
    return lax.dot_general(a, b, (((0,), (0,)), ((), ())), preferred_element_type=f32, precision=_prec(a))


def _split(x, pieces):
    out = []
    for i in range(pieces):
        part = x.astype(bf16)
        out.append(part)
        if i + 1 < pieces:
            x = x - part.astype(f32)
    return out


def _half_dot(x, ones):
    d = functools.partial(jnp.dot, preferred_element_type=f32)
    kh = ones.shape[0]
    if x.shape[1] == kh:
        return d(x, ones)
    return jnp.concatenate([d(x[:, 0:kh], ones), d(x[:, kh:2 * kh], ones)], axis=1)


def _seg_sum(x, ones, pieces=2):
    return sum(_half_dot(part, ones) for part in _split(x, pieces))


def _ones_seg(ones, x, pieces=3):
    d = functools.partial(jnp.dot, preferred_element_type=f32)
    return sum(d(ones, part) for part in _split(x, pieces))


def _sigmoid(x):
    return jax.nn.sigmoid(x)


def _softplus(x):
    return jnp.maximum(x, 0.0) + jnp.log1p(jnp.exp(-jnp.abs(x)))


def _log_sigmoid(x):
    return -_softplus(-x)


def _ada_kernel(c_ref, w_ref, b_ref, o_ref):
    c = c_ref[...]
    s = (c * _sigmoid(c)).astype(bf16)
    o_ref[...] = jnp.dot(s, w_ref[...].astype(bf16), preferred_element_type=f32) + b_ref[...]


def _ada(c_all, w_ada, b_ada):
    rows = c_all.shape[0]
    tn = 512
    n = w_ada.shape[2]
    return pl.pallas_call(
        _ada_kernel,
        grid=(DEPTH, n // tn),
        in_specs=[
            pl.BlockSpec((rows, D_MODEL), lambda l, j: (0, 0)),
            pl.BlockSpec((None, D_MODEL, tn), lambda l, j: (l, 0, j)),
            pl.BlockSpec((None, 1, tn), lambda l, j: (l, 0, j)),
        ],
        out_specs=pl.BlockSpec((None, rows, tn), lambda l, j: (l, 0, j)),
        out_shape=jax.ShapeDtypeStruct((DEPTH, rows, n), f32),
        compiler_params=_cparams(("parallel", "parallel")),
        name="ada_mod",
    )(c_all, w_ada, b_ada.reshape(DEPTH, 1, n))


def _inproj_kernel(x_ref, g_ref, sc_ref, sh_ref, w_ref, o_ref, h_ref):
    @pl.when(pl.program_id(1) == 0)
    def _():
        x = x_ref[...]
        ms = jnp.mean(x * x, axis=-1, keepdims=True)
        h = x * lax.rsqrt(ms + EPS) * g_ref[...]
        h = h * (1.0 + sc_ref[...]) + sh_ref[...]
        h_ref[...] = h.astype(bf16)

    o_ref[...] = jnp.dot(h_ref[...], w_ref[...], preferred_element_type=f32)


def _mod_spec(l, which, per_seq_tiles, tm):
    if per_seq_tiles is None:
        return pl.BlockSpec((None, tm, D_MODEL), lambda i, *_: (l, 0, which))
    return pl.BlockSpec((None, 1, D_MODEL), lambda i, *_: (i // per_seq_tiles * 3 + which, 0, 0))


def _inproj(x2, norm_g3, mod, w_in_p, l, per_seq_tiles, tm):
    t = x2.shape[0]
    tn = 768
    return pl.pallas_call(
        _inproj_kernel,
        grid=(t // tm, PROJ_P // tn),
        in_specs=[
            pl.BlockSpec((tm, D_MODEL), lambda i, j: (i, 0)),
            pl.BlockSpec((None, 1, D_MODEL), lambda i, j: (l, 0, 0)),
            _mod_spec(l, 1, per_seq_tiles, tm),
            _mod_spec(l, 0, per_seq_tiles, tm),
            pl.BlockSpec((None, D_MODEL, tn), lambda i, j: (l, 0, j)),
        ],
        out_specs=pl.BlockSpec((tm, tn), lambda i, j: (i, j)),
        out_shape=jax.ShapeDtypeStruct((t, PROJ_P), f32),
        scratch_shapes=[pltpu.VMEM((tm, D_MODEL), bf16)],
        compiler_params=_cparams(("parallel", "arbitrary")),
        name="in_proj",
    )(x2, norm_g3, mod, mod, w_in_p)


def _outproj_kernel(oa_ref, ob_ref, oc_ref, od_ref, pz_ref, x_ref, gate_ref, w_ref, fg_ref, o_ref, *, final):
    z = pz_ref[...]
    o = jnp.concatenate([oa_ref[...], ob_ref[...], oc_ref[...], od_ref[...]], axis=1)
    o = o * (z * _sigmoid(z))
    y = jnp.dot(o.astype(bf16), w_ref[...], preferred_element_type=f32)
    xn = x_ref[...] + gate_ref[...] * y
    if final:
        ms = jnp.mean(xn * xn, axis=-1, keepdims=True)
        xn = xn * lax.rsqrt(ms + EPS) * fg_ref[...]
    o_ref[...] = xn


def _outproj(outs, proj2, x2, mod, w_out_b, final_g2, l, per_seq_tiles, tm, final):
    t = x2.shape[0]
    mix_spec = pl.BlockSpec((tm, W_BRANCH), lambda i: (i, 0))
    return pl.pallas_call(
        functools.partial(_outproj_kernel, final=final),
        grid=(t // tm,),
        in_specs=[
            mix_spec, mix_spec, mix_spec, mix_spec,
            pl.BlockSpec((tm, D_MODEL), lambda i: (i, C_GATE // D_MODEL)),
            pl.BlockSpec((tm, D_MODEL), lambda i: (i, 0)),
            _mod_spec(l, 2, per_seq_tiles, tm),
            pl.BlockSpec((None, D_MODEL, D_MODEL), lambda i: (l, 0, 0)),
            pl.BlockSpec((1, D_MODEL), lambda i: (0, 0)),
        ],
        out_specs=pl.BlockSpec((tm, D_MODEL), lambda i: (i, 0)),
        out_shape=jax.ShapeDtypeStruct((t, D_MODEL), f32),
        compiler_params=_cparams(("parallel",)),
        name="out_proj",
    )(*outs, proj2, x2, mod, w_out_b, final_g2)


def _rwkv_prologue(r, k, v, lo, pr, pk, pv, plo, mu_rkv, mu_lo, wup, pvec, bo):
    w0, a0, k_k, k_a, r_k = (pvec[i:i + 1, :] for i in range(5))
    xr = r + (pr - r) * mu_rkv[:, 0:512]
    xk = k + (pk - k) * mu_rkv[:, 512:1024]
    xv = v + (pv - v) * mu_rkv[:, 1024:1536]
    xlo = lo + (plo - lo) * mu_lo
    lane = lax.broadcasted_iota(jnp.int32, xlo.shape, 1)
    act = jnp.where(lane < RWKV_LORA, jnp.tanh(xlo), xlo)
    up = _dotf(act, wup)
    w_raw = -_softplus(-(w0 + up[:, 0:512])) - 0.5
    ew = jnp.exp(-jnp.exp(w_raw))
    a = _sigmoid(a0 + up[:, 512:1024])
    kk = xk * k_k
    kk = kk / jnp.maximum(jnp.sqrt(_seg_sum(kk * kk, bo)), 1e-12)
    kh = xk * (1.0 + (a - 1.0) * k_a)
    alp = kk * a
    return dict(
        kap=kk, ew=ew, alp=alp, kh=kh, vv=xv,
        wr=ew * xr,
        ar=_seg_sum(alp * xr, bo),
        kr=_seg_sum(kh * xr, bo),
        bonus=_seg_sum(xr * kh * r_k, bo) * xv,
    )


_RWKV_STEP_KEYS = ("kap", "ew", "alp", "kh", "vv", "wr", "ar", "kr")


def _rwkv_step(s, kap, ew, alp, kh, vv, wr, ar, kr, bo, idt):
    idt_b = idt.astype(bf16)
    sk_hi, sk_mid = _split(s * kap, 2)
    v_hi, v_mid = _split(vv, 2)
    x = jnp.concatenate([sk_hi, sk_mid, (s * wr).astype(bf16), idt_b * v_hi, idt_b * v_mid], axis=0)
    red = _half_dot(x, bo)
    sk = red[0:64] + red[64:128]
    swr = red[128:192]
    vcol = red[192:256] + red[256:320]
    s_new = s * ew - sk * alp + vcol * kh
    o_row = jnp.sum((swr - sk * ar) * idt, axis=0, keepdims=True) + vv * kr
    return s_new, o_row


def _rwkv_epilogue(o, bonus, pvec, bo):
    gn_g, gn_b = pvec[5:6, :], pvec[6:7, :]
    mu = _seg_sum(o, bo) * (1.0 / RWKV_HEAD)
    d = o - mu
    var = _seg_sum(d * d, bo) * (1.0 / RWKV_HEAD)
    return d * lax.rsqrt(var + RWKV_GN_EPS) * gn_g + gn_b + bonus


def _rwkv_seq_kernel(r_ref, k_ref, v_ref, lo_ref, prkv_ref, plo_ref, sin_ref, mu_rkv_ref, mu_lo_ref, wup_ref,
                     pvec_ref, bo_ref, idt_ref, o_ref, sout_ref,
                     s_sc, crkv_sc, clo_sc, kap_sc, ew_sc, alp_sc, kh_sc, vv_sc, wr_sc, ar_sc, kr_sc, oraw_sc,
                     bonus_sc, *, g_seqs, lb):
    tb = pl.program_id(1)

    @pl.when(tb == 0)
    def _():
        s_sc[...] = sin_ref[...]
        crkv_sc[...] = prkv_ref[...]
        clo_sc[...] = plo_ref[...]

    bo = bo_ref[...]
    idt = idt_ref[...]
    pvec = pvec_ref[...]
    step_sc = dict(kap=kap_sc, ew=ew_sc, alp=alp_sc, kh=kh_sc, vv=vv_sc, wr=wr_sc, ar=ar_sc, kr=kr_sc)

    row0 = lax.broadcasted_iota(jnp.int32, (lb, 1), 0) == 0
    for g in range(g_seqs):
        cur = [r_ref[g], k_ref[g], v_ref[g]]
        lo = lo_ref[g]
        carry = crkv_sc[g]
        prev = [jnp.where(row0, carry[:, i * 512:(i + 1) * 512], pltpu.roll(c, 1, 0)) for i, c in enumerate(cur)]
        plo = jnp.where(row0, clo_sc[g], pltpu.roll(lo, 1, 0))
        res = _rwkv_prologue(cur[0], cur[1], cur[2], lo, prev[0], prev[1], prev[2], plo,
                             mu_rkv_ref[...], mu_lo_ref[...], wup_ref[...], pvec, bo)
        for key in _RWKV_STEP_KEYS:
            step_sc[key][g] = res[key]
        bonus_sc[g] = res["bonus"]
        for i, c in enumerate(cur):
            crkv_sc[g, :, i * 512:(i + 1) * 512] = c[lb - 1:lb, :]
        clo_sc[g] = lo[lb - 1:lb, :]

    def body(t, carry):
        for g in range(g_seqs):
            rows = [step_sc[key][g, pl.ds(t, 1), :] for key in _RWKV_STEP_KEYS]
            s_new, o_row = _rwkv_step(s_sc[g], *rows, bo, idt)
            s_sc[g] = s_new
            oraw_sc[g, pl.ds(t, 1), :] = o_row
        return carry

    lax.fori_loop(0, lb, body, 0)

    for g in range(g_seqs):
        o_ref[g] = _rwkv_epilogue(oraw_sc[g], bonus_sc[g], pvec, bo)

    @pl.when(tb == pl.num_programs(1) - 1)
    def _():
        sout_ref[...] = s_sc[...]


def _rwkv_tok_kernel(r_ref, k_ref, v_ref, lo_ref, prkv_ref, plo_ref, sin_ref, mu_rkv_ref, mu_lo_ref, wup_ref,
                     pvec_ref, bo_ref, idt_ref, o_ref, sout_ref, *, g_rows):
    bo = bo_ref[...]
    idt = idt_ref[...]
    pvec = pvec_ref[...]
    prkv = prkv_ref[...]
    res = _rwkv_prologue(r_ref[...], k_ref[...], v_ref[...], lo_ref[...],
                         prkv[:, 0:512], prkv[:, 512:1024], prkv[:, 1024:1536], plo_ref[...],
                         mu_rkv_ref[...], mu_lo_ref[...], wup_ref[...], pvec, bo)
    o_rows = []
    for g in range(g_rows):
        rows = [res[key][g:g + 1, :] for key in _RWKV_STEP_KEYS]
        s_new, o_row = _rwkv_step(sin_ref[g], *rows, bo, idt)
        sout_ref[g] = s_new
        o_rows.append(o_row)
    o_ref[...] = _rwkv_epilogue(jnp.concatenate(o_rows, axis=0), res["bonus"], pvec, bo)


def _rwkv_consts():
    i = jnp.arange(W_BRANCH)
    j = jnp.arange(MXU_TILE)
    bo = (j[:, None] // RWKV_HEAD == j[None, :] // RWKV_HEAD).astype(bf16)
    idt = (jnp.arange(RWKV_HEAD)[:, None] == (i[None, :] % RWKV_HEAD)).astype(f32)
    return bo, idt


def _rwkv_weight_specs(l):
    def cs(shape):
        return pl.BlockSpec((None,) + shape, lambda *_: (l,) + (0,) * len(shape))

    def const(shape):
        return pl.BlockSpec(shape, lambda *_: (0,) * len(shape))

    return [cs((1, 1536)), cs((1, LANE)), cs((LANE, 1024)), cs((8, W_BRANCH)),
            const((MXU_TILE, MXU_TILE)), const((RWKV_HEAD, W_BRANCH))]


def _rwkv_seq(proj3, prev_rkv, prev_lo, s_in, wts, l, lb):
    b, seq_len, _ = proj3.shape
    g = b
    nt = seq_len // lb

    def col(width, off):
        return pl.BlockSpec((g, lb, width), lambda i, t: (i, t, off // width))

    tok_sc = pltpu.VMEM((g, lb, W_BRANCH), f32)
    out = pl.pallas_call(
        functools.partial(_rwkv_seq_kernel, g_seqs=g, lb=lb),
        grid=(b // g, nt),
        in_specs=[
            col(512, C_R), col(512, C_K), col(512, C_V), col(LANE, C_RWKV_LO),
            pl.BlockSpec((g, 1, 1536), lambda i, t: (i, 0, 0)),
            pl.BlockSpec((g, 1, LANE), lambda i, t: (i, 0, 0)),
            pl.BlockSpec((g, RWKV_HEAD, W_BRANCH), lambda i, t: (i, 0, 0)),
        ] + _rwkv_weight_specs(l),
        out_specs=[
            pl.BlockSpec((g, lb, W_BRANCH), lambda i, t: (i, t, 0)),
            pl.BlockSpec((g, RWKV_HEAD, W_BRANCH), lambda i, t: (i, 0, 0)),
        ],
        out_shape=[
            jax.ShapeDtypeStruct((b, seq_len, W_BRANCH), f32),
            jax.ShapeDtypeStruct((b, RWKV_HEAD, W_BRANCH), f32),
        ],
        scratch_shapes=[
            pltpu.VMEM((g, RWKV_HEAD, W_BRANCH), f32),
            pltpu.VMEM((g, 1, 1536), f32),
            pltpu.VMEM((g, 1, LANE), f32),
        ] + [tok_sc] * 10,
        compiler_params=_cparams(("parallel", "arbitrary")),
        name="rwkv_seq",
    )(proj3, proj3, proj3, proj3, prev_rkv, prev_lo, s_in, *wts)
    return out


def _rwkv_tok(proj2, prev_rkv, prev_lo, s_in, wts, l, g):
    b = proj2.shape[0]

    def col(width, off):
        return pl.BlockSpec((g, width), lambda i: (i, off // width))

    return pl.pallas_call(
        functools.partial(_rwkv_tok_kernel, g_rows=g),
        grid=(b // g,),
        in_specs=[
            col(512, C_R), col(512, C_K), col(512, C_V), col(LANE, C_RWKV_LO),
            pl.BlockSpec((g, 1536), lambda i: (i, 0)),
            pl.BlockSpec((g, LANE), lambda i: (i, 0)),
            pl.BlockSpec((g, RWKV_HEAD, W_BRANCH), lambda i: (i, 0, 0)),
        ] + _rwkv_weight_specs(l),
        out_specs=[
            pl.BlockSpec((g, W_BRANCH), lambda i: (i, 0)),
            pl.BlockSpec((g, RWKV_HEAD, W_BRANCH), lambda i: (i, 0, 0)),
        ],
        out_shape=[
            jax.ShapeDtypeStruct((b, W_BRANCH), f32),
            jax.ShapeDtypeStruct((b, RWKV_HEAD, W_BRANCH), f32),
        ],
        compiler_params=_cparams(("parallel",)),
        name="rwkv_tok",
    )(proj2, proj2, proj2, proj2, prev_rkv, prev_lo, s_in, *wts)


def _gla_inputs(q_ref, k_ref, v_ref, lo_ref, gup_ref, gb_ref):
    q = q_ref[...] * (GLA_DK ** -0.5)
    z = _dotf(lo_ref[...], gup_ref[...]) + gb_ref[...]
    g = _log_sigmoid(z) * (1.0 / GLA_GATE_NORM)
    return q, k_ref[...], v_ref[...], g


def _hgrn_lb(logits, l):
    m = jnp.max(logits, axis=0, keepdims=True)
    e = jnp.exp(logits - m)
    sm = e / jnp.sum(e, axis=0, keepdims=True)
    lb = jnp.zeros_like(sm[0:1, :])
    for i in range(1, l + 1):
        lb = lb + sm[i:i + 1, :]
    return lb


def _hgrn_inputs(q_ref, f_ref, i_ref, logits_ref, l):
    lb = _hgrn_lb(logits_ref[...], l)
    f_lo = f_ref[...]
    a = jnp.log(jnp.maximum(lb, LB_FLOOR))
    b = jnp.log1p(-lb) + _log_sigmoid(f_lo)
    logf = jnp.maximum(a, b) + jnp.log1p(jnp.exp(-jnp.abs(a - b)))
    k = (1.0 - lb) * _sigmoid(-f_lo)
    return q_ref[...], k, i_ref[...], logf


def _chunk_core(q, k, v, g, lt_ref, bokv_ref, bdm_ref, bon_ref, gn_ref, st_sc, kpad, bpad, vpad, lb):
    bc = _ones_seg(lt_ref[...], g)
    kpad[CHUNK:CHUNK + lb, :] = k
    bpad[CHUNK:CHUNK + lb, :] = bc
    vpad[CHUNK:CHUNK + lb, :] = v
    rowc = lax.broadcasted_iota(jnp.int32, (lb, 1), 0) & (CHUNK - 1)
    bokv = bokv_ref[...]

    o = jnp.zeros((lb, W_BRANCH), f32)
    for d in range(CHUNK):
        start = CHUNK - d
        ks = kpad[start:start + lb, :]
        bs = bpad[start:start + lb, :]
        vs = vpad[start:start + lb, :]
        m = rowc >= d
        diff = jnp.where(m, bc - bs, 0.0)
        z = jnp.where(m, q * ks * jnp.exp(diff), 0.0)
        o = o + _seg_sum(z, bokv) * vs

    bdm = bdm_ref[...]
    st = st_sc[...]
    outs = []
    for c in range(lb // CHUNK):
        sl = slice(c * CHUNK, (c + 1) * CHUNK)
        bcc = bc[sl]
        blast = bcc[CHUNK - 1:CHUNK, :]
        qe = q[sl] * jnp.exp(bcc)
        outs.append(o[sl] + _dot_nt(qe.astype(bf16), st.astype(bf16)))
        ke = k[sl] * jnp.exp(blast - bcc)
        st = st * jnp.exp(blast) + _dot_tn(v[sl].astype(bf16), ke.astype(bf16)) * bdm
    st_sc[...] = st
    o = jnp.concatenate(outs, axis=0)
    ms = _seg_sum(o * o, bon_ref[...]) * (1.0 / LANE)
    return o * lax.rsqrt(ms + EPS) * gn_ref[...]


def _chunk_seq_kernel(*refs, kind, l, lb):
    n_in = 6 if kind == "gla" else 4
    ins = refs[:n_in]
    lt_ref, bokv_ref, bdm_ref, bon_ref, gn_ref, o_ref, stout_ref, st_sc, kpad, bpad, vpad = refs[n_in:]
    tb = pl.program_id(1)

    @pl.when(tb == 0)
    def _():
        st_sc[...] = jnp.zeros_like(st_sc)
        kpad[0:CHUNK, :] = jnp.zeros((CHUNK, kpad.shape[1]), f32)
        bpad[0:CHUNK, :] = jnp.zeros((CHUNK, bpad.shape[1]), f32)
        vpad[0:CHUNK, :] = jnp.zeros((CHUNK, vpad.shape[1]), f32)

    if kind == "gla":
        q, k, v, g = _gla_inputs(*ins)
    else:
        q, k, v, g = _hgrn_inputs(*ins, l)
    o_ref[...] = _chunk_core(q, k, v, g, lt_ref, bokv_ref, bdm_ref, bon_ref, gn_ref, st_sc, kpad, bpad, vpad, lb)

    @pl.when(tb == pl.num_programs(1) - 1)
    def _():
        stout_ref[...] = st_sc[...]


def _chunk_consts(h, dk, lb):
    f = h * dk
    i = jnp.arange(lb)
    lt = ((i[:, None] // CHUNK == i[None, :] // CHUNK) & (i[:, None] >= i[None, :])).astype(bf16)
    fi = jnp.arange(f)
    oi = jnp.arange(W_BRANCH)
    bokv = (fi[:, None] // dk == oi[None, :] // LANE).astype(bf16)
    bdm = (oi[:, None] // LANE == fi[None, :] // dk).astype(f32)
    ti = jnp.arange(MXU_TILE)
    bon = (ti[:, None] // LANE == ti[None, :] // LANE).astype(bf16)
    if f == W_BRANCH:
        bokv = bon
    return lt, bokv, bdm, bon


def _chunk_seq(kind, proj3, extra, gn3, l, lb):
    b, seq_len, _ = proj3.shape
    h, dk = (GLA_H, GLA_DK) if kind == "gla" else (HGRN_H, HGRN_DK)
    f = h * dk
    nt = seq_len // lb

    def col(width, off):
        return pl.BlockSpec((None, lb, width), lambda i, t: (i, t, off // width))

    def const(shape):
        return pl.BlockSpec(shape, lambda *_: (0,) * len(shape))

    def layer(shape):
        return pl.BlockSpec((None,) + shape, lambda *_: (l,) + (0,) * len(shape))

    if kind == "gla":
        gup, gb = extra
        in_specs = [col(GLA_KW, C_GLA_Q), col(GLA_KW, C_GLA_K), col(W_BRANCH, C_GLA_V), col(LANE, C_GLA_LO),
                    layer((LANE, GLA_KW)), layer((1, GLA_KW))]
        args = [proj3, proj3, proj3, proj3, gup, gb]
    else:
        (logits,) = extra
        in_specs = [col(W_BRANCH, C_HQ), col(W_BRANCH, C_HF), col(W_BRANCH, C_HI), const((DEPTH, W_BRANCH))]
        args = [proj3, proj3, proj3, logits]
    consts = _chunk_consts(h, dk, lb)
    in_specs += [const(c.shape) for c in consts] + [layer((1, W_BRANCH))]
    return pl.pallas_call(
        functools.partial(_chunk_seq_kernel, kind=kind, l=l, lb=lb),
        grid=(b, nt),
        in_specs=in_specs,
        out_specs=[
            pl.BlockSpec((None, lb, W_BRANCH), lambda i, t: (i, t, 0)),
            pl.BlockSpec((None, W_BRANCH, f), lambda i, t: (i, 0, 0)),
        ],
        out_shape=[
            jax.ShapeDtypeStruct((b, seq_len, W_BRANCH), f32),
            jax.ShapeDtypeStruct((b, W_BRANCH, f), f32),
        ],
        scratch_shapes=[
            pltpu.VMEM((W_BRANCH, f), f32),
            pltpu.VMEM((CHUNK + lb, f), f32),
            pltpu.VMEM((CHUNK + lb, f), f32),
            pltpu.VMEM((CHUNK + lb, W_BRANCH), f32),
        ],
        compiler_params=_cparams(("parallel", "arbitrary")),
        name=kind + "_seq",
    )(*args, *consts, gn3)


def _state_from_transposed(st, h, dk):
    blocks = [st[:, i * LANE:(i + 1) * LANE, i * dk:(i + 1) * dk] for i in range(h)]
    return jnp.swapaxes(jnp.stack(blocks, axis=1), 2, 3)


def _block_diag_rows(x, g, dk):
    row = lax.broadcasted_iota(jnp.int32, x.shape, 0)
    lane = lax.broadcasted_iota(jnp.int32, x.shape, 1)
    per = LANE // dk
    pieces = []
    for j in range(g // per):
        keep = row == (j * per + lane // dk)
        pieces.append(jnp.where(keep, x, 0.0))
    return jnp.concatenate(pieces, axis=1)


def _chunk_tok_kernel(*refs, kind, l, g):
    if kind == "gla":
        q_ref, k_ref, v_ref, lo_ref, gup_ref, gb_ref, s_ref, gn_ref, o_ref, sout_ref = refs
        odd = (pl.program_id(1) % 2) == 1
        lane = lax.broadcasted_iota(jnp.int32, (g, LANE), 1)
        own = (lane >= GLA_DK) == odd

        def pick(x):
            return jnp.where(own, x, pltpu.roll(x, GLA_DK, 1))

        q = pick(q_ref[...]) * (GLA_DK ** -0.5)
        k = pick(k_ref[...])
        z = _dotf(lo_ref[...], gup_ref[...]) + gb_ref[...]
        dec = jnp.exp(pick(_log_sigmoid(z) * (1.0 / GLA_GATE_NORM)))
        v = v_ref[...]
        dk = GLA_DK
    else:
        q_ref, f_ref, i_ref, logits_ref, s_ref, gn_ref, o_ref, sout_ref = refs
        q, k, v, logf = _hgrn_inputs(q_ref, f_ref, i_ref, logits_ref, l)
        dec = jnp.exp(logf)
        dk = HGRN_DK
    s = s_ref[...].reshape(g * dk, LANE)
    ones = jnp.ones((g, LANE), f32)
    dcol = _dot_tn(_block_diag_rows(dec, g, dk), ones)
    s_new = s * dcol + _dot_tn(_block_diag_rows(k, g, dk), v)
    o = _dotf(_block_diag_rows(q, g, dk), s_new)
    ms = jnp.mean(o * o, axis=-1, keepdims=True)
    o_ref[...] = o * lax.rsqrt(ms + EPS) * gn_ref[...]
    sout_ref[...] = s_new.reshape(g, dk, LANE)


def _chunk_tok(kind, proj2, extra, gn3, s_in, l, g):
    b = proj2.shape[0]
    h, dk = (GLA_H, GLA_DK) if kind == "gla" else (HGRN_H, HGRN_DK)

    def head_col(off):
        per = LANE // dk
        return pl.BlockSpec((g, LANE), lambda i, j: (i, off // LANE + j // per))

    def const(shape):
        return pl.BlockSpec(shape, lambda *_: (0,) * len(shape))

    if kind == "gla":
        gup, gb = extra
        in_specs = [head_col(C_GLA_Q), head_col(C_GLA_K),
                    pl.BlockSpec((g, LANE), lambda i, j: (i, C_GLA_V // LANE + j)),
                    pl.BlockSpec((g, LANE), lambda i, j: (i, C_GLA_LO // LANE)),
                    pl.BlockSpec((None, LANE, LANE), lambda i, j: (l, 0, j // 2)),
                    pl.BlockSpec((None, 1, LANE), lambda i, j: (l, 0, j // 2))]
        args = [proj2, proj2, proj2, proj2, gup, gb]
    else:
        (logits,) = extra
        in_specs = [head_col(C_HQ), head_col(C_HF), head_col(C_HI),
                    pl.BlockSpec((DEPTH, LANE), lambda i, j: (0, j))]
        args = [proj2, proj2, proj2, logits]
    in_specs += [pl.BlockSpec((g, None, dk, LANE), lambda i, j: (i, j, 0, 0)),
                 pl.BlockSpec((None, 1, LANE), lambda i, j: (l, 0, j))]
    return pl.pallas_call(
        functools.partial(_chunk_tok_kernel, kind=kind, l=l, g=g),
        grid=(b // g, h),
        in_specs=in_specs,
        out_specs=[
            pl.BlockSpec((g, LANE), lambda i, j: (i, j)),
            pl.BlockSpec((g, None, dk, LANE), lambda i, j: (i, j, 0, 0)),
        ],
        out_shape=[
            jax.ShapeDtypeStruct((b, W_BRANCH), f32),
            jax.ShapeDtypeStruct((b, h, dk, LANE), f32),
        ],
        compiler_params=_cparams(("parallel", "parallel")),
        name=kind + "_tok",
    )(*args, s_in, gn3)


def _lru_gates(y, wbd, pvec):
    b_a, b_x, lam = pvec[1:2, :], pvec[2:3, :], pvec[3:4, :]
    gates = _dotf(y, wbd)
    r = _sigmoid(gates[:, 0:W_BRANCH] + b_a)
    ig = _sigmoid(gates[:, W_BRANCH:2 * W_BRANCH] + b_x)
    log_a = -LRU_C * r * _softplus(-lam)
    a = jnp.exp(log_a)
    one_m_a2 = -jnp.tanh(log_a) * (jnp.exp(2.0 * log_a) + 1.0)
    b = jnp.sqrt(one_m_a2) * (ig * y)
    return a, b


def _lru_seq_kernel(x_ref, cw_ref, pvec_ref, wbd_ref, o_ref, xpad, hcar, a_sc, b_sc, *, lb):
    tb = pl.program_id(1)

    @pl.when(tb == 0)
    def _():
        xpad[0:8, :] = jnp.zeros((8, W_BRANCH), f32)
        hcar[...] = jnp.zeros_like(hcar)

    x = x_ref[...]
    xpad[8:8 + lb, :] = x
    cw = cw_ref[...]
    pvec = pvec_ref[...]
    y = pvec[0:1, :] + x * cw[3:4, :]
    for j in range(CONV_W - 1):
        y = y + xpad[5 + j:5 + j + lb, :] * cw[j:j + 1, :]
    a, b = _lru_gates(y, wbd_ref[...], pvec)
    a_sc[...] = a
    b_sc[...] = b

    def body(t, h):
        h = a_sc[pl.ds(t, 1), :] * h + b_sc[pl.ds(t, 1), :]
        o_ref[pl.ds(t, 1), :] = h
        return h

    hcar[...] = lax.fori_loop(0, lb, body, hcar[...], unroll=8)
    xpad[5:8, :] = x[lb - 3:lb, :]


def _lru_seq(proj3, cw, pvec, wbd, l, lb):
    b, seq_len, _ = proj3.shape

    def layer(shape):
        return pl.BlockSpec((None,) + shape, lambda *_: (l,) + (0,) * len(shape))

    return pl.pallas_call(
        functools.partial(_lru_seq_kernel, lb=lb),
        grid=(b, seq_len // lb),
        in_specs=[
            pl.BlockSpec((None, lb, W_BRANCH), lambda i, t: (i, t, C_LRU // W_BRANCH)),
            layer((CONV_W, W_BRANCH)), layer((8, W_BRANCH)), layer((W_BRANCH, 2 * W_BRANCH)),
        ],
        out_specs=pl.BlockSpec((None, lb, W_BRANCH), lambda i, t: (i, t, 0)),
        out_shape=jax.ShapeDtypeStruct((b, seq_len, W_BRANCH), f32),
        scratch_shapes=[
            pltpu.VMEM((8 + lb, W_BRANCH), f32),
            pltpu.VMEM((1, W_BRANCH), f32),
            pltpu.VMEM((lb, W_BRANCH), f32),
            pltpu.VMEM((lb, W_BRANCH), f32),
        ],
        compiler_params=_cparams(("parallel", "arbitrary")),
        name="lru_seq",
    )(proj3, cw, pvec, wbd)


def _lru_tok_kernel(x_ref, b0_ref, b1_ref, b2_ref, h0_ref, cw_ref, pvec_ref, wbd_ref, o_ref):
    cw = cw_ref[...]
    pvec = pvec_ref[...]
    y = (pvec[0:1, :] + b0_ref[...] * cw[0:1, :] + b1_ref[...] * cw[1:2, :] + b2_ref[...] * cw[2:3, :]
         + x_ref[...] * cw[3:4, :])
    a, b = _lru_gates(y, wbd_ref[...], pvec)
    o_ref[...] = a * h0_ref[...] + b


def _lru_tok(proj2, bufs, h0, cw, pvec, wbd, l):
    b = proj2.shape[0]

    def layer(shape):
        return pl.BlockSpec((None,) + shape, lambda *_: (l,) + (0,) * len(shape))

    row = pl.BlockSpec((b, W_BRANCH), lambda i: (0, 0))
    return pl.pallas_call(
        _lru_tok_kernel,
        grid=(1,),
        in_specs=[pl.BlockSpec((b, W_BRANCH), lambda i: (0, C_LRU // W_BRANCH)), row, row, row, row,
                  layer((CONV_W, W_BRANCH)), layer((8, W_BRANCH)), layer((W_BRANCH, 2 * W_BRANCH))],
        out_specs=row,
        out_shape=jax.ShapeDtypeStruct((b, W_BRANCH), f32),
        compiler_params=_cparams(("arbitrary",)),
        name="lru_tok",
    )(proj2, *bufs, h0, cw, pvec, wbd)


def _prep_weights(p):
    w = p["w_in"]
    pad = jnp.zeros((DEPTH, D_MODEL, LANE - GLA_GATE_RANK), w.dtype)
    w_in_p = jnp.concatenate([
        w[:, :, 0:1536], w[:, :, 2176:2688], w[:, :, 2704:4240], w[:, :, 4240:4752], w[:, :, 4752:6800],
        w[:, :, 1664:2176], w[:, :, 1536:1664], w[:, :, 2688:2704], pad], axis=2).astype(bf16)
    zl = jnp.zeros((DEPTH, RWKV_LORA, W_BRANCH), f32)
    wup = jnp.concatenate([jnp.concatenate([p["rwkv_w_up"], zl], axis=2),
                           jnp.concatenate([zl, p["rwkv_a_up"]], axis=2)], axis=1)
    zrow = jnp.zeros((DEPTH, W_BRANCH), f32)
    rwkv_vec = jnp.stack([p["rwkv_w0"], p["rwkv_a0"], p["rwkv_k_k"], p["rwkv_k_a"],
                          p["rwkv_r_k"].reshape(DEPTH, W_BRANCH), p["rwkv_gn_g"], p["rwkv_gn_b"], zrow], axis=1)
    mu = p["rwkv_mu"]
    gup = jnp.concatenate([p["gla_gk_up"], jnp.zeros((DEPTH, LANE - GLA_GATE_RANK, GLA_KW), f32)], axis=1)
    eye = jnp.eye(LRU_BLOCKS, dtype=f32)

    def bd(wb):
        return jnp.einsum("lhij,hg->lhigj", wb, eye).reshape(DEPTH, W_BRANCH, W_BRANCH)

    lru_vec = jnp.stack([p["lru_conv_b"], p["lru_b_a"], p["lru_b_x"], p["lru_lambda"],
                         zrow, zrow, zrow, zrow], axis=1)
    return dict(
        w_in_p=w_in_p,
        w_out_b=p["w_out"].astype(bf16),
        norm_g3=p["norm_g"].reshape(DEPTH, 1, D_MODEL),
        final_g2=p["final_g"].reshape(1, D_MODEL),
        rwkv=(mu[:, None, 0:1536], mu[:, None, 1536:1664], wup, rwkv_vec) + _rwkv_consts(),
        gla=(gup, p["gla_gk_b"].reshape(DEPTH, 1, GLA_KW)),
        gla_gn=p["gla_gn_g"].reshape(DEPTH, 1, W_BRANCH),
        hgrn=(p["hgrn_lb_logits"],),
        hgrn_gn=p["hgrn_gn_g"].reshape(DEPTH, 1, W_BRANCH),
        lru=(p["lru_conv_w"], lru_vec, jnp.concatenate([bd(p["lru_w_a"]), bd(p["lru_w_x"])], axis=2)),
    )


def _shift_state(proj_last):
    return jnp.concatenate([proj_last[..., 0:1536], proj_last[..., C_RWKV_LO:C_RWKV_LO + LANE]], axis=-1)


def _rwkv_state_in(s):
    b = s.shape[0]
    return jnp.transpose(s, (0, 2, 1, 3)).reshape(b, RWKV_HEAD, W_BRANCH)


def _rwkv_state_out(s):
    b = s.shape[0]
    return jnp.transpose(s.reshape(b, RWKV_HEAD, RWKV_H, RWKV_HEAD), (0, 2, 1, 3))


def _trunk_seq(x, mod, wts, lb):
    b, seq_len, _ = x.shape
    t = b * seq_len
    tm = min(512, seq_len)
    per_seq = seq_len // tm
    mod3 = mod.reshape(DEPTH * b * 3, 1, D_MODEL)
    x2 = x.reshape(t, D_MODEL)
    zeros = functools.partial(jnp.zeros, dtype=f32)
    new = []
    for l in range(DEPTH):
        mod_l = mod3[l * b * 3:(l + 1) * b * 3]
        proj2 = _inproj(x2, wts["norm_g3"], mod_l, wts["w_in_p"], l, per_seq, tm)
        proj3 = proj2.reshape(b, seq_len, PROJ_P)
        o_a, s_wkv = _rwkv_seq(proj3, zeros((b, 1, 1536)), zeros((b, 1, LANE)),
                               zeros((b, RWKV_HEAD, W_BRANCH)), wts["rwkv"], l, min(lb, 128))
        o_b, st_gla = _chunk_seq("gla", proj3, wts["gla"], wts["gla_gn"], l, lb)
        o_c, st_hgrn = _chunk_seq("hgrn", proj3, wts["hgrn"], wts["hgrn_gn"], l, lb)
        o_d = _lru_seq(proj3, *wts["lru"], l, lb)
        outs = [o.reshape(t, W_BRANCH) for o in (o_a, o_b, o_c, o_d)]
        tmo = min(256, seq_len)
        x2 = _outproj(outs, proj2, x2, mod_l, wts["w_out_b"], wts["final_g2"], l, seq_len // tmo, tmo,
                      final=(l == DEPTH - 1))
        last = proj3[:, seq_len - 1]
        new.append((
            _shift_state(last),
            _rwkv_state_out(s_wkv),
            _state_from_transposed(st_gla, GLA_H, GLA_DK),
            _state_from_transposed(st_hgrn, HGRN_H, HGRN_DK),
            proj3[:, seq_len - (CONV_W - 1):, C_LRU:C_LRU + W_BRANCH],
            o_d[:, seq_len - 1],
        ))
    return x2.reshape(b, seq_len, D_MODEL), tuple(jnp.stack([n[i] for n in new], axis=0) for i in range(6))


def _trunk_tok(x, mod, states, wts):
    b = x.shape[0]
    s_shift, s_wkv, s_gla, s_hgrn, s_conv, s_h = states
    x2 = x.reshape(b, D_MODEL)
    g = 8
    new = []
    for l in range(DEPTH):
        proj2 = _inproj(x2, wts["norm_g3"], mod, wts["w_in_p"], l, None, b)
        o_a, n_wkv = _rwkv_tok(proj2, s_shift[l][:, 0:1536], s_shift[l][:, 1536:1664],
                               _rwkv_state_in(s_wkv[l]), wts["rwkv"], l, g)
        o_b, n_gla = _chunk_tok("gla", proj2, wts["gla"], wts["gla_gn"], s_gla[l], l, g)
        o_c, n_hgrn = _chunk_tok("hgrn", proj2, wts["hgrn"], wts["hgrn_gn"], s_hgrn[l], l, g)
        conv = s_conv[l]
        o_d = _lru_tok(proj2, [conv[:, 0], conv[:, 1], conv[:, 2]], s_h[l], *wts["lru"], l)
        x2 = _outproj([o_a, o_b, o_c, o_d], proj2, x2, mod, wts["w_out_b"], wts["final_g2"], l, None, b,
                      final=(l == DEPTH - 1))
        n_conv = jnp.concatenate([conv[:, 1:], proj2[:, None, C_LRU:C_LRU + W_BRANCH]], axis=1)
        new.append((_shift_state(proj2), _rwkv_state_out(n_wkv), n_gla, n_hgrn, n_conv, o_d))
    return x2.reshape(b, 1, D_MODEL), tuple(jnp.stack([n[i] for n in new], axis=0) for i in range(6))


def kernel(x_prompt, x_sample, c_prompt, c_sample, state_rwkv_shift, state_rwkv_wkv, state_gla, state_hgrn, state_lru_conv, state_lru_h, norm_g, w_ada, b_ada, w_in, w_out, rwkv_mu, rwkv_w0, rwkv_w_up, rwkv_a0, rwkv_a_up, rwkv_k_k, rwkv_k_a, rwkv_r_k, rwkv_gn_g, rwkv_gn_b, gla_gk_up, gla_gk_b, gla_gn_g, hgrn_lb_logits, hgrn_gn_g, lru_conv_w, lru_conv_b, lru_w_a, lru_b_a, lru_w_x, lru_b_x, lru_lambda, final_g):
    p = dict(norm_g=norm_g, w_in=w_in, w_out=w_out, rwkv_mu=rwkv_mu, rwkv_w0=rwkv_w0, rwkv_w_up=rwkv_w_up,
             rwkv_a0=rwkv_a0, rwkv_a_up=rwkv_a_up, rwkv_k_k=rwkv_k_k, rwkv_k_a=rwkv_k_a, rwkv_r_k=rwkv_r_k,
             rwkv_gn_g=rwkv_gn_g, rwkv_gn_b=rwkv_gn_b, gla_gk_up=gla_gk_up, gla_gk_b=gla_gk_b,
             gla_gn_g=gla_gn_g, hgrn_lb_logits=hgrn_lb_logits, hgrn_gn_g=hgrn_gn_g, lru_conv_w=lru_conv_w,
             lru_conv_b=lru_conv_b, lru_w_a=lru_w_a, lru_b_a=lru_b_a, lru_w_x=lru_w_x, lru_b_x=lru_b_x,
             lru_lambda=lru_lambda, final_g=final_g)
    wts = _prep_weights(p)
    bp = x_prompt.shape[0]
    bs = x_sample.shape[0]
    pad_rows = (-bp) % 8
    c_all = jnp.concatenate([c_prompt, jnp.zeros((pad_rows, D_MODEL), f32), c_sample], axis=0)
    mod = _ada(c_all, w_ada, b_ada)
    mod_p = mod[:, 0:bp]
    mod_s = mod[:, bp + pad_rows:bp + pad_rows + bs]
    y_p, st_p = _trunk_seq(x_prompt, mod_p, wts, 256)
    states = (state_rwkv_shift, state_rwkv_wkv, state_gla, state_hgrn, state_lru_conv, state_lru_h)
    y_s, st_s = _trunk_tok(x_sample, mod_s, states, wts)
    return (y_p, y_s) + st_p + st_s
```

```python
import functools

import jax
import jax.numpy as jnp
from jax import lax
from jax.experimental import pallas as pl
from jax.experimental.pallas import tpu as pltpu

f32 = jnp.float32
bf16 = jnp.bfloat16

D_MODEL = 2048
DEPTH = 4
W_BRANCH = 512
EPS = 1e-6

RWKV_H = 8
RWKV_HEAD = 64
RWKV_LORA = 64
RWKV_PROJ = 3 * W_BRANCH + 2 * RWKV_LORA
RWKV_GN_EPS = 64e-5

GLA_H = 4
GLA_DK = 64
GLA_DV = 128
GLA_KW = GLA_H * GLA_DK
GLA_GATE_RANK = 16
GLA_GATE_NORM = 16.0

HGRN_H = 4
HGRN_DK = 128
HGRN_DV = 128
LB_FLOOR = 1e-30

LRU_BLOCKS = 8
LRU_BLOCK = 64
LRU_C = 8.0
CONV_W = 4

CHUNK = 32

C_R, C_K, C_V = 0, 512, 1024
C_GLA_V = 1536
C_HQ, C_HF, C_HI = 2048, 2560, 3072
C_LRU = 3584
C_GATE = 4096
C_GLA_Q, C_GLA_K = 6144, 6400
C_RWKV_LO = 6656
C_GLA_LO = 6784
PROJ_P = 6912

LANE = 128
MXU_TILE = 256
VMEM_LIMIT = 56 * 1024 * 1024

TM_IN = 1024
TN_IN = 768
TM_OUT = 256
LB_CHUNK = 256
LB_RWKV = 128
TOK_GROUP = 16
TOK_GROUP_RWKV = 8

HIGHEST = lax.Precision.HIGHEST


def _cparams(sem):
    return pltpu.CompilerParams(dimension_semantics=sem, vmem_limit_bytes=VMEM_LIMIT)


def _dotf(a, b):
    return jnp.dot(a, b, preferred_element_type=f32, precision=HIGHEST)


def _prec(a):
    return HIGHEST if a.dtype == f32 else None


def _dot_nt(a, b):
    return lax.dot_general(a, b, (((1,), (1,)), ((), ())), preferred_element_type=f32, precision=_prec(a))


def _dot_tn(a, b):
    return lax.dot_general(a, b, (((0,), (0,)), ((), ())), preferred_element_type=f32, precision=_prec(a))


def _split(x, pieces):
    out = []
    for i in range(pieces):
        part = x.astype(bf16)
        out.append(part)
        if i + 1 < pieces:
            x = x - part.astype(f32)
    return out


def _half_dot(x, ones):
    d = functools.partial(jnp.dot, preferred_element_type=f32)
    kh = ones.shape[0]
    if x.shape[1] == kh:
        return d(x, ones)
    return jnp.concatenate([d(x[:, 0:kh], ones), d(x[:, kh:2 * kh], ones)], axis=1)


def _seg_sum(x, ones, pieces=2):
    return sum(_half_dot(part, ones) for part in _split(x, pieces))


def _ones_seg(ones, x, pieces=3):
    d = functools.partial(jnp.dot, preferred_element_type=f32)
    return sum(d(ones, part) for part in _split(x, pieces))


def _sigmoid(x):
    return jax.nn.sigmoid(x)


def _softplus(x):
    return jnp.maximum(x, 0.0) + jnp.log1p(jnp.exp(-jnp.abs(x)))


def _log_sigmoid(x):
    return -_softplus(-x)


def _ada_kernel(c_ref, w_ref, b_ref, o_ref):
    c = c_ref[...]
    s = (c * _sigmoid(c)).astype(bf16)
    o_ref[...] = jnp.dot(s, w_ref[...].astype(bf16), preferred_element_type=f32) + b_ref[...]


def _ada(c_all, w_ada, b_ada):
    rows = c_all.shape[0]
    tn = 512
    n = w_ada.shape[2]
    return pl.pallas_call(
        _ada_kernel,
        grid=(DEPTH, n // tn),
        in_specs=[
            pl.BlockSpec((rows, D_MODEL), lambda l, j: (0, 0)),
            pl.BlockSpec((None, D_MODEL, tn), lambda l, j: (l, 0, j)),
            pl.BlockSpec((None, 1, tn), lambda l, j: (l, 0, j)),
        ],
        out_specs=pl.BlockSpec((None, rows, tn), lambda l, j: (l, 0, j)),
        out_shape=jax.ShapeDtypeStruct((DEPTH, rows, n), f32),
        compiler_params=_cparams(("parallel", "parallel")),
        name="ada_mod",
    )(c_all, w_ada, b_ada.reshape(DEPTH, 1, n))


def _inproj_kernel(x_ref, g_ref, sc_ref, sh_ref, w_ref, o_ref, h_ref):
    @pl.when(pl.program_id(1) == 0)
    def _():
        x = x_ref[...]
        ms = jnp.mean(x * x, axis=-1, keepdims=True)
        h = x * lax.rsqrt(ms + EPS) * g_ref[...]
        h = h * (1.0 + sc_ref[...]) + sh_ref[...]
        h_ref[...] = h.astype(bf16)

    o_ref[...] = jnp.dot(h_ref[...], w_ref[...], preferred_element_type=f32)


def _mod_spec(l, which, per_seq_tiles, tm):
    if per_seq_tiles is None:
        return pl.BlockSpec((None, tm, D_MODEL), lambda i, *_: (l, 0, which))
    return pl.BlockSpec((None, 1, D_MODEL), lambda i, *_: (i // per_seq_tiles * 3 + which, 0, 0))


def _inproj(x2, norm_g3, mod, w_in_p, l, per_seq_tiles, tm):
    t = x2.shape[0]
    tn = TN_IN
    return pl.pallas_call(
        _inproj_kernel,
        grid=(t // tm, PROJ_P // tn),
        in_specs=[
            pl.BlockSpec((tm, D_MODEL), lambda i, j: (i, 0)),
            pl.BlockSpec((None, 1, D_MODEL), lambda i, j: (l, 0, 0)),
            _mod_spec(l, 1, per_seq_tiles, tm),
            _mod_spec(l, 0, per_seq_tiles, tm),
            pl.BlockSpec((None, D_MODEL, tn), lambda i, j: (l, 0, j)),
        ],
        out_specs=pl.BlockSpec((tm, tn), lambda i, j: (i, j)),
        out_shape=jax.ShapeDtypeStruct((t, PROJ_P), f32),
        scratch_shapes=[pltpu.VMEM((tm, D_MODEL), bf16)],
        compiler_params=_cparams(("parallel", "arbitrary")),
        name="in_proj",
    )(x2, norm_g3, mod, mod, w_in_p)


def _outproj_kernel(oa_ref, ob_ref, oc_ref, od_ref, pz_ref, x_ref, gate_ref, w_ref, fg_ref, o_ref, *, final):
    z = pz_ref[...]
    o = jnp.concatenate([oa_ref[...], ob_ref[...], oc_ref[...], od_ref[...]], axis=1)
    o = o * (z * _sigmoid(z))
    y = jnp.dot(o.astype(bf16), w_ref[...], preferred_element_type=f32)
    xn = x_ref[...] + gate_ref[...] * y
    if final:
        ms = jnp.mean(xn * xn, axis=-1, keepdims=True)
        xn = xn * lax.rsqrt(ms + EPS) * fg_ref[...]
    o_ref[...] = xn


def _outproj(outs, proj2, x2, mod, w_out_b, final_g2, l, per_seq_tiles, tm, final):
    t = x2.shape[0]
    mix_spec = pl.BlockSpec((tm, W_BRANCH), lambda i: (i, 0))
    return pl.pallas_call(
        functools.partial(_outproj_kernel, final=final),
        grid=(t // tm,),
        in_specs=[
            mix_spec, mix_spec, mix_spec, mix_spec,
            pl.BlockSpec((tm, D_MODEL), lambda i: (i, C_GATE // D_MODEL)),
            pl.BlockSpec((tm, D_MODEL), lambda i: (i, 0)),
            _mod_spec(l, 2, per_seq_tiles, tm),
            pl.BlockSpec((None, D_MODEL, D_MODEL), lambda i: (l, 0, 0)),
            pl.BlockSpec((1, D_MODEL), lambda i: (0, 0)),
        ],
        out_specs=pl.BlockSpec((tm, D_MODEL), lambda i: (i, 0)),
        out_shape=jax.ShapeDtypeStruct((t, D_MODEL), f32),
        compiler_params=_cparams(("parallel",)),
        name="out_proj",
    )(*outs, proj2, x2, mod, w_out_b, final_g2)


def _rwkv_prologue(r, k, v, lo, pr, pk, pv, plo, mu_rkv, mu_lo, wup, pvec, bo):
    w0, a0, k_k, k_a, r_k = (pvec[i:i + 1, :] for i in range(5))
    xr = r + (pr - r) * mu_rkv[:, 0:512]
    xk = k + (pk - k) * mu_rkv[:, 512:1024]
    xv = v + (pv - v) * mu_rkv[:, 1024:1536]
    xlo = lo + (plo - lo) * mu_lo
    lane = lax.broadcasted_iota(jnp.int32, xlo.shape, 1)
    act = jnp.where(lane < RWKV_LORA, jnp.tanh(xlo), xlo)
    up = _dotf(act, wup)
    w_raw = -_softplus(-(w0 + up[:, 0:512])) - 0.5
    ew = jnp.exp(-jnp.exp(w_raw))
    a = _sigmoid(a0 + up[:, 512:1024])
    kk = xk * k_k
    kk = kk / jnp.maximum(jnp.sqrt(_seg_sum(kk * kk, bo)), 1e-12)
    kh = xk * (1.0 + (a - 1.0) * k_a)
    alp = kk * a
    ar = _seg_sum(alp * xr, bo)
    return dict(
        kap=kk, ew=ew, alp=alp, kh=kh, vv=xv,
        wr=ew * xr - ar * kk,
        kr=_seg_sum(kh * xr, bo),
        bonus=_seg_sum(xr * kh * r_k, bo) * xv,
    )


_RWKV_STEP_KEYS = ("kap", "ew", "alp", "kh", "vv", "wr", "kr")


def _rwkv_step(s, kap, ew, alp, kh, vv, wr, kr, bo, idt):
    sk_hi, sk_mid = _split(s * kap, 2)
    x = jnp.concatenate([sk_hi, sk_mid, (s * wr).astype(bf16), idt.astype(bf16) * vv.astype(bf16)], axis=0)
    red = _half_dot(x, bo)
    sk = red[0:64] + red[64:128]
    s_new = s * ew - sk * alp + red[192:256] * kh
    o_row = jnp.sum(red[128:192] * idt, axis=0, keepdims=True) + vv * kr
    return s_new, o_row


def _rwkv_epilogue(o, bonus, pvec, bo):
    gn_g, gn_b = pvec[5:6, :], pvec[6:7, :]
    mu = _seg_sum(o, bo) * (1.0 / RWKV_HEAD)
    d = o - mu
    var = _seg_sum(d * d, bo) * (1.0 / RWKV_HEAD)
    return d * lax.rsqrt(var + RWKV_GN_EPS) * gn_g + gn_b + bonus


def _rwkv_seq_kernel(r_ref, k_ref, v_ref, lo_ref, prkv_ref, plo_ref, sin_ref, mu_rkv_ref, mu_lo_ref, wup_ref,
                     pvec_ref, bo_ref, idt_ref, o_ref, sout_ref,
                     s_sc, crkv_sc, clo_sc, kap_sc, ew_sc, alp_sc, kh_sc, vv_sc, wr_sc, kr_sc, oraw_sc,
                     bonus_sc, *, g_seqs, lb):
    tb = pl.program_id(1)

    @pl.when(tb == 0)
    def _():
        s_sc[...] = sin_ref[...]
        crkv_sc[...] = prkv_ref[...]
        clo_sc[...] = plo_ref[...]

    bo = bo_ref[...]
    idt = idt_ref[...]
    pvec = pvec_ref[...]
    step_sc = dict(kap=kap_sc, ew=ew_sc, alp=alp_sc, kh=kh_sc, vv=vv_sc, wr=wr_sc, kr=kr_sc)

    row0 = lax.broadcasted_iota(jnp.int32, (lb, 1), 0) == 0
    for g in range(g_seqs):
        cur = [r_ref[g], k_ref[g], v_ref[g]]
        lo = lo_ref[g]
        carry = crkv_sc[g]
        prev = [jnp.where(row0, carry[:, i * 512:(i + 1) * 512], pltpu.roll(c, 1, 0)) for i, c in enumerate(cur)]
        plo = jnp.where(row0, clo_sc[g], pltpu.roll(lo, 1, 0))
        res = _rwkv_prologue(cur[0], cur[1], cur[2], lo, prev[0], prev[1], prev[2], plo,
                             mu_rkv_ref[...], mu_lo_ref[...], wup_ref[...], pvec, bo)
        for key in _RWKV_STEP_KEYS:
            step_sc[key][g] = res[key]
        bonus_sc[g] = res["bonus"]
        for i, c in enumerate(cur):
            crkv_sc[g, :, i * 512:(i + 1) * 512] = c[lb - 1:lb, :]
        clo_sc[g] = lo[lb - 1:lb, :]

    def body(t, carry):
        for g in range(g_seqs):
            rows = [step_sc[key][g, pl.ds(t, 1), :] for key in _RWKV_STEP_KEYS]
            s_new, o_row = _rwkv_step(s_sc[g], *rows, bo, idt)
            s_sc[g] = s_new
            oraw_sc[g, pl.ds(t, 1), :] = o_row
        return carry

    lax.fori_loop(0, lb, body, 0)

    for g in range(g_seqs):
        o_ref[g] = _rwkv_epilogue(oraw_sc[g], bonus_sc[g], pvec, bo)

    @pl.when(tb == pl.num_programs(1) - 1)
    def _():
        sout_ref[...] = s_sc[...]


def _rwkv_tok_kernel(r_ref, k_ref, v_ref, lo_ref, prkv_ref, plo_ref, sin_ref, mu_rkv_ref, mu_lo_ref, wup_ref,
                     pvec_ref, bo_ref, idt_ref, *rest, g_rows):
    o_ref, sout_ref = rest[-2:]
    bo = bo_ref[...]
    idt = idt_ref[...]
    pvec = pvec_ref[...]
    prkv = prkv_ref[...]
    res = _rwkv_prologue(r_ref[...], k_ref[...], v_ref[...], lo_ref[...],
                         prkv[:, 0:512], prkv[:, 512:1024], prkv[:, 1024:1536], plo_ref[...],
                         mu_rkv_ref[...], mu_lo_ref[...], wup_ref[...], pvec, bo)
    o_rows = []
    for g in range(g_rows):
        rows = [res[key][g:g + 1, :] for key in _RWKV_STEP_KEYS]
        s_new, o_row = _rwkv_step(sin_ref[g], *rows, bo, idt)
        sout_ref[g] = s_new
        o_rows.append(o_row)
    o_ref[...] = _rwkv_epilogue(jnp.concatenate(o_rows, axis=0), res["bonus"], pvec, bo)


def _rwkv_consts():
    i = jnp.arange(W_BRANCH)
    j = jnp.arange(MXU_TILE)
    bo = (j[:, None] // RWKV_HEAD == j[None, :] // RWKV_HEAD).astype(bf16)
    idt = (jnp.arange(RWKV_HEAD)[:, None] == (i[None, :] % RWKV_HEAD)).astype(f32)
    return bo, idt


def _rwkv_weight_specs(l):
    def cs(shape):
        return pl.BlockSpec((None,) + shape, lambda *_: (l,) + (0,) * len(shape))

    def const(shape):
        return pl.BlockSpec(shape, lambda *_: (0,) * len(shape))

    return [cs((1, 1536)), cs((1, LANE)), cs((LANE, 1024)), cs((8, W_BRANCH)),
            const((MXU_TILE, MXU_TILE)), const((RWKV_HEAD, W_BRANCH))]


def _rwkv_seq(proj3, prev_rkv, prev_lo, s_in, wts, l, lb):
    b, seq_len, _ = proj3.shape
    g = b
    nt = seq_len // lb

    def col(width, off):
        return pl.BlockSpec((g, lb, width), lambda i, t: (i, t, off // width))

    tok_sc = pltpu.VMEM((g, lb, W_BRANCH), f32)
    out = pl.pallas_call(
        functools.partial(_rwkv_seq_kernel, g_seqs=g, lb=lb),
        grid=(b // g, nt),
        in_specs=[
            col(512, C_R), col(512, C_K), col(512, C_V), col(LANE, C_RWKV_LO),
            pl.BlockSpec((g, 1, 1536), lambda i, t: (i, 0, 0)),
            pl.BlockSpec((g, 1, LANE), lambda i, t: (i, 0, 0)),
            pl.BlockSpec((g, RWKV_HEAD, W_BRANCH), lambda i, t: (i, 0, 0)),
        ] + _rwkv_weight_specs(l),
        out_specs=[
            pl.BlockSpec((g, lb, W_BRANCH), lambda i, t: (i, t, 0)),
            pl.BlockSpec((g, RWKV_HEAD, W_BRANCH), lambda i, t: (i, 0, 0)),
        ],
        out_shape=[
            jax.ShapeDtypeStruct((b, seq_len, W_BRANCH), f32),
            jax.ShapeDtypeStruct((b, RWKV_HEAD, W_BRANCH), f32),
        ],
        scratch_shapes=[
            pltpu.VMEM((g, RWKV_HEAD, W_BRANCH), f32),
            pltpu.VMEM((g, 1, 1536), f32),
            pltpu.VMEM((g, 1, LANE), f32),
        ] + [tok_sc] * 9,
        compiler_params=_cparams(("parallel", "arbitrary")),
        name="rwkv_seq",
    )(proj3, proj3, proj3, proj3, prev_rkv, prev_lo, s_in, *wts)
    return out


def _rwkv_tok(proj2, prev_rkv, prev_lo, s_all, acc, wts, l, g):
    b = proj2.shape[0]

    def col(width, off):
        return pl.BlockSpec((g, width), lambda i: (i, off // width))

    state_spec = pl.BlockSpec((None, g, RWKV_HEAD, W_BRANCH), lambda i: (l, i, 0, 0))
    in_specs = [
        col(512, C_R), col(512, C_K), col(512, C_V), col(LANE, C_RWKV_LO),
        pl.BlockSpec((None, g, 1536), lambda i: (l, i, 0)),
        pl.BlockSpec((None, g, LANE), lambda i: (l, i, 0)),
        state_spec,
    ] + _rwkv_weight_specs(l)
    args = [proj2, proj2, proj2, proj2, prev_rkv, prev_lo, s_all, *wts]
    aliases = {}
    if acc is not None:
        in_specs.append(pl.BlockSpec(memory_space=pl.ANY))
        aliases = {len(args): 1}
        args.append(acc)
    return pl.pallas_call(
        functools.partial(_rwkv_tok_kernel, g_rows=g),
        grid=(b // g,),
        in_specs=in_specs,
        out_specs=[pl.BlockSpec((g, W_BRANCH), lambda i: (i, 0)), state_spec],
        out_shape=[
            jax.ShapeDtypeStruct((b, W_BRANCH), f32),
            jax.ShapeDtypeStruct(s_all.shape, f32),
        ],
        input_output_aliases=aliases,
        compiler_params=_cparams(("parallel",)),
        name="rwkv_tok",
    )(*args)


def _gla_inputs(q_ref, k_ref, v_ref, lo_ref, gup_ref, gb_ref):
    q = q_ref[...] * (GLA_DK ** -0.5)
    z = _dotf(lo_ref[...], gup_ref[...]) + gb_ref[...]
    g = _log_sigmoid(z) * (1.0 / GLA_GATE_NORM)
    return q, k_ref[...], v_ref[...], g


def _hgrn_lb(logits, l):
    m = jnp.max(logits, axis=0, keepdims=True)
    e = jnp.exp(logits - m)
    sm = e / jnp.sum(e, axis=0, keepdims=True)
    lb = jnp.zeros_like(sm[0:1, :])
    for i in range(1, l + 1):
        lb = lb + sm[i:i + 1, :]
    return lb


def _hgrn_inputs(q_ref, f_ref, i_ref, logits_ref, l):
    lb = _hgrn_lb(logits_ref[...], l)
    f_lo = f_ref[...]
    a = jnp.log(jnp.maximum(lb, LB_FLOOR))
    b = jnp.log1p(-lb) + _log_sigmoid(f_lo)
    logf = jnp.maximum(a, b) + jnp.log1p(jnp.exp(-jnp.abs(a - b)))
    k = (1.0 - lb) * _sigmoid(-f_lo)
    return q_ref[...], k, i_ref[...], logf


def _chunk_core(q, k, v, g, lt_ref, bokv_ref, bdm_ref, bon_ref, gn_ref, st_sc, kpad, bpad, vpad, lb):
    bc = _ones_seg(lt_ref[...], g)
    kpad[CHUNK:CHUNK + lb, :] = k
    bpad[CHUNK:CHUNK + lb, :] = bc
    vpad[CHUNK:CHUNK + lb, :] = v
    rowc = lax.broadcasted_iota(jnp.int32, (lb, 1), 0) & (CHUNK - 1)
    bokv = bokv_ref[...]

    o = jnp.zeros((lb, W_BRANCH), f32)
    for d in range(CHUNK):
        start = CHUNK - d
        ks = kpad[start:start + lb, :]
        bs = bpad[start:start + lb, :]
        vs = vpad[start:start + lb, :]
        m = rowc >= d
        diff = jnp.where(m, bc - bs, 0.0)
        z = jnp.where(m, q * ks * jnp.exp(diff), 0.0)
        o = o + _seg_sum(z, bokv, pieces=1) * vs

    bdm = bdm_ref[...]
    st = st_sc[...]
    outs = []
    for c in range(lb // CHUNK):
        sl = slice(c * CHUNK, (c + 1) * CHUNK)
        bcc = bc[sl]
        blast = bcc[CHUNK - 1:CHUNK, :]
        qe = q[sl] * jnp.exp(bcc)
        outs.append(o[sl] + _dot_nt(qe.astype(bf16), st.astype(bf16)))
        ke = k[sl] * jnp.exp(blast - bcc)
        st = st * jnp.exp(blast) + _dot_tn(v[sl].astype(bf16), ke.astype(bf16)) * bdm
    st_sc[...] = st
    o = jnp.concatenate(outs, axis=0)
    ms = _seg_sum(o * o, bon_ref[...]) * (1.0 / LANE)
    return o * lax.rsqrt(ms + EPS) * gn_ref[...]


def _chunk_seq_kernel(*refs, kind, l, lb):
    n_in = 6 if kind == "gla" else 4
    ins = refs[:n_in]
    lt_ref, bokv_ref, bdm_ref, bon_ref, gn_ref, o_ref, stout_ref, st_sc, kpad, bpad, vpad = refs[n_in:]
    tb = pl.program_id(1)

    @pl.when(tb == 0)
    def _():
        st_sc[...] = jnp.zeros_like(st_sc)
        kpad[0:CHUNK, :] = jnp.zeros((CHUNK, kpad.shape[1]), f32)
        bpad[0:CHUNK, :] = jnp.zeros((CHUNK, bpad.shape[1]), f32)
        vpad[0:CHUNK, :] = jnp.zeros((CHUNK, vpad.shape[1]), f32)

    if kind == "gla":
        q, k, v, g = _gla_inputs(*ins)
    else:
        q, k, v, g = _hgrn_inputs(*ins, l)
    o_ref[...] = _chunk_core(q, k, v, g, lt_ref, bokv_ref, bdm_ref, bon_ref, gn_ref, st_sc, kpad, bpad, vpad, lb)

    @pl.when(tb == pl.num_programs(1) - 1)
    def _():
        stout_ref[...] = st_sc[...]


def _chunk_consts(h, dk, lb):
    f = h * dk
    i = jnp.arange(lb)
    lt = ((i[:, None] // CHUNK == i[None, :] // CHUNK) & (i[:, None] >= i[None, :])).astype(bf16)
    fi = jnp.arange(f)
    oi = jnp.arange(W_BRANCH)
    bokv = (fi[:, None] // dk == oi[None, :] // LANE).astype(bf16)
    bdm = (oi[:, None] // LANE == fi[None, :] // dk).astype(f32)
    ti = jnp.arange(MXU_TILE)
    bon = (ti[:, None] // LANE == ti[None, :] // LANE).astype(bf16)
    if f == W_BRANCH:
        bokv = bon
    return lt, bokv, bdm, bon


def _chunk_seq(kind, proj3, extra, gn3, l, lb):
    b, seq_len, _ = proj3.shape
    h, dk = (GLA_H, GLA_DK) if kind == "gla" else (HGRN_H, HGRN_DK)
    f = h * dk
    nt = seq_len // lb

    def col(width, off):
        return pl.BlockSpec((None, lb, width), lambda i, t: (i, t, off // width))

    def const(shape):
        return pl.BlockSpec(shape, lambda *_: (0,) * len(shape))

    def layer(shape):
        return pl.BlockSpec((None,) + shape, lambda *_: (l,) + (0,) * len(shape))

    if kind == "gla":
        gup, gb = extra
        in_specs = [col(GLA_KW, C_GLA_Q), col(GLA_KW, C_GLA_K), col(W_BRANCH, C_GLA_V), col(LANE, C_GLA_LO),
                    layer((LANE, GLA_KW)), layer((1, GLA_KW))]
        args = [proj3, proj3, proj3, proj3, gup, gb]
    else:
        (logits,) = extra
        in_specs = [col(W_BRANCH, C_HQ), col(W_BRANCH, C_HF), col(W_BRANCH, C_HI), const((DEPTH, W_BRANCH))]
        args = [proj3, proj3, proj3, logits]
    consts = _chunk_consts(h, dk, lb)
    in_specs += [const(c.shape) for c in consts] + [layer((1, W_BRANCH))]
    return pl.pallas_call(
        functools.partial(_chunk_seq_kernel, kind=kind, l=l, lb=lb),
        grid=(b, nt),
        in_specs=in_specs,
        out_specs=[
            pl.BlockSpec((None, lb, W_BRANCH), lambda i, t: (i, t, 0)),
            pl.BlockSpec((None, W_BRANCH, f), lambda i, t: (i, 0, 0)),
        ],
        out_shape=[
            jax.ShapeDtypeStruct((b, seq_len, W_BRANCH), f32),
            jax.ShapeDtypeStruct((b, W_BRANCH, f), f32),
        ],
        scratch_shapes=[
            pltpu.VMEM((W_BRANCH, f), f32),
            pltpu.VMEM((CHUNK + lb, f), f32),
            pltpu.VMEM((CHUNK + lb, f), f32),
            pltpu.VMEM((CHUNK + lb, W_BRANCH), f32),
        ],
        compiler_params=_cparams(("parallel", "arbitrary")),
        name=kind + "_seq",
    )(*args, *consts, gn3)


def _state_from_transposed(st, h, dk):
    blocks = [st[:, i * LANE:(i + 1) * LANE, i * dk:(i + 1) * dk] for i in range(h)]
    return jnp.swapaxes(jnp.stack(blocks, axis=1), 2, 3)


def _block_diag_rows(x, g, dk):
    row = lax.broadcasted_iota(jnp.int32, x.shape, 0)
    lane = lax.broadcasted_iota(jnp.int32, x.shape, 1)
    per = LANE // dk
    pieces = []
    for j in range(g // per):
        keep = row == (j * per + lane // dk)
        pieces.append(jnp.where(keep, x, 0.0))
    return jnp.concatenate(pieces, axis=1)


def _chunk_tok_kernel(*refs, kind, l, g, has_acc):
    ins, (o_ref, sout_ref) = refs[:-2], refs[-2:]
    if has_acc:
        ins = ins[:-1]
    if kind == "gla":
        q_ref, k_ref, v_ref, lo_ref, gup_ref, gb_ref, s_ref, gn_ref = ins
        odd = (pl.program_id(1) % 2) == 1
        lane = lax.broadcasted_iota(jnp.int32, (g, LANE), 1)
        own = (lane >= GLA_DK) == odd

        def pick(x):
            return jnp.where(own, x, pltpu.roll(x, GLA_DK, 1))

        q = pick(q_ref[...]) * (GLA_DK ** -0.5)
        k = pick(k_ref[...])
        z = _dotf(lo_ref[...], gup_ref[...]) + gb_ref[...]
        dec = jnp.exp(pick(_log_sigmoid(z) * (1.0 / GLA_GATE_NORM)))
        v = v_ref[...]
        dk = GLA_DK
    else:
        q_ref, f_ref, i_ref, logits_ref, s_ref, gn_ref = ins
        q, k, v, logf = _hgrn_inputs(q_ref, f_ref, i_ref, logits_ref, l)
        dec = jnp.exp(logf)
        dk = HGRN_DK
    s = s_ref[...].reshape(g * dk, LANE)
    ones = jnp.ones((g, LANE), bf16)
    dcol = sum(_dot_tn(part, ones) for part in _split(_block_diag_rows(dec, g, dk), 2))
    k_hi, k_lo = _split(_block_diag_rows(k, g, dk), 2)
    v_hi, v_lo = _split(v, 2)
    s_new = s * dcol + (_dot_tn(k_hi, v_hi) + _dot_tn(k_hi, v_lo) + _dot_tn(k_lo, v_hi))
    o = jnp.dot(_block_diag_rows(q, g, dk).astype(bf16), s_new.astype(bf16), preferred_element_type=f32)
    ms = jnp.mean(o * o, axis=-1, keepdims=True)
    o_ref[...] = o * lax.rsqrt(ms + EPS) * gn_ref[...]
    sout_ref[...] = s_new.reshape(g, dk, LANE)


def _chunk_tok(kind, proj2, extra, gn3, s_all, acc, l, g):
    b = proj2.shape[0]
    h, dk = (GLA_H, GLA_DK) if kind == "gla" else (HGRN_H, HGRN_DK)

    def head_col(off):
        per = LANE // dk
        return pl.BlockSpec((g, LANE), lambda i, j: (i, off // LANE + j // per))

    if kind == "gla":
        gup, gb = extra
        in_specs = [head_col(C_GLA_Q), head_col(C_GLA_K),
                    pl.BlockSpec((g, LANE), lambda i, j: (i, C_GLA_V // LANE + j)),
                    pl.BlockSpec((g, LANE), lambda i, j: (i, C_GLA_LO // LANE)),
                    pl.BlockSpec((None, LANE, LANE), lambda i, j: (l, 0, j // 2)),
                    pl.BlockSpec((None, 1, LANE), lambda i, j: (l, 0, j // 2))]
        args = [proj2, proj2, proj2, proj2, gup, gb]
    else:
        (logits,) = extra
        in_specs = [head_col(C_HQ), head_col(C_HF), head_col(C_HI),
                    pl.BlockSpec((DEPTH, LANE), lambda i, j: (0, j))]
        args = [proj2, proj2, proj2, logits]
    state_spec = pl.BlockSpec((None, g, None, dk, LANE), lambda i, j: (l, i, j, 0, 0))
    in_specs += [state_spec, pl.BlockSpec((None, 1, LANE), lambda i, j: (l, 0, j))]
    args += [s_all, gn3]
    aliases = {}
    if acc is not None:
        in_specs.append(pl.BlockSpec(memory_space=pl.ANY))
        aliases = {len(args): 1}
        args.append(acc)
    return pl.pallas_call(
        functools.partial(_chunk_tok_kernel, kind=kind, l=l, g=g, has_acc=acc is not None),
        grid=(b // g, h),
        in_specs=in_specs,
        out_specs=[pl.BlockSpec((g, LANE), lambda i, j: (i, j)), state_spec],
        out_shape=[
            jax.ShapeDtypeStruct((b, W_BRANCH), f32),
            jax.ShapeDtypeStruct(s_all.shape, f32),
        ],
        input_output_aliases=aliases,
        compiler_params=_cparams(("parallel", "parallel")),
        name=kind + "_tok",
    )(*args)


def _lru_gates(y, wbd, pvec):
    b_a, b_x, lam = pvec[1:2, :], pvec[2:3, :], pvec[3:4, :]
    gates = _dotf(y, wbd)
    r = _sigmoid(gates[:, 0:W_BRANCH] + b_a)
    ig = _sigmoid(gates[:, W_BRANCH:2 * W_BRANCH] + b_x)
    log_a = -LRU_C * r * _softplus(-lam)
    a = jnp.exp(log_a)
    one_m_a2 = -jnp.tanh(log_a) * (jnp.exp(2.0 * log_a) + 1.0)
    b = jnp.sqrt(one_m_a2) * (ig * y)
    return a, b


def _lru_seq_kernel(x_ref, cw_ref, pvec_ref, wbd_ref, o_ref, xpad, hcar, *, lb):
    tb = pl.program_id(1)

    @pl.when(tb == 0)
    def _():
        xpad[0:8, :] = jnp.zeros((8, W_BRANCH), f32)
        hcar[...] = jnp.zeros_like(hcar)

    x = x_ref[...]
    xpad[8:8 + lb, :] = x
    cw = cw_ref[...]
    pvec = pvec_ref[...]
    y = pvec[0:1, :] + x * cw[3:4, :]
    for j in range(CONV_W - 1):
        y = y + xpad[5 + j:5 + j + lb, :] * cw[j:j + 1, :]
    a, b = _lru_gates(y, wbd_ref[...], pvec)
    row = lax.broadcasted_iota(jnp.int32, (lb, 1), 0)
    s = 1
    while s < lb:
        m = row >= s
        b = jnp.where(m, a * pltpu.roll(b, s, 0) + b, b)
        a = jnp.where(m, a * pltpu.roll(a, s, 0), a)
        s *= 2
    h = a * hcar[...] + b
    o_ref[...] = h
    hcar[...] = h[lb - 1:lb, :]
    xpad[5:8, :] = x[lb - 3:lb, :]


def _lru_seq(proj3, cw, pvec, wbd, l, lb):
    b, seq_len, _ = proj3.shape

    def layer(shape):
        return pl.BlockSpec((None,) + shape, lambda *_: (l,) + (0,) * len(shape))

    return pl.pallas_call(
        functools.partial(_lru_seq_kernel, lb=lb),
        grid=(b, seq_len // lb),
        in_specs=[
            pl.BlockSpec((None, lb, W_BRANCH), lambda i, t: (i, t, C_LRU // W_BRANCH)),
            layer((CONV_W, W_BRANCH)), layer((8, W_BRANCH)), layer((W_BRANCH, 2 * W_BRANCH)),
        ],
        out_specs=pl.BlockSpec((None, lb, W_BRANCH), lambda i, t: (i, t, 0)),
        out_shape=jax.ShapeDtypeStruct((b, seq_len, W_BRANCH), f32),
        scratch_shapes=[
            pltpu.VMEM((8 + lb, W_BRANCH), f32),
            pltpu.VMEM((1, W_BRANCH), f32),
        ],
        compiler_params=_cparams(("parallel", "arbitrary")),
        name="lru_seq",
    )(proj3, cw, pvec, wbd)


def _lru_tok_kernel(x_ref, b0_ref, b1_ref, b2_ref, h0_ref, cw_ref, pvec_ref, wbd_ref, o_ref):
    cw = cw_ref[...]
    pvec = pvec_ref[...]
    y = (pvec[0:1, :] + b0_ref[...] * cw[0:1, :] + b1_ref[...] * cw[1:2, :] + b2_ref[...] * cw[2:3, :]
         + x_ref[...] * cw[3:4, :])
    a, b = _lru_gates(y, wbd_ref[...], pvec)
    o_ref[...] = a * h0_ref[...] + b


def _lru_tok(proj2, bufs, h0, cw, pvec, wbd, l):
    b = proj2.shape[0]

    def layer(shape):
        return pl.BlockSpec((None,) + shape, lambda *_: (l,) + (0,) * len(shape))

    row = pl.BlockSpec((b, W_BRANCH), lambda i: (0, 0))
    return pl.pallas_call(
        _lru_tok_kernel,
        grid=(1,),
        in_specs=[pl.BlockSpec((b, W_BRANCH), lambda i: (0, C_LRU // W_BRANCH)), row, row, row, row,
                  layer((CONV_W, W_BRANCH)), layer((8, W_BRANCH)), layer((W_BRANCH, 2 * W_BRANCH))],
        out_specs=row,
        out_shape=jax.ShapeDtypeStruct((b, W_BRANCH), f32),
        compiler_params=_cparams(("arbitrary",)),
        name="lru_tok",
    )(proj2, *bufs, h0, cw, pvec, wbd)


def _prep_weights(p):
    w = p["w_in"]
    pad = jnp.zeros((DEPTH, D_MODEL, LANE - GLA_GATE_RANK), w.dtype)
    w_in_p = jnp.concatenate([
        w[:, :, 0:1536], w[:, :, 2176:2688], w[:, :, 2704:4240], w[:, :, 4240:4752], w[:, :, 4752:6800],
        w[:, :, 1664:2176], w[:, :, 1536:1664], w[:, :, 2688:2704], pad], axis=2).astype(bf16)
    zl = jnp.zeros((DEPTH, RWKV_LORA, W_BRANCH), f32)
    wup = jnp.concatenate([jnp.concatenate([p["rwkv_w_up"], zl], axis=2),
                           jnp.concatenate([zl, p["rwkv_a_up"]], axis=2)], axis=1)
    zrow = jnp.zeros((DEPTH, W_BRANCH), f32)
    rwkv_vec = jnp.stack([p["rwkv_w0"], p["rwkv_a0"], p["rwkv_k_k"], p["rwkv_k_a"],
                          p["rwkv_r_k"].reshape(DEPTH, W_BRANCH), p["rwkv_gn_g"], p["rwkv_gn_b"], zrow], axis=1)
    mu = p["rwkv_mu"]
    gup = jnp.concatenate([p["gla_gk_up"], jnp.zeros((DEPTH, LANE - GLA_GATE_RANK, GLA_KW), f32)], axis=1)
    eye = jnp.eye(LRU_BLOCKS, dtype=f32)

    def bd(wb):
        return jnp.einsum("lhij,hg->lhigj", wb, eye).reshape(DEPTH, W_BRANCH, W_BRANCH)

    lru_vec = jnp.stack([p["lru_conv_b"], p["lru_b_a"], p["lru_b_x"], p["lru_lambda"],
                         zrow, zrow, zrow, zrow], axis=1)
    return dict(
        w_in_p=w_in_p,
        w_out_b=p["w_out"].astype(bf16),
        norm_g3=p["norm_g"].reshape(DEPTH, 1, D_MODEL),
        final_g2=p["final_g"].reshape(1, D_MODEL),
        rwkv=(mu[:, None, 0:1536], mu[:, None, 1536:1664], wup, rwkv_vec) + _rwkv_consts(),
        gla=(gup, p["gla_gk_b"].reshape(DEPTH, 1, GLA_KW)),
        gla_gn=p["gla_gn_g"].reshape(DEPTH, 1, W_BRANCH),
        hgrn=(p["hgrn_lb_logits"],),
        hgrn_gn=p["hgrn_gn_g"].reshape(DEPTH, 1, W_BRANCH),
        lru=(p["lru_conv_w"], lru_vec, jnp.concatenate([bd(p["lru_w_a"]), bd(p["lru_w_x"])], axis=2)),
    )


def _shift_state(proj_last):
    return jnp.concatenate([proj_last[..., 0:1536], proj_last[..., C_RWKV_LO:C_RWKV_LO + LANE]], axis=-1)


def _rwkv_state_in(s):
    lead = s.shape[:-3]
    return jnp.swapaxes(s, -3, -2).reshape(lead + (RWKV_HEAD, W_BRANCH))


def _rwkv_state_out(s):
    lead = s.shape[:-2]
    return jnp.swapaxes(s.reshape(lead + (RWKV_HEAD, RWKV_H, RWKV_HEAD)), -3, -2)


def _trunk_seq(x, mod, wts):
    b, seq_len, _ = x.shape
    t = b * seq_len
    lb = min(LB_CHUNK, seq_len)
    tm = min(TM_IN, seq_len)
    per_seq = seq_len // tm
    mod3 = mod.reshape(DEPTH * b * 3, 1, D_MODEL)
    x2 = x.reshape(t, D_MODEL)
    zeros = functools.partial(jnp.zeros, dtype=f32)
    new = []
    for l in range(DEPTH):
        mod_l = mod3[l * b * 3:(l + 1) * b * 3]
        proj2 = _inproj(x2, wts["norm_g3"], mod_l, wts["w_in_p"], l, per_seq, tm)
        proj3 = proj2.reshape(b, seq_len, PROJ_P)
        o_a, s_wkv = _rwkv_seq(proj3, zeros((b, 1, 1536)), zeros((b, 1, LANE)),
                               zeros((b, RWKV_HEAD, W_BRANCH)), wts["rwkv"], l, min(LB_RWKV, seq_len))
        o_b, st_gla = _chunk_seq("gla", proj3, wts["gla"], wts["gla_gn"], l, lb)
        o_c, st_hgrn = _chunk_seq("hgrn", proj3, wts["hgrn"], wts["hgrn_gn"], l, lb)
        o_d = _lru_seq(proj3, *wts["lru"], l, lb)
        outs = [o.reshape(t, W_BRANCH) for o in (o_a, o_b, o_c, o_d)]
        tmo = min(TM_OUT, seq_len)
        x2 = _outproj(outs, proj2, x2, mod_l, wts["w_out_b"], wts["final_g2"], l, seq_len // tmo, tmo,
                      final=(l == DEPTH - 1))
        last = proj3[:, seq_len - 1]
        new.append((
            _shift_state(last),
            _rwkv_state_out(s_wkv),
            _state_from_transposed(st_gla, GLA_H, GLA_DK),
            _state_from_transposed(st_hgrn, HGRN_H, HGRN_DK),
            proj3[:, seq_len - (CONV_W - 1):, C_LRU:C_LRU + W_BRANCH],
            o_d[:, seq_len - 1],
        ))
    return x2.reshape(b, seq_len, D_MODEL), tuple(jnp.stack([n[i] for n in new], axis=0) for i in range(6))


def _trunk_tok(x, mod, states, wts):
    b = x.shape[0]
    s_shift, s_wkv, s_gla, s_hgrn, s_conv, s_h = states
    x2 = x.reshape(b, D_MODEL)
    s_wkv_t = _rwkv_state_in(s_wkv)
    prev_rkv, prev_lo = s_shift[:, :, 0:1536], s_shift[:, :, 1536:1664]
    n_wkv = n_gla = n_hgrn = None
    new = []
    for l in range(DEPTH):
        proj2 = _inproj(x2, wts["norm_g3"], mod, wts["w_in_p"], l, None, b)
        o_a, n_wkv = _rwkv_tok(proj2, prev_rkv, prev_lo, s_wkv_t, n_wkv, wts["rwkv"], l, TOK_GROUP_RWKV)
        o_b, n_gla = _chunk_tok("gla", proj2, wts["gla"], wts["gla_gn"], s_gla, n_gla, l, TOK_GROUP)
        o_c, n_hgrn = _chunk_tok("hgrn", proj2, wts["hgrn"], wts["hgrn_gn"], s_hgrn, n_hgrn, l, TOK_GROUP)
        conv = s_conv[l]
        o_d = _lru_tok(proj2, [conv[:, 0], conv[:, 1], conv[:, 2]], s_h[l], *wts["lru"], l)
        x2 = _outproj([o_a, o_b, o_c, o_d], proj2, x2, mod, wts["w_out_b"], wts["final_g2"], l, None, b,
                      final=(l == DEPTH - 1))
        n_conv = jnp.concatenate([conv[:, 1:], proj2[:, None, C_LRU:C_LRU + W_BRANCH]], axis=1)
        new.append((_shift_state(proj2), n_conv, o_d))
    n_shift, n_conv, n_h = (jnp.stack([n[i] for n in new], axis=0) for i in range(3))
    return x2.reshape(b, 1, D_MODEL), (n_shift, _rwkv_state_out(n_wkv), n_gla, n_hgrn, n_conv, n_h)


def kernel(x_prompt, x_sample, c_prompt, c_sample, state_rwkv_shift, state_rwkv_wkv, state_gla, state_hgrn, state_lru_conv, state_lru_h, norm_g, w_ada, b_ada, w_in, w_out, rwkv_mu, rwkv_w0, rwkv_w_up, rwkv_a0, rwkv_a_up, rwkv_k_k, rwkv_k_a, rwkv_r_k, rwkv_gn_g, rwkv_gn_b, gla_gk_up, gla_gk_b, gla_gn_g, hgrn_lb_logits, hgrn_gn_g, lru_conv_w, lru_conv_b, lru_w_a, lru_b_a, lru_w_x, lru_b_x, lru_lambda, final_g):
    p = dict(norm_g=norm_g, w_in=w_in, w_out=w_out, rwkv_mu=rwkv_mu, rwkv_w0=rwkv_w0, rwkv_w_up=rwkv_w_up,
             rwkv_a0=rwkv_a0, rwkv_a_up=rwkv_a_up, rwkv_k_k=rwkv_k_k, rwkv_k_a=rwkv_k_a, rwkv_r_k=rwkv_r_k,
             rwkv_gn_g=rwkv_gn_g, rwkv_gn_b=rwkv_gn_b, gla_gk_up=gla_gk_up, gla_gk_b=gla_gk_b,
             gla_gn_g=gla_gn_g, hgrn_lb_logits=hgrn_lb_logits, hgrn_gn_g=hgrn_gn_g, lru_conv_w=lru_conv_w,
             lru_conv_b=lru_conv_b, lru_w_a=lru_w_a, lru_b_a=lru_b_a, lru_w_x=lru_w_x, lru_b_x=lru_b_x,
             lru_lambda=lru_lambda, final_g=final_g)
    wts = _prep_weights(p)
    bp = x_prompt.shape[0]
    bs = x_sample.shape[0]
    pad_rows = (-bp) % 8
    c_all = jnp.concatenate([c_prompt, jnp.zeros((pad_rows, D_MODEL), f32), c_sample], axis=0)
    mod = _ada(c_all, w_ada, b_ada)
    mod_p = mod[:, 0:bp]
    mod_s = mod[:, bp + pad_rows:bp + pad_rows + bs]
    y_p, st_p = _trunk_seq(x_prompt, mod_p, wts)
    states = (state_rwkv_shift, state_rwkv_wkv, state_gla, state_hgrn, state_lru_conv, state_lru_h)
    y_s, st_s = _trunk_tok(x_sample, mod_s, states, wts)
    return (y_p, y_s) + st_p + st_s
```

```python
import functools

import jax
import jax.numpy as jnp
from jax import lax
from jax.experimental import pallas as pl
from jax.experimental.pallas import tpu as pltpu

f32 = jnp.float32
bf16 = jnp.bfloat16

D_MODEL = 2048
DEPTH = 4
W_BRANCH = 512
EPS = 1e-6

RWKV_H = 8
RWKV_HEAD = 64
RWKV_LORA = 64
RWKV_PROJ = 3 * W_BRANCH + 2 * RWKV_LORA
RWKV_GN_EPS = 64e-5

GLA_H = 4
GLA_DK = 64
GLA_DV = 128
GLA_KW = GLA_H * GLA_DK
GLA_GATE_RANK = 16
GLA_GATE_NORM = 16.0

HGRN_H = 4
HGRN_DK = 128
HGRN_DV = 128
LB_FLOOR = 1e-30

LRU_BLOCKS = 8
LRU_BLOCK = 64
LRU_C = 8.0
CONV_W = 4

CHUNK = 32

C_R, C_K, C_V = 0, 512, 1024
C_GLA_V = 1536
C_HQ, C_HF, C_HI = 2048, 2560, 3072
C_LRU = 3584
C_GATE = 4096
C_GLA_Q, C_GLA_K = 6144, 6400
C_RWKV_LO = 6656
C_GLA_LO = 6784
PROJ_P = 6912

LANE = 128
MXU_TILE = 256
VMEM_LIMIT = 56 * 1024 * 1024

TM_IN = 1024
TN_IN = 768
TM_OUT = 256
LB_CHUNK = 256
LB_RWKV = 128
TOK_GROUP = 16
TOK_GROUP_RWKV = 8

HIGHEST = lax.Precision.HIGHEST


def _cparams(sem):
    return pltpu.CompilerParams(dimension_semantics=sem, vmem_limit_bytes=VMEM_LIMIT)


def _dotf(a, b):
    return jnp.dot(a, b, preferred_element_type=f32, precision=HIGHEST)


def _prec(a):
    return HIGHEST if a.dtype == f32 else None


def _dot_nt(a, b):
    return lax.dot_general(a, b, (((1,), (1,)), ((), ())), preferred_element_type=f32, precision=_prec(a))


def _dot_tn(a, b):
    return lax.dot_general(a, b, (((0,), (0,)), ((), ())), preferred_element_type=f32, precision=_prec(a))


def _split(x, pieces):
    out = []
    for i in range(pieces):
        part = x.astype(bf16)
        out.append(part)
        if i + 1 < pieces:
            x = x - part.astype(f32)
    return out


def _half_dot(x, ones):
    d = functools.partial(jnp.dot, preferred_element_type=f32)
    kh = ones.shape[0]
    if x.shape[1] == kh:
        return d(x, ones)
    return jnp.concatenate([d(x[:, 0:kh], ones), d(x[:, kh:2 * kh], ones)], axis=1)


def _seg_sum(x, ones, pieces=2):
    return sum(_half_dot(part, ones) for part in _split(x, pieces))


def _ones_seg(ones, x, pieces=3):
    d = functools.partial(jnp.dot, preferred_element_type=f32)
    return sum(d(ones, part) for part in _split(x, pieces))


def _sigmoid(x):
    return jax.nn.sigmoid(x)


def _softplus(x):
    return jnp.maximum(x, 0.0) + jnp.log1p(jnp.exp(-jnp.abs(x)))


def _log_sigmoid(x):
    return -_softplus(-x)


def _ada_kernel(c_ref, w_ref, b_ref, o_ref):
    c = c_ref[...]
    s = (c * _sigmoid(c)).astype(bf16)
    o_ref[...] = jnp.dot(s, w_ref[...].astype(bf16), preferred_element_type=f32) + b_ref[...]


def _ada(c_all, w_ada, b_ada):
    rows = c_all.shape[0]
    tn = 512
    n = w_ada.shape[2]
    return pl.pallas_call(
        _ada_kernel,
        grid=(DEPTH, n // tn),
        in_specs=[
            pl.BlockSpec((rows, D_MODEL), lambda l, j: (0, 0)),
            pl.BlockSpec((None, D_MODEL, tn), lambda l, j: (l, 0, j)),
            pl.BlockSpec((None, 1, tn), lambda l, j: (l, 0, j)),
        ],
        out_specs=pl.BlockSpec((None, rows, tn), lambda l, j: (l, 0, j)),
        out_shape=jax.ShapeDtypeStruct((DEPTH, rows, n), f32),
        compiler_params=_cparams(("parallel", "parallel")),
        name="ada_mod",
    )(c_all, w_ada, b_ada.reshape(DEPTH, 1, n))


def _inproj_kernel(x_ref, g_ref, sc_ref, sh_ref, w_ref, o_ref, h_ref):
    @pl.when(pl.program_id(1) == 0)
    def _():
        x = x_ref[...]
        ms = jnp.mean(x * x, axis=-1, keepdims=True)
        h = x * lax.rsqrt(ms + EPS) * g_ref[...]
        h = h * (1.0 + sc_ref[...]) + sh_ref[...]
        h_ref[...] = h.astype(bf16)

    o_ref[...] = jnp.dot(h_ref[...], w_ref[...], preferred_element_type=f32)


def _mod_spec(l, which, per_seq_tiles, tm):
    if per_seq_tiles is None:
        return pl.BlockSpec((None, tm, D_MODEL), lambda i, *_: (l, 0, which))
    return pl.BlockSpec((None, 1, D_MODEL), lambda i, *_: (i // per_seq_tiles * 3 + which, 0, 0))


def _inproj(x2, norm_g3, mod, w_in_p, l, per_seq_tiles, tm):
    t = x2.shape[0]
    tn = TN_IN
    return pl.pallas_call(
        _inproj_kernel,
        grid=(t // tm, PROJ_P // tn),
        in_specs=[
            pl.BlockSpec((tm, D_MODEL), lambda i, j: (i, 0)),
            pl.BlockSpec((None, 1, D_MODEL), lambda i, j: (l, 0, 0)),
            _mod_spec(l, 1, per_seq_tiles, tm),
            _mod_spec(l, 0, per_seq_tiles, tm),
            pl.BlockSpec((None, D_MODEL, tn), lambda i, j: (l, 0, j)),
        ],
        out_specs=pl.BlockSpec((tm, tn), lambda i, j: (i, j)),
        out_shape=jax.ShapeDtypeStruct((t, PROJ_P), f32),
        scratch_shapes=[pltpu.VMEM((tm, D_MODEL), bf16)],
        compiler_params=_cparams(("parallel", "arbitrary")),
        name="in_proj",
    )(x2, norm_g3, mod, mod, w_in_p)


def _outproj_kernel(oa_ref, ob_ref, oc_ref, od_ref, pz_ref, x_ref, gate_ref, w_ref, fg_ref, o_ref, *, final):
    z = pz_ref[...]
    o = jnp.concatenate([oa_ref[...], ob_ref[...], oc_ref[...], od_ref[...]], axis=1)
    o = o * (z * _sigmoid(z))
    y = jnp.dot(o.astype(bf16), w_ref[...], preferred_element_type=f32)
    xn = x_ref[...] + gate_ref[...] * y
    if final:
        ms = jnp.mean(xn * xn, axis=-1, keepdims=True)
        xn = xn * lax.rsqrt(ms + EPS) * fg_ref[...]
    o_ref[...] = xn


def _outproj(outs, proj2, x2, mod, w_out_b, final_g2, l, per_seq_tiles, tm, final):
    t = x2.shape[0]
    mix_spec = pl.BlockSpec((tm, W_BRANCH), lambda i: (i, 0))
    return pl.pallas_call(
        functools.partial(_outproj_kernel, final=final),
        grid=(t // tm,),
        in_specs=[
            mix_spec, mix_spec, mix_spec, mix_spec,
            pl.BlockSpec((tm, D_MODEL), lambda i: (i, C_GATE // D_MODEL)),
            pl.BlockSpec((tm, D_MODEL), lambda i: (i, 0)),
            _mod_spec(l, 2, per_seq_tiles, tm),
            pl.BlockSpec((None, D_MODEL, D_MODEL), lambda i: (l, 0, 0)),
            pl.BlockSpec((1, D_MODEL), lambda i: (0, 0)),
        ],
        out_specs=pl.BlockSpec((tm, D_MODEL), lambda i: (i, 0)),
        out_shape=jax.ShapeDtypeStruct((t, D_MODEL), f32),
        compiler_params=_cparams(("parallel",)),
        name="out_proj",
    )(*outs, proj2, x2, mod, w_out_b, final_g2)


def _rwkv_prologue(r, k, v, lo, pr, pk, pv, plo, mu_rkv, mu_lo, wup, pvec, bo):
    w0, a0, k_k, k_a, r_k = (pvec[i:i + 1, :] for i in range(5))
    xr = r + (pr - r) * mu_rkv[:, 0:512]
    xk = k + (pk - k) * mu_rkv[:, 512:1024]
    xv = v + (pv - v) * mu_rkv[:, 1024:1536]
    xlo = lo + (plo - lo) * mu_lo
    lane = lax.broadcasted_iota(jnp.int32, xlo.shape, 1)
    act = jnp.where(lane < RWKV_LORA, jnp.tanh(xlo), xlo)
    up = _dotf(act, wup)
    w_raw = -_softplus(-(w0 + up[:, 0:512])) - 0.5
    ew = jnp.exp(-jnp.exp(w_raw))
    a = _sigmoid(a0 + up[:, 512:1024])
    kk = xk * k_k
    kk = kk / jnp.maximum(jnp.sqrt(_seg_sum(kk * kk, bo)), 1e-12)
    kh = xk * (1.0 + (a - 1.0) * k_a)
    alp = kk * a
    ar = _seg_sum(alp * xr, bo)
    return dict(
        kap=kk, ew=ew, alp=alp, kh=kh, vv=xv,
        wr=ew * xr - ar * kk,
        kr=_seg_sum(kh * xr, bo),
        bonus=_seg_sum(xr * kh * r_k, bo) * xv,
    )


_RWKV_STEP_KEYS = ("kap", "ew", "alp", "kh", "vv", "wr", "kr")


def _rwkv_step(s, kap, ew, alp, kh, vv, wr, kr, bo, idt):
    sk_hi, sk_mid = _split(s * kap, 2)
    x = jnp.concatenate([sk_hi, sk_mid, (s * wr).astype(bf16), idt.astype(bf16) * vv.astype(bf16)], axis=0)
    red = _half_dot(x, bo)
    sk = red[0:64] + red[64:128]
    s_new = s * ew - sk * alp + red[192:256] * kh
    o_row = jnp.sum(red[128:192] * idt, axis=0, keepdims=True) + vv * kr
    return s_new, o_row


def _rwkv_epilogue(o, bonus, pvec, bo):
    gn_g, gn_b = pvec[5:6, :], pvec[6:7, :]
    mu = _seg_sum(o, bo) * (1.0 / RWKV_HEAD)
    d = o - mu
    var = _seg_sum(d * d, bo) * (1.0 / RWKV_HEAD)
    return d * lax.rsqrt(var + RWKV_GN_EPS) * gn_g + gn_b + bonus


def _rwkv_seq_kernel(r_ref, k_ref, v_ref, lo_ref, prkv_ref, plo_ref, sin_ref, mu_rkv_ref, mu_lo_ref, wup_ref,
                     pvec_ref, bo_ref, idt_ref, o_ref, sout_ref,
                     s_sc, crkv_sc, clo_sc, kap_sc, ew_sc, alp_sc, kh_sc, vv_sc, wr_sc, kr_sc, oraw_sc,
                     bonus_sc, *, g_seqs, lb):
    tb = pl.program_id(1)

    @pl.when(tb == 0)
    def _():
        s_sc[...] = sin_ref[...]
        crkv_sc[...] = prkv_ref[...]
        clo_sc[...] = plo_ref[...]

    bo = bo_ref[...]
    idt = idt_ref[...]
    pvec = pvec_ref[...]
    step_sc = dict(kap=kap_sc, ew=ew_sc, alp=alp_sc, kh=kh_sc, vv=vv_sc, wr=wr_sc, kr=kr_sc)

    row0 = lax.broadcasted_iota(jnp.int32, (lb, 1), 0) == 0
    for g in range(g_seqs):
        cur = [r_ref[g], k_ref[g], v_ref[g]]
        lo = lo_ref[g]
        carry = crkv_sc[g]
        prev = [jnp.where(row0, carry[:, i * 512:(i + 1) * 512], pltpu.roll(c, 1, 0)) for i, c in enumerate(cur)]
        plo = jnp.where(row0, clo_sc[g], pltpu.roll(lo, 1, 0))
        res = _rwkv_prologue(cur[0], cur[1], cur[2], lo, prev[0], prev[1], prev[2], plo,
                             mu_rkv_ref[...], mu_lo_ref[...], wup_ref[...], pvec, bo)
        for key in _RWKV_STEP_KEYS:
            step_sc[key][g] = res[key]
        bonus_sc[g] = res["bonus"]
        for i, c in enumerate(cur):
            crkv_sc[g, :, i * 512:(i + 1) * 512] = c[lb - 1:lb, :]
        clo_sc[g] = lo[lb - 1:lb, :]

    def body(t, carry):
        for g in range(g_seqs):
            rows = [step_sc[key][g, pl.ds(t, 1), :] for key in _RWKV_STEP_KEYS]
            s_new, o_row = _rwkv_step(s_sc[g], *rows, bo, idt)
            s_sc[g] = s_new
            oraw_sc[g, pl.ds(t, 1), :] = o_row
        return carry

    lax.fori_loop(0, lb, body, 0)

    for g in range(g_seqs):
        o_ref[g] = _rwkv_epilogue(oraw_sc[g], bonus_sc[g], pvec, bo)

    @pl.when(tb == pl.num_programs(1) - 1)
    def _():
        sout_ref[...] = s_sc[...]


def _rwkv_tok_kernel(r_ref, k_ref, v_ref, lo_ref, prkv_ref, plo_ref, sin_ref, mu_rkv_ref, mu_lo_ref, wup_ref,
                     pvec_ref, bo_ref, idt_ref, *rest, g_rows):
    o_ref, sout_ref = rest[-2:]
    bo = bo_ref[...]
    idt = idt_ref[...]
    pvec = pvec_ref[...]
    prkv = prkv_ref[...]
    res = _rwkv_prologue(r_ref[...], k_ref[...], v_ref[...], lo_ref[...],
                         prkv[:, 0:512], prkv[:, 512:1024], prkv[:, 1024:1536], plo_ref[...],
                         mu_rkv_ref[...], mu_lo_ref[...], wup_ref[...], pvec, bo)
    o_rows = []
    for g in range(g_rows):
        rows = [res[key][g:g + 1, :] for key in _RWKV_STEP_KEYS]
        s_new, o_row = _rwkv_step(sin_ref[g], *rows, bo, idt)
        sout_ref[g] = s_new
        o_rows.append(o_row)
    o_ref[...] = _rwkv_epilogue(jnp.concatenate(o_rows, axis=0), res["bonus"], pvec, bo)


def _rwkv_consts():
    i = jnp.arange(W_BRANCH)
    j = jnp.arange(MXU_TILE)
    bo = (j[:, None] // RWKV_HEAD == j[None, :] // RWKV_HEAD).astype(bf16)
    idt = (jnp.arange(RWKV_HEAD)[:, None] == (i[None, :] % RWKV_HEAD)).astype(f32)
    return bo, idt


def _rwkv_weight_specs(l):
    def cs(shape):
        return pl.BlockSpec((None,) + shape, lambda *_: (l,) + (0,) * len(shape))

    def const(shape):
        return pl.BlockSpec(shape, lambda *_: (0,) * len(shape))

    return [cs((1, 1536)), cs((1, LANE)), cs((LANE, 1024)), cs((8, W_BRANCH)),
            const((MXU_TILE, MXU_TILE)), const((RWKV_HEAD, W_BRANCH))]


def _rwkv_seq(proj3, prev_rkv, prev_lo, s_in, wts, l, lb):
    b, seq_len, _ = proj3.shape
    g = b
    nt = seq_len // lb

    def col(width, off):
        return pl.BlockSpec((g, lb, width), lambda i, t: (i, t, off // width))

    tok_sc = pltpu.VMEM((g, lb, W_BRANCH), f32)
    out = pl.pallas_call(
        functools.partial(_rwkv_seq_kernel, g_seqs=g, lb=lb),
        grid=(b // g, nt),
        in_specs=[
            col(512, C_R), col(512, C_K), col(512, C_V), col(LANE, C_RWKV_LO),
            pl.BlockSpec((g, 1, 1536), lambda i, t: (i, 0, 0)),
            pl.BlockSpec((g, 1, LANE), lambda i, t: (i, 0, 0)),
            pl.BlockSpec((g, RWKV_HEAD, W_BRANCH), lambda i, t: (i, 0, 0)),
        ] + _rwkv_weight_specs(l),
        out_specs=[
            pl.BlockSpec((g, lb, W_BRANCH), lambda i, t: (i, t, 0)),
            pl.BlockSpec((g, RWKV_HEAD, W_BRANCH), lambda i, t: (i, 0, 0)),
        ],
        out_shape=[
            jax.ShapeDtypeStruct((b, seq_len, W_BRANCH), f32),
            jax.ShapeDtypeStruct((b, RWKV_HEAD, W_BRANCH), f32),
        ],
        scratch_shapes=[
            pltpu.VMEM((g, RWKV_HEAD, W_BRANCH), f32),
            pltpu.VMEM((g, 1, 1536), f32),
            pltpu.VMEM((g, 1, LANE), f32),
        ] + [tok_sc] * 9,
        compiler_params=_cparams(("parallel", "arbitrary")),
        name="rwkv_seq",
    )(proj3, proj3, proj3, proj3, prev_rkv, prev_lo, s_in, *wts)
    return out


def _rwkv_tok(proj2, prev_rkv, prev_lo, s_all, acc, wts, l, g):
    b = proj2.shape[0]

    def col(width, off):
        return pl.BlockSpec((g, width), lambda i: (i, off // width))

    state_spec = pl.BlockSpec((None, g, RWKV_HEAD, W_BRANCH), lambda i: (l, i, 0, 0))
    in_specs = [
        col(512, C_R), col(512, C_K), col(512, C_V), col(LANE, C_RWKV_LO),
        pl.BlockSpec((None, g, 1536), lambda i: (l, i, 0)),
        pl.BlockSpec((None, g, LANE), lambda i: (l, i, 0)),
        state_spec,
    ] + _rwkv_weight_specs(l)
    args = [proj2, proj2, proj2, proj2, prev_rkv, prev_lo, s_all, *wts]
    aliases = {}
    if acc is not None:
        in_specs.append(pl.BlockSpec(memory_space=pl.ANY))
        aliases = {len(args): 1}
        args.append(acc)
    return pl.pallas_call(
        functools.partial(_rwkv_tok_kernel, g_rows=g),
        grid=(b // g,),
        in_specs=in_specs,
        out_specs=[pl.BlockSpec((g, W_BRANCH), lambda i: (i, 0)), state_spec],
        out_shape=[
            jax.ShapeDtypeStruct((b, W_BRANCH), f32),
            jax.ShapeDtypeStruct(s_all.shape, f32),
        ],
        input_output_aliases=aliases,
        compiler_params=_cparams(("parallel",)),
        name="rwkv_tok",
    )(*args)


def _gla_inputs(q_ref, k_ref, v_ref, lo_ref, gup_ref, gb_ref):
    q = q_ref[...] * (GLA_DK ** -0.5)
    z = _dotf(lo_ref[...], gup_ref[...]) + gb_ref[...]
    g = _log_sigmoid(z) * (1.0 / GLA_GATE_NORM)
    return q, k_ref[...], v_ref[...], g


def _hgrn_lb(logits, l):
    m = jnp.max(logits, axis=0, keepdims=True)
    e = jnp.exp(logits - m)
    sm = e / jnp.sum(e, axis=0, keepdims=True)
    lb = jnp.zeros_like(sm[0:1, :])
    for i in range(1, l + 1):
        lb = lb + sm[i:i + 1, :]
    return lb


def _hgrn_inputs(q_ref, f_ref, i_ref, logits_ref, l):
    lb = _hgrn_lb(logits_ref[...], l)
    f_lo = f_ref[...]
    a = jnp.log(jnp.maximum(lb, LB_FLOOR))
    b = jnp.log1p(-lb) + _log_sigmoid(f_lo)
    logf = jnp.maximum(a, b) + jnp.log1p(jnp.exp(-jnp.abs(a - b)))
    k = (1.0 - lb) * _sigmoid(-f_lo)
    return q_ref[...], k, i_ref[...], logf


def _chunk_core(q, k, v, g, lt_ref, bokv_ref, bdm_ref, bon_ref, gn_ref, st_sc, kpad, bpad, vpad, lb):
    bc = _ones_seg(lt_ref[...], g)
    nc = lb // CHUNK
    f = q.shape[1]
    kpad[...] = k.reshape(nc, CHUNK, f)
    bpad[...] = bc.reshape(nc, CHUNK, f)
    vpad[...] = v.reshape(nc, CHUNK, W_BRANCH)
    q3 = q.reshape(nc, CHUNK, f)
    bc3 = bc.reshape(nc, CHUNK, f)
    bokv = bokv_ref[...]

    o3 = None
    for r0 in range(0, CHUNK, 8):
        rows = CHUNK - r0
        q_s, bc_s = q3[:, r0:, :], bc3[:, r0:, :]
        rowc = lax.broadcasted_iota(jnp.int32, (1, rows, 1), 1) + r0
        acc = jnp.zeros((nc, rows, W_BRANCH), f32)
        for j in range(r0, r0 + 8):
            kj, bj, vj = kpad[:, j:j + 1, :], bpad[:, j:j + 1, :], vpad[:, j:j + 1, :]
            z = jnp.where(rowc >= j, q_s * kj * jnp.exp(bc_s - bj), 0.0)
            att = _seg_sum(z.reshape(nc * rows, f), bokv, pieces=1)
            acc = acc + att.reshape(nc, rows, W_BRANCH) * vj
        if r0:
            acc = jnp.concatenate([jnp.zeros((nc, r0, W_BRANCH), f32), acc], axis=1)
        o3 = acc if o3 is None else o3 + acc
    o = o3.reshape(lb, W_BRANCH)

    bdm = bdm_ref[...]
    st = st_sc[...]
    outs = []
    for c in range(lb // CHUNK):
        sl = slice(c * CHUNK, (c + 1) * CHUNK)
        bcc = bc[sl]
        blast = bcc[CHUNK - 1:CHUNK, :]
        qe = q[sl] * jnp.exp(bcc)
        outs.append(o[sl] + _dot_nt(qe.astype(bf16), st.astype(bf16)))
        ke = k[sl] * jnp.exp(blast - bcc)
        st = st * jnp.exp(blast) + _dot_tn(v[sl].astype(bf16), ke.astype(bf16)) * bdm
    st_sc[...] = st
    o = jnp.concatenate(outs, axis=0)
    ms = _seg_sum(o * o, bon_ref[...]) * (1.0 / LANE)
    return o * lax.rsqrt(ms + EPS) * gn_ref[...]


def _chunk_seq_kernel(*refs, kind, l, lb):
    n_in = 6 if kind == "gla" else 4
    ins = refs[:n_in]
    lt_ref, bokv_ref, bdm_ref, bon_ref, gn_ref, o_ref, stout_ref, st_sc, kpad, bpad, vpad = refs[n_in:]
    tb = pl.program_id(1)

    @pl.when(tb == 0)
    def _():
        st_sc[...] = jnp.zeros_like(st_sc)

    if kind == "gla":
        q, k, v, g = _gla_inputs(*ins)
    else:
        q, k, v, g = _hgrn_inputs(*ins, l)
    o_ref[...] = _chunk_core(q, k, v, g, lt_ref, bokv_ref, bdm_ref, bon_ref, gn_ref, st_sc, kpad, bpad, vpad, lb)

    @pl.when(tb == pl.num_programs(1) - 1)
    def _():
        stout_ref[...] = st_sc[...]


def _chunk_consts(h, dk, lb):
    f = h * dk
    i = jnp.arange(lb)
    lt = ((i[:, None] // CHUNK == i[None, :] // CHUNK) & (i[:, None] >= i[None, :])).astype(bf16)
    fi = jnp.arange(f)
    oi = jnp.arange(W_BRANCH)
    bokv = (fi[:, None] // dk == oi[None, :] // LANE).astype(bf16)
    bdm = (oi[:, None] // LANE == fi[None, :] // dk).astype(f32)
    ti = jnp.arange(MXU_TILE)
    bon = (ti[:, None] // LANE == ti[None, :] // LANE).astype(bf16)
    if f == W_BRANCH:
        bokv = bon
    return lt, bokv, bdm, bon


def _chunk_seq(kind, proj3, extra, gn3, l, lb):
    b, seq_len, _ = proj3.shape
    h, dk = (GLA_H, GLA_DK) if kind == "gla" else (HGRN_H, HGRN_DK)
    f = h * dk
    nt = seq_len // lb

    def col(width, off):
        return pl.BlockSpec((None, lb, width), lambda i, t: (i, t, off // width))

    def const(shape):
        return pl.BlockSpec(shape, lambda *_: (0,) * len(shape))

    def layer(shape):
        return pl.BlockSpec((None,) + shape, lambda *_: (l,) + (0,) * len(shape))

    if kind == "gla":
        gup, gb = extra
        in_specs = [col(GLA_KW, C_GLA_Q), col(GLA_KW, C_GLA_K), col(W_BRANCH, C_GLA_V), col(LANE, C_GLA_LO),
                    layer((LANE, GLA_KW)), layer((1, GLA_KW))]
        args = [proj3, proj3, proj3, proj3, gup, gb]
    else:
        (logits,) = extra
        in_specs = [col(W_BRANCH, C_HQ), col(W_BRANCH, C_HF), col(W_BRANCH, C_HI), const((DEPTH, W_BRANCH))]
        args = [proj3, proj3, proj3, logits]
    consts = _chunk_consts(h, dk, lb)
    in_specs += [const(c.shape) for c in consts] + [layer((1, W_BRANCH))]
    return pl.pallas_call(
        functools.partial(_chunk_seq_kernel, kind=kind, l=l, lb=lb),
        grid=(b, nt),
        in_specs=in_specs,
        out_specs=[
            pl.BlockSpec((None, lb, W_BRANCH), lambda i, t: (i, t, 0)),
            pl.BlockSpec((None, W_BRANCH, f), lambda i, t: (i, 0, 0)),
        ],
        out_shape=[
            jax.ShapeDtypeStruct((b, seq_len, W_BRANCH), f32),
            jax.ShapeDtypeStruct((b, W_BRANCH, f), f32),
        ],
        scratch_shapes=[
            pltpu.VMEM((W_BRANCH, f), f32),
            pltpu.VMEM((lb // CHUNK, CHUNK, f), f32),
            pltpu.VMEM((lb // CHUNK, CHUNK, f), f32),
            pltpu.VMEM((lb // CHUNK, CHUNK, W_BRANCH), f32),
        ],
        compiler_params=_cparams(("parallel", "arbitrary")),
        name=kind + "_seq",
    )(*args, *consts, gn3)


def _state_from_transposed(st, h, dk):
    blocks = [st[:, i * LANE:(i + 1) * LANE, i * dk:(i + 1) * dk] for i in range(h)]
    return jnp.swapaxes(jnp.stack(blocks, axis=1), 2, 3)


def _block_diag_rows(x, g, dk):
    row = lax.broadcasted_iota(jnp.int32, x.shape, 0)
    lane = lax.broadcasted_iota(jnp.int32, x.shape, 1)
    per = LANE // dk
    pieces = []
    for j in range(g // per):
        keep = row == (j * per + lane // dk)
        pieces.append(jnp.where(keep, x, 0.0))
    return jnp.concatenate(pieces, axis=1)


def _chunk_tok_kernel(*refs, kind, l, g, has_acc):
    ins, (o_ref, sout_ref) = refs[:-2], refs[-2:]
    if has_acc:
        ins = ins[:-1]
    if kind == "gla":
        q_ref, k_ref, v_ref, lo_ref, gup_ref, gb_ref, s_ref, gn_ref = ins
        odd = (pl.program_id(1) % 2) == 1
        lane = lax.broadcasted_iota(jnp.int32, (g, LANE), 1)
        own = (lane >= GLA_DK) == odd

        def pick(x):
            return jnp.where(own, x, pltpu.roll(x, GLA_DK, 1))

        q = pick(q_ref[...]) * (GLA_DK ** -0.5)
        k = pick(k_ref[...])
        z = _dotf(lo_ref[...], gup_ref[...]) + gb_ref[...]
        dec = jnp.exp(pick(_log_sigmoid(z) * (1.0 / GLA_GATE_NORM)))
        v = v_ref[...]
        dk = GLA_DK
    else:
        q_ref, f_ref, i_ref, logits_ref, s_ref, gn_ref = ins
        q, k, v, logf = _hgrn_inputs(q_ref, f_ref, i_ref, logits_ref, l)
        dec = jnp.exp(logf)
        dk = HGRN_DK
    s = s_ref[...].reshape(g * dk, LANE)
    ones = jnp.ones((g, LANE), bf16)
    dcol = sum(_dot_tn(part, ones) for part in _split(_block_diag_rows(dec, g, dk), 2))
    k_hi, k_lo = _split(_block_diag_rows(k, g, dk), 2)
    v_hi, v_lo = _split(v, 2)
    s_new = s * dcol + (_dot_tn(k_hi, v_hi) + _dot_tn(k_hi, v_lo) + _dot_tn(k_lo, v_hi))
    o = jnp.dot(_block_diag_rows(q, g, dk).astype(bf16), s_new.astype(bf16), preferred_element_type=f32)
    ms = jnp.mean(o * o, axis=-1, keepdims=True)
    o_ref[...] = o * lax.rsqrt(ms + EPS) * gn_ref[...]
    sout_ref[...] = s_new.reshape(g, dk, LANE)


def _chunk_tok(kind, proj2, extra, gn3, s_all, acc, l, g):
    b = proj2.shape[0]
    h, dk = (GLA_H, GLA_DK) if kind == "gla" else (HGRN_H, HGRN_DK)

    def head_col(off):
        per = LANE // dk
        return pl.BlockSpec((g, LANE), lambda i, j: (i, off // LANE + j // per))

    if kind == "gla":
        gup, gb = extra
        in_specs = [head_col(C_GLA_Q), head_col(C_GLA_K),
                    pl.BlockSpec((g, LANE), lambda i, j: (i, C_GLA_V // LANE + j)),
                    pl.BlockSpec((g, LANE), lambda i, j: (i, C_GLA_LO // LANE)),
                    pl.BlockSpec((None, LANE, LANE), lambda i, j: (l, 0, j // 2)),
                    pl.BlockSpec((None, 1, LANE), lambda i, j: (l, 0, j // 2))]
        args = [proj2, proj2, proj2, proj2, gup, gb]
    else:
        (logits,) = extra
        in_specs = [head_col(C_HQ), head_col(C_HF), head_col(C_HI),
                    pl.BlockSpec((DEPTH, LANE), lambda i, j: (0, j))]
        args = [proj2, proj2, proj2, logits]
    state_spec = pl.BlockSpec((None, g, None, dk, LANE), lambda i, j: (l, i, j, 0, 0))
    in_specs += [state_spec, pl.BlockSpec((None, 1, LANE), lambda i, j: (l, 0, j))]
    args += [s_all, gn3]
    aliases = {}
    if acc is not None:
        in_specs.append(pl.BlockSpec(memory_space=pl.ANY))
        aliases = {len(args): 1}
        args.append(acc)
    return pl.pallas_call(
        functools.partial(_chunk_tok_kernel, kind=kind, l=l, g=g, has_acc=acc is not None),
        grid=(b // g, h),
        in_specs=in_specs,
        out_specs=[pl.BlockSpec((g, LANE), lambda i, j: (i, j)), state_spec],
        out_shape=[
            jax.ShapeDtypeStruct((b, W_BRANCH), f32),
            jax.ShapeDtypeStruct(s_all.shape, f32),
        ],
        input_output_aliases=aliases,
        compiler_params=_cparams(("parallel", "parallel")),
        name=kind + "_tok",
    )(*args)


def _lru_gate_dot(y_hi, y_lo, w_ref, col0):
    d = functools.partial(jnp.dot, preferred_element_type=f32)
    halves = []
    for h0 in range(0, W_BRANCH, MXU_TILE):
        w_hi = w_ref[0, h0:h0 + MXU_TILE, col0 + h0:col0 + h0 + MXU_TILE]
        w_lo = w_ref[1, h0:h0 + MXU_TILE, col0 + h0:col0 + h0 + MXU_TILE]
        a, b = y_hi[:, h0:h0 + MXU_TILE], y_lo[:, h0:h0 + MXU_TILE]
        halves.append(d(a, w_hi) + d(a, w_lo) + d(b, w_hi))
    return jnp.concatenate(halves, axis=1)


def _lru_gates(y, w_ref, pvec):
    b_a, b_x, lam = pvec[1:2, :], pvec[2:3, :], pvec[3:4, :]
    y_hi, y_lo = _split(y, 2)
    r = _sigmoid(_lru_gate_dot(y_hi, y_lo, w_ref, 0) + b_a)
    ig = _sigmoid(_lru_gate_dot(y_hi, y_lo, w_ref, W_BRANCH) + b_x)
    log_a = -LRU_C * r * _softplus(-lam)
    a = jnp.exp(log_a)
    one_m_a2 = -jnp.tanh(log_a) * (jnp.exp(2.0 * log_a) + 1.0)
    b = jnp.sqrt(one_m_a2) * (ig * y)
    return a, b


def _lru_seq_kernel(x_ref, cw_ref, pvec_ref, wbd_ref, o_ref, xpad, hcar, *, lb):
    tb = pl.program_id(1)

    @pl.when(tb == 0)
    def _():
        xpad[0:8, :] = jnp.zeros((8, W_BRANCH), f32)
        hcar[...] = jnp.zeros_like(hcar)

    x = x_ref[...]
    xpad[8:8 + lb, :] = x
    cw = cw_ref[...]
    pvec = pvec_ref[...]
    y = pvec[0:1, :] + x * cw[3:4, :]
    for j in range(CONV_W - 1):
        y = y + xpad[5 + j:5 + j + lb, :] * cw[j:j + 1, :]
    a, b = _lru_gates(y, wbd_ref, pvec)
    row = lax.broadcasted_iota(jnp.int32, (lb, 1), 0)
    s = 1
    while s < lb:
        m = row >= s
        b = jnp.where(m, a * pltpu.roll(b, s, 0) + b, b)
        a = jnp.where(m, a * pltpu.roll(a, s, 0), a)
        s *= 2
    h = a * hcar[...] + b
    o_ref[...] = h
    hcar[...] = h[lb - 1:lb, :]
    xpad[5:8, :] = x[lb - 3:lb, :]


def _lru_seq(proj3, cw, pvec, wbd, l, lb):
    b, seq_len, _ = proj3.shape

    def layer(shape):
        return pl.BlockSpec((None,) + shape, lambda *_: (l,) + (0,) * len(shape))

    return pl.pallas_call(
        functools.partial(_lru_seq_kernel, lb=lb),
        grid=(b, seq_len // lb),
        in_specs=[
            pl.BlockSpec((None, lb, W_BRANCH), lambda i, t: (i, t, C_LRU // W_BRANCH)),
            layer((CONV_W, W_BRANCH)), layer((8, W_BRANCH)), layer((2, W_BRANCH, 2 * W_BRANCH)),
        ],
        out_specs=pl.BlockSpec((None, lb, W_BRANCH), lambda i, t: (i, t, 0)),
        out_shape=jax.ShapeDtypeStruct((b, seq_len, W_BRANCH), f32),
        scratch_shapes=[
            pltpu.VMEM((8 + lb, W_BRANCH), f32),
            pltpu.VMEM((1, W_BRANCH), f32),
        ],
        compiler_params=_cparams(("parallel", "arbitrary")),
        name="lru_seq",
    )(proj3, cw, pvec, wbd)


def _lru_tok_kernel(x_ref, b0_ref, b1_ref, b2_ref, h0_ref, cw_ref, pvec_ref, wbd_ref, o_ref):
    cw = cw_ref[...]
    pvec = pvec_ref[...]
    y = (pvec[0:1, :] + b0_ref[...] * cw[0:1, :] + b1_ref[...] * cw[1:2, :] + b2_ref[...] * cw[2:3, :]
         + x_ref[...] * cw[3:4, :])
    a, b = _lru_gates(y, wbd_ref, pvec)
    o_ref[...] = a * h0_ref[...] + b


def _lru_tok(proj2, bufs, h0, cw, pvec, wbd, l):
    b = proj2.shape[0]

    def layer(shape):
        return pl.BlockSpec((None,) + shape, lambda *_: (l,) + (0,) * len(shape))

    row = pl.BlockSpec((b, W_BRANCH), lambda i: (0, 0))
    return pl.pallas_call(
        _lru_tok_kernel,
        grid=(1,),
        in_specs=[pl.BlockSpec((b, W_BRANCH), lambda i: (0, C_LRU // W_BRANCH)), row, row, row, row,
                  layer((CONV_W, W_BRANCH)), layer((8, W_BRANCH)), layer((2, W_BRANCH, 2 * W_BRANCH))],
        out_specs=row,
        out_shape=jax.ShapeDtypeStruct((b, W_BRANCH), f32),
        compiler_params=_cparams(("arbitrary",)),
        name="lru_tok",
    )(proj2, *bufs, h0, cw, pvec, wbd)


def _prep_weights(p):
    w = p["w_in"]
    pad = jnp.zeros((DEPTH, D_MODEL, LANE - GLA_GATE_RANK), w.dtype)
    w_in_p = jnp.concatenate([
        w[:, :, 0:1536], w[:, :, 2176:2688], w[:, :, 2704:4240], w[:, :, 4240:4752], w[:, :, 4752:6800],
        w[:, :, 1664:2176], w[:, :, 1536:1664], w[:, :, 2688:2704], pad], axis=2).astype(bf16)
    zl = jnp.zeros((DEPTH, RWKV_LORA, W_BRANCH), f32)
    wup = jnp.concatenate([jnp.concatenate([p["rwkv_w_up"], zl], axis=2),
                           jnp.concatenate([zl, p["rwkv_a_up"]], axis=2)], axis=1)
    zrow = jnp.zeros((DEPTH, W_BRANCH), f32)
    rwkv_vec = jnp.stack([p["rwkv_w0"], p["rwkv_a0"], p["rwkv_k_k"], p["rwkv_k_a"],
                          p["rwkv_r_k"].reshape(DEPTH, W_BRANCH), p["rwkv_gn_g"], p["rwkv_gn_b"], zrow], axis=1)
    mu = p["rwkv_mu"]
    gup = jnp.concatenate([p["gla_gk_up"], jnp.zeros((DEPTH, LANE - GLA_GATE_RANK, GLA_KW), f32)], axis=1)
    eye = jnp.eye(LRU_BLOCKS, dtype=f32)

    def bd(wb):
        return jnp.einsum("lhij,hg->lhigj", wb, eye).reshape(DEPTH, W_BRANCH, W_BRANCH)

    lru_vec = jnp.stack([p["lru_conv_b"], p["lru_b_a"], p["lru_b_x"], p["lru_lambda"],
                         zrow, zrow, zrow, zrow], axis=1)
    lru_w = jnp.concatenate([bd(p["lru_w_a"]), bd(p["lru_w_x"])], axis=2)
    lru_w_hi = lru_w.astype(bf16)
    lru_w_lo = (lru_w - lru_w_hi.astype(f32)).astype(bf16)
    return dict(
        w_in_p=w_in_p,
        w_out_b=p["w_out"].astype(bf16),
        norm_g3=p["norm_g"].reshape(DEPTH, 1, D_MODEL),
        final_g2=p["final_g"].reshape(1, D_MODEL),
        rwkv=(mu[:, None, 0:1536], mu[:, None, 1536:1664], wup, rwkv_vec) + _rwkv_consts(),
        gla=(gup, p["gla_gk_b"].reshape(DEPTH, 1, GLA_KW)),
        gla_gn=p["gla_gn_g"].reshape(DEPTH, 1, W_BRANCH),
        hgrn=(p["hgrn_lb_logits"],),
        hgrn_gn=p["hgrn_gn_g"].reshape(DEPTH, 1, W_BRANCH),
        lru=(p["lru_conv_w"], lru_vec, jnp.stack([lru_w_hi, lru_w_lo], axis=1)),
    )


def _shift_state(proj_last):
    return jnp.concatenate([proj_last[..., 0:1536], proj_last[..., C_RWKV_LO:C_RWKV_LO + LANE]], axis=-1)


def _rwkv_state_in(s):
    lead = s.shape[:-3]
    return jnp.swapaxes(s, -3, -2).reshape(lead + (RWKV_HEAD, W_BRANCH))


def _rwkv_state_out(s):
    lead = s.shape[:-2]
    return jnp.swapaxes(s.reshape(lead + (RWKV_HEAD, RWKV_H, RWKV_HEAD)), -3, -2)


def _trunk_seq(x, mod, wts):
    b, seq_len, _ = x.shape
    t = b * seq_len
    lb = min(LB_CHUNK, seq_len)
    tm = min(TM_IN, seq_len)
    per_seq = seq_len // tm
    mod3 = mod.reshape(DEPTH * b * 3, 1, D_MODEL)
    x2 = x.reshape(t, D_MODEL)
    zeros = functools.partial(jnp.zeros, dtype=f32)
    new = []
    for l in range(DEPTH):
        mod_l = mod3[l * b * 3:(l + 1) * b * 3]
        proj2 = _inproj(x2, wts["norm_g3"], mod_l, wts["w_in_p"], l, per_seq, tm)
        proj3 = proj2.reshape(b, seq_len, PROJ_P)
        o_a, s_wkv = _rwkv_seq(proj3, zeros((b, 1, 1536)), zeros((b, 1, LANE)),
                               zeros((b, RWKV_HEAD, W_BRANCH)), wts["rwkv"], l, min(LB_RWKV, seq_len))
        o_b, st_gla = _chunk_seq("gla", proj3, wts["gla"], wts["gla_gn"], l, lb)
        o_c, st_hgrn = _chunk_seq("hgrn", proj3, wts["hgrn"], wts["hgrn_gn"], l, lb)
        o_d = _lru_seq(proj3, *wts["lru"], l, lb)
        outs = [o.reshape(t, W_BRANCH) for o in (o_a, o_b, o_c, o_d)]
        tmo = min(TM_OUT, seq_len)
        x2 = _outproj(outs, proj2, x2, mod_l, wts["w_out_b"], wts["final_g2"], l, seq_len // tmo, tmo,
                      final=(l == DEPTH - 1))
        last = proj3[:, seq_len - 1]
        new.append((
            _shift_state(last),
            _rwkv_state_out(s_wkv),
            _state_from_transposed(st_gla, GLA_H, GLA_DK),
            _state_from_transposed(st_hgrn, HGRN_H, HGRN_DK),
            proj3[:, seq_len - (CONV_W - 1):, C_LRU:C_LRU + W_BRANCH],
            o_d[:, seq_len - 1],
        ))
    return x2.reshape(b, seq_len, D_MODEL), tuple(jnp.stack([n[i] for n in new], axis=0) for i in range(6))


def _trunk_tok(x, mod, states, wts):
    b = x.shape[0]
    s_shift, s_wkv, s_gla, s_hgrn, s_conv, s_h = states
    x2 = x.reshape(b, D_MODEL)
    s_wkv_t = _rwkv_state_in(s_wkv)
    prev_rkv, prev_lo = s_shift[:, :, 0:1536], s_shift[:, :, 1536:1664]
    n_wkv = n_gla = n_hgrn = None
    new = []
    for l in range(DEPTH):
        proj2 = _inproj(x2, wts["norm_g3"], mod, wts["w_in_p"], l, None, b)
        o_a, n_wkv = _rwkv_tok(proj2, prev_rkv, prev_lo, s_wkv_t, n_wkv, wts["rwkv"], l, TOK_GROUP_RWKV)
        o_b, n_gla = _chunk_tok("gla", proj2, wts["gla"], wts["gla_gn"], s_gla, n_gla, l, TOK_GROUP)
        o_c, n_hgrn = _chunk_tok("hgrn", proj2, wts["hgrn"], wts["hgrn_gn"], s_hgrn, n_hgrn, l, TOK_GROUP)
        conv = s_conv[l]
        o_d = _lru_tok(proj2, [conv[:, 0], conv[:, 1], conv[:, 2]], s_h[l], *wts["lru"], l)
        x2 = _outproj([o_a, o_b, o_c, o_d], proj2, x2, mod, wts["w_out_b"], wts["final_g2"], l, None, b,
                      final=(l == DEPTH - 1))
        n_conv = jnp.concatenate([conv[:, 1:], proj2[:, None, C_LRU:C_LRU + W_BRANCH]], axis=1)
        new.append((_shift_state(proj2), n_conv, o_d))
    n_shift, n_conv, n_h = (jnp.stack([n[i] for n in new], axis=0) for i in range(3))
    return x2.reshape(b, 1, D_MODEL), (n_shift, _rwkv_state_out(n_wkv), n_gla, n_hgrn, n_conv, n_h)


def kernel(x_prompt, x_sample, c_prompt, c_sample, state_rwkv_shift, state_rwkv_wkv, state_gla, state_hgrn, state_lru_conv, state_lru_h, norm_g, w_ada, b_ada, w_in, w_out, rwkv_mu, rwkv_w0, rwkv_w_up, rwkv_a0, rwkv_a_up, rwkv_k_k, rwkv_k_a, rwkv_r_k, rwkv_gn_g, rwkv_gn_b, gla_gk_up, gla_gk_b, gla_gn_g, hgrn_lb_logits, hgrn_gn_g, lru_conv_w, lru_conv_b, lru_w_a, lru_b_a, lru_w_x, lru_b_x, lru_lambda, final_g):
    p = dict(norm_g=norm_g, w_in=w_in, w_out=w_out, rwkv_mu=rwkv_mu, rwkv_w0=rwkv_w0, rwkv_w_up=rwkv_w_up,
             rwkv_a0=rwkv_a0, rwkv_a_up=rwkv_a_up, rwkv_k_k=rwkv_k_k, rwkv_k_a=rwkv_k_a, rwkv_r_k=rwkv_r_k,
             rwkv_gn_g=rwkv_gn_g, rwkv_gn_b=rwkv_gn_b, gla_gk_up=gla_gk_up, gla_gk_b=gla_gk_b,
             gla_gn_g=gla_gn_g, hgrn_lb_logits=hgrn_lb_logits, hgrn_gn_g=hgrn_gn_g, lru_conv_w=lru_conv_w,
             lru_conv_b=lru_conv_b, lru_w_a=lru_w_a, lru_b_a=lru_b_a, lru_w_x=lru_w_x, lru_b_x=lru_b_x,
             lru_lambda=lru_lambda, final_g=final_g)
    wts = _prep_weights(p)
    bp = x_prompt.shape[0]
    bs = x_sample.shape[0]
    pad_rows = (-bp) % 8
    c_all = jnp.concatenate([c_prompt, jnp.zeros((pad_rows, D_MODEL), f32), c_sample], axis=0)
    mod = _ada(c_all, w_ada, b_ada)
    mod_p = mod[:, 0:bp]
    mod_s = mod[:, bp + pad_rows:bp + pad_rows + bs]
    y_p, st_p = _trunk_seq(x_prompt, mod_p, wts)
    states = (state_rwkv_shift, state_rwkv_wkv, state_gla, state_hgrn, state_lru_conv, state_lru_h)
    y_s, st_s = _trunk_tok(x_sample, mod_s, states, wts)
    return (y_p, y_s) + st_p + st_s
```

```python
import functools

import jax
import jax.numpy as jnp
from jax import lax
from jax.experimental import pallas as pl
from jax.experimental.pallas import tpu as pltpu

f32 = jnp.float32
bf16 = jnp.bfloat16

D_MODEL = 2048
DEPTH = 4
W_BRANCH = 512
EPS = 1e-6

RWKV_H = 8
RWKV_HEAD = 64
RWKV_LORA = 64
RWKV_PROJ = 3 * W_BRANCH + 2 * RWKV_LORA
RWKV_GN_EPS = 64e-5

GLA_H = 4
GLA_DK = 64
GLA_DV = 128
GLA_KW = GLA_H * GLA_DK
GLA_GATE_RANK = 16
GLA_GATE_NORM = 16.0

HGRN_H = 4
HGRN_DK = 128
HGRN_DV = 128
LB_FLOOR = 1e-30

LRU_BLOCKS = 8
LRU_BLOCK = 64
LRU_C = 8.0
CONV_W = 4

CHUNK = 32

C_R, C_K, C_V = 0, 512, 1024
C_GLA_V = 1536
C_HQ, C_HF, C_HI = 2048, 2560, 3072
C_LRU = 3584
C_GATE = 4096
C_GLA_Q, C_GLA_K = 6144, 6400
C_RWKV_LO = 6656
C_GLA_LO = 6784
PROJ_P = 6912

LANE = 128
MXU_TILE = 256
VMEM_LIMIT = 56 * 1024 * 1024

TM_IN = 1024
TN_IN = 768
TM_OUT = 256
LB_CHUNK = 256
LB_RWKV = 128
TOK_GROUP = 16
TOK_GROUP_RWKV = 8

HIGHEST = lax.Precision.HIGHEST


def _cparams(sem):
    return pltpu.CompilerParams(dimension_semantics=sem, vmem_limit_bytes=VMEM_LIMIT)


def _dotf(a, b):
    return jnp.dot(a, b, preferred_element_type=f32, precision=HIGHEST)


def _prec(a):
    return HIGHEST if a.dtype == f32 else None


def _dot_nt(a, b):
    return lax.dot_general(a, b, (((1,), (1,)), ((), ())), preferred_element_type=f32, precision=_prec(a))


def _dot_tn(a, b):
    return lax.dot_general(a, b, (((0,), (0,)), ((), ())), preferred_element_type=f32, precision=_prec(a))


def _split(x, pieces):
    out = []
    for i in range(pieces):
        part = x.astype(bf16)
        out.append(part)
        if i + 1 < pieces:
            x = x - part.astype(f32)
    return out


def _half_dot(x, ones):
    d = functools.partial(jnp.dot, preferred_element_type=f32)
    kh = ones.shape[0]
    if x.shape[1] == kh:
        return d(x, ones)
    return jnp.concatenate([d(x[:, 0:kh], ones), d(x[:, kh:2 * kh], ones)], axis=1)


def _seg_sum(x, ones, pieces=2):
    return sum(_half_dot(part, ones) for part in _split(x, pieces))


def _ones_seg(ones, x, pieces=3):
    d = functools.partial(jnp.dot, preferred_element_type=f32)
    return sum(d(ones, part) for part in _split(x, pieces))


def _sigmoid(x):
    return jax.nn.sigmoid(x)


def _softplus(x):
    return jnp.maximum(x, 0.0) + jnp.log1p(jnp.exp(-jnp.abs(x)))


def _log_sigmoid(x):
    return -_softplus(-x)


def _ada_kernel(c_ref, w_ref, b_ref, o_ref):
    c = c_ref[...]
    s = (c * _sigmoid(c)).astype(bf16)
    o_ref[...] = jnp.dot(s, w_ref[...].astype(bf16), preferred_element_type=f32) + b_ref[...]


def _ada(c_all, w_ada, b_ada):
    rows = c_all.shape[0]
    tn = 512
    n = w_ada.shape[2]
    return pl.pallas_call(
        _ada_kernel,
        grid=(DEPTH, n // tn),
        in_specs=[
            pl.BlockSpec((rows, D_MODEL), lambda l, j: (0, 0)),
            pl.BlockSpec((None, D_MODEL, tn), lambda l, j: (l, 0, j)),
            pl.BlockSpec((None, 1, tn), lambda l, j: (l, 0, j)),
        ],
        out_specs=pl.BlockSpec((None, rows, tn), lambda l, j: (l, 0, j)),
        out_shape=jax.ShapeDtypeStruct((DEPTH, rows, n), f32),
        compiler_params=_cparams(("parallel", "parallel")),
        name="ada_mod",
    )(c_all, w_ada, b_ada.reshape(DEPTH, 1, n))


def _inproj_kernel(x_ref, g_ref, sc_ref, sh_ref, w_ref, o_ref, h_ref):
    @pl.when(pl.program_id(1) == 0)
    def _():
        x = x_ref[...]
        ms = jnp.mean(x * x, axis=-1, keepdims=True)
        h = x * lax.rsqrt(ms + EPS) * g_ref[...]
        h = h * (1.0 + sc_ref[...]) + sh_ref[...]
        h_ref[...] = h.astype(bf16)

    o_ref[...] = jnp.dot(h_ref[...], w_ref[...], preferred_element_type=f32)


def _mod_spec(l, which, per_seq_tiles, tm):
    if per_seq_tiles is None:
        return pl.BlockSpec((None, tm, D_MODEL), lambda i, *_: (l, 0, which))
    return pl.BlockSpec((None, 1, D_MODEL), lambda i, *_: (i // per_seq_tiles * 3 + which, 0, 0))


def _inproj(x2, norm_g3, mod, w_in_p, l, per_seq_tiles, tm):
    t = x2.shape[0]
    tn = TN_IN
    return pl.pallas_call(
        _inproj_kernel,
        grid=(t // tm, PROJ_P // tn),
        in_specs=[
            pl.BlockSpec((tm, D_MODEL), lambda i, j: (i, 0)),
            pl.BlockSpec((None, 1, D_MODEL), lambda i, j: (l, 0, 0)),
            _mod_spec(l, 1, per_seq_tiles, tm),
            _mod_spec(l, 0, per_seq_tiles, tm),
            pl.BlockSpec((None, D_MODEL, tn), lambda i, j: (l, 0, j)),
        ],
        out_specs=pl.BlockSpec((tm, tn), lambda i, j: (i, j)),
        out_shape=jax.ShapeDtypeStruct((t, PROJ_P), f32),
        scratch_shapes=[pltpu.VMEM((tm, D_MODEL), bf16)],
        compiler_params=_cparams(("parallel", "arbitrary")),
        name="in_proj",
    )(x2, norm_g3, mod, mod, w_in_p)


def _outproj_kernel(oa_ref, ob_ref, oc_ref, od_ref, pz_ref, x_ref, gate_ref, w_ref, fg_ref, o_ref, *, final):
    z = pz_ref[...]
    o = jnp.concatenate([oa_ref[...], ob_ref[...], oc_ref[...], od_ref[...]], axis=1)
    o = o * (z * _sigmoid(z))
    y = jnp.dot(o.astype(bf16), w_ref[...], preferred_element_type=f32)
    xn = x_ref[...] + gate_ref[...] * y
    if final:
        ms = jnp.mean(xn * xn, axis=-1, keepdims=True)
        xn = xn * lax.rsqrt(ms + EPS) * fg_ref[...]
    o_ref[...] = xn


def _outproj(outs, proj2, x2, mod, w_out_b, final_g2, l, per_seq_tiles, tm, final):
    t = x2.shape[0]
    mix_spec = pl.BlockSpec((tm, W_BRANCH), lambda i: (i, 0))
    return pl.pallas_call(
        functools.partial(_outproj_kernel, final=final),
        grid=(t // tm,),
        in_specs=[
            mix_spec, mix_spec, mix_spec, mix_spec,
            pl.BlockSpec((tm, D_MODEL), lambda i: (i, C_GATE // D_MODEL)),
            pl.BlockSpec((tm, D_MODEL), lambda i: (i, 0)),
            _mod_spec(l, 2, per_seq_tiles, tm),
            pl.BlockSpec((None, D_MODEL, D_MODEL), lambda i: (l, 0, 0)),
            pl.BlockSpec((1, D_MODEL), lambda i: (0, 0)),
        ],
        out_specs=pl.BlockSpec((tm, D_MODEL), lambda i: (i, 0)),
        out_shape=jax.ShapeDtypeStruct((t, D_MODEL), f32),
        compiler_params=_cparams(("parallel",)),
        name="out_proj",
    )(*outs, proj2, x2, mod, w_out_b, final_g2)


def _rwkv_prologue(r, k, v, lo, pr, pk, pv, plo, mu_rkv, mu_lo, wup, pvec, bo):
    w0, a0, k_k, k_a, r_k = (pvec[i:i + 1, :] for i in range(5))
    xr = r + (pr - r) * mu_rkv[:, 0:512]
    xk = k + (pk - k) * mu_rkv[:, 512:1024]
    xv = v + (pv - v) * mu_rkv[:, 1024:1536]
    xlo = lo + (plo - lo) * mu_lo
    lane = lax.broadcasted_iota(jnp.int32, xlo.shape, 1)
    act = jnp.where(lane < RWKV_LORA, jnp.tanh(xlo), xlo)
    up = _dotf(act, wup)
    w_raw = -_softplus(-(w0 + up[:, 0:512])) - 0.5
    ew = jnp.exp(-jnp.exp(w_raw))
    a = _sigmoid(a0 + up[:, 512:1024])
    kk = xk * k_k
    kk = kk / jnp.maximum(jnp.sqrt(_seg_sum(kk * kk, bo)), 1e-12)
    kh = xk * (1.0 + (a - 1.0) * k_a)
    alp = kk * a
    ar = _seg_sum(alp * xr, bo)
    return dict(
        kap=kk, ew=ew, alp=alp, kh=kh, vv=xv,
        wr=ew * xr - ar * kk,
        kr=_seg_sum(kh * xr, bo),
        bonus=_seg_sum(xr * kh * r_k, bo) * xv,
    )


_RWKV_STEP_KEYS = ("kap", "ew", "alp", "kh", "vv", "wr", "kr")


def _rwkv_step(s, kap, ew, alp, kh, vv, wr, kr, bo, idt):
    sk_hi, sk_mid = _split(s * kap, 2)
    x = jnp.concatenate([sk_hi, sk_mid, (s * wr).astype(bf16), idt.astype(bf16) * vv.astype(bf16)], axis=0)
    red = _half_dot(x, bo)
    sk = red[0:64] + red[64:128]
    s_new = s * ew - sk * alp + red[192:256] * kh
    o_row = jnp.sum(red[128:192] * idt, axis=0, keepdims=True) + vv * kr
    return s_new, o_row


def _rwkv_epilogue(o, bonus, pvec, bo):
    gn_g, gn_b = pvec[5:6, :], pvec[6:7, :]
    mu = _seg_sum(o, bo) * (1.0 / RWKV_HEAD)
    d = o - mu
    var = _seg_sum(d * d, bo) * (1.0 / RWKV_HEAD)
    return d * lax.rsqrt(var + RWKV_GN_EPS) * gn_g + gn_b + bonus


def _rwkv_seq_kernel(r_ref, k_ref, v_ref, lo_ref, prkv_ref, plo_ref, sin_ref, mu_rkv_ref, mu_lo_ref, wup_ref,
                     pvec_ref, bo_ref, idt_ref, o_ref, sout_ref,
                     s_sc, crkv_sc, clo_sc, kap_sc, ew_sc, alp_sc, kh_sc, vv_sc, wr_sc, kr_sc, oraw_sc,
                     bonus_sc, *, g_seqs, lb):
    tb = pl.program_id(1)

    @pl.when(tb == 0)
    def _():
        s_sc[...] = sin_ref[...]
        crkv_sc[...] = prkv_ref[...]
        clo_sc[...] = plo_ref[...]

    bo = bo_ref[...]
    idt = idt_ref[...]
    pvec = pvec_ref[...]
    step_sc = dict(kap=kap_sc, ew=ew_sc, alp=alp_sc, kh=kh_sc, vv=vv_sc, wr=wr_sc, kr=kr_sc)

    row0 = lax.broadcasted_iota(jnp.int32, (lb, 1), 0) == 0
    for g in range(g_seqs):
        cur = [r_ref[g], k_ref[g], v_ref[g]]
        lo = lo_ref[g]
        carry = crkv_sc[g]
        prev = [jnp.where(row0, carry[:, i * 512:(i + 1) * 512], pltpu.roll(c, 1, 0)) for i, c in enumerate(cur)]
        plo = jnp.where(row0, clo_sc[g], pltpu.roll(lo, 1, 0))
        res = _rwkv_prologue(cur[0], cur[1], cur[2], lo, prev[0], prev[1], prev[2], plo,
                             mu_rkv_ref[...], mu_lo_ref[...], wup_ref[...], pvec, bo)
        for key in _RWKV_STEP_KEYS:
            step_sc[key][g] = res[key]
        bonus_sc[g] = res["bonus"]
        for i, c in enumerate(cur):
            crkv_sc[g, :, i * 512:(i + 1) * 512] = c[lb - 1:lb, :]
        clo_sc[g] = lo[lb - 1:lb, :]

    def body(t, carry):
        for g in range(g_seqs):
            rows = [step_sc[key][g, pl.ds(t, 1), :] for key in _RWKV_STEP_KEYS]
            s_new, o_row = _rwkv_step(s_sc[g], *rows, bo, idt)
            s_sc[g] = s_new
            oraw_sc[g, pl.ds(t, 1), :] = o_row
        return carry

    lax.fori_loop(0, lb, body, 0, unroll=4)

    for g in range(g_seqs):
        o_ref[g] = _rwkv_epilogue(oraw_sc[g], bonus_sc[g], pvec, bo)

    @pl.when(tb == pl.num_programs(1) - 1)
    def _():
        sout_ref[...] = s_sc[...]


def _rwkv_tok_kernel(r_ref, k_ref, v_ref, lo_ref, prkv_ref, plo_ref, sin_ref, mu_rkv_ref, mu_lo_ref, wup_ref,
                     pvec_ref, bo_ref, idt_ref, *rest, g_rows):
    o_ref, sout_ref = rest[-2:]
    bo = bo_ref[...]
    idt = idt_ref[...]
    pvec = pvec_ref[...]
    prkv = prkv_ref[...]
    res = _rwkv_prologue(r_ref[...], k_ref[...], v_ref[...], lo_ref[...],
                         prkv[:, 0:512], prkv[:, 512:1024], prkv[:, 1024:1536], plo_ref[...],
                         mu_rkv_ref[...], mu_lo_ref[...], wup_ref[...], pvec, bo)
    o_rows = []
    for g in range(g_rows):
        rows = [res[key][g:g + 1, :] for key in _RWKV_STEP_KEYS]
        s_new, o_row = _rwkv_step(sin_ref[g], *rows, bo, idt)
        sout_ref[g] = s_new
        o_rows.append(o_row)
    o_ref[...] = _rwkv_epilogue(jnp.concatenate(o_rows, axis=0), res["bonus"], pvec, bo)


def _rwkv_consts():
    i = jnp.arange(W_BRANCH)
    j = jnp.arange(MXU_TILE)
    bo = (j[:, None] // RWKV_HEAD == j[None, :] // RWKV_HEAD).astype(bf16)
    idt = (jnp.arange(RWKV_HEAD)[:, None] == (i[None, :] % RWKV_HEAD)).astype(f32)
    return bo, idt


def _rwkv_weight_specs(l):
    def cs(shape):
        return pl.BlockSpec((None,) + shape, lambda *_: (l,) + (0,) * len(shape))

    def const(shape):
        return pl.BlockSpec(shape, lambda *_: (0,) * len(shape))

    return [cs((1, 1536)), cs((1, LANE)), cs((LANE, 1024)), cs((8, W_BRANCH)),
            const((MXU_TILE, MXU_TILE)), const((RWKV_HEAD, W_BRANCH))]


def _rwkv_seq(proj3, prev_rkv, prev_lo, s_in, wts, l, lb):
    b, seq_len, _ = proj3.shape
    g = b
    nt = seq_len // lb

    def col(width, off):
        return pl.BlockSpec((g, lb, width), lambda i, t: (i, t, off // width))

    tok_sc = pltpu.VMEM((g, lb, W_BRANCH), f32)
    out = pl.pallas_call(
        functools.partial(_rwkv_seq_kernel, g_seqs=g, lb=lb),
        grid=(b // g, nt),
        in_specs=[
            col(512, C_R), col(512, C_K), col(512, C_V), col(LANE, C_RWKV_LO),
            pl.BlockSpec((g, 1, 1536), lambda i, t: (i, 0, 0)),
            pl.BlockSpec((g, 1, LANE), lambda i, t: (i, 0, 0)),
            pl.BlockSpec((g, RWKV_HEAD, W_BRANCH), lambda i, t: (i, 0, 0)),
        ] + _rwkv_weight_specs(l),
        out_specs=[
            pl.BlockSpec((g, lb, W_BRANCH), lambda i, t: (i, t, 0)),
            pl.BlockSpec((g, RWKV_HEAD, W_BRANCH), lambda i, t: (i, 0, 0)),
        ],
        out_shape=[
            jax.ShapeDtypeStruct((b, seq_len, W_BRANCH), f32),
            jax.ShapeDtypeStruct((b, RWKV_HEAD, W_BRANCH), f32),
        ],
        scratch_shapes=[
            pltpu.VMEM((g, RWKV_HEAD, W_BRANCH), f32),
            pltpu.VMEM((g, 1, 1536), f32),
            pltpu.VMEM((g, 1, LANE), f32),
        ] + [tok_sc] * 9,
        compiler_params=_cparams(("parallel", "arbitrary")),
        name="rwkv_seq",
    )(proj3, proj3, proj3, proj3, prev_rkv, prev_lo, s_in, *wts)
    return out


def _rwkv_tok(proj2, prev_rkv, prev_lo, s_all, acc, wts, l, g):
    b = proj2.shape[0]

    def col(width, off):
        return pl.BlockSpec((g, width), lambda i: (i, off // width))

    state_spec = pl.BlockSpec((None, g, RWKV_HEAD, W_BRANCH), lambda i: (l, i, 0, 0))
    in_specs = [
        col(512, C_R), col(512, C_K), col(512, C_V), col(LANE, C_RWKV_LO),
        pl.BlockSpec((None, g, 1536), lambda i: (l, i, 0)),
        pl.BlockSpec((None, g, LANE), lambda i: (l, i, 0)),
        state_spec,
    ] + _rwkv_weight_specs(l)
    args = [proj2, proj2, proj2, proj2, prev_rkv, prev_lo, s_all, *wts]
    aliases = {}
    if acc is not None:
        in_specs.append(pl.BlockSpec(memory_space=pl.ANY))
        aliases = {len(args): 1}
        args.append(acc)
    return pl.pallas_call(
        functools.partial(_rwkv_tok_kernel, g_rows=g),
        grid=(b // g,),
        in_specs=in_specs,
        out_specs=[pl.BlockSpec((g, W_BRANCH), lambda i: (i, 0)), state_spec],
        out_shape=[
            jax.ShapeDtypeStruct((b, W_BRANCH), f32),
            jax.ShapeDtypeStruct(s_all.shape, f32),
        ],
        input_output_aliases=aliases,
        compiler_params=_cparams(("parallel",)),
        name="rwkv_tok",
    )(*args)


def _gla_inputs(q_ref, k_ref, v_ref, lo_ref, gup_ref, gb_ref):
    q = q_ref[...] * (GLA_DK ** -0.5)
    z = _dotf(lo_ref[...], gup_ref[...]) + gb_ref[...]
    g = _log_sigmoid(z) * (1.0 / GLA_GATE_NORM)
    return q, k_ref[...], v_ref[...], g


def _hgrn_lb(logits, l):
    m = jnp.max(logits, axis=0, keepdims=True)
    e = jnp.exp(logits - m)
    sm = e / jnp.sum(e, axis=0, keepdims=True)
    lb = jnp.zeros_like(sm[0:1, :])
    for i in range(1, l + 1):
        lb = lb + sm[i:i + 1, :]
    return lb


def _hgrn_inputs(q_ref, f_ref, i_ref, logits_ref, l):
    lb = _hgrn_lb(logits_ref[...], l)
    f_lo = f_ref[...]
    a = jnp.log(jnp.maximum(lb, LB_FLOOR))
    b = jnp.log1p(-lb) + _log_sigmoid(f_lo)
    logf = jnp.maximum(a, b) + jnp.log1p(jnp.exp(-jnp.abs(a - b)))
    k = (1.0 - lb) * _sigmoid(-f_lo)
    return q_ref[...], k, i_ref[...], logf


def _chunk_core(q, k, v, g, lt_ref, esum_ref, vmask_ref, gmask_ref, bon_ref, gn_ref, st_sc, kpad, bpad, lb,
                heads_per_group):
    bc = _ones_seg(lt_ref[...], g)
    nc = lb // CHUNK
    f = q.shape[1]
    kpad[...] = k.reshape(nc, CHUNK, f)
    bpad[...] = bc.reshape(nc, CHUNK, f)
    q3 = q.reshape(nc, CHUNK, f)
    bc3 = bc.reshape(nc, CHUNK, f)
    esum = esum_ref[...]

    lane_j = lax.broadcasted_iota(jnp.int32, (1, 1, LANE), 2) & (CHUNK - 1)
    att3 = None
    for r0 in range(0, CHUNK, 8):
        rows = CHUNK - r0
        q_s, bc_s = q3[:, r0:, :], bc3[:, r0:, :]
        rowc = lax.broadcasted_iota(jnp.int32, (1, rows, 1), 1) + r0
        acc = jnp.zeros((nc, rows, LANE), f32)
        for j in range(r0, r0 + 8):
            kj, bj = kpad[:, j:j + 1, :], bpad[:, j:j + 1, :]
            z = jnp.where(rowc >= j, q_s * kj * jnp.exp(bc_s - bj), 0.0)
            a = jnp.dot(z.reshape(nc * rows, f).astype(bf16), esum, preferred_element_type=f32)
            acc = acc + jnp.where(lane_j == j, a.reshape(nc, rows, LANE), 0.0)
        if r0:
            acc = jnp.concatenate([jnp.zeros((nc, r0, LANE), f32), acc], axis=1)
        att3 = acc if att3 is None else att3 + acc
    n_heads = LANE // CHUNK
    v3 = v.reshape(nc, CHUNK, W_BRANCH)
    vbd = jnp.concatenate([v3] * n_heads, axis=1) * vmask_ref[...]
    o = lax.dot_general(att3.astype(bf16), vbd.astype(bf16), (((2,), (1,)), ((0,), (0,))),
                        preferred_element_type=f32).reshape(lb, W_BRANCH)

    ng = f // LANE
    vw = W_BRANCH // ng
    sts = [st_sc[gi] for gi in range(ng)]
    outs = []
    for c in range(nc):
        sl = slice(c * CHUNK, (c + 1) * CHUNK)
        bcc = bc[sl]
        blast = bcc[CHUNK - 1:CHUNK, :]
        dec = jnp.exp(blast)
        qe = (q[sl] * jnp.exp(bcc)).astype(bf16)
        ke = (k[sl] * jnp.exp(blast - bcc)).astype(bf16)
        vb = v[sl].astype(bf16)
        parts = []
        for gi in range(ng):
            kl = slice(gi * LANE, (gi + 1) * LANE)
            vl = slice(gi * vw, (gi + 1) * vw)
            parts.append(_dot_nt(qe[:, kl], sts[gi].astype(bf16)))
            upd = _dot_tn(vb[:, vl], ke[:, kl])
            if heads_per_group > 1:
                upd = upd * gmask_ref[...]
            sts[gi] = sts[gi] * dec[:, kl] + upd
        outs.append(o[sl] + jnp.concatenate(parts, axis=1))
    for gi in range(ng):
        st_sc[gi] = sts[gi]
    o = jnp.concatenate(outs, axis=0)
    ms = _seg_sum(o * o, bon_ref[...]) * (1.0 / LANE)
    return o * lax.rsqrt(ms + EPS) * gn_ref[...]


def _chunk_seq_kernel(*refs, kind, l, lb):
    n_in = 6 if kind == "gla" else 4
    ins = refs[:n_in]
    lt_ref, esum_ref, vmask_ref, gmask_ref, bon_ref, gn_ref, o_ref, stout_ref, st_sc, kpad, bpad = refs[n_in:]
    tb = pl.program_id(1)

    @pl.when(tb == 0)
    def _():
        st_sc[...] = jnp.zeros_like(st_sc)

    if kind == "gla":
        q, k, v, g = _gla_inputs(*ins)
        heads_per_group = LANE // GLA_DK
    else:
        q, k, v, g = _hgrn_inputs(*ins, l)
        heads_per_group = LANE // HGRN_DK
    o_ref[...] = _chunk_core(q, k, v, g, lt_ref, esum_ref, vmask_ref, gmask_ref, bon_ref, gn_ref, st_sc, kpad,
                             bpad, lb, heads_per_group)

    @pl.when(tb == pl.num_programs(1) - 1)
    def _():
        stout_ref[...] = st_sc[...]


def _chunk_consts(h, dk, lb):
    f = h * dk
    i = jnp.arange(lb)
    lt = ((i[:, None] // CHUNK == i[None, :] // CHUNK) & (i[:, None] >= i[None, :])).astype(bf16)
    fi = jnp.arange(f)
    li = jnp.arange(LANE)
    oi = jnp.arange(W_BRANCH)
    vw = W_BRANCH // (f // LANE)
    esum = (fi[:, None] // dk == li[None, :] // CHUNK).astype(bf16)
    vmask = (li[:, None] // CHUNK == oi[None, :] // LANE).astype(f32)
    gmask = (jnp.arange(vw)[:, None] // LANE == li[None, :] // dk).astype(f32)
    ti = jnp.arange(MXU_TILE)
    bon = (ti[:, None] // LANE == ti[None, :] // LANE).astype(bf16)
    return lt, esum, vmask, gmask, bon


def _chunk_seq(kind, proj3, extra, gn3, l, lb):
    b, seq_len, _ = proj3.shape
    h, dk = (GLA_H, GLA_DK) if kind == "gla" else (HGRN_H, HGRN_DK)
    f = h * dk
    nt = seq_len // lb

    def col(width, off):
        return pl.BlockSpec((None, lb, width), lambda i, t: (i, t, off // width))

    def const(shape):
        return pl.BlockSpec(shape, lambda *_: (0,) * len(shape))

    def layer(shape):
        return pl.BlockSpec((None,) + shape, lambda *_: (l,) + (0,) * len(shape))

    if kind == "gla":
        gup, gb = extra
        in_specs = [col(GLA_KW, C_GLA_Q), col(GLA_KW, C_GLA_K), col(W_BRANCH, C_GLA_V), col(LANE, C_GLA_LO),
                    layer((LANE, GLA_KW)), layer((1, GLA_KW))]
        args = [proj3, proj3, proj3, proj3, gup, gb]
    else:
        (logits,) = extra
        in_specs = [col(W_BRANCH, C_HQ), col(W_BRANCH, C_HF), col(W_BRANCH, C_HI), const((DEPTH, W_BRANCH))]
        args = [proj3, proj3, proj3, logits]
    consts = _chunk_consts(h, dk, lb)
    in_specs += [const(c.shape) for c in consts] + [layer((1, W_BRANCH))]
    ng = f // LANE
    st_shape = (ng, W_BRANCH // ng, LANE)
    return pl.pallas_call(
        functools.partial(_chunk_seq_kernel, kind=kind, l=l, lb=lb),
        grid=(b, nt),
        in_specs=in_specs,
        out_specs=[
            pl.BlockSpec((None, lb, W_BRANCH), lambda i, t: (i, t, 0)),
            pl.BlockSpec((None,) + st_shape, lambda i, t: (i, 0, 0, 0)),
        ],
        out_shape=[
            jax.ShapeDtypeStruct((b, seq_len, W_BRANCH), f32),
            jax.ShapeDtypeStruct((b,) + st_shape, f32),
        ],
        scratch_shapes=[
            pltpu.VMEM(st_shape, f32),
            pltpu.VMEM((lb // CHUNK, CHUNK, f), f32),
            pltpu.VMEM((lb // CHUNK, CHUNK, f), f32),
        ],
        compiler_params=_cparams(("parallel", "arbitrary")),
        name=kind + "_seq",
    )(*args, *consts, gn3)


def _state_from_transposed(st, h, dk):
    b, ng = st.shape[0], st.shape[1]
    hpg = h // ng
    st = st.reshape(b, ng, hpg, LANE, hpg, dk)
    diag = jnp.stack([st[:, :, i, :, i, :] for i in range(hpg)], axis=2)
    return jnp.swapaxes(diag.reshape(b, h, LANE, dk), 2, 3)


def _block_diag_rows(x, g, dk):
    row = lax.broadcasted_iota(jnp.int32, x.shape, 0)
    lane = lax.broadcasted_iota(jnp.int32, x.shape, 1)
    per = LANE // dk
    pieces = []
    for j in range(g // per):
        keep = row == (j * per + lane // dk)
        pieces.append(jnp.where(keep, x, 0.0))
    return jnp.concatenate(pieces, axis=1)


def _chunk_tok_kernel(*refs, kind, l, g, has_acc):
    ins, (o_ref, sout_ref) = refs[:-2], refs[-2:]
    if has_acc:
        ins = ins[:-1]
    if kind == "gla":
        q_ref, k_ref, v_ref, lo_ref, gup_ref, gb_ref, s_ref, gn_ref = ins
        odd = (pl.program_id(1) % 2) == 1
        lane = lax.broadcasted_iota(jnp.int32, (g, LANE), 1)
        own = (lane >= GLA_DK) == odd

        def pick(x):
            return jnp.where(own, x, pltpu.roll(x, GLA_DK, 1))

        q = pick(q_ref[...]) * (GLA_DK ** -0.5)
        k = pick(k_ref[...])
        z = _dotf(lo_ref[...], gup_ref[...]) + gb_ref[...]
        dec = jnp.exp(pick(_log_sigmoid(z) * (1.0 / GLA_GATE_NORM)))
        v = v_ref[...]
        dk = GLA_DK
    else:
        q_ref, f_ref, i_ref, logits_ref, s_ref, gn_ref = ins
        q, k, v, logf = _hgrn_inputs(q_ref, f_ref, i_ref, logits_ref, l)
        dec = jnp.exp(logf)
        dk = HGRN_DK
    s = s_ref[...].reshape(g * dk, LANE)
    ones = jnp.ones((g, LANE), bf16)
    dcol = sum(_dot_tn(part, ones) for part in _split(_block_diag_rows(dec, g, dk), 2))
    k_hi, k_lo = _split(_block_diag_rows(k, g, dk), 2)
    v_hi, v_lo = _split(v, 2)
    s_new = s * dcol + (_dot_tn(k_hi, v_hi) + _dot_tn(k_hi, v_lo) + _dot_tn(k_lo, v_hi))
    o = jnp.dot(_block_diag_rows(q, g, dk).astype(bf16), s_new.astype(bf16), preferred_element_type=f32)
    ms = jnp.mean(o * o, axis=-1, keepdims=True)
    o_ref[...] = o * lax.rsqrt(ms + EPS) * gn_ref[...]
    sout_ref[...] = s_new.reshape(g, dk, LANE)


def _chunk_tok(kind, proj2, extra, gn3, s_all, acc, l, g):
    b = proj2.shape[0]
    h, dk = (GLA_H, GLA_DK) if kind == "gla" else (HGRN_H, HGRN_DK)

    def head_col(off):
        per = LANE // dk
        return pl.BlockSpec((g, LANE), lambda i, j: (i, off // LANE + j // per))

    if kind == "gla":
        gup, gb = extra
        in_specs = [head_col(C_GLA_Q), head_col(C_GLA_K),
                    pl.BlockSpec((g, LANE), lambda i, j: (i, C_GLA_V // LANE + j)),
                    pl.BlockSpec((g, LANE), lambda i, j: (i, C_GLA_LO // LANE)),
                    pl.BlockSpec((None, LANE, LANE), lambda i, j: (l, 0, j // 2)),
                    pl.BlockSpec((None, 1, LANE), lambda i, j: (l, 0, j // 2))]
        args = [proj2, proj2, proj2, proj2, gup, gb]
    else:
        (logits,) = extra
        in_specs = [head_col(C_HQ), head_col(C_HF), head_col(C_HI),
                    pl.BlockSpec((DEPTH, LANE), lambda i, j: (0, j))]
        args = [proj2, proj2, proj2, logits]
    state_spec = pl.BlockSpec((None, g, None, dk, LANE), lambda i, j: (l, i, j, 0, 0))
    in_specs += [state_spec, pl.BlockSpec((None, 1, LANE), lambda i, j: (l, 0, j))]
    args += [s_all, gn3]
    aliases = {}
    if acc is not None:
        in_specs.append(pl.BlockSpec(memory_space=pl.ANY))
        aliases = {len(args): 1}
        args.append(acc)
    return pl.pallas_call(
        functools.partial(_chunk_tok_kernel, kind=kind, l=l, g=g, has_acc=acc is not None),
        grid=(b // g, h),
        in_specs=in_specs,
        out_specs=[pl.BlockSpec((g, LANE), lambda i, j: (i, j)), state_spec],
        out_shape=[
            jax.ShapeDtypeStruct((b, W_BRANCH), f32),
            jax.ShapeDtypeStruct(s_all.shape, f32),
        ],
        input_output_aliases=aliases,
        compiler_params=_cparams(("parallel", "parallel")),
        name=kind + "_tok",
    )(*args)


def _lru_gate_dot(y_hi, y_lo, w_ref, col0):
    d = functools.partial(jnp.dot, preferred_element_type=f32)
    halves = []
    for h0 in range(0, W_BRANCH, MXU_TILE):
        w_hi = w_ref[0, h0:h0 + MXU_TILE, col0 + h0:col0 + h0 + MXU_TILE]
        w_lo = w_ref[1, h0:h0 + MXU_TILE, col0 + h0:col0 + h0 + MXU_TILE]
        a, b = y_hi[:, h0:h0 + MXU_TILE], y_lo[:, h0:h0 + MXU_TILE]
        halves.append(d(a, w_hi) + d(a, w_lo) + d(b, w_hi))
    return jnp.concatenate(halves, axis=1)


def _lru_gates(y, w_ref, pvec):
    b_a, b_x, lam = pvec[1:2, :], pvec[2:3, :], pvec[3:4, :]
    y_hi, y_lo = _split(y, 2)
    r = _sigmoid(_lru_gate_dot(y_hi, y_lo, w_ref, 0) + b_a)
    ig = _sigmoid(_lru_gate_dot(y_hi, y_lo, w_ref, W_BRANCH) + b_x)
    log_a = -LRU_C * r * _softplus(-lam)
    a = jnp.exp(log_a)
    one_m_a2 = -jnp.tanh(log_a) * (jnp.exp(2.0 * log_a) + 1.0)
    b = jnp.sqrt(one_m_a2) * (ig * y)
    return a, b


def _lru_seq_kernel(x_ref, cw_ref, pvec_ref, wbd_ref, o_ref, xpad, hcar, *, lb):
    tb = pl.program_id(1)

    @pl.when(tb == 0)
    def _():
        xpad[0:8, :] = jnp.zeros((8, W_BRANCH), f32)
        hcar[...] = jnp.zeros_like(hcar)

    x = x_ref[...]
    xpad[8:8 + lb, :] = x
    cw = cw_ref[...]
    pvec = pvec_ref[...]
    y = pvec[0:1, :] + x * cw[3:4, :]
    for j in range(CONV_W - 1):
        y = y + xpad[5 + j:5 + j + lb, :] * cw[j:j + 1, :]
    a, b = _lru_gates(y, wbd_ref, pvec)
    row = lax.broadcasted_iota(jnp.int32, (lb, 1), 0)
    s = 1
    while s < lb:
        m = row >= s
        b = jnp.where(m, a * pltpu.roll(b, s, 0) + b, b)
        a = jnp.where(m, a * pltpu.roll(a, s, 0), a)
        s *= 2
    h = a * hcar[...] + b
    o_ref[...] = h
    hcar[...] = h[lb - 1:lb, :]
    xpad[5:8, :] = x[lb - 3:lb, :]


def _lru_seq(proj3, cw, pvec, wbd, l, lb):
    b, seq_len, _ = proj3.shape

    def layer(shape):
        return pl.BlockSpec((None,) + shape, lambda *_: (l,) + (0,) * len(shape))

    return pl.pallas_call(
        functools.partial(_lru_seq_kernel, lb=lb),
        grid=(b, seq_len // lb),
        in_specs=[
            pl.BlockSpec((None, lb, W_BRANCH), lambda i, t: (i, t, C_LRU // W_BRANCH)),
            layer((CONV_W, W_BRANCH)), layer((8, W_BRANCH)), layer((2, W_BRANCH, 2 * W_BRANCH)),
        ],
        out_specs=pl.BlockSpec((None, lb, W_BRANCH), lambda i, t: (i, t, 0)),
        out_shape=jax.ShapeDtypeStruct((b, seq_len, W_BRANCH), f32),
        scratch_shapes=[
            pltpu.VMEM((8 + lb, W_BRANCH), f32),
            pltpu.VMEM((1, W_BRANCH), f32),
        ],
        compiler_params=_cparams(("parallel", "arbitrary")),
        name="lru_seq",
    )(proj3, cw, pvec, wbd)


def _lru_tok_kernel(x_ref, b0_ref, b1_ref, b2_ref, h0_ref, cw_ref, pvec_ref, wbd_ref, o_ref):
    cw = cw_ref[...]
    pvec = pvec_ref[...]
    y = (pvec[0:1, :] + b0_ref[...] * cw[0:1, :] + b1_ref[...] * cw[1:2, :] + b2_ref[...] * cw[2:3, :]
         + x_ref[...] * cw[3:4, :])
    a, b = _lru_gates(y, wbd_ref, pvec)
    o_ref[...] = a * h0_ref[...] + b


def _lru_tok(proj2, bufs, h0, cw, pvec, wbd, l):
    b = proj2.shape[0]

    def layer(shape):
        return pl.BlockSpec((None,) + shape, lambda *_: (l,) + (0,) * len(shape))

    row = pl.BlockSpec((b, W_BRANCH), lambda i: (0, 0))
    return pl.pallas_call(
        _lru_tok_kernel,
        grid=(1,),
        in_specs=[pl.BlockSpec((b, W_BRANCH), lambda i: (0, C_LRU // W_BRANCH)), row, row, row, row,
                  layer((CONV_W, W_BRANCH)), layer((8, W_BRANCH)), layer((2, W_BRANCH, 2 * W_BRANCH))],
        out_specs=row,
        out_shape=jax.ShapeDtypeStruct((b, W_BRANCH), f32),
        compiler_params=_cparams(("arbitrary",)),
        name="lru_tok",
    )(proj2, *bufs, h0, cw, pvec, wbd)


def _prep_weights(p):
    w = p["w_in"]
    pad = jnp.zeros((DEPTH, D_MODEL, LANE - GLA_GATE_RANK), w.dtype)
    w_in_p = jnp.concatenate([
        w[:, :, 0:1536], w[:, :, 2176:2688], w[:, :, 2704:4240], w[:, :, 4240:4752], w[:, :, 4752:6800],
        w[:, :, 1664:2176], w[:, :, 1536:1664], w[:, :, 2688:2704], pad], axis=2).astype(bf16)
    zl = jnp.zeros((DEPTH, RWKV_LORA, W_BRANCH), f32)
    wup = jnp.concatenate([jnp.concatenate([p["rwkv_w_up"], zl], axis=2),
                           jnp.concatenate([zl, p["rwkv_a_up"]], axis=2)], axis=1)
    zrow = jnp.zeros((DEPTH, W_BRANCH), f32)
    rwkv_vec = jnp.stack([p["rwkv_w0"], p["rwkv_a0"], p["rwkv_k_k"], p["rwkv_k_a"],
                          p["rwkv_r_k"].reshape(DEPTH, W_BRANCH), p["rwkv_gn_g"], p["rwkv_gn_b"], zrow], axis=1)
    mu = p["rwkv_mu"]
    gup = jnp.concatenate([p["gla_gk_up"], jnp.zeros((DEPTH, LANE - GLA_GATE_RANK, GLA_KW), f32)], axis=1)
    eye = jnp.eye(LRU_BLOCKS, dtype=f32)

    def bd(wb):
        return jnp.einsum("lhij,hg->lhigj", wb, eye).reshape(DEPTH, W_BRANCH, W_BRANCH)

    lru_vec = jnp.stack([p["lru_conv_b"], p["lru_b_a"], p["lru_b_x"], p["lru_lambda"],
                         zrow, zrow, zrow, zrow], axis=1)
    lru_w = jnp.concatenate([bd(p["lru_w_a"]), bd(p["lru_w_x"])], axis=2)
    lru_w_hi = lru_w.astype(bf16)
    lru_w_lo = (lru_w - lru_w_hi.astype(f32)).astype(bf16)
    return dict(
        w_in_p=w_in_p,
        w_out_b=p["w_out"].astype(bf16),
        norm_g3=p["norm_g"].reshape(DEPTH, 1, D_MODEL),
        final_g2=p["final_g"].reshape(1, D_MODEL),
        rwkv=(mu[:, None, 0:1536], mu[:, None, 1536:1664], wup, rwkv_vec) + _rwkv_consts(),
        gla=(gup, p["gla_gk_b"].reshape(DEPTH, 1, GLA_KW)),
        gla_gn=p["gla_gn_g"].reshape(DEPTH, 1, W_BRANCH),
        hgrn=(p["hgrn_lb_logits"],),
        hgrn_gn=p["hgrn_gn_g"].reshape(DEPTH, 1, W_BRANCH),
        lru=(p["lru_conv_w"], lru_vec, jnp.stack([lru_w_hi, lru_w_lo], axis=1)),
    )


def _shift_state(proj_last):
    return jnp.concatenate([proj_last[..., 0:1536], proj_last[..., C_RWKV_LO:C_RWKV_LO + LANE]], axis=-1)


def _rwkv_state_in(s):
    lead = s.shape[:-3]
    return jnp.swapaxes(s, -3, -2).reshape(lead + (RWKV_HEAD, W_BRANCH))


def _rwkv_state_out(s):
    lead = s.shape[:-2]
    return jnp.swapaxes(s.reshape(lead + (RWKV_HEAD, RWKV_H, RWKV_HEAD)), -3, -2)


def _trunk_seq(x, mod, wts):
    b, seq_len, _ = x.shape
    t = b * seq_len
    lb = min(LB_CHUNK, seq_len)
    tm = min(TM_IN, seq_len)
    per_seq = seq_len // tm
    mod3 = mod.reshape(DEPTH * b * 3, 1, D_MODEL)
    x2 = x.reshape(t, D_MODEL)
    zeros = functools.partial(jnp.zeros, dtype=f32)
    new = []
    for l in range(DEPTH):
        mod_l = mod3[l * b * 3:(l + 1) * b * 3]
        proj2 = _inproj(x2, wts["norm_g3"], mod_l, wts["w_in_p"], l, per_seq, tm)
        proj3 = proj2.reshape(b, seq_len, PROJ_P)
        o_a, s_wkv = _rwkv_seq(proj3, zeros((b, 1, 1536)), zeros((b, 1, LANE)),
                               zeros((b, RWKV_HEAD, W_BRANCH)), wts["rwkv"], l, min(LB_RWKV, seq_len))
        o_b, st_gla = _chunk_seq("gla", proj3, wts["gla"], wts["gla_gn"], l, lb)
        o_c, st_hgrn = _chunk_seq("hgrn", proj3, wts["hgrn"], wts["hgrn_gn"], l, lb)
        o_d = _lru_seq(proj3, *wts["lru"], l, lb)
        outs = [o.reshape(t, W_BRANCH) for o in (o_a, o_b, o_c, o_d)]
        tmo = min(TM_OUT, seq_len)
        x2 = _outproj(outs, proj2, x2, mod_l, wts["w_out_b"], wts["final_g2"], l, seq_len // tmo, tmo,
                      final=(l == DEPTH - 1))
        last = proj3[:, seq_len - 1]
        new.append((
            _shift_state(last),
            _rwkv_state_out(s_wkv),
            _state_from_transposed(st_gla, GLA_H, GLA_DK),
            _state_from_transposed(st_hgrn, HGRN_H, HGRN_DK),
            proj3[:, seq_len - (CONV_W - 1):, C_LRU:C_LRU + W_BRANCH],
            o_d[:, seq_len - 1],
        ))
    return x2.reshape(b, seq_len, D_MODEL), tuple(jnp.stack([n[i] for n in new], axis=0) for i in range(6))


def _trunk_tok(x, mod, states, wts):
    b = x.shape[0]
    s_shift, s_wkv, s_gla, s_hgrn, s_conv, s_h = states
    x2 = x.reshape(b, D_MODEL)
    s_wkv_t = _rwkv_state_in(s_wkv)
    prev_rkv, prev_lo = s_shift[:, :, 0:1536], s_shift[:, :, 1536:1664]
    n_wkv = n_gla = n_hgrn = None
    new = []
    for l in range(DEPTH):
        proj2 = _inproj(x2, wts["norm_g3"], mod, wts["w_in_p"], l, None, b)
        o_a, n_wkv = _rwkv_tok(proj2, prev_rkv, prev_lo, s_wkv_t, n_wkv, wts["rwkv"], l, TOK_GROUP_RWKV)
        o_b, n_gla = _chunk_tok("gla", proj2, wts["gla"], wts["gla_gn"], s_gla, n_gla, l, TOK_GROUP)
        o_c, n_hgrn = _chunk_tok("hgrn", proj2, wts["hgrn"], wts["hgrn_gn"], s_hgrn, n_hgrn, l, TOK_GROUP)
        conv = s_conv[l]
        o_d = _lru_tok(proj2, [conv[:, 0], conv[:, 1], conv[:, 2]], s_h[l], *wts["lru"], l)
        x2 = _outproj([o_a, o_b, o_c, o_d], proj2, x2, mod, wts["w_out_b"], wts["final_g2"], l, None, b,
                      final=(l == DEPTH - 1))
        n_conv = jnp.concatenate([conv[:, 1:], proj2[:, None, C_LRU:C_LRU + W_BRANCH]], axis=1)
        new.append((_shift_state(proj2), n_conv, o_d))
    n_shift, n_conv, n_h = (jnp.stack([n[i] for n in new], axis=0) for i in range(3))
    return x2.reshape(b, 1, D_MODEL), (n_shift, _rwkv_state_out(n_wkv), n_gla, n_hgrn, n_conv, n_h)


def kernel(x_prompt, x_sample, c_prompt, c_sample, state_rwkv_shift, state_rwkv_wkv, state_gla, state_hgrn, state_lru_conv, state_lru_h, norm_g, w_ada, b_ada, w_in, w_out, rwkv_mu, rwkv_w0, rwkv_w_up, rwkv_a0, rwkv_a_up, rwkv_k_k, rwkv_k_a, rwkv_r_k, rwkv_gn_g, rwkv_gn_b, gla_gk_up, gla_gk_b, gla_gn_g, hgrn_lb_logits, hgrn_gn_g, lru_conv_w, lru_conv_b, lru_w_a, lru_b_a, lru_w_x, lru_b_x, lru_lambda, final_g):
    p = dict(norm_g=norm_g, w_in=w_in, w_out=w_out, rwkv_mu=rwkv_mu, rwkv_w0=rwkv_w0, rwkv_w_up=rwkv_w_up,
             rwkv_a0=rwkv_a0, rwkv_a_up=rwkv_a_up, rwkv_k_k=rwkv_k_k, rwkv_k_a=rwkv_k_a, rwkv_r_k=rwkv_r_k,
             rwkv_gn_g=rwkv_gn_g, rwkv_gn_b=rwkv_gn_b, gla_gk_up=gla_gk_up, gla_gk_b=gla_gk_b,
             gla_gn_g=gla_gn_g, hgrn_lb_logits=hgrn_lb_logits, hgrn_gn_g=hgrn_gn_g, lru_conv_w=lru_conv_w,
             lru_conv_b=lru_conv_b, lru_w_a=lru_w_a, lru_b_a=lru_b_a, lru_w_x=lru_w_x, lru_b_x=lru_b_x,
             lru_lambda=lru_lambda, final_g=final_g)
    wts = _prep_weights(p)
    bp = x_prompt.shape[0]
    bs = x_sample.shape[0]
    pad_rows = (-bp) % 8
    c_all = jnp.concatenate([c_prompt, jnp.zeros((pad_rows, D_MODEL), f32), c_sample], axis=0)
    mod = _ada(c_all, w_ada, b_ada)
    mod_p = mod[:, 0:bp]
    mod_s = mod[:, bp + pad_rows:bp + pad_rows + bs]
    y_p, st_p = _trunk_seq(x_prompt, mod_p, wts)
    states = (state_rwkv_shift, state_rwkv_wkv, state_gla, state_hgrn, state_lru_conv, state_lru_h)
    y_s, st_s = _trunk_tok(x_sample, mod_s, states, wts)
    return (y_p, y_s) + st_p + st_s
```

```python
import functools

import jax
import jax.numpy as jnp
from jax import lax
from jax.experimental import pallas as pl
from jax.experimental.pallas import tpu as pltpu

f32 = jnp.float32
bf16 = jnp.bfloat16

D_MODEL = 2048
DEPTH = 4
W_BRANCH = 512
EPS = 1e-6

RWKV_H = 8
RWKV_HEAD = 64
RWKV_LORA = 64
RWKV_PROJ = 3 * W_BRANCH + 2 * RWKV_LORA
RWKV_GN_EPS = 64e-5

GLA_H = 4
GLA_DK = 64
GLA_DV = 128
GLA_KW = GLA_H * GLA_DK
GLA_GATE_RANK = 16
GLA_GATE_NORM = 16.0

HGRN_H = 4
HGRN_DK = 128
HGRN_DV = 128
LB_FLOOR = 1e-30

LRU_BLOCKS = 8
LRU_BLOCK = 64
LRU_C = 8.0
CONV_W = 4

CHUNK = 32

C_R, C_K, C_V = 0, 512, 1024
C_GLA_V = 1536
C_HQ, C_HF, C_HI = 2048, 2560, 3072
C_LRU = 3584
C_GATE = 4096
C_GLA_Q, C_GLA_K = 6144, 6400
C_RWKV_LO = 6656
C_GLA_LO = 6784
PROJ_P = 6912

LANE = 128
MXU_TILE = 256
VMEM_LIMIT = 56 * 1024 * 1024

TM_IN = 1024
TN_IN = 768
TM_OUT = 256
LB_CHUNK = 256
LB_RWKV = 128
TOK_GROUP = 16
TOK_GROUP_RWKV = 8

HIGHEST = lax.Precision.HIGHEST


def _cparams(sem):
    return pltpu.CompilerParams(dimension_semantics=sem, vmem_limit_bytes=VMEM_LIMIT)


def _dotf(a, b):
    return jnp.dot(a, b, preferred_element_type=f32, precision=HIGHEST)


def _prec(a):
    return HIGHEST if a.dtype == f32 else None


def _dot_nt(a, b):
    return lax.dot_general(a, b, (((1,), (1,)), ((), ())), preferred_element_type=f32, precision=_prec(a))


def _dot_tn(a, b):
    return lax.dot_general(a, b, (((0,), (0,)), ((), ())), preferred_element_type=f32, precision=_prec(a))


def _split(x, pieces):
    out = []
    for i in range(pieces):
        part = x.astype(bf16)
        out.append(part)
        if i + 1 < pieces:
            x = x - part.astype(f32)
    return out


def _half_dot(x, ones):
    d = functools.partial(jnp.dot, preferred_element_type=f32)
    kh = ones.shape[0]
    if x.shape[1] == kh:
        return d(x, ones)
    return jnp.concatenate([d(x[:, 0:kh], ones), d(x[:, kh:2 * kh], ones)], axis=1)


def _seg_sum(x, ones, pieces=2):
    return sum(_half_dot(part, ones) for part in _split(x, pieces))


def _ones_seg(ones, x, pieces=3):
    d = functools.partial(jnp.dot, preferred_element_type=f32)
    return sum(d(ones, part) for part in _split(x, pieces))


def _sigmoid(x):
    return jax.nn.sigmoid(x)


def _softplus(x):
    return jnp.maximum(x, 0.0) + jnp.log1p(jnp.exp(-jnp.abs(x)))


def _log_sigmoid(x):
    return -_softplus(-x)


def _ada_kernel(c_ref, w_ref, b_ref, o_ref):
    c = c_ref[...]
    s = (c * _sigmoid(c)).astype(bf16)
    o_ref[...] = jnp.dot(s, w_ref[...].astype(bf16), preferred_element_type=f32) + b_ref[...]


def _ada(c_all, w_ada, b_ada):
    rows = c_all.shape[0]
    tn = 512
    n = w_ada.shape[2]
    return pl.pallas_call(
        _ada_kernel,
        grid=(DEPTH, n // tn),
        in_specs=[
            pl.BlockSpec((rows, D_MODEL), lambda l, j: (0, 0)),
            pl.BlockSpec((None, D_MODEL, tn), lambda l, j: (l, 0, j)),
            pl.BlockSpec((None, 1, tn), lambda l, j: (l, 0, j)),
        ],
        out_specs=pl.BlockSpec((None, rows, tn), lambda l, j: (l, 0, j)),
        out_shape=jax.ShapeDtypeStruct((DEPTH, rows, n), f32),
        compiler_params=_cparams(("parallel", "parallel")),
        name="ada_mod",
    )(c_all, w_ada, b_ada.reshape(DEPTH, 1, n))


def _inproj_kernel(x_ref, g_ref, sc_ref, sh_ref, w_ref, o_ref, h_ref):
    @pl.when(pl.program_id(1) == 0)
    def _():
        x = x_ref[...]
        ms = jnp.mean(x * x, axis=-1, keepdims=True)
        h = x * lax.rsqrt(ms + EPS) * g_ref[...]
        h = h * (1.0 + sc_ref[...]) + sh_ref[...]
        h_ref[...] = h.astype(bf16)

    o_ref[...] = _dot_nt(h_ref[...], w_ref[...])


def _mod_spec(l, which, per_seq_tiles, tm):
    if per_seq_tiles is None:
        return pl.BlockSpec((None, tm, D_MODEL), lambda i, *_: (l, 0, which))
    return pl.BlockSpec((None, 1, D_MODEL), lambda i, *_: (i // per_seq_tiles * 3 + which, 0, 0))


def _inproj(x2, norm_g3, mod, w_in_p, l, per_seq_tiles, tm):
    t = x2.shape[0]
    tn = TN_IN
    return pl.pallas_call(
        _inproj_kernel,
        grid=(t // tm, PROJ_P // tn),
        in_specs=[
            pl.BlockSpec((tm, D_MODEL), lambda i, j: (i, 0)),
            pl.BlockSpec((None, 1, D_MODEL), lambda i, j: (l, 0, 0)),
            _mod_spec(l, 1, per_seq_tiles, tm),
            _mod_spec(l, 0, per_seq_tiles, tm),
            pl.BlockSpec((None, tn, D_MODEL), lambda i, j: (l, j, 0)),
        ],
        out_specs=pl.BlockSpec((tm, tn), lambda i, j: (i, j)),
        out_shape=jax.ShapeDtypeStruct((t, PROJ_P), f32),
        scratch_shapes=[pltpu.VMEM((tm, D_MODEL), bf16)],
        compiler_params=_cparams(("parallel", "arbitrary")),
        name="in_proj",
    )(x2, norm_g3, mod, mod, w_in_p)


def _outproj_kernel(oa_ref, ob_ref, oc_ref, od_ref, pz_ref, x_ref, gate_ref, w_ref, fg_ref, o_ref, *, final):
    z = pz_ref[...]
    o = jnp.concatenate([oa_ref[...], ob_ref[...], oc_ref[...], od_ref[...]], axis=1)
    o = o * (z * _sigmoid(z))
    y = jnp.dot(o.astype(bf16), w_ref[...], preferred_element_type=f32)
    xn = x_ref[...] + gate_ref[...] * y
    if final:
        ms = jnp.mean(xn * xn, axis=-1, keepdims=True)
        xn = xn * lax.rsqrt(ms + EPS) * fg_ref[...]
    o_ref[...] = xn


def _outproj(outs, proj2, x2, mod, w_out_b, final_g2, l, per_seq_tiles, tm, final):
    t = x2.shape[0]
    mix_spec = pl.BlockSpec((tm, W_BRANCH), lambda i: (i, 0))
    return pl.pallas_call(
        functools.partial(_outproj_kernel, final=final),
        grid=(t // tm,),
        in_specs=[
            mix_spec, mix_spec, mix_spec, mix_spec,
            pl.BlockSpec((tm, D_MODEL), lambda i: (i, C_GATE // D_MODEL)),
            pl.BlockSpec((tm, D_MODEL), lambda i: (i, 0)),
            _mod_spec(l, 2, per_seq_tiles, tm),
            pl.BlockSpec((None, D_MODEL, D_MODEL), lambda i: (l, 0, 0)),
            pl.BlockSpec((1, D_MODEL), lambda i: (0, 0)),
        ],
        out_specs=pl.BlockSpec((tm, D_MODEL), lambda i: (i, 0)),
        out_shape=jax.ShapeDtypeStruct((t, D_MODEL), f32),
        compiler_params=_cparams(("parallel",)),
        name="out_proj",
    )(*outs, proj2, x2, mod, w_out_b, final_g2)


def _rwkv_prologue(r, k, v, lo, pr, pk, pv, plo, mu_rkv, mu_lo, wup, pvec, bo):
    w0, a0, k_k, k_a, r_k = (pvec[i:i + 1, :] for i in range(5))
    xr = r + (pr - r) * mu_rkv[:, 0:512]
    xk = k + (pk - k) * mu_rkv[:, 512:1024]
    xv = v + (pv - v) * mu_rkv[:, 1024:1536]
    xlo = lo + (plo - lo) * mu_lo
    lane = lax.broadcasted_iota(jnp.int32, xlo.shape, 1)
    act = jnp.where(lane < RWKV_LORA, jnp.tanh(xlo), xlo)
    up = _dotf(act, wup)
    w_raw = -_softplus(-(w0 + up[:, 0:512])) - 0.5
    ew = jnp.exp(-jnp.exp(w_raw))
    a = _sigmoid(a0 + up[:, 512:1024])
    kk = xk * k_k
    kk = kk / jnp.maximum(jnp.sqrt(_seg_sum(kk * kk, bo)), 1e-12)
    kh = xk * (1.0 + (a - 1.0) * k_a)
    alp = kk * a
    ar = _seg_sum(alp * xr, bo)
    return dict(
        kap=kk, ew=ew, alp=alp, kh=kh, vv=xv,
        wr=ew * xr - ar * kk,
        kr=_seg_sum(kh * xr, bo),
        bonus=_seg_sum(xr * kh * r_k, bo) * xv,
    )


_RWKV_STEP_KEYS = ("kap", "ew", "alp", "kh", "vv", "wr", "kr")


def _rwkv_step(s, kap, ew, alp, kh, vv, wr, kr, bo, idt):
    sk_hi, sk_mid = _split(s * kap, 2)
    x = jnp.concatenate([sk_hi, sk_mid, (s * wr).astype(bf16), idt.astype(bf16) * vv.astype(bf16)], axis=0)
    red = _half_dot(x, bo)
    sk = red[0:64] + red[64:128]
    s_new = s * ew - sk * alp + red[192:256] * kh
    o_row = jnp.sum(red[128:192] * idt, axis=0, keepdims=True) + vv * kr
    return s_new, o_row


def _rwkv_epilogue(o, bonus, pvec, bo):
    gn_g, gn_b = pvec[5:6, :], pvec[6:7, :]
    mu = _seg_sum(o, bo) * (1.0 / RWKV_HEAD)
    d = o - mu
    var = _seg_sum(d * d, bo) * (1.0 / RWKV_HEAD)
    return d * lax.rsqrt(var + RWKV_GN_EPS) * gn_g + gn_b + bonus


def _rwkv_seq_kernel(r_ref, k_ref, v_ref, lo_ref, prkv_ref, plo_ref, sin_ref, mu_rkv_ref, mu_lo_ref, wup_ref,
                     pvec_ref, bo_ref, idt_ref, o_ref, sout_ref,
                     s_sc, crkv_sc, clo_sc, kap_sc, ew_sc, alp_sc, kh_sc, vv_sc, wr_sc, kr_sc, oraw_sc,
                     bonus_sc, *, g_seqs, lb):
    tb = pl.program_id(1)

    @pl.when(tb == 0)
    def _():
        s_sc[...] = sin_ref[...]
        crkv_sc[...] = prkv_ref[...]
        clo_sc[...] = plo_ref[...]

    bo = bo_ref[...]
    idt = idt_ref[...]
    pvec = pvec_ref[...]
    step_sc = dict(kap=kap_sc, ew=ew_sc, alp=alp_sc, kh=kh_sc, vv=vv_sc, wr=wr_sc, kr=kr_sc)

    row0 = lax.broadcasted_iota(jnp.int32, (lb, 1), 0) == 0
    for g in range(g_seqs):
        cur = [r_ref[g], k_ref[g], v_ref[g]]
        lo = lo_ref[g]
        carry = crkv_sc[g]
        prev = [jnp.where(row0, carry[:, i * 512:(i + 1) * 512], pltpu.roll(c, 1, 0)) for i, c in enumerate(cur)]
        plo = jnp.where(row0, clo_sc[g], pltpu.roll(lo, 1, 0))
        res = _rwkv_prologue(cur[0], cur[1], cur[2], lo, prev[0], prev[1], prev[2], plo,
                             mu_rkv_ref[...], mu_lo_ref[...], wup_ref[...], pvec, bo)
        for key in _RWKV_STEP_KEYS:
            step_sc[key][g] = res[key]
        bonus_sc[g] = res["bonus"]
        for i, c in enumerate(cur):
            crkv_sc[g, :, i * 512:(i + 1) * 512] = c[lb - 1:lb, :]
        clo_sc[g] = lo[lb - 1:lb, :]

    def body(t, carry):
        for g in range(g_seqs):
            rows = [step_sc[key][g, pl.ds(t, 1), :] for key in _RWKV_STEP_KEYS]
            s_new, o_row = _rwkv_step(s_sc[g], *rows, bo, idt)
            s_sc[g] = s_new
            oraw_sc[g, pl.ds(t, 1), :] = o_row
        return carry

    lax.fori_loop(0, lb, body, 0, unroll=8)

    for g in range(g_seqs):
        o_ref[g] = _rwkv_epilogue(oraw_sc[g], bonus_sc[g], pvec, bo)

    @pl.when(tb == pl.num_programs(1) - 1)
    def _():
        sout_ref[...] = s_sc[...]


def _rwkv_tok_kernel(r_ref, k_ref, v_ref, lo_ref, prkv_ref, plo_ref, sin_ref, mu_rkv_ref, mu_lo_ref, wup_ref,
                     pvec_ref, bo_ref, idt_ref, *rest, g_rows):
    o_ref, sout_ref = rest[-2:]
    bo = bo_ref[...]
    idt = idt_ref[...]
    pvec = pvec_ref[...]
    prkv = prkv_ref[...]
    res = _rwkv_prologue(r_ref[...], k_ref[...], v_ref[...], lo_ref[...],
                         prkv[:, 0:512], prkv[:, 512:1024], prkv[:, 1024:1536], plo_ref[...],
                         mu_rkv_ref[...], mu_lo_ref[...], wup_ref[...], pvec, bo)
    o_rows = []
    for g in range(g_rows):
        rows = [res[key][g:g + 1, :] for key in _RWKV_STEP_KEYS]
        s_new, o_row = _rwkv_step(sin_ref[g], *rows, bo, idt)
        sout_ref[g] = s_new
        o_rows.append(o_row)
    o_ref[...] = _rwkv_epilogue(jnp.concatenate(o_rows, axis=0), res["bonus"], pvec, bo)


def _rwkv_consts():
    i = jnp.arange(W_BRANCH)
    j = jnp.arange(MXU_TILE)
    bo = (j[:, None] // RWKV_HEAD == j[None, :] // RWKV_HEAD).astype(bf16)
    idt = (jnp.arange(RWKV_HEAD)[:, None] == (i[None, :] % RWKV_HEAD)).astype(f32)
    return bo, idt


def _rwkv_weight_specs(l):
    def cs(shape):
        return pl.BlockSpec((None,) + shape, lambda *_: (l,) + (0,) * len(shape))

    def const(shape):
        return pl.BlockSpec(shape, lambda *_: (0,) * len(shape))

    return [cs((1, 1536)), cs((1, LANE)), cs((LANE, 1024)), cs((8, W_BRANCH)),
            const((MXU_TILE, MXU_TILE)), const((RWKV_HEAD, W_BRANCH))]


def _rwkv_seq(proj3, prev_rkv, prev_lo, s_in, wts, l, lb):
    b, seq_len, _ = proj3.shape
    g = b
    nt = seq_len // lb

    def col(width, off):
        return pl.BlockSpec((g, lb, width), lambda i, t: (i, t, off // width))

    tok_sc = pltpu.VMEM((g, lb, W_BRANCH), f32)
    out = pl.pallas_call(
        functools.partial(_rwkv_seq_kernel, g_seqs=g, lb=lb),
        grid=(b // g, nt),
        in_specs=[
            col(512, C_R), col(512, C_K), col(512, C_V), col(LANE, C_RWKV_LO),
            pl.BlockSpec((g, 1, 1536), lambda i, t: (i, 0, 0)),
            pl.BlockSpec((g, 1, LANE), lambda i, t: (i, 0, 0)),
            pl.BlockSpec((g, RWKV_HEAD, W_BRANCH), lambda i, t: (i, 0, 0)),
        ] + _rwkv_weight_specs(l),
        out_specs=[
            pl.BlockSpec((g, lb, W_BRANCH), lambda i, t: (i, t, 0)),
            pl.BlockSpec((g, RWKV_HEAD, W_BRANCH), lambda i, t: (i, 0, 0)),
        ],
        out_shape=[
            jax.ShapeDtypeStruct((b, seq_len, W_BRANCH), f32),
            jax.ShapeDtypeStruct((b, RWKV_HEAD, W_BRANCH), f32),
        ],
        scratch_shapes=[
            pltpu.VMEM((g, RWKV_HEAD, W_BRANCH), f32),
            pltpu.VMEM((g, 1, 1536), f32),
            pltpu.VMEM((g, 1, LANE), f32),
        ] + [tok_sc] * 9,
        compiler_params=_cparams(("parallel", "arbitrary")),
        name="rwkv_seq",
    )(proj3, proj3, proj3, proj3, prev_rkv, prev_lo, s_in, *wts)
    return out


def _rwkv_tok(proj2, prev_rkv, prev_lo, s_all, acc, wts, l, g):
    b = proj2.shape[0]

    def col(width, off):
        return pl.BlockSpec((g, width), lambda i: (i, off // width))

    state_spec = pl.BlockSpec((None, g, RWKV_HEAD, W_BRANCH), lambda i: (l, i, 0, 0))
    in_specs = [
        col(512, C_R), col(512, C_K), col(512, C_V), col(LANE, C_RWKV_LO),
        pl.BlockSpec((None, g, 1536), lambda i: (l, i, 0)),
        pl.BlockSpec((None, g, LANE), lambda i: (l, i, 0)),
        state_spec,
    ] + _rwkv_weight_specs(l)
    args = [proj2, proj2, proj2, proj2, prev_rkv, prev_lo, s_all, *wts]
    aliases = {}
    if acc is not None:
        in_specs.append(pl.BlockSpec(memory_space=pl.ANY))
        aliases = {len(args): 1}
        args.append(acc)
    return pl.pallas_call(
        functools.partial(_rwkv_tok_kernel, g_rows=g),
        grid=(b // g,),
        in_specs=in_specs,
        out_specs=[pl.BlockSpec((g, W_BRANCH), lambda i: (i, 0)), state_spec],
        out_shape=[
            jax.ShapeDtypeStruct((b, W_BRANCH), f32),
            jax.ShapeDtypeStruct(s_all.shape, f32),
        ],
        input_output_aliases=aliases,
        compiler_params=_cparams(("parallel",)),
        name="rwkv_tok",
    )(*args)


def _gla_inputs(q_ref, k_ref, v_ref, lo_ref, gup_ref, gb_ref):
    q = q_ref[...] * (GLA_DK ** -0.5)
    z = _dotf(lo_ref[...], gup_ref[...]) + gb_ref[...]
    g = _log_sigmoid(z) * (1.0 / GLA_GATE_NORM)
    return q, k_ref[...], v_ref[...], g


def _hgrn_lb(logits, l):
    m = jnp.max(logits, axis=0, keepdims=True)
    e = jnp.exp(logits - m)
    sm = e / jnp.sum(e, axis=0, keepdims=True)
    lb = jnp.zeros_like(sm[0:1, :])
    for i in range(1, l + 1):
        lb = lb + sm[i:i + 1, :]
    return lb


def _hgrn_inputs(q_ref, f_ref, i_ref, logits_ref, l):
    lb = _hgrn_lb(logits_ref[...], l)
    f_lo = f_ref[...]
    a = jnp.log(jnp.maximum(lb, LB_FLOOR))
    b = jnp.log1p(-lb) + _log_sigmoid(f_lo)
    logf = jnp.maximum(a, b) + jnp.log1p(jnp.exp(-jnp.abs(a - b)))
    k = (1.0 - lb) * _sigmoid(-f_lo)
    return q_ref[...], k, i_ref[...], logf


def _chunk_core(q, k, v, g, lt_ref, esum_ref, vmask_ref, gmask_ref, bon_ref, gn_ref, st_sc, kpad, bpad, lb,
                heads_per_group):
    bc = _ones_seg(lt_ref[...], g)
    nc = lb // CHUNK
    f = q.shape[1]
    kpad[...] = k.reshape(nc, CHUNK, f)
    bpad[...] = bc.reshape(nc, CHUNK, f)
    q3 = q.reshape(nc, CHUNK, f)
    bc3 = bc.reshape(nc, CHUNK, f)
    esum = esum_ref[...]

    lane_j = lax.broadcasted_iota(jnp.int32, (1, 1, LANE), 2) & (CHUNK - 1)
    att3 = None
    for r0 in range(0, CHUNK, 8):
        rows = CHUNK - r0
        q_s, bc_s = q3[:, r0:, :], bc3[:, r0:, :]
        rowc = lax.broadcasted_iota(jnp.int32, (1, rows, 1), 1) + r0
        acc = jnp.zeros((nc, rows, LANE), f32)
        for j in range(r0, r0 + 8):
            kj, bj = kpad[:, j:j + 1, :], bpad[:, j:j + 1, :]
            z = jnp.where(rowc >= j, q_s * kj * jnp.exp(bc_s - bj), 0.0)
            a = jnp.dot(z.reshape(nc * rows, f).astype(bf16), esum, preferred_element_type=f32)
            acc = acc + jnp.where(lane_j == j, a.reshape(nc, rows, LANE), 0.0)
        if r0:
            acc = jnp.concatenate([jnp.zeros((nc, r0, LANE), f32), acc], axis=1)
        att3 = acc if att3 is None else att3 + acc
    n_heads = LANE // CHUNK
    v3 = v.reshape(nc, CHUNK, W_BRANCH)
    vbd = jnp.concatenate([v3] * n_heads, axis=1) * vmask_ref[...]
    o = lax.dot_general(att3.astype(bf16), vbd.astype(bf16), (((2,), (1,)), ((0,), (0,))),
                        preferred_element_type=f32).reshape(lb, W_BRANCH)

    ng = f // LANE
    vw = W_BRANCH // ng
    sts = [st_sc[gi] for gi in range(ng)]
    outs = []
    for c in range(nc):
        sl = slice(c * CHUNK, (c + 1) * CHUNK)
        bcc = bc[sl]
        blast = bcc[CHUNK - 1:CHUNK, :]
        dec = jnp.exp(blast)
        qe = (q[sl] * jnp.exp(bcc)).astype(bf16)
        ke = (k[sl] * jnp.exp(blast - bcc)).astype(bf16)
        vb = v[sl].astype(bf16)
        parts = []
        for gi in range(ng):
            kl = slice(gi * LANE, (gi + 1) * LANE)
            vl = slice(gi * vw, (gi + 1) * vw)
            parts.append(_dot_nt(qe[:, kl], sts[gi].astype(bf16)))
            upd = _dot_tn(vb[:, vl], ke[:, kl])
            if heads_per_group > 1:
                upd = upd * gmask_ref[...]
            sts[gi] = sts[gi] * dec[:, kl] + upd
        outs.append(o[sl] + jnp.concatenate(parts, axis=1))
    for gi in range(ng):
        st_sc[gi] = sts[gi]
    o = jnp.concatenate(outs, axis=0)
    ms = _seg_sum(o * o, bon_ref[...]) * (1.0 / LANE)
    return o * lax.rsqrt(ms + EPS) * gn_ref[...]


def _chunk_seq_kernel(*refs, kind, l, lb):
    n_in = 6 if kind == "gla" else 4
    ins = refs[:n_in]
    lt_ref, esum_ref, vmask_ref, gmask_ref, bon_ref, gn_ref, o_ref, stout_ref, st_sc, kpad, bpad = refs[n_in:]
    tb = pl.program_id(1)

    @pl.when(tb == 0)
    def _():
        st_sc[...] = jnp.zeros_like(st_sc)

    if kind == "gla":
        q, k, v, g = _gla_inputs(*ins)
        heads_per_group = LANE // GLA_DK
    else:
        q, k, v, g = _hgrn_inputs(*ins, l)
        heads_per_group = LANE // HGRN_DK
    o_ref[...] = _chunk_core(q, k, v, g, lt_ref, esum_ref, vmask_ref, gmask_ref, bon_ref, gn_ref, st_sc, kpad,
                             bpad, lb, heads_per_group)

    @pl.when(tb == pl.num_programs(1) - 1)
    def _():
        stout_ref[...] = st_sc[...]


def _chunk_consts(h, dk, lb):
    f = h * dk
    i = jnp.arange(lb)
    lt = ((i[:, None] // CHUNK == i[None, :] // CHUNK) & (i[:, None] >= i[None, :])).astype(bf16)
    fi = jnp.arange(f)
    li = jnp.arange(LANE)
    oi = jnp.arange(W_BRANCH)
    vw = W_BRANCH // (f // LANE)
    esum = (fi[:, None] // dk == li[None, :] // CHUNK).astype(bf16)
    vmask = (li[:, None] // CHUNK == oi[None, :] // LANE).astype(f32)
    gmask = (jnp.arange(vw)[:, None] // LANE == li[None, :] // dk).astype(f32)
    ti = jnp.arange(MXU_TILE)
    bon = (ti[:, None] // LANE == ti[None, :] // LANE).astype(bf16)
    return lt, esum, vmask, gmask, bon


def _chunk_seq(kind, proj3, extra, gn3, l, lb):
    b, seq_len, _ = proj3.shape
    h, dk = (GLA_H, GLA_DK) if kind == "gla" else (HGRN_H, HGRN_DK)
    f = h * dk
    nt = seq_len // lb

    def col(width, off):
        return pl.BlockSpec((None, lb, width), lambda i, t: (i, t, off // width))

    def const(shape):
        return pl.BlockSpec(shape, lambda *_: (0,) * len(shape))

    def layer(shape):
        return pl.BlockSpec((None,) + shape, lambda *_: (l,) + (0,) * len(shape))

    if kind == "gla":
        gup, gb = extra
        in_specs = [col(GLA_KW, C_GLA_Q), col(GLA_KW, C_GLA_K), col(W_BRANCH, C_GLA_V), col(LANE, C_GLA_LO),
                    layer((LANE, GLA_KW)), layer((1, GLA_KW))]
        args = [proj3, proj3, proj3, proj3, gup, gb]
    else:
        (logits,) = extra
        in_specs = [col(W_BRANCH, C_HQ), col(W_BRANCH, C_HF), col(W_BRANCH, C_HI), const((DEPTH, W_BRANCH))]
        args = [proj3, proj3, proj3, logits]
    consts = _chunk_consts(h, dk, lb)
    in_specs += [const(c.shape) for c in consts] + [layer((1, W_BRANCH))]
    ng = f // LANE
    st_shape = (ng, W_BRANCH // ng, LANE)
    return pl.pallas_call(
        functools.partial(_chunk_seq_kernel, kind=kind, l=l, lb=lb),
        grid=(b, nt),
        in_specs=in_specs,
        out_specs=[
            pl.BlockSpec((None, lb, W_BRANCH), lambda i, t: (i, t, 0)),
            pl.BlockSpec((None,) + st_shape, lambda i, t: (i, 0, 0, 0)),
        ],
        out_shape=[
            jax.ShapeDtypeStruct((b, seq_len, W_BRANCH), f32),
            jax.ShapeDtypeStruct((b,) + st_shape, f32),
        ],
        scratch_shapes=[
            pltpu.VMEM(st_shape, f32),
            pltpu.VMEM((lb // CHUNK, CHUNK, f), f32),
            pltpu.VMEM((lb // CHUNK, CHUNK, f), f32),
        ],
        compiler_params=_cparams(("parallel", "arbitrary")),
        name=kind + "_seq",
    )(*args, *consts, gn3)


def _state_from_transposed(st, h, dk):
    b, ng = st.shape[0], st.shape[1]
    hpg = h // ng
    st = st.reshape(b, ng, hpg, LANE, hpg, dk)
    diag = jnp.stack([st[:, :, i, :, i, :] for i in range(hpg)], axis=2)
    return jnp.swapaxes(diag.reshape(b, h, LANE, dk), 2, 3)


def _key_columns(x, g, dk, odd):
    xt = jnp.concatenate([x, jnp.zeros((LANE - g, LANE), f32)], axis=0).T
    if dk == LANE:
        return xt
    return jnp.where(odd, xt[dk:2 * dk], xt[0:dk])


def _chunk_tok_kernel(*refs, kind, l, g, has_acc):
    ins, (o_ref, sout_ref) = refs[:-2], refs[-2:]
    if has_acc:
        ins = ins[:-1]
    odd = (pl.program_id(1) % 2) == 1
    if kind == "gla":
        q_ref, k_ref, v_ref, lo_ref, gup_ref, gb_ref, s_ref, gn_ref = ins
        q = q_ref[...] * (GLA_DK ** -0.5)
        k = k_ref[...]
        z = _dotf(lo_ref[...], gup_ref[...]) + gb_ref[...]
        dec = jnp.exp(_log_sigmoid(z) * (1.0 / GLA_GATE_NORM))
        v = v_ref[...]
        dk = GLA_DK
    else:
        q_ref, f_ref, i_ref, logits_ref, s_ref, gn_ref = ins
        q, k, v, logf = _hgrn_inputs(q_ref, f_ref, i_ref, logits_ref, l)
        dec = jnp.exp(logf)
        dk = HGRN_DK
    qt, kt, dt = (_key_columns(x, g, dk, odd) for x in (q, k, dec))
    o_rows = []
    for b in range(g):
        qcol, kcol, dcol = (jnp.broadcast_to(t[:, b:b + 1], (dk, LANE)) for t in (qt, kt, dt))
        s_new = s_ref[b] * dcol + kcol * v[b:b + 1, :]
        sout_ref[b] = s_new
        o_rows.append(jnp.sum(qcol * s_new, axis=0, keepdims=True))
    o = jnp.concatenate(o_rows, axis=0)
    ms = jnp.mean(o * o, axis=-1, keepdims=True)
    o_ref[...] = o * lax.rsqrt(ms + EPS) * gn_ref[...]


def _chunk_tok(kind, proj2, extra, gn3, s_all, acc, l, g):
    b = proj2.shape[0]
    h, dk = (GLA_H, GLA_DK) if kind == "gla" else (HGRN_H, HGRN_DK)

    def head_col(off):
        per = LANE // dk
        return pl.BlockSpec((g, LANE), lambda i, j: (i, off // LANE + j // per))

    if kind == "gla":
        gup, gb = extra
        in_specs = [head_col(C_GLA_Q), head_col(C_GLA_K),
                    pl.BlockSpec((g, LANE), lambda i, j: (i, C_GLA_V // LANE + j)),
                    pl.BlockSpec((g, LANE), lambda i, j: (i, C_GLA_LO // LANE)),
                    pl.BlockSpec((None, LANE, LANE), lambda i, j: (l, 0, j // 2)),
                    pl.BlockSpec((None, 1, LANE), lambda i, j: (l, 0, j // 2))]
        args = [proj2, proj2, proj2, proj2, gup, gb]
    else:
        (logits,) = extra
        in_specs = [head_col(C_HQ), head_col(C_HF), head_col(C_HI),
                    pl.BlockSpec((DEPTH, LANE), lambda i, j: (0, j))]
        args = [proj2, proj2, proj2, logits]
    state_spec = pl.BlockSpec((None, g, None, dk, LANE), lambda i, j: (l, i, j, 0, 0))
    in_specs += [state_spec, pl.BlockSpec((None, 1, LANE), lambda i, j: (l, 0, j))]
    args += [s_all, gn3]
    aliases = {}
    if acc is not None:
        in_specs.append(pl.BlockSpec(memory_space=pl.ANY))
        aliases = {len(args): 1}
        args.append(acc)
    return pl.pallas_call(
        functools.partial(_chunk_tok_kernel, kind=kind, l=l, g=g, has_acc=acc is not None),
        grid=(b // g, h),
        in_specs=in_specs,
        out_specs=[pl.BlockSpec((g, LANE), lambda i, j: (i, j)), state_spec],
        out_shape=[
            jax.ShapeDtypeStruct((b, W_BRANCH), f32),
            jax.ShapeDtypeStruct(s_all.shape, f32),
        ],
        input_output_aliases=aliases,
        compiler_params=_cparams(("parallel", "parallel")),
        name=kind + "_tok",
    )(*args)


def _lru_gate_dot(y_hi, y_lo, w_ref, col0):
    d = functools.partial(jnp.dot, preferred_element_type=f32)
    halves = []
    for h0 in range(0, W_BRANCH, MXU_TILE):
        w_hi = w_ref[0, h0:h0 + MXU_TILE, col0 + h0:col0 + h0 + MXU_TILE]
        w_lo = w_ref[1, h0:h0 + MXU_TILE, col0 + h0:col0 + h0 + MXU_TILE]
        a, b = y_hi[:, h0:h0 + MXU_TILE], y_lo[:, h0:h0 + MXU_TILE]
        halves.append(d(a, w_hi) + d(a, w_lo) + d(b, w_hi))
    return jnp.concatenate(halves, axis=1)


def _lru_gates(y, w_ref, pvec):
    b_a, b_x, lam = pvec[1:2, :], pvec[2:3, :], pvec[3:4, :]
    y_hi, y_lo = _split(y, 2)
    r = _sigmoid(_lru_gate_dot(y_hi, y_lo, w_ref, 0) + b_a)
    ig = _sigmoid(_lru_gate_dot(y_hi, y_lo, w_ref, W_BRANCH) + b_x)
    log_a = -LRU_C * r * _softplus(-lam)
    a = jnp.exp(log_a)
    one_m_a2 = -jnp.tanh(log_a) * (jnp.exp(2.0 * log_a) + 1.0)
    b = jnp.sqrt(one_m_a2) * (ig * y)
    return a, b


def _lru_seq_kernel(x_ref, cw_ref, pvec_ref, wbd_ref, o_ref, xpad, hcar, *, lb):
    tb = pl.program_id(1)

    @pl.when(tb == 0)
    def _():
        xpad[0:8, :] = jnp.zeros((8, W_BRANCH), f32)
        hcar[...] = jnp.zeros_like(hcar)

    x = x_ref[...]
    xpad[8:8 + lb, :] = x
    cw = cw_ref[...]
    pvec = pvec_ref[...]
    y = pvec[0:1, :] + x * cw[3:4, :]
    for j in range(CONV_W - 1):
        y = y + xpad[5 + j:5 + j + lb, :] * cw[j:j + 1, :]
    a, b = _lru_gates(y, wbd_ref, pvec)
    row = lax.broadcasted_iota(jnp.int32, (lb, 1), 0)
    s = 1
    while s < lb:
        m = row >= s
        b = jnp.where(m, a * pltpu.roll(b, s, 0) + b, b)
        a = jnp.where(m, a * pltpu.roll(a, s, 0), a)
        s *= 2
    h = a * hcar[...] + b
    o_ref[...] = h
    hcar[...] = h[lb - 1:lb, :]
    xpad[5:8, :] = x[lb - 3:lb, :]


def _lru_seq(proj3, cw, pvec, wbd, l, lb):
    b, seq_len, _ = proj3.shape

    def layer(shape):
        return pl.BlockSpec((None,) + shape, lambda *_: (l,) + (0,) * len(shape))

    return pl.pallas_call(
        functools.partial(_lru_seq_kernel, lb=lb),
        grid=(b, seq_len // lb),
        in_specs=[
            pl.BlockSpec((None, lb, W_BRANCH), lambda i, t: (i, t, C_LRU // W_BRANCH)),
            layer((CONV_W, W_BRANCH)), layer((8, W_BRANCH)), layer((2, W_BRANCH, 2 * W_BRANCH)),
        ],
        out_specs=pl.BlockSpec((None, lb, W_BRANCH), lambda i, t: (i, t, 0)),
        out_shape=jax.ShapeDtypeStruct((b, seq_len, W_BRANCH), f32),
        scratch_shapes=[
            pltpu.VMEM((8 + lb, W_BRANCH), f32),
            pltpu.VMEM((1, W_BRANCH), f32),
        ],
        compiler_params=_cparams(("parallel", "arbitrary")),
        name="lru_seq",
    )(proj3, cw, pvec, wbd)


def _lru_tok_kernel(x_ref, b0_ref, b1_ref, b2_ref, h0_ref, cw_ref, pvec_ref, wbd_ref, o_ref):
    cw = cw_ref[...]
    pvec = pvec_ref[...]
    y = (pvec[0:1, :] + b0_ref[...] * cw[0:1, :] + b1_ref[...] * cw[1:2, :] + b2_ref[...] * cw[2:3, :]
         + x_ref[...] * cw[3:4, :])
    a, b = _lru_gates(y, wbd_ref, pvec)
    o_ref[...] = a * h0_ref[...] + b


def _lru_tok(proj2, bufs, h0, cw, pvec, wbd, l):
    b = proj2.shape[0]

    def layer(shape):
        return pl.BlockSpec((None,) + shape, lambda *_: (l,) + (0,) * len(shape))

    row = pl.BlockSpec((b, W_BRANCH), lambda i: (0, 0))
    return pl.pallas_call(
        _lru_tok_kernel,
        grid=(1,),
        in_specs=[pl.BlockSpec((b, W_BRANCH), lambda i: (0, C_LRU // W_BRANCH)), row, row, row, row,
                  layer((CONV_W, W_BRANCH)), layer((8, W_BRANCH)), layer((2, W_BRANCH, 2 * W_BRANCH))],
        out_specs=row,
        out_shape=jax.ShapeDtypeStruct((b, W_BRANCH), f32),
        compiler_params=_cparams(("arbitrary",)),
        name="lru_tok",
    )(proj2, *bufs, h0, cw, pvec, wbd)


def _prep_weights(p):
    w = jnp.swapaxes(p["w_in"], 1, 2)
    pad = jnp.zeros((DEPTH, LANE - GLA_GATE_RANK, D_MODEL), w.dtype)
    w_in_p = jnp.concatenate([
        w[:, 0:1536], w[:, 2176:2688], w[:, 2704:4240], w[:, 4240:4752], w[:, 4752:6800],
        w[:, 1664:2176], w[:, 1536:1664], w[:, 2688:2704], pad], axis=1).astype(bf16)
    zl = jnp.zeros((DEPTH, RWKV_LORA, W_BRANCH), f32)
    wup = jnp.concatenate([jnp.concatenate([p["rwkv_w_up"], zl], axis=2),
                           jnp.concatenate([zl, p["rwkv_a_up"]], axis=2)], axis=1)
    zrow = jnp.zeros((DEPTH, W_BRANCH), f32)
    rwkv_vec = jnp.stack([p["rwkv_w0"], p["rwkv_a0"], p["rwkv_k_k"], p["rwkv_k_a"],
                          p["rwkv_r_k"].reshape(DEPTH, W_BRANCH), p["rwkv_gn_g"], p["rwkv_gn_b"], zrow], axis=1)
    mu = p["rwkv_mu"]
    gup = jnp.concatenate([p["gla_gk_up"], jnp.zeros((DEPTH, LANE - GLA_GATE_RANK, GLA_KW), f32)], axis=1)
    eye = jnp.eye(LRU_BLOCKS, dtype=f32)

    def bd(wb):
        return jnp.einsum("lhij,hg->lhigj", wb, eye).reshape(DEPTH, W_BRANCH, W_BRANCH)

    lru_vec = jnp.stack([p["lru_conv_b"], p["lru_b_a"], p["lru_b_x"], p["lru_lambda"],
                         zrow, zrow, zrow, zrow], axis=1)
    lru_w = jnp.concatenate([bd(p["lru_w_a"]), bd(p["lru_w_x"])], axis=2)
    lru_w_hi = lru_w.astype(bf16)
    lru_w_lo = (lru_w - lru_w_hi.astype(f32)).astype(bf16)
    return dict(
        w_in_p=w_in_p,
        w_out_b=p["w_out"].astype(bf16),
        norm_g3=p["norm_g"].reshape(DEPTH, 1, D_MODEL),
        final_g2=p["final_g"].reshape(1, D_MODEL),
        rwkv=(mu[:, None, 0:1536], mu[:, None, 1536:1664], wup, rwkv_vec) + _rwkv_consts(),
        gla=(gup, p["gla_gk_b"].reshape(DEPTH, 1, GLA_KW)),
        gla_gn=p["gla_gn_g"].reshape(DEPTH, 1, W_BRANCH),
        hgrn=(p["hgrn_lb_logits"],),
        hgrn_gn=p["hgrn_gn_g"].reshape(DEPTH, 1, W_BRANCH),
        lru=(p["lru_conv_w"], lru_vec, jnp.stack([lru_w_hi, lru_w_lo], axis=1)),
    )


def _shift_state(proj_last):
    return jnp.concatenate([proj_last[..., 0:1536], proj_last[..., C_RWKV_LO:C_RWKV_LO + LANE]], axis=-1)


def _rwkv_state_in(s):
    lead = s.shape[:-3]
    return jnp.swapaxes(s, -3, -2).reshape(lead + (RWKV_HEAD, W_BRANCH))


def _rwkv_state_out(s):
    lead = s.shape[:-2]
    return jnp.swapaxes(s.reshape(lead + (RWKV_HEAD, RWKV_H, RWKV_HEAD)), -3, -2)


def _trunk_seq(x, mod, wts):
    b, seq_len, _ = x.shape
    t = b * seq_len
    lb = min(LB_CHUNK, seq_len)
    tm = min(TM_IN, seq_len)
    per_seq = seq_len // tm
    mod3 = mod.reshape(DEPTH * b * 3, 1, D_MODEL)
    x2 = x.reshape(t, D_MODEL)
    zeros = functools.partial(jnp.zeros, dtype=f32)
    new = []
    for l in range(DEPTH):
        mod_l = mod3[l * b * 3:(l + 1) * b * 3]
        proj2 = _inproj(x2, wts["norm_g3"], mod_l, wts["w_in_p"], l, per_seq, tm)
        proj3 = proj2.reshape(b, seq_len, PROJ_P)
        o_a, s_wkv = _rwkv_seq(proj3, zeros((b, 1, 1536)), zeros((b, 1, LANE)),
                               zeros((b, RWKV_HEAD, W_BRANCH)), wts["rwkv"], l, min(LB_RWKV, seq_len))
        o_b, st_gla = _chunk_seq("gla", proj3, wts["gla"], wts["gla_gn"], l, lb)
        o_c, st_hgrn = _chunk_seq("hgrn", proj3, wts["hgrn"], wts["hgrn_gn"], l, lb)
        o_d = _lru_seq(proj3, *wts["lru"], l, lb)
        outs = [o.reshape(t, W_BRANCH) for o in (o_a, o_b, o_c, o_d)]
        tmo = min(TM_OUT, seq_len)
        x2 = _outproj(outs, proj2, x2, mod_l, wts["w_out_b"], wts["final_g2"], l, seq_len // tmo, tmo,
                      final=(l == DEPTH - 1))
        last = proj3[:, seq_len - 1]
        new.append((
            _shift_state(last),
            _rwkv_state_out(s_wkv),
            _state_from_transposed(st_gla, GLA_H, GLA_DK),
            _state_from_transposed(st_hgrn, HGRN_H, HGRN_DK),
            proj3[:, seq_len - (CONV_W - 1):, C_LRU:C_LRU + W_BRANCH],
            o_d[:, seq_len - 1],
        ))
    return x2.reshape(b, seq_len, D_MODEL), tuple(jnp.stack([n[i] for n in new], axis=0) for i in range(6))


def _trunk_tok(x, mod, states, wts):
    b = x.shape[0]
    s_shift, s_wkv, s_gla, s_hgrn, s_conv, s_h = states
    x2 = x.reshape(b, D_MODEL)
    s_wkv_t = _rwkv_state_in(s_wkv)
    prev_rkv, prev_lo = s_shift[:, :, 0:1536], s_shift[:, :, 1536:1664]
    n_wkv = n_gla = n_hgrn = None
    new = []
    for l in range(DEPTH):
        proj2 = _inproj(x2, wts["norm_g3"], mod, wts["w_in_p"], l, None, b)
        o_a, n_wkv = _rwkv_tok(proj2, prev_rkv, prev_lo, s_wkv_t, n_wkv, wts["rwkv"], l, TOK_GROUP_RWKV)
        o_b, n_gla = _chunk_tok("gla", proj2, wts["gla"], wts["gla_gn"], s_gla, n_gla, l, TOK_GROUP)
        o_c, n_hgrn = _chunk_tok("hgrn", proj2, wts["hgrn"], wts["hgrn_gn"], s_hgrn, n_hgrn, l, TOK_GROUP)
        conv = s_conv[l]
        o_d = _lru_tok(proj2, [conv[:, 0], conv[:, 1], conv[:, 2]], s_h[l], *wts["lru"], l)
        x2 = _outproj([o_a, o_b, o_c, o_d], proj2, x2, mod, wts["w_out_b"], wts["final_g2"], l, None, b,
                      final=(l == DEPTH - 1))
        n_conv = jnp.concatenate([conv[:, 1:], proj2[:, None, C_LRU:C_LRU + W_BRANCH]], axis=1)
        new.append((_shift_state(proj2), n_conv, o_d))
    n_shift, n_conv, n_h = (jnp.stack([n[i] for n in new], axis=0) for i in range(3))
    return x2.reshape(b, 1, D_MODEL), (n_shift, _rwkv_state_out(n_wkv), n_gla, n_hgrn, n_conv, n_h)


def kernel(x_prompt, x_sample, c_prompt, c_sample, state_rwkv_shift, state_rwkv_wkv, state_gla, state_hgrn, state_lru_conv, state_lru_h, norm_g, w_ada, b_ada, w_in, w_out, rwkv_mu, rwkv_w0, rwkv_w_up, rwkv_a0, rwkv_a_up, rwkv_k_k, rwkv_k_a, rwkv_r_k, rwkv_gn_g, rwkv_gn_b, gla_gk_up, gla_gk_b, gla_gn_g, hgrn_lb_logits, hgrn_gn_g, lru_conv_w, lru_conv_b, lru_w_a, lru_b_a, lru_w_x, lru_b_x, lru_lambda, final_g):
    p = dict(norm_g=norm_g, w_in=w_in, w_out=w_out, rwkv_mu=rwkv_mu, rwkv_w0=rwkv_w0, rwkv_w_up=rwkv_w_up,
             rwkv_a0=rwkv_a0, rwkv_a_up=rwkv_a_up, rwkv_k_k=rwkv_k_k, rwkv_k_a=rwkv_k_a, rwkv_r_k=rwkv_r_k,
             rwkv_gn_g=rwkv_gn_g, rwkv_gn_b=rwkv_gn_b, gla_gk_up=gla_gk_up, gla_gk_b=gla_gk_b,
             gla_gn_g=gla_gn_g, hgrn_lb_logits=hgrn_lb_logits, hgrn_gn_g=hgrn_gn_g, lru_conv_w=lru_conv_w,
             lru_conv_b=lru_conv_b, lru_w_a=lru_w_a, lru_b_a=lru_b_a, lru_w_x=lru_w_x, lru_b_x=lru_b_x,
             lru_lambda=lru_lambda, final_g=final_g)
    wts = _prep_weights(p)
    bp = x_prompt.shape[0]
    bs = x_sample.shape[0]
    pad_rows = (-bp) % 8
    c_all = jnp.concatenate([c_prompt, jnp.zeros((pad_rows, D_MODEL), f32), c_sample], axis=0)
    mod = _ada(c_all, w_ada, b_ada)
    mod_p = mod[:, 0:bp]
    mod_s = mod[:, bp + pad_rows:bp + pad_rows + bs]
    y_p, st_p = _trunk_seq(x_prompt, mod_p, wts)
    states = (state_rwkv_shift, state_rwkv_wkv, state_gla, state_hgrn, state_lru_conv, state_lru_h)
    y_s, st_s = _trunk_tok(x_sample, mod_s, states, wts)
    return (y_p, y_s) + st_p + st_s
```

```python
import functools

import jax
import jax.numpy as jnp
from jax import lax
from jax.experimental import pallas as pl
from jax.experimental.pallas import tpu as pltpu

f32 = jnp.float32
bf16 = jnp.bfloat16

D_MODEL = 2048
DEPTH = 4
W_BRANCH = 512
EPS = 1e-6

RWKV_H = 8
RWKV_HEAD = 64
RWKV_LORA = 64
RWKV_PROJ = 3 * W_BRANCH + 2 * RWKV_LORA
RWKV_GN_EPS = 64e-5

GLA_H = 4
GLA_DK = 64
GLA_DV = 128
GLA_KW = GLA_H * GLA_DK
GLA_GATE_RANK = 16
GLA_GATE_NORM = 16.0

HGRN_H = 4
HGRN_DK = 128
HGRN_DV = 128
LB_FLOOR = 1e-30

LRU_BLOCKS = 8
LRU_BLOCK = 64
LRU_C = 8.0
CONV_W = 4

CHUNK = 32

C_R, C_K, C_V = 0, 512, 1024
C_GLA_V = 1536
C_HQ, C_HF, C_HI = 2048, 2560, 3072
C_LRU = 3584
C_GATE = 4096
C_GLA_Q, C_GLA_K = 6144, 6400
C_RWKV_LO = 6656
C_GLA_LO = 6784
PROJ_P = 6912

LANE = 128
MXU_TILE = 256
VMEM_LIMIT = 56 * 1024 * 1024

TM_IN = 1024
TN_IN = 768
TM_OUT = 256
LB_CHUNK = 256
LB_RWKV = 128
TOK_GROUP = 16
TOK_GROUP_RWKV = 8

HIGHEST = lax.Precision.HIGHEST


def _cparams(sem):
    return pltpu.CompilerParams(dimension_semantics=sem, vmem_limit_bytes=VMEM_LIMIT)


def _dotf(a, b):
    return jnp.dot(a, b, preferred_element_type=f32, precision=HIGHEST)


def _prec(a):
    return HIGHEST if a.dtype == f32 else None


def _dot_nt(a, b):
    return lax.dot_general(a, b, (((1,), (1,)), ((), ())), preferred_element_type=f32, precision=_prec(a))


def _dot_tn(a, b):
    return lax.dot_general(a, b, (((0,), (0,)), ((), ())), preferred_element_type=f32, precision=_prec(a))


def _split(x, pieces):
    out = []
    for i in range(pieces):
        part = x.astype(bf16)
        out.append(part)
        if i + 1 < pieces:
            x = x - part.astype(f32)
    return out


def _half_dot(x, ones):
    d = functools.partial(jnp.dot, preferred_element_type=f32)
    kh = ones.shape[0]
    if x.shape[1] == kh:
        return d(x, ones)
    return jnp.concatenate([d(x[:, 0:kh], ones), d(x[:, kh:2 * kh], ones)], axis=1)


def _seg_sum(x, ones, pieces=2):
    return sum(_half_dot(part, ones) for part in _split(x, pieces))


def _ones_seg(ones, x, pieces=2):
    d = functools.partial(jnp.dot, preferred_element_type=f32)
    return sum(d(ones, part) for part in _split(x, pieces))


def _sigmoid(x):
    return jax.nn.sigmoid(x)


def _softplus(x):
    return jnp.maximum(x, 0.0) + jnp.log1p(jnp.exp(-jnp.abs(x)))


def _log_sigmoid(x):
    return -_softplus(-x)


def _ada_kernel(c_ref, w_ref, b_ref, o_ref):
    c = c_ref[...]
    s = (c * _sigmoid(c)).astype(bf16)
    o_ref[...] = jnp.dot(s, w_ref[...].astype(bf16), preferred_element_type=f32) + b_ref[...]


def _ada(c_all, w_ada, b_ada):
    rows = c_all.shape[0]
    tn = 512
    n = w_ada.shape[2]
    return pl.pallas_call(
        _ada_kernel,
        grid=(DEPTH, n // tn),
        in_specs=[
            pl.BlockSpec((rows, D_MODEL), lambda l, j: (0, 0)),
            pl.BlockSpec((None, D_MODEL, tn), lambda l, j: (l, 0, j)),
            pl.BlockSpec((None, 1, tn), lambda l, j: (l, 0, j)),
        ],
        out_specs=pl.BlockSpec((None, rows, tn), lambda l, j: (l, 0, j)),
        out_shape=jax.ShapeDtypeStruct((DEPTH, rows, n), f32),
        compiler_params=_cparams(("parallel", "parallel")),
        name="ada_mod",
    )(c_all, w_ada, b_ada.reshape(DEPTH, 1, n))


def _inproj_kernel(x_ref, g_ref, sc_ref, sh_ref, w_ref, o_ref, h_ref):
    @pl.when(pl.program_id(1) == 0)
    def _():
        x = x_ref[...]
        ms = jnp.mean(x * x, axis=-1, keepdims=True)
        h = x * lax.rsqrt(ms + EPS) * g_ref[...]
        h = h * (1.0 + sc_ref[...]) + sh_ref[...]
        h_ref[...] = h.astype(bf16)

    o_ref[...] = _dot_nt(h_ref[...], w_ref[...])


def _mod_spec(l, which, per_seq_tiles, tm):
    if per_seq_tiles is None:
        return pl.BlockSpec((None, tm, D_MODEL), lambda i, *_: (l, 0, which))
    return pl.BlockSpec((None, 1, D_MODEL), lambda i, *_: (i // per_seq_tiles * 3 + which, 0, 0))


def _inproj(x2, norm_g3, mod, w_in_p, l, per_seq_tiles, tm):
    t = x2.shape[0]
    tn = TN_IN
    return pl.pallas_call(
        _inproj_kernel,
        grid=(t // tm, PROJ_P // tn),
        in_specs=[
            pl.BlockSpec((tm, D_MODEL), lambda i, j: (i, 0)),
            pl.BlockSpec((None, 1, D_MODEL), lambda i, j: (l, 0, 0)),
            _mod_spec(l, 1, per_seq_tiles, tm),
            _mod_spec(l, 0, per_seq_tiles, tm),
            pl.BlockSpec((None, tn, D_MODEL), lambda i, j: (l, j, 0)),
        ],
        out_specs=pl.BlockSpec((tm, tn), lambda i, j: (i, j)),
        out_shape=jax.ShapeDtypeStruct((t, PROJ_P), f32),
        scratch_shapes=[pltpu.VMEM((tm, D_MODEL), bf16)],
        compiler_params=_cparams(("parallel", "arbitrary")),
        name="in_proj",
    )(x2, norm_g3, mod, mod, w_in_p)


def _outproj_kernel(oa_ref, ob_ref, oc_ref, od_ref, pz_ref, x_ref, gate_ref, w_ref, fg_ref, o_ref, *, final):
    z = pz_ref[...]
    o = jnp.concatenate([oa_ref[...], ob_ref[...], oc_ref[...], od_ref[...]], axis=1)
    o = o * (z * _sigmoid(z))
    y = jnp.dot(o.astype(bf16), w_ref[...], preferred_element_type=f32)
    xn = x_ref[...] + gate_ref[...] * y
    if final:
        ms = jnp.mean(xn * xn, axis=-1, keepdims=True)
        xn = xn * lax.rsqrt(ms + EPS) * fg_ref[...]
    o_ref[...] = xn


def _outproj(outs, proj2, x2, mod, w_out_b, final_g2, l, per_seq_tiles, tm, final):
    t = x2.shape[0]
    mix_spec = pl.BlockSpec((tm, W_BRANCH), lambda i: (i, 0))
    return pl.pallas_call(
        functools.partial(_outproj_kernel, final=final),
        grid=(t // tm,),
        in_specs=[
            mix_spec, mix_spec, mix_spec, mix_spec,
            pl.BlockSpec((tm, D_MODEL), lambda i: (i, C_GATE // D_MODEL)),
            pl.BlockSpec((tm, D_MODEL), lambda i: (i, 0)),
            _mod_spec(l, 2, per_seq_tiles, tm),
            pl.BlockSpec((None, D_MODEL, D_MODEL), lambda i: (l, 0, 0)),
            pl.BlockSpec((1, D_MODEL), lambda i: (0, 0)),
        ],
        out_specs=pl.BlockSpec((tm, D_MODEL), lambda i: (i, 0)),
        out_shape=jax.ShapeDtypeStruct((t, D_MODEL), f32),
        compiler_params=_cparams(("parallel",)),
        name="out_proj",
    )(*outs, proj2, x2, mod, w_out_b, final_g2)


def _rwkv_prologue(r, k, v, lo, pr, pk, pv, plo, mu_rkv, mu_lo, wup, pvec, bo):
    w0, a0, k_k, k_a, r_k = (pvec[i:i + 1, :] for i in range(5))
    xr = r + (pr - r) * mu_rkv[:, 0:512]
    xk = k + (pk - k) * mu_rkv[:, 512:1024]
    xv = v + (pv - v) * mu_rkv[:, 1024:1536]
    xlo = lo + (plo - lo) * mu_lo
    lane = lax.broadcasted_iota(jnp.int32, xlo.shape, 1)
    act = jnp.where(lane < RWKV_LORA, jnp.tanh(xlo), xlo)
    up = _dotf(act, wup)
    w_raw = -_softplus(-(w0 + up[:, 0:512])) - 0.5
    ew = jnp.exp(-jnp.exp(w_raw))
    a = _sigmoid(a0 + up[:, 512:1024])
    kk = xk * k_k
    kk = kk / jnp.maximum(jnp.sqrt(_seg_sum(kk * kk, bo)), 1e-12)
    kh = xk * (1.0 + (a - 1.0) * k_a)
    alp = kk * a
    ar = _seg_sum(alp * xr, bo)
    return dict(
        kap=kk, ew=ew, alp=alp, kh=kh, vv=xv,
        wr=ew * xr - ar * kk,
        kr=_seg_sum(kh * xr, bo),
        bonus=_seg_sum(xr * kh * r_k, bo) * xv,
    )


_RWKV_STEP_KEYS = ("kap", "ew", "alp", "kh", "vv", "wr", "kr")


def _rwkv_step(s, kap, ew, alp, kh, vv, wr, kr, bo, idt):
    x = jnp.concatenate([(s * kap).astype(bf16), (s * wr).astype(bf16), idt.astype(bf16) * vv.astype(bf16)], axis=0)
    red = _half_dot(x, bo)
    s_new = s * ew - red[0:64] * alp + red[128:192] * kh
    o_row = jnp.sum(red[64:128] * idt, axis=0, keepdims=True) + vv * kr
    return s_new, o_row


def _rwkv_epilogue(o, bonus, pvec, bo):
    gn_g, gn_b = pvec[5:6, :], pvec[6:7, :]
    mu = _seg_sum(o, bo) * (1.0 / RWKV_HEAD)
    d = o - mu
    var = _seg_sum(d * d, bo) * (1.0 / RWKV_HEAD)
    return d * lax.rsqrt(var + RWKV_GN_EPS) * gn_g + gn_b + bonus


def _rwkv_seq_kernel(r_ref, k_ref, v_ref, lo_ref, prkv_ref, plo_ref, sin_ref, mu_rkv_ref, mu_lo_ref, wup_ref,
                     pvec_ref, bo_ref, idt_ref, o_ref, sout_ref,
                     s_sc, crkv_sc, clo_sc, kap_sc, ew_sc, alp_sc, kh_sc, vv_sc, wr_sc, kr_sc, oraw_sc,
                     bonus_sc, *, g_seqs, lb):
    tb = pl.program_id(1)

    @pl.when(tb == 0)
    def _():
        s_sc[...] = sin_ref[...]
        crkv_sc[...] = prkv_ref[...]
        clo_sc[...] = plo_ref[...]

    bo = bo_ref[...]
    idt = idt_ref[...]
    pvec = pvec_ref[...]
    step_sc = dict(kap=kap_sc, ew=ew_sc, alp=alp_sc, kh=kh_sc, vv=vv_sc, wr=wr_sc, kr=kr_sc)

    row0 = lax.broadcasted_iota(jnp.int32, (lb, 1), 0) == 0
    for g in range(g_seqs):
        cur = [r_ref[g], k_ref[g], v_ref[g]]
        lo = lo_ref[g]
        carry = crkv_sc[g]
        prev = [jnp.where(row0, carry[:, i * 512:(i + 1) * 512], pltpu.roll(c, 1, 0)) for i, c in enumerate(cur)]
        plo = jnp.where(row0, clo_sc[g], pltpu.roll(lo, 1, 0))
        res = _rwkv_prologue(cur[0], cur[1], cur[2], lo, prev[0], prev[1], prev[2], plo,
                             mu_rkv_ref[...], mu_lo_ref[...], wup_ref[...], pvec, bo)
        for key in _RWKV_STEP_KEYS:
            step_sc[key][g] = res[key]
        bonus_sc[g] = res["bonus"]
        for i, c in enumerate(cur):
            crkv_sc[g, :, i * 512:(i + 1) * 512] = c[lb - 1:lb, :]
        clo_sc[g] = lo[lb - 1:lb, :]

    def body(t, carry):
        for g in range(g_seqs):
            rows = [step_sc[key][g, pl.ds(t, 1), :] for key in _RWKV_STEP_KEYS]
            s_new, o_row = _rwkv_step(s_sc[g], *rows, bo, idt)
            s_sc[g] = s_new
            oraw_sc[g, pl.ds(t, 1), :] = o_row
        return carry

    lax.fori_loop(0, lb, body, 0, unroll=8)

    for g in range(g_seqs):
        o_ref[g] = _rwkv_epilogue(oraw_sc[g], bonus_sc[g], pvec, bo)

    @pl.when(tb == pl.num_programs(1) - 1)
    def _():
        sout_ref[...] = s_sc[...]


def _rwkv_tok_kernel(r_ref, k_ref, v_ref, lo_ref, prkv_ref, plo_ref, sin_ref, mu_rkv_ref, mu_lo_ref, wup_ref,
                     pvec_ref, bo_ref, idt_ref, *rest, g_rows):
    o_ref, sout_ref = rest[-2:]
    bo = bo_ref[...]
    idt = idt_ref[...]
    pvec = pvec_ref[...]
    prkv = prkv_ref[...]
    res = _rwkv_prologue(r_ref[...], k_ref[...], v_ref[...], lo_ref[...],
                         prkv[:, 0:512], prkv[:, 512:1024], prkv[:, 1024:1536], plo_ref[...],
                         mu_rkv_ref[...], mu_lo_ref[...], wup_ref[...], pvec, bo)
    o_rows = []
    for g in range(g_rows):
        rows = [res[key][g:g + 1, :] for key in _RWKV_STEP_KEYS]
        s_new, o_row = _rwkv_step(sin_ref[g], *rows, bo, idt)
        sout_ref[g] = s_new
        o_rows.append(o_row)
    o_ref[...] = _rwkv_epilogue(jnp.concatenate(o_rows, axis=0), res["bonus"], pvec, bo)


def _rwkv_consts():
    i = jnp.arange(W_BRANCH)
    j = jnp.arange(MXU_TILE)
    bo = (j[:, None] // RWKV_HEAD == j[None, :] // RWKV_HEAD).astype(bf16)
    idt = (jnp.arange(RWKV_HEAD)[:, None] == (i[None, :] % RWKV_HEAD)).astype(f32)
    return bo, idt


def _rwkv_weight_specs(l):
    def cs(shape):
        return pl.BlockSpec((None,) + shape, lambda *_: (l,) + (0,) * len(shape))

    def const(shape):
        return pl.BlockSpec(shape, lambda *_: (0,) * len(shape))

    return [cs((1, 1536)), cs((1, LANE)), cs((LANE, 1024)), cs((8, W_BRANCH)),
            const((MXU_TILE, MXU_TILE)), const((RWKV_HEAD, W_BRANCH))]


def _rwkv_seq(proj3, prev_rkv, prev_lo, s_in, wts, l, lb):
    b, seq_len, _ = proj3.shape
    g = b
    nt = seq_len // lb

    def col(width, off):
        return pl.BlockSpec((g, lb, width), lambda i, t: (i, t, off // width))

    tok_sc = pltpu.VMEM((g, lb, W_BRANCH), f32)
    out = pl.pallas_call(
        functools.partial(_rwkv_seq_kernel, g_seqs=g, lb=lb),
        grid=(b // g, nt),
        in_specs=[
            col(512, C_R), col(512, C_K), col(512, C_V), col(LANE, C_RWKV_LO),
            pl.BlockSpec((g, 1, 1536), lambda i, t: (i, 0, 0)),
            pl.BlockSpec((g, 1, LANE), lambda i, t: (i, 0, 0)),
            pl.BlockSpec((g, RWKV_HEAD, W_BRANCH), lambda i, t: (i, 0, 0)),
        ] + _rwkv_weight_specs(l),
        out_specs=[
            pl.BlockSpec((g, lb, W_BRANCH), lambda i, t: (i, t, 0)),
            pl.BlockSpec((g, RWKV_HEAD, W_BRANCH), lambda i, t: (i, 0, 0)),
        ],
        out_shape=[
            jax.ShapeDtypeStruct((b, seq_len, W_BRANCH), f32),
            jax.ShapeDtypeStruct((b, RWKV_HEAD, W_BRANCH), f32),
        ],
        scratch_shapes=[
            pltpu.VMEM((g, RWKV_HEAD, W_BRANCH), f32),
            pltpu.VMEM((g, 1, 1536), f32),
            pltpu.VMEM((g, 1, LANE), f32),
        ] + [tok_sc] * 9,
        compiler_params=_cparams(("parallel", "arbitrary")),
        name="rwkv_seq",
    )(proj3, proj3, proj3, proj3, prev_rkv, prev_lo, s_in, *wts)
    return out


def _rwkv_tok(proj2, prev_rkv, prev_lo, s_all, acc, wts, l, g):
    b = proj2.shape[0]

    def col(width, off):
        return pl.BlockSpec((g, width), lambda i: (i, off // width))

    state_spec = pl.BlockSpec((None, g, RWKV_HEAD, W_BRANCH), lambda i: (l, i, 0, 0))
    in_specs = [
        col(512, C_R), col(512, C_K), col(512, C_V), col(LANE, C_RWKV_LO),
        pl.BlockSpec((None, g, 1536), lambda i: (l, i, 0)),
        pl.BlockSpec((None, g, LANE), lambda i: (l, i, 0)),
        state_spec,
    ] + _rwkv_weight_specs(l)
    args = [proj2, proj2, proj2, proj2, prev_rkv, prev_lo, s_all, *wts]
    aliases = {}
    if acc is not None:
        in_specs.append(pl.BlockSpec(memory_space=pl.ANY))
        aliases = {len(args): 1}
        args.append(acc)
    return pl.pallas_call(
        functools.partial(_rwkv_tok_kernel, g_rows=g),
        grid=(b // g,),
        in_specs=in_specs,
        out_specs=[pl.BlockSpec((g, W_BRANCH), lambda i: (i, 0)), state_spec],
        out_shape=[
            jax.ShapeDtypeStruct((b, W_BRANCH), f32),
            jax.ShapeDtypeStruct(s_all.shape, f32),
        ],
        input_output_aliases=aliases,
        compiler_params=_cparams(("parallel",)),
        name="rwkv_tok",
    )(*args)


def _gla_inputs(q_ref, k_ref, v_ref, lo_ref, gup_ref, gb_ref):
    q = q_ref[...] * (GLA_DK ** -0.5)
    z = _dotf(lo_ref[...], gup_ref[...]) + gb_ref[...]
    g = _log_sigmoid(z) * (1.0 / GLA_GATE_NORM)
    return q, k_ref[...], v_ref[...], g


def _hgrn_lb(logits, l):
    m = jnp.max(logits, axis=0, keepdims=True)
    e = jnp.exp(logits - m)
    sm = e / jnp.sum(e, axis=0, keepdims=True)
    lb = jnp.zeros_like(sm[0:1, :])
    for i in range(1, l + 1):
        lb = lb + sm[i:i + 1, :]
    return lb


def _hgrn_inputs(q_ref, f_ref, i_ref, logits_ref, l):
    lb = _hgrn_lb(logits_ref[...], l)
    f_lo = f_ref[...]
    a = jnp.log(jnp.maximum(lb, LB_FLOOR))
    b = jnp.log1p(-lb) + _log_sigmoid(f_lo)
    logf = jnp.maximum(a, b) + jnp.log1p(jnp.exp(-jnp.abs(a - b)))
    k = (1.0 - lb) * _sigmoid(-f_lo)
    return q_ref[...], k, i_ref[...], logf


def _chunk_core(q, k, v, g, lt_ref, esum_ref, vmask_ref, gmask_ref, bon_ref, gn_ref, st_sc, kpad, bpad, lb,
                heads_per_group):
    bc = _ones_seg(lt_ref[...], g)
    nc = lb // CHUNK
    f = q.shape[1]
    kpad[...] = k.reshape(nc, CHUNK, f)
    bpad[...] = bc.reshape(nc, CHUNK, f)
    q3 = q.reshape(nc, CHUNK, f)
    bc3 = bc.reshape(nc, CHUNK, f)
    esum = esum_ref[...]

    lane_j = lax.broadcasted_iota(jnp.int32, (1, 1, LANE), 2) & (CHUNK - 1)
    att3 = None
    for r0 in range(0, CHUNK, 8):
        rows = CHUNK - r0
        q_s, bc_s = q3[:, r0:, :], bc3[:, r0:, :]
        rowc = lax.broadcasted_iota(jnp.int32, (1, rows, 1), 1) + r0
        acc = jnp.zeros((nc, rows, LANE), f32)
        for j in range(r0, r0 + 8):
            kj, bj = kpad[:, j:j + 1, :], bpad[:, j:j + 1, :]
            z = jnp.where(rowc >= j, q_s * kj * jnp.exp(bc_s - bj), 0.0)
            a = jnp.dot(z.reshape(nc * rows, f).astype(bf16), esum, preferred_element_type=f32)
            acc = acc + jnp.where(lane_j == j, a.reshape(nc, rows, LANE), 0.0)
        if r0:
            acc = jnp.concatenate([jnp.zeros((nc, r0, LANE), f32), acc], axis=1)
        att3 = acc if att3 is None else att3 + acc
    n_heads = LANE // CHUNK
    v3 = v.reshape(nc, CHUNK, W_BRANCH)
    vbd = jnp.concatenate([v3] * n_heads, axis=1) * vmask_ref[...]
    o = lax.dot_general(att3.astype(bf16), vbd.astype(bf16), (((2,), (1,)), ((0,), (0,))),
                        preferred_element_type=f32).reshape(lb, W_BRANCH)

    ng = f // LANE
    vw = W_BRANCH // ng
    sts = [st_sc[gi] for gi in range(ng)]
    outs = []
    for c in range(nc):
        sl = slice(c * CHUNK, (c + 1) * CHUNK)
        bcc = bc[sl]
        blast = bcc[CHUNK - 1:CHUNK, :]
        dec = jnp.exp(blast)
        qe = (q[sl] * jnp.exp(bcc)).astype(bf16)
        ke = (k[sl] * jnp.exp(blast - bcc)).astype(bf16)
        vb = v[sl].astype(bf16)
        parts = []
        for gi in range(ng):
            kl = slice(gi * LANE, (gi + 1) * LANE)
            vl = slice(gi * vw, (gi + 1) * vw)
            parts.append(_dot_nt(qe[:, kl], sts[gi].astype(bf16)))
            upd = _dot_tn(vb[:, vl], ke[:, kl])
            if heads_per_group > 1:
                upd = upd * gmask_ref[...]
            sts[gi] = sts[gi] * dec[:, kl] + upd
        outs.append(o[sl] + jnp.concatenate(parts, axis=1))
    for gi in range(ng):
        st_sc[gi] = sts[gi]
    o = jnp.concatenate(outs, axis=0)
    ms = _seg_sum(o * o, bon_ref[...]) * (1.0 / LANE)
    return o * lax.rsqrt(ms + EPS) * gn_ref[...]


def _chunk_seq_kernel(*refs, kind, l, lb):
    n_in = 6 if kind == "gla" else 4
    ins = refs[:n_in]
    lt_ref, esum_ref, vmask_ref, gmask_ref, bon_ref, gn_ref, o_ref, stout_ref, st_sc, kpad, bpad = refs[n_in:]
    tb = pl.program_id(1)

    @pl.when(tb == 0)
    def _():
        st_sc[...] = jnp.zeros_like(st_sc)

    if kind == "gla":
        q, k, v, g = _gla_inputs(*ins)
        heads_per_group = LANE // GLA_DK
    else:
        q, k, v, g = _hgrn_inputs(*ins, l)
        heads_per_group = LANE // HGRN_DK
    o_ref[...] = _chunk_core(q, k, v, g, lt_ref, esum_ref, vmask_ref, gmask_ref, bon_ref, gn_ref, st_sc, kpad,
                             bpad, lb, heads_per_group)

    @pl.when(tb == pl.num_programs(1) - 1)
    def _():
        stout_ref[...] = st_sc[...]


def _chunk_consts(h, dk, lb):
    f = h * dk
    i = jnp.arange(lb)
    lt = ((i[:, None] // CHUNK == i[None, :] // CHUNK) & (i[:, None] >= i[None, :])).astype(bf16)
    fi = jnp.arange(f)
    li = jnp.arange(LANE)
    oi = jnp.arange(W_BRANCH)
    vw = W_BRANCH // (f // LANE)
    esum = (fi[:, None] // dk == li[None, :] // CHUNK).astype(bf16)
    vmask = (li[:, None] // CHUNK == oi[None, :] // LANE).astype(f32)
    gmask = (jnp.arange(vw)[:, None] // LANE == li[None, :] // dk).astype(f32)
    ti = jnp.arange(MXU_TILE)
    bon = (ti[:, None] // LANE == ti[None, :] // LANE).astype(bf16)
    return lt, esum, vmask, gmask, bon


def _chunk_seq(kind, proj3, extra, gn3, l, lb):
    b, seq_len, _ = proj3.shape
    h, dk = (GLA_H, GLA_DK) if kind == "gla" else (HGRN_H, HGRN_DK)
    f = h * dk
    nt = seq_len // lb

    def col(width, off):
        return pl.BlockSpec((None, lb, width), lambda i, t: (i, t, off // width))

    def const(shape):
        return pl.BlockSpec(shape, lambda *_: (0,) * len(shape))

    def layer(shape):
        return pl.BlockSpec((None,) + shape, lambda *_: (l,) + (0,) * len(shape))

    if kind == "gla":
        gup, gb = extra
        in_specs = [col(GLA_KW, C_GLA_Q), col(GLA_KW, C_GLA_K), col(W_BRANCH, C_GLA_V), col(LANE, C_GLA_LO),
                    layer((LANE, GLA_KW)), layer((1, GLA_KW))]
        args = [proj3, proj3, proj3, proj3, gup, gb]
    else:
        (logits,) = extra
        in_specs = [col(W_BRANCH, C_HQ), col(W_BRANCH, C_HF), col(W_BRANCH, C_HI), const((DEPTH, W_BRANCH))]
        args = [proj3, proj3, proj3, logits]
    consts = _chunk_consts(h, dk, lb)
    in_specs += [const(c.shape) for c in consts] + [layer((1, W_BRANCH))]
    ng = f // LANE
    st_shape = (ng, W_BRANCH // ng, LANE)
    return pl.pallas_call(
        functools.partial(_chunk_seq_kernel, kind=kind, l=l, lb=lb),
        grid=(b, nt),
        in_specs=in_specs,
        out_specs=[
            pl.BlockSpec((None, lb, W_BRANCH), lambda i, t: (i, t, 0)),
            pl.BlockSpec((None,) + st_shape, lambda i, t: (i, 0, 0, 0)),
        ],
        out_shape=[
            jax.ShapeDtypeStruct((b, seq_len, W_BRANCH), f32),
            jax.ShapeDtypeStruct((b,) + st_shape, f32),
        ],
        scratch_shapes=[
            pltpu.VMEM(st_shape, f32),
            pltpu.VMEM((lb // CHUNK, CHUNK, f), f32),
            pltpu.VMEM((lb // CHUNK, CHUNK, f), f32),
        ],
        compiler_params=_cparams(("parallel", "arbitrary")),
        name=kind + "_seq",
    )(*args, *consts, gn3)


def _state_from_transposed(st, h, dk):
    b, ng = st.shape[0], st.shape[1]
    hpg = h // ng
    st = st.reshape(b, ng, hpg, LANE, hpg, dk)
    diag = jnp.stack([st[:, :, i, :, i, :] for i in range(hpg)], axis=2)
    return jnp.swapaxes(diag.reshape(b, h, LANE, dk), 2, 3)


def _key_columns(x, g, dk, odd):
    xt = jnp.concatenate([x, jnp.zeros((LANE - g, LANE), f32)], axis=0).T
    if dk == LANE:
        return xt
    return jnp.where(odd, xt[dk:2 * dk], xt[0:dk])


def _chunk_tok_kernel(*refs, kind, l, g, has_acc):
    ins, (o_ref, sout_ref) = refs[:-2], refs[-2:]
    if has_acc:
        ins = ins[:-1]
    odd = (pl.program_id(1) % 2) == 1
    if kind == "gla":
        q_ref, k_ref, v_ref, lo_ref, gup_ref, gb_ref, s_ref, gn_ref = ins
        q = q_ref[...] * (GLA_DK ** -0.5)
        k = k_ref[...]
        z = _dotf(lo_ref[...], gup_ref[...]) + gb_ref[...]
        dec = jnp.exp(_log_sigmoid(z) * (1.0 / GLA_GATE_NORM))
        v = v_ref[...]
        dk = GLA_DK
    else:
        q_ref, f_ref, i_ref, logits_ref, s_ref, gn_ref = ins
        q, k, v, logf = _hgrn_inputs(q_ref, f_ref, i_ref, logits_ref, l)
        dec = jnp.exp(logf)
        dk = HGRN_DK
    qt, kt, dt = (_key_columns(x, g, dk, odd) for x in (q, k, dec))
    o_rows = []
    for b in range(g):
        qcol, kcol, dcol = (jnp.broadcast_to(t[:, b:b + 1], (dk, LANE)) for t in (qt, kt, dt))
        s_new = s_ref[b] * dcol + kcol * v[b:b + 1, :]
        sout_ref[b] = s_new
        o_rows.append(jnp.sum(qcol * s_new, axis=0, keepdims=True))
    o = jnp.concatenate(o_rows, axis=0)
    ms = jnp.mean(o * o, axis=-1, keepdims=True)
    o_ref[...] = o * lax.rsqrt(ms + EPS) * gn_ref[...]


def _chunk_tok(kind, proj2, extra, gn3, s_all, acc, l, g):
    b = proj2.shape[0]
    h, dk = (GLA_H, GLA_DK) if kind == "gla" else (HGRN_H, HGRN_DK)

    def head_col(off):
        per = LANE // dk
        return pl.BlockSpec((g, LANE), lambda i, j: (i, off // LANE + j // per))

    if kind == "gla":
        gup, gb = extra
        in_specs = [head_col(C_GLA_Q), head_col(C_GLA_K),
                    pl.BlockSpec((g, LANE), lambda i, j: (i, C_GLA_V // LANE + j)),
                    pl.BlockSpec((g, LANE), lambda i, j: (i, C_GLA_LO // LANE)),
                    pl.BlockSpec((None, LANE, LANE), lambda i, j: (l, 0, j // 2)),
                    pl.BlockSpec((None, 1, LANE), lambda i, j: (l, 0, j // 2))]
        args = [proj2, proj2, proj2, proj2, gup, gb]
    else:
        (logits,) = extra
        in_specs = [head_col(C_HQ), head_col(C_HF), head_col(C_HI),
                    pl.BlockSpec((DEPTH, LANE), lambda i, j: (0, j))]
        args = [proj2, proj2, proj2, logits]
    state_spec = pl.BlockSpec((None, g, None, dk, LANE), lambda i, j: (l, i, j, 0, 0))
    in_specs += [state_spec, pl.BlockSpec((None, 1, LANE), lambda i, j: (l, 0, j))]
    args += [s_all, gn3]
    aliases = {}
    if acc is not None:
        in_specs.append(pl.BlockSpec(memory_space=pl.ANY))
        aliases = {len(args): 1}
        args.append(acc)
    return pl.pallas_call(
        functools.partial(_chunk_tok_kernel, kind=kind, l=l, g=g, has_acc=acc is not None),
        grid=(b // g, h),
        in_specs=in_specs,
        out_specs=[pl.BlockSpec((g, LANE), lambda i, j: (i, j)), state_spec],
        out_shape=[
            jax.ShapeDtypeStruct((b, W_BRANCH), f32),
            jax.ShapeDtypeStruct(s_all.shape, f32),
        ],
        input_output_aliases=aliases,
        compiler_params=_cparams(("parallel", "parallel")),
        name=kind + "_tok",
    )(*args)


def _lru_gate_dot(y_hi, y_lo, w_ref, col0):
    d = functools.partial(jnp.dot, preferred_element_type=f32)
    halves = []
    for h0 in range(0, W_BRANCH, MXU_TILE):
        w_hi = w_ref[0, h0:h0 + MXU_TILE, col0 + h0:col0 + h0 + MXU_TILE]
        w_lo = w_ref[1, h0:h0 + MXU_TILE, col0 + h0:col0 + h0 + MXU_TILE]
        a, b = y_hi[:, h0:h0 + MXU_TILE], y_lo[:, h0:h0 + MXU_TILE]
        halves.append(d(a, w_hi) + d(a, w_lo) + d(b, w_hi))
    return jnp.concatenate(halves, axis=1)


def _lru_gates(y, w_ref, pvec):
    b_a, b_x, lam = pvec[1:2, :], pvec[2:3, :], pvec[3:4, :]
    y_hi, y_lo = _split(y, 2)
    r = _sigmoid(_lru_gate_dot(y_hi, y_lo, w_ref, 0) + b_a)
    ig = _sigmoid(_lru_gate_dot(y_hi, y_lo, w_ref, W_BRANCH) + b_x)
    log_a = -LRU_C * r * _softplus(-lam)
    a = jnp.exp(log_a)
    one_m_a2 = -jnp.tanh(log_a) * (jnp.exp(2.0 * log_a) + 1.0)
    b = jnp.sqrt(one_m_a2) * (ig * y)
    return a, b


def _lru_seq_kernel(x_ref, cw_ref, pvec_ref, wbd_ref, o_ref, xpad, hcar, *, lb):
    tb = pl.program_id(1)

    @pl.when(tb == 0)
    def _():
        xpad[0:8, :] = jnp.zeros((8, W_BRANCH), f32)
        hcar[...] = jnp.zeros_like(hcar)

    x = x_ref[...]
    xpad[8:8 + lb, :] = x
    cw = cw_ref[...]
    pvec = pvec_ref[...]
    y = pvec[0:1, :] + x * cw[3:4, :]
    for j in range(CONV_W - 1):
        y = y + xpad[5 + j:5 + j + lb, :] * cw[j:j + 1, :]
    a, b = _lru_gates(y, wbd_ref, pvec)
    row = lax.broadcasted_iota(jnp.int32, (lb, 1), 0)
    s = 1
    while s < lb:
        m = row >= s
        b = jnp.where(m, a * pltpu.roll(b, s, 0) + b, b)
        a = jnp.where(m, a * pltpu.roll(a, s, 0), a)
        s *= 2
    h = a * hcar[...] + b
    o_ref[...] = h
    hcar[...] = h[lb - 1:lb, :]
    xpad[5:8, :] = x[lb - 3:lb, :]


def _lru_seq(proj3, cw, pvec, wbd, l, lb):
    b, seq_len, _ = proj3.shape

    def layer(shape):
        return pl.BlockSpec((None,) + shape, lambda *_: (l,) + (0,) * len(shape))

    return pl.pallas_call(
        functools.partial(_lru_seq_kernel, lb=lb),
        grid=(b, seq_len // lb),
        in_specs=[
            pl.BlockSpec((None, lb, W_BRANCH), lambda i, t: (i, t, C_LRU // W_BRANCH)),
            layer((CONV_W, W_BRANCH)), layer((8, W_BRANCH)), layer((2, W_BRANCH, 2 * W_BRANCH)),
        ],
        out_specs=pl.BlockSpec((None, lb, W_BRANCH), lambda i, t: (i, t, 0)),
        out_shape=jax.ShapeDtypeStruct((b, seq_len, W_BRANCH), f32),
        scratch_shapes=[
            pltpu.VMEM((8 + lb, W_BRANCH), f32),
            pltpu.VMEM((1, W_BRANCH), f32),
        ],
        compiler_params=_cparams(("parallel", "arbitrary")),
        name="lru_seq",
    )(proj3, cw, pvec, wbd)


def _lru_tok_kernel(x_ref, b0_ref, b1_ref, b2_ref, h0_ref, cw_ref, pvec_ref, wbd_ref, o_ref):
    cw = cw_ref[...]
    pvec = pvec_ref[...]
    y = (pvec[0:1, :] + b0_ref[...] * cw[0:1, :] + b1_ref[...] * cw[1:2, :] + b2_ref[...] * cw[2:3, :]
         + x_ref[...] * cw[3:4, :])
    a, b = _lru_gates(y, wbd_ref, pvec)
    o_ref[...] = a * h0_ref[...] + b


def _lru_tok(proj2, bufs, h0, cw, pvec, wbd, l):
    b = proj2.shape[0]

    def layer(shape):
        return pl.BlockSpec((None,) + shape, lambda *_: (l,) + (0,) * len(shape))

    row = pl.BlockSpec((b, W_BRANCH), lambda i: (0, 0))
    return pl.pallas_call(
        _lru_tok_kernel,
        grid=(1,),
        in_specs=[pl.BlockSpec((b, W_BRANCH), lambda i: (0, C_LRU // W_BRANCH)), row, row, row, row,
                  layer((CONV_W, W_BRANCH)), layer((8, W_BRANCH)), layer((2, W_BRANCH, 2 * W_BRANCH))],
        out_specs=row,
        out_shape=jax.ShapeDtypeStruct((b, W_BRANCH), f32),
        compiler_params=_cparams(("arbitrary",)),
        name="lru_tok",
    )(proj2, *bufs, h0, cw, pvec, wbd)


def _prep_weights(p):
    w = jnp.swapaxes(p["w_in"], 1, 2)
    pad = jnp.zeros((DEPTH, LANE - GLA_GATE_RANK, D_MODEL), w.dtype)
    w_in_p = jnp.concatenate([
        w[:, 0:1536], w[:, 2176:2688], w[:, 2704:4240], w[:, 4240:4752], w[:, 4752:6800],
        w[:, 1664:2176], w[:, 1536:1664], w[:, 2688:2704], pad], axis=1).astype(bf16)
    zl = jnp.zeros((DEPTH, RWKV_LORA, W_BRANCH), f32)
    wup = jnp.concatenate([jnp.concatenate([p["rwkv_w_up"], zl], axis=2),
                           jnp.concatenate([zl, p["rwkv_a_up"]], axis=2)], axis=1)
    zrow = jnp.zeros((DEPTH, W_BRANCH), f32)
    rwkv_vec = jnp.stack([p["rwkv_w0"], p["rwkv_a0"], p["rwkv_k_k"], p["rwkv_k_a"],
                          p["rwkv_r_k"].reshape(DEPTH, W_BRANCH), p["rwkv_gn_g"], p["rwkv_gn_b"], zrow], axis=1)
    mu = p["rwkv_mu"]
    gup = jnp.concatenate([p["gla_gk_up"], jnp.zeros((DEPTH, LANE - GLA_GATE_RANK, GLA_KW), f32)], axis=1)
    eye = jnp.eye(LRU_BLOCKS, dtype=f32)

    def bd(wb):
        return jnp.einsum("lhij,hg->lhigj", wb, eye).reshape(DEPTH, W_BRANCH, W_BRANCH)

    lru_vec = jnp.stack([p["lru_conv_b"], p["lru_b_a"], p["lru_b_x"], p["lru_lambda"],
                         zrow, zrow, zrow, zrow], axis=1)
    lru_w = jnp.concatenate([bd(p["lru_w_a"]), bd(p["lru_w_x"])], axis=2)
    lru_w_hi = lru_w.astype(bf16)
    lru_w_lo = (lru_w - lru_w_hi.astype(f32)).astype(bf16)
    return dict(
        w_in_p=w_in_p,
        w_out_b=p["w_out"].astype(bf16),
        norm_g3=p["norm_g"].reshape(DEPTH, 1, D_MODEL),
        final_g2=p["final_g"].reshape(1, D_MODEL),
        rwkv=(mu[:, None, 0:1536], mu[:, None, 1536:1664], wup, rwkv_vec) + _rwkv_consts(),
        gla=(gup, p["gla_gk_b"].reshape(DEPTH, 1, GLA_KW)),
        gla_gn=p["gla_gn_g"].reshape(DEPTH, 1, W_BRANCH),
        hgrn=(p["hgrn_lb_logits"],),
        hgrn_gn=p["hgrn_gn_g"].reshape(DEPTH, 1, W_BRANCH),
        lru=(p["lru_conv_w"], lru_vec, jnp.stack([lru_w_hi, lru_w_lo], axis=1)),
    )


def _shift_state(proj_last):
    return jnp.concatenate([proj_last[..., 0:1536], proj_last[..., C_RWKV_LO:C_RWKV_LO + LANE]], axis=-1)


def _rwkv_state_in(s):
    lead = s.shape[:-3]
    return jnp.swapaxes(s, -3, -2).reshape(lead + (RWKV_HEAD, W_BRANCH))


def _rwkv_state_out(s):
    lead = s.shape[:-2]
    return jnp.swapaxes(s.reshape(lead + (RWKV_HEAD, RWKV_H, RWKV_HEAD)), -3, -2)


def _trunk_seq(x, mod, wts):
    b, seq_len, _ = x.shape
    t = b * seq_len
    lb = min(LB_CHUNK, seq_len)
    tm = min(TM_IN, seq_len)
    per_seq = seq_len // tm
    mod3 = mod.reshape(DEPTH * b * 3, 1, D_MODEL)
    x2 = x.reshape(t, D_MODEL)
    zeros = functools.partial(jnp.zeros, dtype=f32)
    new = []
    for l in range(DEPTH):
        mod_l = mod3[l * b * 3:(l + 1) * b * 3]
        proj2 = _inproj(x2, wts["norm_g3"], mod_l, wts["w_in_p"], l, per_seq, tm)
        proj3 = proj2.reshape(b, seq_len, PROJ_P)
        o_a, s_wkv = _rwkv_seq(proj3, zeros((b, 1, 1536)), zeros((b, 1, LANE)),
                               zeros((b, RWKV_HEAD, W_BRANCH)), wts["rwkv"], l, min(LB_RWKV, seq_len))
        o_b, st_gla = _chunk_seq("gla", proj3, wts["gla"], wts["gla_gn"], l, lb)
        o_c, st_hgrn = _chunk_seq("hgrn", proj3, wts["hgrn"], wts["hgrn_gn"], l, lb)
        o_d = _lru_seq(proj3, *wts["lru"], l, lb)
        outs = [o.reshape(t, W_BRANCH) for o in (o_a, o_b, o_c, o_d)]
        tmo = min(TM_OUT, seq_len)
        x2 = _outproj(outs, proj2, x2, mod_l, wts["w_out_b"], wts["final_g2"], l, seq_len // tmo, tmo,
                      final=(l == DEPTH - 1))
        last = proj3[:, seq_len - 1]
        new.append((
            _shift_state(last),
            _rwkv_state_out(s_wkv),
            _state_from_transposed(st_gla, GLA_H, GLA_DK),
            _state_from_transposed(st_hgrn, HGRN_H, HGRN_DK),
            proj3[:, seq_len - (CONV_W - 1):, C_LRU:C_LRU + W_BRANCH],
            o_d[:, seq_len - 1],
        ))
    return x2.reshape(b, seq_len, D_MODEL), tuple(jnp.stack([n[i] for n in new], axis=0) for i in range(6))


def _trunk_tok(x, mod, states, wts):
    b = x.shape[0]
    s_shift, s_wkv, s_gla, s_hgrn, s_conv, s_h = states
    x2 = x.reshape(b, D_MODEL)
    s_wkv_t = _rwkv_state_in(s_wkv)
    prev_rkv, prev_lo = s_shift[:, :, 0:1536], s_shift[:, :, 1536:1664]
    n_wkv = n_gla = n_hgrn = None
    new = []
    for l in range(DEPTH):
        proj2 = _inproj(x2, wts["norm_g3"], mod, wts["w_in_p"], l, None, b)
        o_a, n_wkv = _rwkv_tok(proj2, prev_rkv, prev_lo, s_wkv_t, n_wkv, wts["rwkv"], l, TOK_GROUP_RWKV)
        o_b, n_gla = _chunk_tok("gla", proj2, wts["gla"], wts["gla_gn"], s_gla, n_gla, l, TOK_GROUP)
        o_c, n_hgrn = _chunk_tok("hgrn", proj2, wts["hgrn"], wts["hgrn_gn"], s_hgrn, n_hgrn, l, TOK_GROUP)
        conv = s_conv[l]
        o_d = _lru_tok(proj2, [conv[:, 0], conv[:, 1], conv[:, 2]], s_h[l], *wts["lru"], l)
        x2 = _outproj([o_a, o_b, o_c, o_d], proj2, x2, mod, wts["w_out_b"], wts["final_g2"], l, None, b,
                      final=(l == DEPTH - 1))
        n_conv = jnp.concatenate([conv[:, 1:], proj2[:, None, C_LRU:C_LRU + W_BRANCH]], axis=1)
        new.append((_shift_state(proj2), n_conv, o_d))
    n_shift, n_conv, n_h = (jnp.stack([n[i] for n in new], axis=0) for i in range(3))
    return x2.reshape(b, 1, D_MODEL), (n_shift, _rwkv_state_out(n_wkv), n_gla, n_hgrn, n_conv, n_h)


def kernel(x_prompt, x_sample, c_prompt, c_sample, state_rwkv_shift, state_rwkv_wkv, state_gla, state_hgrn, state_lru_conv, state_lru_h, norm_g, w_ada, b_ada, w_in, w_out, rwkv_mu, rwkv_w0, rwkv_w_up, rwkv_a0, rwkv_a_up, rwkv_k_k, rwkv_k_a, rwkv_r_k, rwkv_gn_g, rwkv_gn_b, gla_gk_up, gla_gk_b, gla_gn_g, hgrn_lb_logits, hgrn_gn_g, lru_conv_w, lru_conv_b, lru_w_a, lru_b_a, lru_w_x, lru_b_x, lru_lambda, final_g):
    p = dict(norm_g=norm_g, w_in=w_in, w_out=w_out, rwkv_mu=rwkv_mu, rwkv_w0=rwkv_w0, rwkv_w_up=rwkv_w_up,
             rwkv_a0=rwkv_a0, rwkv_a_up=rwkv_a_up, rwkv_k_k=rwkv_k_k, rwkv_k_a=rwkv_k_a, rwkv_r_k=rwkv_r_k,
             rwkv_gn_g=rwkv_gn_g, rwkv_gn_b=rwkv_gn_b, gla_gk_up=gla_gk_up, gla_gk_b=gla_gk_b,
             gla_gn_g=gla_gn_g, hgrn_lb_logits=hgrn_lb_logits, hgrn_gn_g=hgrn_gn_g, lru_conv_w=lru_conv_w,
             lru_conv_b=lru_conv_b, lru_w_a=lru_w_a, lru_b_a=lru_b_a, lru_w_x=lru_w_x, lru_b_x=lru_b_x,
             lru_lambda=lru_lambda, final_g=final_g)
    wts = _prep_weights(p)
    bp = x_prompt.shape[0]
    bs = x_sample.shape[0]
    pad_rows = (-bp) % 8
    c_all = jnp.concatenate([c_prompt, jnp.zeros((pad_rows, D_MODEL), f32), c_sample], axis=0)
    mod = _ada(c_all, w_ada, b_ada)
    mod_p = mod[:, 0:bp]
    mod_s = mod[:, bp + pad_rows:bp + pad_rows + bs]
    y_p, st_p = _trunk_seq(x_prompt, mod_p, wts)
    states = (state_rwkv_shift, state_rwkv_wkv, state_gla, state_hgrn, state_lru_conv, state_lru_h)
    y_s, st_s = _trunk_tok(x_sample, mod_s, states, wts)
    return (y_p, y_s) + st_p + st_s
```

```python
import functools

import jax
import jax.numpy as jnp
from jax import lax
from jax.experimental import pallas as pl
from jax.experimental.pallas import tpu as pltpu

f32 = jnp.float32
bf16 = jnp.bfloat16

D_MODEL = 2048
DEPTH = 4
W_BRANCH = 512
EPS = 1e-6

RWKV_H = 8
RWKV_HEAD = 64
RWKV_LORA = 64
RWKV_PROJ = 3 * W_BRANCH + 2 * RWKV_LORA
RWKV_GN_EPS = 64e-5

GLA_H = 4
GLA_DK = 64
GLA_DV = 128
GLA_KW = GLA_H * GLA_DK
GLA_GATE_RANK = 16
GLA_GATE_NORM = 16.0

HGRN_H = 4
HGRN_DK = 128
HGRN_DV = 128
LB_FLOOR = 1e-30

LRU_BLOCKS = 8
LRU_BLOCK = 64
LRU_C = 8.0
CONV_W = 4

CHUNK = 32

C_R, C_K, C_V = 0, 512, 1024
C_GLA_V = 1536
C_HQ, C_HF, C_HI = 2048, 2560, 3072
C_LRU = 3584
C_GATE = 4096
C_GLA_Q, C_GLA_K = 6144, 6400
C_RWKV_LO = 6656
C_GLA_LO = 6784
PROJ_P = 6912

LANE = 128
MXU_TILE = 256
VMEM_LIMIT = 56 * 1024 * 1024

TM_IN = 1024
TN_IN = 768
TM_OUT = 256
LB_CHUNK = 256
LB_RWKV = 128
RWKV_GROUP = 8
TOK_GROUP = 16
TOK_GROUP_RWKV = 8

HIGHEST = lax.Precision.HIGHEST


def _cparams(sem):
    return pltpu.CompilerParams(dimension_semantics=sem, vmem_limit_bytes=VMEM_LIMIT)


def _dotf(a, b):
    return jnp.dot(a, b, preferred_element_type=f32, precision=HIGHEST)


def _prec(a):
    return HIGHEST if a.dtype == f32 else None


def _dot_nt(a, b):
    return lax.dot_general(a, b, (((1,), (1,)), ((), ())), preferred_element_type=f32, precision=_prec(a))


def _dot_tn(a, b):
    return lax.dot_general(a, b, (((0,), (0,)), ((), ())), preferred_element_type=f32, precision=_prec(a))


def _split(x, pieces):
    out = []
    for i in range(pieces):
        part = x.astype(bf16)
        out.append(part)
        if i + 1 < pieces:
            x = x - part.astype(f32)
    return out


def _half_dot(x, ones):
    d = functools.partial(jnp.dot, preferred_element_type=f32)
    kh = ones.shape[0]
    if x.shape[1] == kh:
        return d(x, ones)
    return jnp.concatenate([d(x[:, 0:kh], ones), d(x[:, kh:2 * kh], ones)], axis=1)


def _seg_sum(x, ones, pieces=2):
    return sum(_half_dot(part, ones) for part in _split(x, pieces))


def _ones_seg(ones, x, pieces=2):
    d = functools.partial(jnp.dot, preferred_element_type=f32)
    return sum(d(ones, part) for part in _split(x, pieces))


def _sigmoid(x):
    return jax.nn.sigmoid(x)


def _softplus(x):
    return jnp.maximum(x, 0.0) + jnp.log1p(jnp.exp(-jnp.abs(x)))


def _log_sigmoid(x):
    return -_softplus(-x)


def _ada_kernel(c_ref, w_ref, b_ref, o_ref):
    c = c_ref[...]
    s = (c * _sigmoid(c)).astype(bf16)
    o_ref[...] = jnp.dot(s, w_ref[...].astype(bf16), preferred_element_type=f32) + b_ref[...]


def _ada(c_all, w_ada, b_ada):
    rows = c_all.shape[0]
    tn = 512
    n = w_ada.shape[2]
    return pl.pallas_call(
        _ada_kernel,
        grid=(DEPTH, n // tn),
        in_specs=[
            pl.BlockSpec((rows, D_MODEL), lambda l, j: (0, 0)),
            pl.BlockSpec((None, D_MODEL, tn), lambda l, j: (l, 0, j)),
            pl.BlockSpec((None, 1, tn), lambda l, j: (l, 0, j)),
        ],
        out_specs=pl.BlockSpec((None, rows, tn), lambda l, j: (l, 0, j)),
        out_shape=jax.ShapeDtypeStruct((DEPTH, rows, n), f32),
        compiler_params=_cparams(("parallel", "parallel")),
        name="ada_mod",
    )(c_all, w_ada, b_ada.reshape(DEPTH, 1, n))


def _inproj_kernel(x_ref, g_ref, sc_ref, sh_ref, w_ref, o_ref, h_ref):
    @pl.when(pl.program_id(1) == 0)
    def _():
        x = x_ref[...]
        ms = jnp.mean(x * x, axis=-1, keepdims=True)
        h = x * lax.rsqrt(ms + EPS) * g_ref[...]
        h = h * (1.0 + sc_ref[...]) + sh_ref[...]
        h_ref[...] = h.astype(bf16)

    o_ref[...] = _dot_nt(h_ref[...], w_ref[...])


def _mod_spec(l, which, per_seq_tiles, tm):
    if per_seq_tiles is None:
        return pl.BlockSpec((None, tm, D_MODEL), lambda i, *_: (l, 0, which))
    return pl.BlockSpec((None, 1, D_MODEL), lambda i, *_: (i // per_seq_tiles * 3 + which, 0, 0))


def _inproj(x2, norm_g3, mod, w_in_p, l, per_seq_tiles, tm):
    t = x2.shape[0]
    tn = TN_IN
    return pl.pallas_call(
        _inproj_kernel,
        grid=(t // tm, PROJ_P // tn),
        in_specs=[
            pl.BlockSpec((tm, D_MODEL), lambda i, j: (i, 0)),
            pl.BlockSpec((None, 1, D_MODEL), lambda i, j: (l, 0, 0)),
            _mod_spec(l, 1, per_seq_tiles, tm),
            _mod_spec(l, 0, per_seq_tiles, tm),
            pl.BlockSpec((None, tn, D_MODEL), lambda i, j: (l, j, 0)),
        ],
        out_specs=pl.BlockSpec((tm, tn), lambda i, j: (i, j)),
        out_shape=jax.ShapeDtypeStruct((t, PROJ_P), f32),
        scratch_shapes=[pltpu.VMEM((tm, D_MODEL), bf16)],
        compiler_params=_cparams(("parallel", "arbitrary")),
        name="in_proj",
    )(x2, norm_g3, mod, mod, w_in_p)


def _outproj_kernel(oa_ref, ob_ref, oc_ref, od_ref, pz_ref, x_ref, gate_ref, w_ref, fg_ref, o_ref, *, final):
    z = pz_ref[...]
    o = jnp.concatenate([oa_ref[...], ob_ref[...], oc_ref[...], od_ref[...]], axis=1)
    o = o * (z * _sigmoid(z))
    y = jnp.dot(o.astype(bf16), w_ref[...], preferred_element_type=f32)
    xn = x_ref[...] + gate_ref[...] * y
    if final:
        ms = jnp.mean(xn * xn, axis=-1, keepdims=True)
        xn = xn * lax.rsqrt(ms + EPS) * fg_ref[...]
    o_ref[...] = xn


def _outproj(outs, proj2, x2, mod, w_out_b, final_g2, l, per_seq_tiles, tm, final):
    t = x2.shape[0]
    mix_spec = pl.BlockSpec((tm, W_BRANCH), lambda i: (i, 0))
    return pl.pallas_call(
        functools.partial(_outproj_kernel, final=final),
        grid=(t // tm,),
        in_specs=[
            mix_spec, mix_spec, mix_spec, mix_spec,
            pl.BlockSpec((tm, D_MODEL), lambda i: (i, C_GATE // D_MODEL)),
            pl.BlockSpec((tm, D_MODEL), lambda i: (i, 0)),
            _mod_spec(l, 2, per_seq_tiles, tm),
            pl.BlockSpec((None, D_MODEL, D_MODEL), lambda i: (l, 0, 0)),
            pl.BlockSpec((1, D_MODEL), lambda i: (0, 0)),
        ],
        out_specs=pl.BlockSpec((tm, D_MODEL), lambda i: (i, 0)),
        out_shape=jax.ShapeDtypeStruct((t, D_MODEL), f32),
        compiler_params=_cparams(("parallel",)),
        name="out_proj",
    )(*outs, proj2, x2, mod, w_out_b, final_g2)


def _rwkv_prologue(r, k, v, lo, pr, pk, pv, plo, mu_rkv, mu_lo, wup, pvec, bo, sub_block=0):
    w0, a0, k_k, k_a, r_k = (pvec[i:i + 1, :] for i in range(5))
    xr = r + (pr - r) * mu_rkv[:, 0:512]
    xk = k + (pk - k) * mu_rkv[:, 512:1024]
    xv = v + (pv - v) * mu_rkv[:, 1024:1536]
    xlo = lo + (plo - lo) * mu_lo
    lane = lax.broadcasted_iota(jnp.int32, xlo.shape, 1)
    act = jnp.where(lane < RWKV_LORA, jnp.tanh(xlo), xlo)
    act_hi, act_lo = _split(act, 2)
    d = functools.partial(jnp.dot, preferred_element_type=f32)
    up = d(act_hi, wup[0]) + d(act_hi, wup[1]) + d(act_lo, wup[0])
    w_raw = -_softplus(-(w0 + up[:, 0:512])) - 0.5
    logw = -jnp.exp(w_raw)
    a = _sigmoid(a0 + up[:, 512:1024])
    kk = xk * k_k
    kk = kk / jnp.maximum(jnp.sqrt(_seg_sum(kk * kk, bo)), 1e-12)
    kh = xk * (1.0 + (a - 1.0) * k_a)
    alp = kk * a
    ar = _seg_sum(alp * xr, bo, pieces=1)
    out = dict(vv=xv, kr=_seg_sum(kh * xr, bo, pieces=1), bonus=_seg_sum(xr * kh * r_k, bo, pieces=1) * xv)
    if not sub_block:
        ew = jnp.exp(logw)
        out.update(kap=kk, alp=alp, kh=kh, ew=ew, wr=ew * xr - ar * kk)
        return out
    rowi = lax.broadcasted_iota(jnp.int32, (logw.shape[0], 1), 0) & (sub_block - 1)
    lc = logw
    step = 1
    while step < sub_block:
        lc = lc + jnp.where(rowi >= step, pltpu.roll(lc, step, 0), 0.0)
        step *= 2
    gam, ginv = jnp.exp(lc), jnp.exp(-lc)
    kap = kk * jnp.exp(lc - logw)
    out.update(kap=kap, alp=alp * ginv, kh=kh * ginv, gam=gam, wr=gam * xr - ar * kap)
    return out


_RWKV_STEP_KEYS = ("kap", "alp", "kh", "vv", "wr", "kr")


def _rwkv_step(s, kap, alp, kh, vv, wr, kr, bo, idt, ew=None):
    s_new, o_row = [], []
    for c0 in range(0, W_BRANCH, MXU_TILE):
        sl = slice(c0, c0 + MXU_TILE)
        s_h, idt_h, v_h = s[:, sl], idt[:, sl], vv[:, sl]
        s_b = s_h.astype(bf16)
        x = jnp.concatenate([s_b * kap[:, sl].astype(bf16), s_b * wr[:, sl].astype(bf16),
                             idt_h.astype(bf16) * v_h.astype(bf16)], axis=0)
        red = jnp.dot(x, bo, preferred_element_type=f32)
        kept = s_h if ew is None else s_h * ew[:, sl]
        s_new.append(kept - red[0:64] * alp[:, sl] + red[128:192] * kh[:, sl])
        o_row.append(jnp.sum(red[64:128] * idt_h, axis=0, keepdims=True) + v_h * kr[:, sl])
    return jnp.concatenate(s_new, axis=1), jnp.concatenate(o_row, axis=1)


def _rwkv_epilogue(o, bonus, pvec, bo):
    gn_g, gn_b = pvec[5:6, :], pvec[6:7, :]
    mu = _seg_sum(o, bo, pieces=1) * (1.0 / RWKV_HEAD)
    d = o - mu
    var = _seg_sum(d * d, bo, pieces=1) * (1.0 / RWKV_HEAD)
    return d * lax.rsqrt(var + RWKV_GN_EPS) * gn_g + gn_b + bonus


def _rwkv_seq_kernel(r_ref, k_ref, v_ref, lo_ref, prkv_ref, plo_ref, sin_ref, mu_rkv_ref, mu_lo_ref, wup_ref,
                     pvec_ref, bo_ref, idt_ref, o_ref, sout_ref,
                     s_sc, crkv_sc, clo_sc, kap_sc, gam_sc, alp_sc, kh_sc, vv_sc, wr_sc, kr_sc, oraw_sc,
                     bonus_sc, *, g_seqs, lb):
    tb = pl.program_id(1)

    @pl.when(tb == 0)
    def _():
        s_sc[...] = sin_ref[...]
        crkv_sc[...] = prkv_ref[...]
        clo_sc[...] = plo_ref[...]

    bo = bo_ref[...]
    idt = idt_ref[...]
    pvec = pvec_ref[...]
    step_sc = dict(kap=kap_sc, alp=alp_sc, kh=kh_sc, vv=vv_sc, wr=wr_sc, kr=kr_sc)

    row0 = lax.broadcasted_iota(jnp.int32, (lb, 1), 0) == 0
    for g in range(g_seqs):
        cur = [r_ref[g], k_ref[g], v_ref[g]]
        lo = lo_ref[g]
        carry = crkv_sc[g]
        prev = [jnp.where(row0, carry[:, i * 512:(i + 1) * 512], pltpu.roll(c, 1, 0)) for i, c in enumerate(cur)]
        plo = jnp.where(row0, clo_sc[g], pltpu.roll(lo, 1, 0))
        res = _rwkv_prologue(cur[0], cur[1], cur[2], lo, prev[0], prev[1], prev[2], plo,
                             mu_rkv_ref[...], mu_lo_ref[...], wup_ref[...], pvec, bo, sub_block=RWKV_GROUP)
        for key in _RWKV_STEP_KEYS:
            step_sc[key][g] = res[key]
        gam_sc[g] = res["gam"]
        bonus_sc[g] = res["bonus"]
        for i, c in enumerate(cur):
            crkv_sc[g, :, i * 512:(i + 1) * 512] = c[lb - 1:lb, :]
        clo_sc[g] = lo[lb - 1:lb, :]

    def group(i, carry):
        t0 = pl.multiple_of(i * RWKV_GROUP, RWKV_GROUP)
        states = [s_sc[g] for g in range(g_seqs)]
        for j in range(RWKV_GROUP):
            for g in range(g_seqs):
                rows = [step_sc[key][g, pl.ds(t0 + j, 1), :] for key in _RWKV_STEP_KEYS]
                states[g], o_row = _rwkv_step(states[g], *rows, bo, idt)
                oraw_sc[g, pl.ds(t0 + j, 1), :] = o_row
        for g in range(g_seqs):
            s_sc[g] = states[g] * gam_sc[g, pl.ds(t0 + RWKV_GROUP - 1, 1), :]
        return carry

    lax.fori_loop(0, lb // RWKV_GROUP, group, 0)

    for g in range(g_seqs):
        o_ref[g] = _rwkv_epilogue(oraw_sc[g], bonus_sc[g], pvec, bo)

    @pl.when(tb == pl.num_programs(1) - 1)
    def _():
        sout_ref[...] = s_sc[...]


def _rwkv_tok_kernel(r_ref, k_ref, v_ref, lo_ref, prkv_ref, plo_ref, sin_ref, mu_rkv_ref, mu_lo_ref, wup_ref,
                     pvec_ref, bo_ref, idt_ref, *rest, g_rows):
    o_ref, sout_ref = rest[-2:]
    bo = bo_ref[...]
    idt = idt_ref[...]
    pvec = pvec_ref[...]
    prkv = prkv_ref[...]
    res = _rwkv_prologue(r_ref[...], k_ref[...], v_ref[...], lo_ref[...],
                         prkv[:, 0:512], prkv[:, 512:1024], prkv[:, 1024:1536], plo_ref[...],
                         mu_rkv_ref[...], mu_lo_ref[...], wup_ref[...], pvec, bo)
    o_rows = []
    for g in range(g_rows):
        rows = [res[key][g:g + 1, :] for key in _RWKV_STEP_KEYS]
        s_new, o_row = _rwkv_step(sin_ref[g], *rows, bo, idt, ew=res["ew"][g:g + 1, :])
        sout_ref[g] = s_new
        o_rows.append(o_row)
    o_ref[...] = _rwkv_epilogue(jnp.concatenate(o_rows, axis=0), res["bonus"], pvec, bo)


def _rwkv_consts():
    i = jnp.arange(W_BRANCH)
    j = jnp.arange(MXU_TILE)
    bo = (j[:, None] // RWKV_HEAD == j[None, :] // RWKV_HEAD).astype(bf16)
    idt = (jnp.arange(RWKV_HEAD)[:, None] == (i[None, :] % RWKV_HEAD)).astype(f32)
    return bo, idt


def _rwkv_weight_specs(l):
    def cs(shape):
        return pl.BlockSpec((None,) + shape, lambda *_: (l,) + (0,) * len(shape))

    def const(shape):
        return pl.BlockSpec(shape, lambda *_: (0,) * len(shape))

    return [cs((1, 1536)), cs((1, LANE)), cs((2, LANE, 1024)), cs((8, W_BRANCH)),
            const((MXU_TILE, MXU_TILE)), const((RWKV_HEAD, W_BRANCH))]


def _rwkv_seq(proj3, prev_rkv, prev_lo, s_in, wts, l, lb):
    b, seq_len, _ = proj3.shape
    g = b
    nt = seq_len // lb

    def col(width, off):
        return pl.BlockSpec((g, lb, width), lambda i, t: (i, t, off // width))

    tok_sc = pltpu.VMEM((g, lb, W_BRANCH), f32)
    out = pl.pallas_call(
        functools.partial(_rwkv_seq_kernel, g_seqs=g, lb=lb),
        grid=(b // g, nt),
        in_specs=[
            col(512, C_R), col(512, C_K), col(512, C_V), col(LANE, C_RWKV_LO),
            pl.BlockSpec((g, 1, 1536), lambda i, t: (i, 0, 0)),
            pl.BlockSpec((g, 1, LANE), lambda i, t: (i, 0, 0)),
            pl.BlockSpec((g, RWKV_HEAD, W_BRANCH), lambda i, t: (i, 0, 0)),
        ] + _rwkv_weight_specs(l),
        out_specs=[
            pl.BlockSpec((g, lb, W_BRANCH), lambda i, t: (i, t, 0)),
            pl.BlockSpec((g, RWKV_HEAD, W_BRANCH), lambda i, t: (i, 0, 0)),
        ],
        out_shape=[
            jax.ShapeDtypeStruct((b, seq_len, W_BRANCH), f32),
            jax.ShapeDtypeStruct((b, RWKV_HEAD, W_BRANCH), f32),
        ],
        scratch_shapes=[
            pltpu.VMEM((g, RWKV_HEAD, W_BRANCH), f32),
            pltpu.VMEM((g, 1, 1536), f32),
            pltpu.VMEM((g, 1, LANE), f32),
        ] + [tok_sc] * 9,
        compiler_params=_cparams(("parallel", "arbitrary")),
        name="rwkv_seq",
    )(proj3, proj3, proj3, proj3, prev_rkv, prev_lo, s_in, *wts)
    return out


def _rwkv_tok(proj2, prev_rkv, prev_lo, s_all, acc, wts, l, g):
    b = proj2.shape[0]

    def col(width, off):
        return pl.BlockSpec((g, width), lambda i: (i, off // width))

    state_spec = pl.BlockSpec((None, g, RWKV_HEAD, W_BRANCH), lambda i: (l, i, 0, 0))
    in_specs = [
        col(512, C_R), col(512, C_K), col(512, C_V), col(LANE, C_RWKV_LO),
        pl.BlockSpec((None, g, 1536), lambda i: (l, i, 0)),
        pl.BlockSpec((None, g, LANE), lambda i: (l, i, 0)),
        state_spec,
    ] + _rwkv_weight_specs(l)
    args = [proj2, proj2, proj2, proj2, prev_rkv, prev_lo, s_all, *wts]
    aliases = {}
    if acc is not None:
        in_specs.append(pl.BlockSpec(memory_space=pl.ANY))
        aliases = {len(args): 1}
        args.append(acc)
    return pl.pallas_call(
        functools.partial(_rwkv_tok_kernel, g_rows=g),
        grid=(b // g,),
        in_specs=in_specs,
        out_specs=[pl.BlockSpec((g, W_BRANCH), lambda i: (i, 0)), state_spec],
        out_shape=[
            jax.ShapeDtypeStruct((b, W_BRANCH), f32),
            jax.ShapeDtypeStruct(s_all.shape, f32),
        ],
        input_output_aliases=aliases,
        compiler_params=_cparams(("parallel",)),
        name="rwkv_tok",
    )(*args)


def _gla_inputs(q_ref, k_ref, v_ref, lo_ref, gup_ref, gb_ref):
    q = q_ref[...] * (GLA_DK ** -0.5)
    z = _dotf(lo_ref[...], gup_ref[...]) + gb_ref[...]
    g = _log_sigmoid(z) * (1.0 / GLA_GATE_NORM)
    return q, k_ref[...], v_ref[...], g


def _hgrn_lb(logits, l):
    m = jnp.max(logits, axis=0, keepdims=True)
    e = jnp.exp(logits - m)
    sm = e / jnp.sum(e, axis=0, keepdims=True)
    lb = jnp.zeros_like(sm[0:1, :])
    for i in range(1, l + 1):
        lb = lb + sm[i:i + 1, :]
    return lb


def _hgrn_inputs(q_ref, f_ref, i_ref, logits_ref, l):
    lb = _hgrn_lb(logits_ref[...], l)
    f_lo = f_ref[...]
    a = jnp.log(jnp.maximum(lb, LB_FLOOR))
    b = jnp.log1p(-lb) + _log_sigmoid(f_lo)
    logf = jnp.maximum(a, b) + jnp.log1p(jnp.exp(-jnp.abs(a - b)))
    k = (1.0 - lb) * _sigmoid(-f_lo)
    return q_ref[...], k, i_ref[...], logf


def _chunk_core(q, k, v, g, lt_ref, esum_ref, vmask_ref, gmask_ref, bon_ref, gn_ref, st_sc, kpad, bpad, lb,
                heads_per_group):
    bc = _ones_seg(lt_ref[...], g)
    nc = lb // CHUNK
    f = q.shape[1]
    kpad[...] = k.reshape(nc, CHUNK, f)
    bpad[...] = bc.reshape(nc, CHUNK, f)
    q3 = q.reshape(nc, CHUNK, f)
    bc3 = bc.reshape(nc, CHUNK, f)
    esum = esum_ref[...]

    lane_j = lax.broadcasted_iota(jnp.int32, (1, 1, LANE), 2) & (CHUNK - 1)
    att3 = None
    for r0 in range(0, CHUNK, 8):
        rows = CHUNK - r0
        q_s, bc_s = q3[:, r0:, :], bc3[:, r0:, :]
        rowc = lax.broadcasted_iota(jnp.int32, (1, rows, 1), 1) + r0
        acc = jnp.zeros((nc, rows, LANE), f32)
        for j in range(r0, r0 + 8):
            kj, bj = kpad[:, j:j + 1, :], bpad[:, j:j + 1, :]
            z = jnp.where(rowc >= j, q_s * kj * jnp.exp(bc_s - bj), 0.0)
            a = jnp.dot(z.reshape(nc * rows, f).astype(bf16), esum, preferred_element_type=f32)
            acc = acc + jnp.where(lane_j == j, a.reshape(nc, rows, LANE), 0.0)
        if r0:
            acc = jnp.concatenate([jnp.zeros((nc, r0, LANE), f32), acc], axis=1)
        att3 = acc if att3 is None else att3 + acc
    n_heads = LANE // CHUNK
    v3 = v.reshape(nc, CHUNK, W_BRANCH)
    vbd = jnp.concatenate([v3] * n_heads, axis=1) * vmask_ref[...]
    o = lax.dot_general(att3.astype(bf16), vbd.astype(bf16), (((2,), (1,)), ((0,), (0,))),
                        preferred_element_type=f32).reshape(lb, W_BRANCH)

    ng = f // LANE
    vw = W_BRANCH // ng
    sts = [st_sc[gi] for gi in range(ng)]
    outs = []
    for c in range(nc):
        sl = slice(c * CHUNK, (c + 1) * CHUNK)
        bcc = bc[sl]
        blast = bcc[CHUNK - 1:CHUNK, :]
        dec = jnp.exp(blast)
        qe = (q[sl] * jnp.exp(bcc)).astype(bf16)
        ke = (k[sl] * jnp.exp(blast - bcc)).astype(bf16)
        vb = v[sl].astype(bf16)
        parts = []
        for gi in range(ng):
            kl = slice(gi * LANE, (gi + 1) * LANE)
            vl = slice(gi * vw, (gi + 1) * vw)
            parts.append(_dot_nt(qe[:, kl], sts[gi].astype(bf16)))
            upd = _dot_tn(vb[:, vl], ke[:, kl])
            if heads_per_group > 1:
                upd = upd * gmask_ref[...]
            sts[gi] = sts[gi] * dec[:, kl] + upd
        outs.append(o[sl] + jnp.concatenate(parts, axis=1))
    for gi in range(ng):
        st_sc[gi] = sts[gi]
    o = jnp.concatenate(outs, axis=0)
    ms = _seg_sum(o * o, bon_ref[...], pieces=1) * (1.0 / LANE)
    return o * lax.rsqrt(ms + EPS) * gn_ref[...]


def _chunk_seq_kernel(*refs, kind, l, lb):
    n_in = 6 if kind == "gla" else 4
    ins = refs[:n_in]
    lt_ref, esum_ref, vmask_ref, gmask_ref, bon_ref, gn_ref, o_ref, stout_ref, st_sc, kpad, bpad = refs[n_in:]
    tb = pl.program_id(1)

    @pl.when(tb == 0)
    def _():
        st_sc[...] = jnp.zeros_like(st_sc)

    if kind == "gla":
        q, k, v, g = _gla_inputs(*ins)
        heads_per_group = LANE // GLA_DK
    else:
        q, k, v, g = _hgrn_inputs(*ins, l)
        heads_per_group = LANE // HGRN_DK
    o_ref[...] = _chunk_core(q, k, v, g, lt_ref, esum_ref, vmask_ref, gmask_ref, bon_ref, gn_ref, st_sc, kpad,
                             bpad, lb, heads_per_group)

    @pl.when(tb == pl.num_programs(1) - 1)
    def _():
        stout_ref[...] = st_sc[...]


def _chunk_consts(h, dk, lb):
    f = h * dk
    i = jnp.arange(lb)
    lt = ((i[:, None] // CHUNK == i[None, :] // CHUNK) & (i[:, None] >= i[None, :])).astype(bf16)
    fi = jnp.arange(f)
    li = jnp.arange(LANE)
    oi = jnp.arange(W_BRANCH)
    vw = W_BRANCH // (f // LANE)
    esum = (fi[:, None] // dk == li[None, :] // CHUNK).astype(bf16)
    vmask = (li[:, None] // CHUNK == oi[None, :] // LANE).astype(f32)
    gmask = (jnp.arange(vw)[:, None] // LANE == li[None, :] // dk).astype(f32)
    ti = jnp.arange(MXU_TILE)
    bon = (ti[:, None] // LANE == ti[None, :] // LANE).astype(bf16)
    return lt, esum, vmask, gmask, bon


def _chunk_seq(kind, proj3, extra, gn3, l, lb):
    b, seq_len, _ = proj3.shape
    h, dk = (GLA_H, GLA_DK) if kind == "gla" else (HGRN_H, HGRN_DK)
    f = h * dk
    nt = seq_len // lb

    def col(width, off):
        return pl.BlockSpec((None, lb, width), lambda i, t: (i, t, off // width))

    def const(shape):
        return pl.BlockSpec(shape, lambda *_: (0,) * len(shape))

    def layer(shape):
        return pl.BlockSpec((None,) + shape, lambda *_: (l,) + (0,) * len(shape))

    if kind == "gla":
        gup, gb = extra
        in_specs = [col(GLA_KW, C_GLA_Q), col(GLA_KW, C_GLA_K), col(W_BRANCH, C_GLA_V), col(LANE, C_GLA_LO),
                    layer((LANE, GLA_KW)), layer((1, GLA_KW))]
        args = [proj3, proj3, proj3, proj3, gup, gb]
    else:
        (logits,) = extra
        in_specs = [col(W_BRANCH, C_HQ), col(W_BRANCH, C_HF), col(W_BRANCH, C_HI), const((DEPTH, W_BRANCH))]
        args = [proj3, proj3, proj3, logits]
    consts = _chunk_consts(h, dk, lb)
    in_specs += [const(c.shape) for c in consts] + [layer((1, W_BRANCH))]
    ng = f // LANE
    st_shape = (ng, W_BRANCH // ng, LANE)
    return pl.pallas_call(
        functools.partial(_chunk_seq_kernel, kind=kind, l=l, lb=lb),
        grid=(b, nt),
        in_specs=in_specs,
        out_specs=[
            pl.BlockSpec((None, lb, W_BRANCH), lambda i, t: (i, t, 0)),
            pl.BlockSpec((None,) + st_shape, lambda i, t: (i, 0, 0, 0)),
        ],
        out_shape=[
            jax.ShapeDtypeStruct((b, seq_len, W_BRANCH), f32),
            jax.ShapeDtypeStruct((b,) + st_shape, f32),
        ],
        scratch_shapes=[
            pltpu.VMEM(st_shape, f32),
            pltpu.VMEM((lb // CHUNK, CHUNK, f), f32),
            pltpu.VMEM((lb // CHUNK, CHUNK, f), f32),
        ],
        compiler_params=_cparams(("parallel", "arbitrary")),
        name=kind + "_seq",
    )(*args, *consts, gn3)


def _state_from_transposed(st, h, dk):
    b, ng = st.shape[0], st.shape[1]
    hpg = h // ng
    st = st.reshape(b, ng, hpg, LANE, hpg, dk)
    diag = jnp.stack([st[:, :, i, :, i, :] for i in range(hpg)], axis=2)
    return jnp.swapaxes(diag.reshape(b, h, LANE, dk), 2, 3)


def _key_columns(x, g, dk, odd):
    xt = jnp.concatenate([x, jnp.zeros((LANE - g, LANE), f32)], axis=0).T
    if dk == LANE:
        return xt
    return jnp.where(odd, xt[dk:2 * dk], xt[0:dk])


def _chunk_tok_kernel(*refs, kind, l, g, has_acc):
    ins, (o_ref, sout_ref) = refs[:-2], refs[-2:]
    if has_acc:
        ins = ins[:-1]
    odd = (pl.program_id(1) % 2) == 1
    if kind == "gla":
        q_ref, k_ref, v_ref, lo_ref, gup_ref, gb_ref, s_ref, gn_ref = ins
        q = q_ref[...] * (GLA_DK ** -0.5)
        k = k_ref[...]
        z = _dotf(lo_ref[...], gup_ref[...]) + gb_ref[...]
        dec = jnp.exp(_log_sigmoid(z) * (1.0 / GLA_GATE_NORM))
        v = v_ref[...]
        dk = GLA_DK
    else:
        q_ref, f_ref, i_ref, logits_ref, s_ref, gn_ref = ins
        q, k, v, logf = _hgrn_inputs(q_ref, f_ref, i_ref, logits_ref, l)
        dec = jnp.exp(logf)
        dk = HGRN_DK
    qt, kt, dt = (_key_columns(x, g, dk, odd) for x in (q, k, dec))
    o_rows = []
    for b in range(g):
        qcol, kcol, dcol = (jnp.broadcast_to(t[:, b:b + 1], (dk, LANE)) for t in (qt, kt, dt))
        s_new = s_ref[b] * dcol + kcol * v[b:b + 1, :]
        sout_ref[b] = s_new
        o_rows.append(jnp.sum(qcol * s_new, axis=0, keepdims=True))
    o = jnp.concatenate(o_rows, axis=0)
    ms = jnp.mean(o * o, axis=-1, keepdims=True)
    o_ref[...] = o * lax.rsqrt(ms + EPS) * gn_ref[...]


def _chunk_tok(kind, proj2, extra, gn3, s_all, acc, l, g):
    b = proj2.shape[0]
    h, dk = (GLA_H, GLA_DK) if kind == "gla" else (HGRN_H, HGRN_DK)

    def head_col(off):
        per = LANE // dk
        return pl.BlockSpec((g, LANE), lambda i, j: (i, off // LANE + j // per))

    if kind == "gla":
        gup, gb = extra
        in_specs = [head_col(C_GLA_Q), head_col(C_GLA_K),
                    pl.BlockSpec((g, LANE), lambda i, j: (i, C_GLA_V // LANE + j)),
                    pl.BlockSpec((g, LANE), lambda i, j: (i, C_GLA_LO // LANE)),
                    pl.BlockSpec((None, LANE, LANE), lambda i, j: (l, 0, j // 2)),
                    pl.BlockSpec((None, 1, LANE), lambda i, j: (l, 0, j // 2))]
        args = [proj2, proj2, proj2, proj2, gup, gb]
    else:
        (logits,) = extra
        in_specs = [head_col(C_HQ), head_col(C_HF), head_col(C_HI),
                    pl.BlockSpec((DEPTH, LANE), lambda i, j: (0, j))]
        args = [proj2, proj2, proj2, logits]
    state_spec = pl.BlockSpec((None, g, None, dk, LANE), lambda i, j: (l, i, j, 0, 0))
    in_specs += [state_spec, pl.BlockSpec((None, 1, LANE), lambda i, j: (l, 0, j))]
    args += [s_all, gn3]
    aliases = {}
    if acc is not None:
        in_specs.append(pl.BlockSpec(memory_space=pl.ANY))
        aliases = {len(args): 1}
        args.append(acc)
    return pl.pallas_call(
        functools.partial(_chunk_tok_kernel, kind=kind, l=l, g=g, has_acc=acc is not None),
        grid=(b // g, h),
        in_specs=in_specs,
        out_specs=[pl.BlockSpec((g, LANE), lambda i, j: (i, j)), state_spec],
        out_shape=[
            jax.ShapeDtypeStruct((b, W_BRANCH), f32),
            jax.ShapeDtypeStruct(s_all.shape, f32),
        ],
        input_output_aliases=aliases,
        compiler_params=_cparams(("parallel", "parallel")),
        name=kind + "_tok",
    )(*args)


def _lru_gate_dot(y_hi, y_lo, w_ref, col0):
    d = functools.partial(jnp.dot, preferred_element_type=f32)
    halves = []
    for h0 in range(0, W_BRANCH, MXU_TILE):
        w_hi = w_ref[0, h0:h0 + MXU_TILE, col0 + h0:col0 + h0 + MXU_TILE]
        w_lo = w_ref[1, h0:h0 + MXU_TILE, col0 + h0:col0 + h0 + MXU_TILE]
        a, b = y_hi[:, h0:h0 + MXU_TILE], y_lo[:, h0:h0 + MXU_TILE]
        halves.append(d(a, w_hi) + d(a, w_lo) + d(b, w_hi))
    return jnp.concatenate(halves, axis=1)


def _lru_gates(y, w_ref, pvec):
    b_a, b_x, lam = pvec[1:2, :], pvec[2:3, :], pvec[3:4, :]
    y_hi, y_lo = _split(y, 2)
    r = _sigmoid(_lru_gate_dot(y_hi, y_lo, w_ref, 0) + b_a)
    ig = _sigmoid(_lru_gate_dot(y_hi, y_lo, w_ref, W_BRANCH) + b_x)
    log_a = -LRU_C * r * _softplus(-lam)
    a = jnp.exp(log_a)
    one_m_a2 = -jnp.tanh(log_a) * (jnp.exp(2.0 * log_a) + 1.0)
    b = jnp.sqrt(one_m_a2) * (ig * y)
    return a, b


def _lru_seq_kernel(x_ref, cw_ref, pvec_ref, wbd_ref, o_ref, xpad, hcar, *, lb):
    tb = pl.program_id(1)

    @pl.when(tb == 0)
    def _():
        xpad[0:8, :] = jnp.zeros((8, W_BRANCH), f32)
        hcar[...] = jnp.zeros_like(hcar)

    x = x_ref[...]
    xpad[8:8 + lb, :] = x
    cw = cw_ref[...]
    pvec = pvec_ref[...]
    y = pvec[0:1, :] + x * cw[3:4, :]
    for j in range(CONV_W - 1):
        y = y + xpad[5 + j:5 + j + lb, :] * cw[j:j + 1, :]
    a, b = _lru_gates(y, wbd_ref, pvec)
    row = lax.broadcasted_iota(jnp.int32, (lb, 1), 0)
    s = 1
    while s < lb:
        m = row >= s
        b = jnp.where(m, a * pltpu.roll(b, s, 0) + b, b)
        a = jnp.where(m, a * pltpu.roll(a, s, 0), a)
        s *= 2
    h = a * hcar[...] + b
    o_ref[...] = h
    hcar[...] = h[lb - 1:lb, :]
    xpad[5:8, :] = x[lb - 3:lb, :]


def _lru_seq(proj3, cw, pvec, wbd, l, lb):
    b, seq_len, _ = proj3.shape

    def layer(shape):
        return pl.BlockSpec((None,) + shape, lambda *_: (l,) + (0,) * len(shape))

    return pl.pallas_call(
        functools.partial(_lru_seq_kernel, lb=lb),
        grid=(b, seq_len // lb),
        in_specs=[
            pl.BlockSpec((None, lb, W_BRANCH), lambda i, t: (i, t, C_LRU // W_BRANCH)),
            layer((CONV_W, W_BRANCH)), layer((8, W_BRANCH)), layer((2, W_BRANCH, 2 * W_BRANCH)),
        ],
        out_specs=pl.BlockSpec((None, lb, W_BRANCH), lambda i, t: (i, t, 0)),
        out_shape=jax.ShapeDtypeStruct((b, seq_len, W_BRANCH), f32),
        scratch_shapes=[
            pltpu.VMEM((8 + lb, W_BRANCH), f32),
            pltpu.VMEM((1, W_BRANCH), f32),
        ],
        compiler_params=_cparams(("parallel", "arbitrary")),
        name="lru_seq",
    )(proj3, cw, pvec, wbd)


def _lru_tok_kernel(x_ref, b0_ref, b1_ref, b2_ref, h0_ref, cw_ref, pvec_ref, wbd_ref, o_ref):
    cw = cw_ref[...]
    pvec = pvec_ref[...]
    y = (pvec[0:1, :] + b0_ref[...] * cw[0:1, :] + b1_ref[...] * cw[1:2, :] + b2_ref[...] * cw[2:3, :]
         + x_ref[...] * cw[3:4, :])
    a, b = _lru_gates(y, wbd_ref, pvec)
    o_ref[...] = a * h0_ref[...] + b


def _lru_tok(proj2, bufs, h0, cw, pvec, wbd, l):
    b = proj2.shape[0]

    def layer(shape):
        return pl.BlockSpec((None,) + shape, lambda *_: (l,) + (0,) * len(shape))

    row = pl.BlockSpec((b, W_BRANCH), lambda i: (0, 0))
    return pl.pallas_call(
        _lru_tok_kernel,
        grid=(1,),
        in_specs=[pl.BlockSpec((b, W_BRANCH), lambda i: (0, C_LRU // W_BRANCH)), row, row, row, row,
                  layer((CONV_W, W_BRANCH)), layer((8, W_BRANCH)), layer((2, W_BRANCH, 2 * W_BRANCH))],
        out_specs=row,
        out_shape=jax.ShapeDtypeStruct((b, W_BRANCH), f32),
        compiler_params=_cparams(("arbitrary",)),
        name="lru_tok",
    )(proj2, *bufs, h0, cw, pvec, wbd)


def _prep_weights(p):
    w = jnp.swapaxes(p["w_in"], 1, 2)
    pad = jnp.zeros((DEPTH, LANE - GLA_GATE_RANK, D_MODEL), w.dtype)
    w_in_p = jnp.concatenate([
        w[:, 0:1536], w[:, 2176:2688], w[:, 2704:4240], w[:, 4240:4752], w[:, 4752:6800],
        w[:, 1664:2176], w[:, 1536:1664], w[:, 2688:2704], pad], axis=1).astype(bf16)
    zl = jnp.zeros((DEPTH, RWKV_LORA, W_BRANCH), f32)
    wup = jnp.concatenate([jnp.concatenate([p["rwkv_w_up"], zl], axis=2),
                           jnp.concatenate([zl, p["rwkv_a_up"]], axis=2)], axis=1)
    wup_hi = wup.astype(bf16)
    wup = jnp.stack([wup_hi, (wup - wup_hi.astype(f32)).astype(bf16)], axis=1)
    zrow = jnp.zeros((DEPTH, W_BRANCH), f32)
    rwkv_vec = jnp.stack([p["rwkv_w0"], p["rwkv_a0"], p["rwkv_k_k"], p["rwkv_k_a"],
                          p["rwkv_r_k"].reshape(DEPTH, W_BRANCH), p["rwkv_gn_g"], p["rwkv_gn_b"], zrow], axis=1)
    mu = p["rwkv_mu"]
    gup = jnp.concatenate([p["gla_gk_up"], jnp.zeros((DEPTH, LANE - GLA_GATE_RANK, GLA_KW), f32)], axis=1)
    eye = jnp.eye(LRU_BLOCKS, dtype=f32)

    def bd(wb):
        return jnp.einsum("lhij,hg->lhigj", wb, eye).reshape(DEPTH, W_BRANCH, W_BRANCH)

    lru_vec = jnp.stack([p["lru_conv_b"], p["lru_b_a"], p["lru_b_x"], p["lru_lambda"],
                         zrow, zrow, zrow, zrow], axis=1)
    lru_w = jnp.concatenate([bd(p["lru_w_a"]), bd(p["lru_w_x"])], axis=2)
    lru_w_hi = lru_w.astype(bf16)
    lru_w_lo = (lru_w - lru_w_hi.astype(f32)).astype(bf16)
    return dict(
        w_in_p=w_in_p,
        w_out_b=p["w_out"].astype(bf16),
        norm_g3=p["norm_g"].reshape(DEPTH, 1, D_MODEL),
        final_g2=p["final_g"].reshape(1, D_MODEL),
        rwkv=(mu[:, None, 0:1536], mu[:, None, 1536:1664], wup, rwkv_vec) + _rwkv_consts(),
        gla=(gup, p["gla_gk_b"].reshape(DEPTH, 1, GLA_KW)),
        gla_gn=p["gla_gn_g"].reshape(DEPTH, 1, W_BRANCH),
        hgrn=(p["hgrn_lb_logits"],),
        hgrn_gn=p["hgrn_gn_g"].reshape(DEPTH, 1, W_BRANCH),
        lru=(p["lru_conv_w"], lru_vec, jnp.stack([lru_w_hi, lru_w_lo], axis=1)),
    )


def _shift_state(proj_last):
    return jnp.concatenate([proj_last[..., 0:1536], proj_last[..., C_RWKV_LO:C_RWKV_LO + LANE]], axis=-1)


def _rwkv_state_in(s):
    lead = s.shape[:-3]
    return jnp.swapaxes(s, -3, -2).reshape(lead + (RWKV_HEAD, W_BRANCH))


def _rwkv_state_out(s):
    lead = s.shape[:-2]
    return jnp.swapaxes(s.reshape(lead + (RWKV_HEAD, RWKV_H, RWKV_HEAD)), -3, -2)


def _trunk_seq(x, mod, wts):
    b, seq_len, _ = x.shape
    t = b * seq_len
    lb = min(LB_CHUNK, seq_len)
    tm = min(TM_IN, seq_len)
    per_seq = seq_len // tm
    mod3 = mod.reshape(DEPTH * b * 3, 1, D_MODEL)
    x2 = x.reshape(t, D_MODEL)
    zeros = functools.partial(jnp.zeros, dtype=f32)
    new = []
    for l in range(DEPTH):
        mod_l = mod3[l * b * 3:(l + 1) * b * 3]
        proj2 = _inproj(x2, wts["norm_g3"], mod_l, wts["w_in_p"], l, per_seq, tm)
        proj3 = proj2.reshape(b, seq_len, PROJ_P)
        o_a, s_wkv = _rwkv_seq(proj3, zeros((b, 1, 1536)), zeros((b, 1, LANE)),
                               zeros((b, RWKV_HEAD, W_BRANCH)), wts["rwkv"], l, min(LB_RWKV, seq_len))
        o_b, st_gla = _chunk_seq("gla", proj3, wts["gla"], wts["gla_gn"], l, lb)
        o_c, st_hgrn = _chunk_seq("hgrn", proj3, wts["hgrn"], wts["hgrn_gn"], l, lb)
        o_d = _lru_seq(proj3, *wts["lru"], l, lb)
        outs = [o.reshape(t, W_BRANCH) for o in (o_a, o_b, o_c, o_d)]
        tmo = min(TM_OUT, seq_len)
        x2 = _outproj(outs, proj2, x2, mod_l, wts["w_out_b"], wts["final_g2"], l, seq_len // tmo, tmo,
                      final=(l == DEPTH - 1))
        last = proj3[:, seq_len - 1]
        new.append((
            _shift_state(last),
            _rwkv_state_out(s_wkv),
            _state_from_transposed(st_gla, GLA_H, GLA_DK),
            _state_from_transposed(st_hgrn, HGRN_H, HGRN_DK),
            proj3[:, seq_len - (CONV_W - 1):, C_LRU:C_LRU + W_BRANCH],
            o_d[:, seq_len - 1],
        ))
    return x2.reshape(b, seq_len, D_MODEL), tuple(jnp.stack([n[i] for n in new], axis=0) for i in range(6))


def _trunk_tok(x, mod, states, wts):
    b = x.shape[0]
    s_shift, s_wkv, s_gla, s_hgrn, s_conv, s_h = states
    x2 = x.reshape(b, D_MODEL)
    s_wkv_t = _rwkv_state_in(s_wkv)
    prev_rkv, prev_lo = s_shift[:, :, 0:1536], s_shift[:, :, 1536:1664]
    n_wkv = n_gla = n_hgrn = None
    new = []
    for l in range(DEPTH):
        proj2 = _inproj(x2, wts["norm_g3"], mod, wts["w_in_p"], l, None, b)
        o_a, n_wkv = _rwkv_tok(proj2, prev_rkv, prev_lo, s_wkv_t, n_wkv, wts["rwkv"], l, TOK_GROUP_RWKV)
        o_b, n_gla = _chunk_tok("gla", proj2, wts["gla"], wts["gla_gn"], s_gla, n_gla, l, TOK_GROUP)
        o_c, n_hgrn = _chunk_tok("hgrn", proj2, wts["hgrn"], wts["hgrn_gn"], s_hgrn, n_hgrn, l, TOK_GROUP)
        conv = s_conv[l]
        o_d = _lru_tok(proj2, [conv[:, 0], conv[:, 1], conv[:, 2]], s_h[l], *wts["lru"], l)
        x2 = _outproj([o_a, o_b, o_c, o_d], proj2, x2, mod, wts["w_out_b"], wts["final_g2"], l, None, b,
                      final=(l == DEPTH - 1))
        n_conv = jnp.concatenate([conv[:, 1:], proj2[:, None, C_LRU:C_LRU + W_BRANCH]], axis=1)
        new.append((_shift_state(proj2), n_conv, o_d))
    n_shift, n_conv, n_h = (jnp.stack([n[i] for n in new], axis=0) for i in range(3))
    return x2.reshape(b, 1, D_MODEL), (n_shift, _rwkv_state_out(n_wkv), n_gla, n_hgrn, n_conv, n_h)


def kernel(x_prompt, x_sample, c_prompt, c_sample, state_rwkv_shift, state_rwkv_wkv, state_gla, state_hgrn, state_lru_conv, state_lru_h, norm_g, w_ada, b_ada, w_in, w_out, rwkv_mu, rwkv_w0, rwkv_w_up, rwkv_a0, rwkv_a_up, rwkv_k_k, rwkv_k_a, rwkv_r_k, rwkv_gn_g, rwkv_gn_b, gla_gk_up, gla_gk_b, gla_gn_g, hgrn_lb_logits, hgrn_gn_g, lru_conv_w, lru_conv_b, lru_w_a, lru_b_a, lru_w_x, lru_b_x, lru_lambda, final_g):
    p = dict(norm_g=norm_g, w_in=w_in, w_out=w_out, rwkv_mu=rwkv_mu, rwkv_w0=rwkv_w0, rwkv_w_up=rwkv_w_up,
             rwkv_a0=rwkv_a0, rwkv_a_up=rwkv_a_up, rwkv_k_k=rwkv_k_k, rwkv_k_a=rwkv_k_a, rwkv_r_k=rwkv_r_k,
             rwkv_gn_g=rwkv_gn_g, rwkv_gn_b=rwkv_gn_b, gla_gk_up=gla_gk_up, gla_gk_b=gla_gk_b,
             gla_gn_g=gla_gn_g, hgrn_lb_logits=hgrn_lb_logits, hgrn_gn_g=hgrn_gn_g, lru_conv_w=lru_conv_w,
             lru_conv_b=lru_conv_b, lru_w_a=lru_w_a, lru_b_a=lru_b_a, lru_w_x=lru_w_x, lru_b_x=lru_b_x,
             lru_lambda=lru_lambda, final_g=final_g)
    wts = _prep_weights(p)
    bp = x_prompt.shape[0]
    bs = x_sample.shape[0]
    pad_rows = (-bp) % 8
    c_all = jnp.concatenate([c_prompt, jnp.zeros((pad_rows, D_MODEL), f32), c_sample], axis=0)
    mod = _ada(c_all, w_ada, b_ada)
    mod_p = mod[:, 0:bp]
    mod_s = mod[:, bp + pad_rows:bp + pad_rows + bs]
    y_p, st_p = _trunk_seq(x_prompt, mod_p, wts)
    states = (state_rwkv_shift, state_rwkv_wkv, state_gla, state_hgrn, state_lru_conv, state_lru_h)
    y_s, st_s = _trunk_tok(x_sample, mod_s, states, wts)
    return (y_p, y_s) + st_p + st_s
```

```python
import functools

import jax
import jax.numpy as jnp
from jax import lax
from jax.experimental import pallas as pl
from jax.experimental.pallas import tpu as pltpu

f32 = jnp.float32
bf16 = jnp.bfloat16

D_MODEL = 2048
DEPTH = 4
W_BRANCH = 512
EPS = 1e-6

RWKV_H = 8
RWKV_HEAD = 64
RWKV_LORA = 64
RWKV_PROJ = 3 * W_BRANCH + 2 * RWKV_LORA
RWKV_GN_EPS = 64e-5

GLA_H = 4
GLA_DK = 64
GLA_DV = 128
GLA_KW = GLA_H * GLA_DK
GLA_GATE_RANK = 16
GLA_GATE_NORM = 16.0

HGRN_H = 4
HGRN_DK = 128
HGRN_DV = 128
LB_FLOOR = 1e-30

LRU_BLOCKS = 8
LRU_BLOCK = 64
LRU_C = 8.0
CONV_W = 4

CHUNK = 32

C_R, C_K, C_V = 0, 512, 1024
C_GLA_V = 1536
C_HQ, C_HF, C_HI = 2048, 2560, 3072
C_LRU = 3584
C_GATE = 4096
C_GLA_Q, C_GLA_K = 6144, 6400
C_RWKV_LO = 6656
C_GLA_LO = 6784
PROJ_P = 6912

LANE = 128
MXU_TILE = 256
VMEM_LIMIT = 56 * 1024 * 1024

TM_IN = 1024
TN_IN = 1152
TM_OUT = 256
LB_CHUNK = 256
LB_RWKV = 128
RWKV_GROUP = 8
TOK_GROUP = 16

HIGHEST = lax.Precision.HIGHEST


def _cparams(sem):
    return pltpu.CompilerParams(dimension_semantics=sem, vmem_limit_bytes=VMEM_LIMIT)


def _dotf(a, b):
    return jnp.dot(a, b, preferred_element_type=f32, precision=HIGHEST)


def _prec(a):
    return HIGHEST if a.dtype == f32 else None


def _dot_nt(a, b):
    return lax.dot_general(a, b, (((1,), (1,)), ((), ())), preferred_element_type=f32, precision=_prec(a))


def _dot_tn(a, b):
    return lax.dot_general(a, b, (((0,), (0,)), ((), ())), preferred_element_type=f32, precision=_prec(a))


def _split(x, pieces):
    out = []
    for i in range(pieces):
        part = x.astype(bf16)
        out.append(part)
        if i + 1 < pieces:
            x = x - part.astype(f32)
    return out


def _half_dot(x, ones):
    d = functools.partial(jnp.dot, preferred_element_type=f32)
    kh = ones.shape[0]
    if x.shape[1] == kh:
        return d(x, ones)
    return jnp.concatenate([d(x[:, 0:kh], ones), d(x[:, kh:2 * kh], ones)], axis=1)


def _seg_sum(x, ones, pieces=2):
    return sum(_half_dot(part, ones) for part in _split(x, pieces))


def _ones_seg(ones, x, pieces=2):
    d = functools.partial(jnp.dot, preferred_element_type=f32)
    return sum(d(ones, part) for part in _split(x, pieces))


def _sigmoid(x):
    return jax.nn.sigmoid(x)


def _softplus(x):
    return jnp.maximum(x, 0.0) + jnp.log1p(jnp.exp(-jnp.abs(x)))


def _log_sigmoid(x):
    return -_softplus(-x)


def _ada_kernel(c_ref, w_ref, b_ref, o_ref):
    c = c_ref[...]
    s = (c * _sigmoid(c)).astype(bf16)
    o_ref[...] = jnp.dot(s, w_ref[...].astype(bf16), preferred_element_type=f32) + b_ref[...]


def _ada(c_all, w_ada, b_ada):
    rows = c_all.shape[0]
    tn = 512
    n = w_ada.shape[2]
    return pl.pallas_call(
        _ada_kernel,
        grid=(DEPTH, n // tn),
        in_specs=[
            pl.BlockSpec((rows, D_MODEL), lambda l, j: (0, 0)),
            pl.BlockSpec((None, D_MODEL, tn), lambda l, j: (l, 0, j)),
            pl.BlockSpec((None, 1, tn), lambda l, j: (l, 0, j)),
        ],
        out_specs=pl.BlockSpec((None, rows, tn), lambda l, j: (l, 0, j)),
        out_shape=jax.ShapeDtypeStruct((DEPTH, rows, n), f32),
        compiler_params=_cparams(("parallel", "parallel")),
        name="ada_mod",
    )(c_all, w_ada, b_ada.reshape(DEPTH, 1, n))


def _inproj_kernel(x_ref, g_ref, sc_ref, sh_ref, w_ref, o_ref, h_ref):
    @pl.when(pl.program_id(1) == 0)
    def _():
        x = x_ref[...]
        ms = jnp.mean(x * x, axis=-1, keepdims=True)
        h = x * lax.rsqrt(ms + EPS) * g_ref[...]
        h = h * (1.0 + sc_ref[...]) + sh_ref[...]
        h_ref[...] = h.astype(bf16)

    o_ref[...] = _dot_nt(h_ref[...], w_ref[...])


def _mod_spec(l, which, per_seq_tiles, tm):
    if per_seq_tiles is None:
        return pl.BlockSpec((None, tm, D_MODEL), lambda i, *_: (l, 0, which))
    return pl.BlockSpec((None, 1, D_MODEL), lambda i, *_: (i // per_seq_tiles * 3 + which, 0, 0))


def _inproj(x2, norm_g3, mod, w_in_p, l, per_seq_tiles, tm):
    t = x2.shape[0]
    tn = TN_IN
    return pl.pallas_call(
        _inproj_kernel,
        grid=(t // tm, PROJ_P // tn),
        in_specs=[
            pl.BlockSpec((tm, D_MODEL), lambda i, j: (i, 0)),
            pl.BlockSpec((None, 1, D_MODEL), lambda i, j: (l, 0, 0)),
            _mod_spec(l, 1, per_seq_tiles, tm),
            _mod_spec(l, 0, per_seq_tiles, tm),
            pl.BlockSpec((None, tn, D_MODEL), lambda i, j: (l, j, 0)),
        ],
        out_specs=pl.BlockSpec((tm, tn), lambda i, j: (i, j)),
        out_shape=jax.ShapeDtypeStruct((t, PROJ_P), f32),
        scratch_shapes=[pltpu.VMEM((tm, D_MODEL), bf16)],
        compiler_params=_cparams(("parallel", "arbitrary")),
        name="in_proj",
    )(x2, norm_g3, mod, mod, w_in_p)


def _outproj_kernel(oa_ref, ob_ref, oc_ref, od_ref, pz_ref, x_ref, gate_ref, w_ref, fg_ref, o_ref, *, final):
    z = pz_ref[...]
    o = jnp.concatenate([oa_ref[...], ob_ref[...], oc_ref[...], od_ref[...]], axis=1)
    o = o * (z * _sigmoid(z))
    y = jnp.dot(o.astype(bf16), w_ref[...], preferred_element_type=f32)
    xn = x_ref[...] + gate_ref[...] * y
    if final:
        ms = jnp.mean(xn * xn, axis=-1, keepdims=True)
        xn = xn * lax.rsqrt(ms + EPS) * fg_ref[...]
    o_ref[...] = xn


def _outproj(outs, proj2, x2, mod, w_out_b, final_g2, l, per_seq_tiles, tm, final):
    t = x2.shape[0]
    mix_spec = pl.BlockSpec((tm, W_BRANCH), lambda i: (i, 0))
    return pl.pallas_call(
        functools.partial(_outproj_kernel, final=final),
        grid=(t // tm,),
        in_specs=[
            mix_spec, mix_spec, mix_spec, mix_spec,
            pl.BlockSpec((tm, D_MODEL), lambda i: (i, C_GATE // D_MODEL)),
            pl.BlockSpec((tm, D_MODEL), lambda i: (i, 0)),
            _mod_spec(l, 2, per_seq_tiles, tm),
            pl.BlockSpec((None, D_MODEL, D_MODEL), lambda i: (l, 0, 0)),
            pl.BlockSpec((1, D_MODEL), lambda i: (0, 0)),
        ],
        out_specs=pl.BlockSpec((tm, D_MODEL), lambda i: (i, 0)),
        out_shape=jax.ShapeDtypeStruct((t, D_MODEL), f32),
        compiler_params=_cparams(("parallel",)),
        name="out_proj",
    )(*outs, proj2, x2, mod, w_out_b, final_g2)


def _rwkv_prologue(r, k, v, lo, pr, pk, pv, plo, mu_rkv, mu_lo, wup, pvec, bo, sub_block=0):
    w0, a0, k_k, k_a, r_k = (pvec[i:i + 1, :] for i in range(5))
    xr = r + (pr - r) * mu_rkv[:, 0:512]
    xk = k + (pk - k) * mu_rkv[:, 512:1024]
    xv = v + (pv - v) * mu_rkv[:, 1024:1536]
    xlo = lo + (plo - lo) * mu_lo
    lane = lax.broadcasted_iota(jnp.int32, xlo.shape, 1)
    act = jnp.where(lane < RWKV_LORA, jnp.tanh(xlo), xlo)
    act_hi, act_lo = _split(act, 2)
    d = functools.partial(jnp.dot, preferred_element_type=f32)
    up = d(act_hi, wup[0]) + d(act_hi, wup[1]) + d(act_lo, wup[0])
    w_raw = -_softplus(-(w0 + up[:, 0:512])) - 0.5
    logw = -jnp.exp(w_raw)
    a = _sigmoid(a0 + up[:, 512:1024])
    kk = xk * k_k
    kk = kk / jnp.maximum(jnp.sqrt(_seg_sum(kk * kk, bo)), 1e-12)
    kh = xk * (1.0 + (a - 1.0) * k_a)
    alp = kk * a
    ar = _seg_sum(alp * xr, bo, pieces=1)
    out = dict(vv=xv, xr=xr, kr=_seg_sum(kh * xr, bo, pieces=1),
               bonus=_seg_sum(xr * kh * r_k, bo, pieces=1) * xv)
    if not sub_block:
        ew = jnp.exp(logw)
        out.update(kap=kk, alp=alp, kh=kh, ew=ew, wr=ew * xr - ar * kk)
        return out
    rowi = lax.broadcasted_iota(jnp.int32, (logw.shape[0], 1), 0) & (sub_block - 1)
    lc = logw
    step = 1
    while step < sub_block:
        lc = lc + jnp.where(rowi >= step, pltpu.roll(lc, step, 0), 0.0)
        step *= 2
    gam, ginv = jnp.exp(lc), jnp.exp(-lc)
    kap = kk * jnp.exp(lc - logw)
    out.update(kap=kap, alp=alp * ginv, kh=kh * ginv, gam=gam, wr=gam * xr - ar * kap)
    return out


_RWKV_STEP_KEYS = ("kap", "alp", "kh", "vv", "wr", "kr")


def _rwkv_step(s, kap, alp, kh, vv, wr, kr, bo, idt, ew=None):
    s_new, o_row = [], []
    for c0 in range(0, W_BRANCH, MXU_TILE):
        sl = slice(c0, c0 + MXU_TILE)
        s_h, idt_h, v_h = s[:, sl], idt[:, sl], vv[:, sl]
        s_b = s_h.astype(bf16)
        x = jnp.concatenate([s_b * kap[:, sl].astype(bf16), s_b * wr[:, sl].astype(bf16),
                             idt_h.astype(bf16) * v_h.astype(bf16)], axis=0)
        red = jnp.dot(x, bo, preferred_element_type=f32)
        kept = s_h if ew is None else s_h * ew[:, sl]
        s_new.append(kept - red[0:64] * alp[:, sl] + red[128:192] * kh[:, sl])
        o_row.append(jnp.sum(red[64:128] * idt_h, axis=0, keepdims=True) + v_h * kr[:, sl])
    return jnp.concatenate(s_new, axis=1), jnp.concatenate(o_row, axis=1)


def _rwkv_epilogue(o, bonus, pvec, bo):
    gn_g, gn_b = pvec[5:6, :], pvec[6:7, :]
    mu = _seg_sum(o, bo, pieces=1) * (1.0 / RWKV_HEAD)
    d = o - mu
    var = _seg_sum(d * d, bo, pieces=1) * (1.0 / RWKV_HEAD)
    return d * lax.rsqrt(var + RWKV_GN_EPS) * gn_g + gn_b + bonus


def _rwkv_seq_kernel(r_ref, k_ref, v_ref, lo_ref, prkv_ref, plo_ref, sin_ref, mu_rkv_ref, mu_lo_ref, wup_ref,
                     pvec_ref, bo_ref, idt_ref, o_ref, sout_ref,
                     s_sc, crkv_sc, clo_sc, kap_sc, gam_sc, alp_sc, kh_sc, vv_sc, wr_sc, kr_sc, oraw_sc,
                     bonus_sc, *, g_seqs, lb):
    tb = pl.program_id(1)

    @pl.when(tb == 0)
    def _():
        s_sc[...] = sin_ref[...]
        crkv_sc[...] = prkv_ref[...]
        clo_sc[...] = plo_ref[...]

    bo = bo_ref[...]
    idt = idt_ref[...]
    pvec = pvec_ref[...]
    step_sc = dict(kap=kap_sc, alp=alp_sc, kh=kh_sc, vv=vv_sc, wr=wr_sc, kr=kr_sc)

    row0 = lax.broadcasted_iota(jnp.int32, (lb, 1), 0) == 0
    for g in range(g_seqs):
        cur = [r_ref[g], k_ref[g], v_ref[g]]
        lo = lo_ref[g]
        carry = crkv_sc[g]
        prev = [jnp.where(row0, carry[:, i * 512:(i + 1) * 512], pltpu.roll(c, 1, 0)) for i, c in enumerate(cur)]
        plo = jnp.where(row0, clo_sc[g], pltpu.roll(lo, 1, 0))
        res = _rwkv_prologue(cur[0], cur[1], cur[2], lo, prev[0], prev[1], prev[2], plo,
                             mu_rkv_ref[...], mu_lo_ref[...], wup_ref[...], pvec, bo, sub_block=RWKV_GROUP)
        for key in _RWKV_STEP_KEYS:
            step_sc[key][g] = res[key]
        gam_sc[g] = res["gam"]
        bonus_sc[g] = res["bonus"]
        for i, c in enumerate(cur):
            crkv_sc[g, :, i * 512:(i + 1) * 512] = c[lb - 1:lb, :]
        clo_sc[g] = lo[lb - 1:lb, :]

    def group(i, carry):
        t0 = pl.multiple_of(i * RWKV_GROUP, RWKV_GROUP)
        states = [s_sc[g] for g in range(g_seqs)]
        for j in range(RWKV_GROUP):
            for g in range(g_seqs):
                rows = [step_sc[key][g, pl.ds(t0 + j, 1), :] for key in _RWKV_STEP_KEYS]
                states[g], o_row = _rwkv_step(states[g], *rows, bo, idt)
                oraw_sc[g, pl.ds(t0 + j, 1), :] = o_row
        for g in range(g_seqs):
            s_sc[g] = states[g] * gam_sc[g, pl.ds(t0 + RWKV_GROUP - 1, 1), :]
        return carry

    lax.fori_loop(0, lb // RWKV_GROUP, group, 0)

    for g in range(g_seqs):
        o_ref[g] = _rwkv_epilogue(oraw_sc[g], bonus_sc[g], pvec, bo)

    @pl.when(tb == pl.num_programs(1) - 1)
    def _():
        sout_ref[...] = s_sc[...]


_RWKV_TOK_KEYS = ("kap", "ew", "alp", "kh", "vv", "xr")


def _rwkv_tok_kernel(r_ref, k_ref, v_ref, lo_ref, prkv_ref, plo_ref, sin_ref, mu_rkv_ref, mu_lo_ref, wup_ref,
                     pvec_ref, bo_ref, idt_ref, *rest):
    del idt_ref
    o_ref, sout_ref = rest[-10:-8]
    kap_t, ew_t, alp_t, kh_t, v_t, r_t, o_t, bonus_sc = rest[-8:]
    h = pl.program_id(0)
    bo = bo_ref[...]
    pvec = pvec_ref[...]

    @pl.when(h == 0)
    def _():
        prkv = prkv_ref[...]
        res = _rwkv_prologue(r_ref[...], k_ref[...], v_ref[...], lo_ref[...],
                             prkv[:, 0:512], prkv[:, 512:1024], prkv[:, 1024:1536], plo_ref[...],
                             mu_rkv_ref[...], mu_lo_ref[...], wup_ref[...], pvec, bo)
        for key, dst in zip(_RWKV_TOK_KEYS, (kap_t, ew_t, alp_t, kh_t, v_t, r_t)):
            dst[...] = res[key].T
        bonus_sc[...] = res["bonus"]

    row0 = pl.multiple_of(h * RWKV_HEAD, RWKV_HEAD)
    keys = pl.ds(row0, RWKV_HEAD)
    kap, ew, alp, kh, rr = kap_t[keys, :], ew_t[keys, :], alp_t[keys, :], kh_t[keys, :], r_t[keys, :]

    def value_row(v, carry):
        s = sin_ref[v]
        sk = jnp.sum(s * kap, axis=0, keepdims=True)
        s_new = s * ew - sk * alp + v_t[pl.ds(row0 + v, 1), :] * kh
        sout_ref[v] = s_new
        o_t[pl.ds(row0 + v, 1), :] = jnp.sum(s_new * rr, axis=0, keepdims=True)
        return carry

    lax.fori_loop(0, RWKV_HEAD, value_row, 0, unroll=4)

    @pl.when(h == pl.num_programs(0) - 1)
    def _():
        o_ref[...] = _rwkv_epilogue(o_t[...].T, bonus_sc[...], pvec, bo)


def _rwkv_consts():
    i = jnp.arange(W_BRANCH)
    j = jnp.arange(MXU_TILE)
    bo = (j[:, None] // RWKV_HEAD == j[None, :] // RWKV_HEAD).astype(bf16)
    idt = (jnp.arange(RWKV_HEAD)[:, None] == (i[None, :] % RWKV_HEAD)).astype(f32)
    return bo, idt


def _rwkv_weight_specs(l):
    def cs(shape):
        return pl.BlockSpec((None,) + shape, lambda *_: (l,) + (0,) * len(shape))

    def const(shape):
        return pl.BlockSpec(shape, lambda *_: (0,) * len(shape))

    return [cs((1, 1536)), cs((1, LANE)), cs((2, LANE, 1024)), cs((8, W_BRANCH)),
            const((MXU_TILE, MXU_TILE)), const((RWKV_HEAD, W_BRANCH))]


def _rwkv_seq(proj3, prev_rkv, prev_lo, s_in, wts, l, lb):
    b, seq_len, _ = proj3.shape
    g = b
    nt = seq_len // lb

    def col(width, off):
        return pl.BlockSpec((g, lb, width), lambda i, t: (i, t, off // width))

    tok_sc = pltpu.VMEM((g, lb, W_BRANCH), f32)
    out = pl.pallas_call(
        functools.partial(_rwkv_seq_kernel, g_seqs=g, lb=lb),
        grid=(b // g, nt),
        in_specs=[
            col(512, C_R), col(512, C_K), col(512, C_V), col(LANE, C_RWKV_LO),
            pl.BlockSpec((g, 1, 1536), lambda i, t: (i, 0, 0)),
            pl.BlockSpec((g, 1, LANE), lambda i, t: (i, 0, 0)),
            pl.BlockSpec((g, RWKV_HEAD, W_BRANCH), lambda i, t: (i, 0, 0)),
        ] + _rwkv_weight_specs(l),
        out_specs=[
            pl.BlockSpec((g, lb, W_BRANCH), lambda i, t: (i, t, 0)),
            pl.BlockSpec((g, RWKV_HEAD, W_BRANCH), lambda i, t: (i, 0, 0)),
        ],
        out_shape=[
            jax.ShapeDtypeStruct((b, seq_len, W_BRANCH), f32),
            jax.ShapeDtypeStruct((b, RWKV_HEAD, W_BRANCH), f32),
        ],
        scratch_shapes=[
            pltpu.VMEM((g, RWKV_HEAD, W_BRANCH), f32),
            pltpu.VMEM((g, 1, 1536), f32),
            pltpu.VMEM((g, 1, LANE), f32),
        ] + [tok_sc] * 9,
        compiler_params=_cparams(("parallel", "arbitrary")),
        name="rwkv_seq",
    )(proj3, proj3, proj3, proj3, prev_rkv, prev_lo, s_in, *wts)
    return out


def _rwkv_tok(proj2, prev_rkv, prev_lo, s_all, acc, wts, l):
    b = proj2.shape[0]
    assert b == LANE, "the single-token RWKV-7 kernel keeps exactly one lane tile of sequences"

    def col(width, off):
        return pl.BlockSpec((b, width), lambda i: (0, off // width))

    state_spec = pl.BlockSpec((None, None, RWKV_HEAD, RWKV_HEAD, b), lambda i: (l, i, 0, 0, 0))
    in_specs = [
        col(512, C_R), col(512, C_K), col(512, C_V), col(LANE, C_RWKV_LO),
        pl.BlockSpec((None, b, 1536), lambda i: (l, 0, 0)),
        pl.BlockSpec((None, b, LANE), lambda i: (l, 0, 0)),
        state_spec,
    ] + _rwkv_weight_specs(l)
    args = [proj2, proj2, proj2, proj2, prev_rkv, prev_lo, s_all, *wts]
    aliases = {}
    if acc is not None:
        in_specs.append(pl.BlockSpec(memory_space=pl.ANY))
        aliases = {len(args): 1}
        args.append(acc)
    feat_sc = pltpu.VMEM((W_BRANCH, b), f32)
    return pl.pallas_call(
        _rwkv_tok_kernel,
        grid=(RWKV_H,),
        in_specs=in_specs,
        out_specs=[pl.BlockSpec((b, W_BRANCH), lambda i: (0, 0)), state_spec],
        out_shape=[
            jax.ShapeDtypeStruct((b, W_BRANCH), f32),
            jax.ShapeDtypeStruct(s_all.shape, f32),
        ],
        scratch_shapes=[feat_sc] * 7 + [pltpu.VMEM((b, W_BRANCH), f32)],
        input_output_aliases=aliases,
        compiler_params=_cparams(("arbitrary",)),
        name="rwkv_tok",
    )(*args)


def _gla_inputs(q_ref, k_ref, v_ref, lo_ref, gup_ref, gb_ref):
    q = q_ref[...] * (GLA_DK ** -0.5)
    z = _dotf(lo_ref[...], gup_ref[...]) + gb_ref[...]
    g = _log_sigmoid(z) * (1.0 / GLA_GATE_NORM)
    return q, k_ref[...], v_ref[...], g


def _hgrn_lb(logits, l):
    m = jnp.max(logits, axis=0, keepdims=True)
    e = jnp.exp(logits - m)
    sm = e / jnp.sum(e, axis=0, keepdims=True)
    lb = jnp.zeros_like(sm[0:1, :])
    for i in range(1, l + 1):
        lb = lb + sm[i:i + 1, :]
    return lb


def _hgrn_inputs(q_ref, f_ref, i_ref, logits_ref, l):
    lb = _hgrn_lb(logits_ref[...], l)
    f_lo = f_ref[...]
    a = jnp.log(jnp.maximum(lb, LB_FLOOR))
    b = jnp.log1p(-lb) + _log_sigmoid(f_lo)
    logf = jnp.maximum(a, b) + jnp.log1p(jnp.exp(-jnp.abs(a - b)))
    k = (1.0 - lb) * _sigmoid(-f_lo)
    return q_ref[...], k, i_ref[...], logf


def _chunk_core(q, k, v, g, lt_ref, esum_ref, vmask_ref, gmask_ref, bon_ref, gn_ref, st_sc, kpad, bpad, lb,
                heads_per_group):
    bc = _ones_seg(lt_ref[...], g)
    nc = lb // CHUNK
    f = q.shape[1]
    kpad[...] = k.reshape(nc, CHUNK, f)
    bpad[...] = bc.reshape(nc, CHUNK, f)
    q3 = q.reshape(nc, CHUNK, f)
    bc3 = bc.reshape(nc, CHUNK, f)
    esum = esum_ref[...]

    lane_j = lax.broadcasted_iota(jnp.int32, (1, 1, LANE), 2) & (CHUNK - 1)
    att3 = None
    for r0 in range(0, CHUNK, 8):
        rows = CHUNK - r0
        q_s, bc_s = q3[:, r0:, :], bc3[:, r0:, :]
        rowc = lax.broadcasted_iota(jnp.int32, (1, rows, 1), 1) + r0
        acc = jnp.zeros((nc, rows, LANE), f32)
        for j in range(r0, r0 + 8):
            kj, bj = kpad[:, j:j + 1, :], bpad[:, j:j + 1, :]
            z = jnp.where(rowc >= j, q_s * kj * jnp.exp(bc_s - bj), 0.0)
            a = jnp.dot(z.reshape(nc * rows, f).astype(bf16), esum, preferred_element_type=f32)
            acc = acc + jnp.where(lane_j == j, a.reshape(nc, rows, LANE), 0.0)
        if r0:
            acc = jnp.concatenate([jnp.zeros((nc, r0, LANE), f32), acc], axis=1)
        att3 = acc if att3 is None else att3 + acc
    n_heads = LANE // CHUNK
    v3 = v.reshape(nc, CHUNK, W_BRANCH)
    vbd = jnp.concatenate([v3] * n_heads, axis=1) * vmask_ref[...]
    o = lax.dot_general(att3.astype(bf16), vbd.astype(bf16), (((2,), (1,)), ((0,), (0,))),
                        preferred_element_type=f32).reshape(lb, W_BRANCH)

    ng = f // LANE
    vw = W_BRANCH // ng
    sts = [st_sc[gi] for gi in range(ng)]
    outs = []
    for c in range(nc):
        sl = slice(c * CHUNK, (c + 1) * CHUNK)
        bcc = bc[sl]
        blast = bcc[CHUNK - 1:CHUNK, :]
        dec = jnp.exp(blast)
        qe = (q[sl] * jnp.exp(bcc)).astype(bf16)
        ke = (k[sl] * jnp.exp(blast - bcc)).astype(bf16)
        vb = v[sl].astype(bf16)
        parts = []
        for gi in range(ng):
            kl = slice(gi * LANE, (gi + 1) * LANE)
            vl = slice(gi * vw, (gi + 1) * vw)
            parts.append(_dot_nt(qe[:, kl], sts[gi].astype(bf16)))
            upd = _dot_tn(vb[:, vl], ke[:, kl])
            if heads_per_group > 1:
                upd = upd * gmask_ref[...]
            sts[gi] = sts[gi] * dec[:, kl] + upd
        outs.append(o[sl] + jnp.concatenate(parts, axis=1))
    for gi in range(ng):
        st_sc[gi] = sts[gi]
    o = jnp.concatenate(outs, axis=0)
    ms = _seg_sum(o * o, bon_ref[...], pieces=1) * (1.0 / LANE)
    return o * lax.rsqrt(ms + EPS) * gn_ref[...]


def _chunk_seq_kernel(*refs, kind, l, lb):
    n_in = 6 if kind == "gla" else 4
    ins = refs[:n_in]
    lt_ref, esum_ref, vmask_ref, gmask_ref, bon_ref, gn_ref, o_ref, stout_ref, st_sc, kpad, bpad = refs[n_in:]
    tb = pl.program_id(1)

    @pl.when(tb == 0)
    def _():
        st_sc[...] = jnp.zeros_like(st_sc)

    if kind == "gla":
        q, k, v, g = _gla_inputs(*ins)
        heads_per_group = LANE // GLA_DK
    else:
        q, k, v, g = _hgrn_inputs(*ins, l)
        heads_per_group = LANE // HGRN_DK
    o_ref[...] = _chunk_core(q, k, v, g, lt_ref, esum_ref, vmask_ref, gmask_ref, bon_ref, gn_ref, st_sc, kpad,
                             bpad, lb, heads_per_group)

    @pl.when(tb == pl.num_programs(1) - 1)
    def _():
        stout_ref[...] = st_sc[...]


def _chunk_consts(h, dk, lb):
    f = h * dk
    i = jnp.arange(lb)
    lt = ((i[:, None] // CHUNK == i[None, :] // CHUNK) & (i[:, None] >= i[None, :])).astype(bf16)
    fi = jnp.arange(f)
    li = jnp.arange(LANE)
    oi = jnp.arange(W_BRANCH)
    vw = W_BRANCH // (f // LANE)
    esum = (fi[:, None] // dk == li[None, :] // CHUNK).astype(bf16)
    vmask = (li[:, None] // CHUNK == oi[None, :] // LANE).astype(f32)
    gmask = (jnp.arange(vw)[:, None] // LANE == li[None, :] // dk).astype(f32)
    ti = jnp.arange(MXU_TILE)
    bon = (ti[:, None] // LANE == ti[None, :] // LANE).astype(bf16)
    return lt, esum, vmask, gmask, bon


def _chunk_seq(kind, proj3, extra, gn3, l, lb):
    b, seq_len, _ = proj3.shape
    h, dk = (GLA_H, GLA_DK) if kind == "gla" else (HGRN_H, HGRN_DK)
    f = h * dk
    nt = seq_len // lb

    def col(width, off):
        return pl.BlockSpec((None, lb, width), lambda i, t: (i, t, off // width))

    def const(shape):
        return pl.BlockSpec(shape, lambda *_: (0,) * len(shape))

    def layer(shape):
        return pl.BlockSpec((None,) + shape, lambda *_: (l,) + (0,) * len(shape))

    if kind == "gla":
        gup, gb = extra
        in_specs = [col(GLA_KW, C_GLA_Q), col(GLA_KW, C_GLA_K), col(W_BRANCH, C_GLA_V), col(LANE, C_GLA_LO),
                    layer((LANE, GLA_KW)), layer((1, GLA_KW))]
        args = [proj3, proj3, proj3, proj3, gup, gb]
    else:
        (logits,) = extra
        in_specs = [col(W_BRANCH, C_HQ), col(W_BRANCH, C_HF), col(W_BRANCH, C_HI), const((DEPTH, W_BRANCH))]
        args = [proj3, proj3, proj3, logits]
    consts = _chunk_consts(h, dk, lb)
    in_specs += [const(c.shape) for c in consts] + [layer((1, W_BRANCH))]
    ng = f // LANE
    st_shape = (ng, W_BRANCH // ng, LANE)
    return pl.pallas_call(
        functools.partial(_chunk_seq_kernel, kind=kind, l=l, lb=lb),
        grid=(b, nt),
        in_specs=in_specs,
        out_specs=[
            pl.BlockSpec((None, lb, W_BRANCH), lambda i, t: (i, t, 0)),
            pl.BlockSpec((None,) + st_shape, lambda i, t: (i, 0, 0, 0)),
        ],
        out_shape=[
            jax.ShapeDtypeStruct((b, seq_len, W_BRANCH), f32),
            jax.ShapeDtypeStruct((b,) + st_shape, f32),
        ],
        scratch_shapes=[
            pltpu.VMEM(st_shape, f32),
            pltpu.VMEM((lb // CHUNK, CHUNK, f), f32),
            pltpu.VMEM((lb // CHUNK, CHUNK, f), f32),
        ],
        compiler_params=_cparams(("parallel", "arbitrary")),
        name=kind + "_seq",
    )(*args, *consts, gn3)


def _state_from_transposed(st, h, dk):
    b, ng = st.shape[0], st.shape[1]
    hpg = h // ng
    st = st.reshape(b, ng, hpg, LANE, hpg, dk)
    diag = jnp.stack([st[:, :, i, :, i, :] for i in range(hpg)], axis=2)
    return jnp.swapaxes(diag.reshape(b, h, LANE, dk), 2, 3)


def _key_columns(x, g, dk, odd):
    xt = jnp.concatenate([x, jnp.zeros((LANE - g, LANE), f32)], axis=0).T
    if dk == LANE:
        return xt
    return jnp.where(odd, xt[dk:2 * dk], xt[0:dk])


def _chunk_tok_kernel(*refs, kind, l, g, has_acc):
    ins, (o_ref, sout_ref) = refs[:-2], refs[-2:]
    if has_acc:
        ins = ins[:-1]
    odd = (pl.program_id(1) % 2) == 1
    if kind == "gla":
        q_ref, k_ref, v_ref, lo_ref, gup_ref, gb_ref, s_ref, gn_ref = ins
        q = q_ref[...] * (GLA_DK ** -0.5)
        k = k_ref[...]
        z = _dotf(lo_ref[...], gup_ref[...]) + gb_ref[...]
        dec = jnp.exp(_log_sigmoid(z) * (1.0 / GLA_GATE_NORM))
        v = v_ref[...]
        dk = GLA_DK
    else:
        q_ref, f_ref, i_ref, logits_ref, s_ref, gn_ref = ins
        q, k, v, logf = _hgrn_inputs(q_ref, f_ref, i_ref, logits_ref, l)
        dec = jnp.exp(logf)
        dk = HGRN_DK
    qt, kt, dt = (_key_columns(x, g, dk, odd) for x in (q, k, dec))
    o_rows = []
    for b in range(g):
        qcol, kcol, dcol = (jnp.broadcast_to(t[:, b:b + 1], (dk, LANE)) for t in (qt, kt, dt))
        s_new = s_ref[b] * dcol + kcol * v[b:b + 1, :]
        sout_ref[b] = s_new
        o_rows.append(jnp.sum(qcol * s_new, axis=0, keepdims=True))
    o = jnp.concatenate(o_rows, axis=0)
    ms = jnp.mean(o * o, axis=-1, keepdims=True)
    o_ref[...] = o * lax.rsqrt(ms + EPS) * gn_ref[...]


def _chunk_tok(kind, proj2, extra, gn3, s_all, acc, l, g):
    b = proj2.shape[0]
    h, dk = (GLA_H, GLA_DK) if kind == "gla" else (HGRN_H, HGRN_DK)

    def head_col(off):
        per = LANE // dk
        return pl.BlockSpec((g, LANE), lambda i, j: (i, off // LANE + j // per))

    if kind == "gla":
        gup, gb = extra
        in_specs = [head_col(C_GLA_Q), head_col(C_GLA_K),
                    pl.BlockSpec((g, LANE), lambda i, j: (i, C_GLA_V // LANE + j)),
                    pl.BlockSpec((g, LANE), lambda i, j: (i, C_GLA_LO // LANE)),
                    pl.BlockSpec((None, LANE, LANE), lambda i, j: (l, 0, j // 2)),
                    pl.BlockSpec((None, 1, LANE), lambda i, j: (l, 0, j // 2))]
        args = [proj2, proj2, proj2, proj2, gup, gb]
    else:
        (logits,) = extra
        in_specs = [head_col(C_HQ), head_col(C_HF), head_col(C_HI),
                    pl.BlockSpec((DEPTH, LANE), lambda i, j: (0, j))]
        args = [proj2, proj2, proj2, logits]
    state_spec = pl.BlockSpec((None, g, None, dk, LANE), lambda i, j: (l, i, j, 0, 0))
    in_specs += [state_spec, pl.BlockSpec((None, 1, LANE), lambda i, j: (l, 0, j))]
    args += [s_all, gn3]
    aliases = {}
    if acc is not None:
        in_specs.append(pl.BlockSpec(memory_space=pl.ANY))
        aliases = {len(args): 1}
        args.append(acc)
    return pl.pallas_call(
        functools.partial(_chunk_tok_kernel, kind=kind, l=l, g=g, has_acc=acc is not None),
        grid=(b // g, h),
        in_specs=in_specs,
        out_specs=[pl.BlockSpec((g, LANE), lambda i, j: (i, j)), state_spec],
        out_shape=[
            jax.ShapeDtypeStruct((b, W_BRANCH), f32),
            jax.ShapeDtypeStruct(s_all.shape, f32),
        ],
        input_output_aliases=aliases,
        compiler_params=_cparams(("parallel", "parallel")),
        name=kind + "_tok",
    )(*args)


def _lru_gate_dot(y_hi, y_lo, w_ref, col0):
    d = functools.partial(jnp.dot, preferred_element_type=f32)
    halves = []
    for h0 in range(0, W_BRANCH, MXU_TILE):
        w_hi = w_ref[0, h0:h0 + MXU_TILE, col0 + h0:col0 + h0 + MXU_TILE]
        w_lo = w_ref[1, h0:h0 + MXU_TILE, col0 + h0:col0 + h0 + MXU_TILE]
        a, b = y_hi[:, h0:h0 + MXU_TILE], y_lo[:, h0:h0 + MXU_TILE]
        halves.append(d(a, w_hi) + d(a, w_lo) + d(b, w_hi))
    return jnp.concatenate(halves, axis=1)


def _lru_gates(y, w_ref, pvec):
    b_a, b_x, lam = pvec[1:2, :], pvec[2:3, :], pvec[3:4, :]
    y_hi, y_lo = _split(y, 2)
    r = _sigmoid(_lru_gate_dot(y_hi, y_lo, w_ref, 0) + b_a)
    ig = _sigmoid(_lru_gate_dot(y_hi, y_lo, w_ref, W_BRANCH) + b_x)
    log_a = -LRU_C * r * _softplus(-lam)
    a = jnp.exp(log_a)
    one_m_a2 = -jnp.tanh(log_a) * (jnp.exp(2.0 * log_a) + 1.0)
    b = jnp.sqrt(one_m_a2) * (ig * y)
    return a, b


def _lru_seq_kernel(x_ref, cw_ref, pvec_ref, wbd_ref, o_ref, xpad, hcar, *, lb):
    tb = pl.program_id(1)

    @pl.when(tb == 0)
    def _():
        xpad[0:8, :] = jnp.zeros((8, W_BRANCH), f32)
        hcar[...] = jnp.zeros_like(hcar)

    x = x_ref[...]
    xpad[8:8 + lb, :] = x
    cw = cw_ref[...]
    pvec = pvec_ref[...]
    y = pvec[0:1, :] + x * cw[3:4, :]
    for j in range(CONV_W - 1):
        y = y + xpad[5 + j:5 + j + lb, :] * cw[j:j + 1, :]
    a, b = _lru_gates(y, wbd_ref, pvec)
    row = lax.broadcasted_iota(jnp.int32, (lb, 1), 0)
    s = 1
    while s < lb:
        m = row >= s
        b = jnp.where(m, a * pltpu.roll(b, s, 0) + b, b)
        a = jnp.where(m, a * pltpu.roll(a, s, 0), a)
        s *= 2
    h = a * hcar[...] + b
    o_ref[...] = h
    hcar[...] = h[lb - 1:lb, :]
    xpad[5:8, :] = x[lb - 3:lb, :]


def _lru_seq(proj3, cw, pvec, wbd, l, lb):
    b, seq_len, _ = proj3.shape

    def layer(shape):
        return pl.BlockSpec((None,) + shape, lambda *_: (l,) + (0,) * len(shape))

    return pl.pallas_call(
        functools.partial(_lru_seq_kernel, lb=lb),
        grid=(b, seq_len // lb),
        in_specs=[
            pl.BlockSpec((None, lb, W_BRANCH), lambda i, t: (i, t, C_LRU // W_BRANCH)),
            layer((CONV_W, W_BRANCH)), layer((8, W_BRANCH)), layer((2, W_BRANCH, 2 * W_BRANCH)),
        ],
        out_specs=pl.BlockSpec((None, lb, W_BRANCH), lambda i, t: (i, t, 0)),
        out_shape=jax.ShapeDtypeStruct((b, seq_len, W_BRANCH), f32),
        scratch_shapes=[
            pltpu.VMEM((8 + lb, W_BRANCH), f32),
            pltpu.VMEM((1, W_BRANCH), f32),
        ],
        compiler_params=_cparams(("parallel", "arbitrary")),
        name="lru_seq",
    )(proj3, cw, pvec, wbd)


def _lru_tok_kernel(x_ref, b0_ref, b1_ref, b2_ref, h0_ref, cw_ref, pvec_ref, wbd_ref, o_ref):
    cw = cw_ref[...]
    pvec = pvec_ref[...]
    y = (pvec[0:1, :] + b0_ref[...] * cw[0:1, :] + b1_ref[...] * cw[1:2, :] + b2_ref[...] * cw[2:3, :]
         + x_ref[...] * cw[3:4, :])
    a, b = _lru_gates(y, wbd_ref, pvec)
    o_ref[...] = a * h0_ref[...] + b


def _lru_tok(proj2, bufs, h0, cw, pvec, wbd, l):
    b = proj2.shape[0]

    def layer(shape):
        return pl.BlockSpec((None,) + shape, lambda *_: (l,) + (0,) * len(shape))

    row = pl.BlockSpec((b, W_BRANCH), lambda i: (0, 0))
    return pl.pallas_call(
        _lru_tok_kernel,
        grid=(1,),
        in_specs=[pl.BlockSpec((b, W_BRANCH), lambda i: (0, C_LRU // W_BRANCH)), row, row, row, row,
                  layer((CONV_W, W_BRANCH)), layer((8, W_BRANCH)), layer((2, W_BRANCH, 2 * W_BRANCH))],
        out_specs=row,
        out_shape=jax.ShapeDtypeStruct((b, W_BRANCH), f32),
        compiler_params=_cparams(("arbitrary",)),
        name="lru_tok",
    )(proj2, *bufs, h0, cw, pvec, wbd)


def _prep_weights(p):
    w = jnp.swapaxes(p["w_in"], 1, 2)
    row_groups = [(0, 1536), (2176, 2688), (2704, 6800), (1664, 2176), (1536, 1664), (2688, 2704)]
    pad = jnp.zeros((DEPTH, LANE - GLA_GATE_RANK, D_MODEL), bf16)
    w_in_p = jnp.concatenate([w[:, a:b].astype(bf16) for a, b in row_groups] + [pad], axis=1)
    zl = jnp.zeros((DEPTH, RWKV_LORA, W_BRANCH), f32)
    wup = jnp.concatenate([jnp.concatenate([p["rwkv_w_up"], zl], axis=2),
                           jnp.concatenate([zl, p["rwkv_a_up"]], axis=2)], axis=1)
    wup_hi = wup.astype(bf16)
    wup = jnp.stack([wup_hi, (wup - wup_hi.astype(f32)).astype(bf16)], axis=1)
    zrow = jnp.zeros((DEPTH, W_BRANCH), f32)
    rwkv_vec = jnp.stack([p["rwkv_w0"], p["rwkv_a0"], p["rwkv_k_k"], p["rwkv_k_a"],
                          p["rwkv_r_k"].reshape(DEPTH, W_BRANCH), p["rwkv_gn_g"], p["rwkv_gn_b"], zrow], axis=1)
    mu = p["rwkv_mu"]
    gup = jnp.concatenate([p["gla_gk_up"], jnp.zeros((DEPTH, LANE - GLA_GATE_RANK, GLA_KW), f32)], axis=1)
    eye = jnp.eye(LRU_BLOCKS, dtype=f32)

    def bd(wb):
        return jnp.einsum("lhij,hg->lhigj", wb, eye).reshape(DEPTH, W_BRANCH, W_BRANCH)

    lru_vec = jnp.stack([p["lru_conv_b"], p["lru_b_a"], p["lru_b_x"], p["lru_lambda"],
                         zrow, zrow, zrow, zrow], axis=1)
    lru_w = jnp.concatenate([bd(p["lru_w_a"]), bd(p["lru_w_x"])], axis=2)
    lru_w_hi = lru_w.astype(bf16)
    lru_w_lo = (lru_w - lru_w_hi.astype(f32)).astype(bf16)
    return dict(
        w_in_p=w_in_p,
        w_out_b=p["w_out"].astype(bf16),
        norm_g3=p["norm_g"].reshape(DEPTH, 1, D_MODEL),
        final_g2=p["final_g"].reshape(1, D_MODEL),
        rwkv=(mu[:, None, 0:1536], mu[:, None, 1536:1664], wup, rwkv_vec) + _rwkv_consts(),
        gla=(gup, p["gla_gk_b"].reshape(DEPTH, 1, GLA_KW)),
        gla_gn=p["gla_gn_g"].reshape(DEPTH, 1, W_BRANCH),
        hgrn=(p["hgrn_lb_logits"],),
        hgrn_gn=p["hgrn_gn_g"].reshape(DEPTH, 1, W_BRANCH),
        lru=(p["lru_conv_w"], lru_vec, jnp.stack([lru_w_hi, lru_w_lo], axis=1)),
    )


def _shift_state(proj_last):
    return jnp.concatenate([proj_last[..., 0:1536], proj_last[..., C_RWKV_LO:C_RWKV_LO + LANE]], axis=-1)


def _rwkv_state_in(s):
    lead = s.shape[:-3]
    return jnp.swapaxes(s, -3, -2).reshape(lead + (RWKV_HEAD, W_BRANCH))


def _rwkv_state_out(s):
    lead = s.shape[:-2]
    return jnp.swapaxes(s.reshape(lead + (RWKV_HEAD, RWKV_H, RWKV_HEAD)), -3, -2)


def _trunk_seq(x, mod, wts):
    b, seq_len, _ = x.shape
    t = b * seq_len
    lb = min(LB_CHUNK, seq_len)
    tm = min(TM_IN, seq_len)
    per_seq = seq_len // tm
    mod3 = mod.reshape(DEPTH * b * 3, 1, D_MODEL)
    x2 = x.reshape(t, D_MODEL)
    zeros = functools.partial(jnp.zeros, dtype=f32)
    new = []
    for l in range(DEPTH):
        mod_l = mod3[l * b * 3:(l + 1) * b * 3]
        proj2 = _inproj(x2, wts["norm_g3"], mod_l, wts["w_in_p"], l, per_seq, tm)
        proj3 = proj2.reshape(b, seq_len, PROJ_P)
        o_a, s_wkv = _rwkv_seq(proj3, zeros((b, 1, 1536)), zeros((b, 1, LANE)),
                               zeros((b, RWKV_HEAD, W_BRANCH)), wts["rwkv"], l, min(LB_RWKV, seq_len))
        o_b, st_gla = _chunk_seq("gla", proj3, wts["gla"], wts["gla_gn"], l, lb)
        o_c, st_hgrn = _chunk_seq("hgrn", proj3, wts["hgrn"], wts["hgrn_gn"], l, lb)
        o_d = _lru_seq(proj3, *wts["lru"], l, lb)
        outs = [o.reshape(t, W_BRANCH) for o in (o_a, o_b, o_c, o_d)]
        tmo = min(TM_OUT, seq_len)
        x2 = _outproj(outs, proj2, x2, mod_l, wts["w_out_b"], wts["final_g2"], l, seq_len // tmo, tmo,
                      final=(l == DEPTH - 1))
        last = proj3[:, seq_len - 1]
        new.append((
            _shift_state(last),
            _rwkv_state_out(s_wkv),
            _state_from_transposed(st_gla, GLA_H, GLA_DK),
            _state_from_transposed(st_hgrn, HGRN_H, HGRN_DK),
            proj3[:, seq_len - (CONV_W - 1):, C_LRU:C_LRU + W_BRANCH],
            o_d[:, seq_len - 1],
        ))
    return x2.reshape(b, seq_len, D_MODEL), tuple(jnp.stack([n[i] for n in new], axis=0) for i in range(6))


def _trunk_tok(x, mod, states, wts):
    b = x.shape[0]
    s_shift, s_wkv, s_gla, s_hgrn, s_conv, s_h = states
    x2 = x.reshape(b, D_MODEL)
    s_wkv_t = jnp.transpose(s_wkv, (0, 2, 3, 4, 1))
    prev_rkv, prev_lo = s_shift[:, :, 0:1536], s_shift[:, :, 1536:1664]
    n_wkv = n_gla = n_hgrn = None
    new = []
    for l in range(DEPTH):
        proj2 = _inproj(x2, wts["norm_g3"], mod, wts["w_in_p"], l, None, b)
        o_a, n_wkv = _rwkv_tok(proj2, prev_rkv, prev_lo, s_wkv_t, n_wkv, wts["rwkv"], l)
        o_b, n_gla = _chunk_tok("gla", proj2, wts["gla"], wts["gla_gn"], s_gla, n_gla, l, TOK_GROUP)
        o_c, n_hgrn = _chunk_tok("hgrn", proj2, wts["hgrn"], wts["hgrn_gn"], s_hgrn, n_hgrn, l, TOK_GROUP)
        conv = s_conv[l]
        o_d = _lru_tok(proj2, [conv[:, 0], conv[:, 1], conv[:, 2]], s_h[l], *wts["lru"], l)
        x2 = _outproj([o_a, o_b, o_c, o_d], proj2, x2, mod, wts["w_out_b"], wts["final_g2"], l, None, b,
                      final=(l == DEPTH - 1))
        n_conv = jnp.concatenate([conv[:, 1:], proj2[:, None, C_LRU:C_LRU + W_BRANCH]], axis=1)
        new.append((_shift_state(proj2), n_conv, o_d))
    n_shift, n_conv, n_h = (jnp.stack([n[i] for n in new], axis=0) for i in range(3))
    n_wkv = jnp.transpose(n_wkv, (0, 4, 1, 2, 3))
    return x2.reshape(b, 1, D_MODEL), (n_shift, n_wkv, n_gla, n_hgrn, n_conv, n_h)


def kernel(x_prompt, x_sample, c_prompt, c_sample, state_rwkv_shift, state_rwkv_wkv, state_gla, state_hgrn, state_lru_conv, state_lru_h, norm_g, w_ada, b_ada, w_in, w_out, rwkv_mu, rwkv_w0, rwkv_w_up, rwkv_a0, rwkv_a_up, rwkv_k_k, rwkv_k_a, rwkv_r_k, rwkv_gn_g, rwkv_gn_b, gla_gk_up, gla_gk_b, gla_gn_g, hgrn_lb_logits, hgrn_gn_g, lru_conv_w, lru_conv_b, lru_w_a, lru_b_a, lru_w_x, lru_b_x, lru_lambda, final_g):
    p = dict(norm_g=norm_g, w_in=w_in, w_out=w_out, rwkv_mu=rwkv_mu, rwkv_w0=rwkv_w0, rwkv_w_up=rwkv_w_up,
             rwkv_a0=rwkv_a0, rwkv_a_up=rwkv_a_up, rwkv_k_k=rwkv_k_k, rwkv_k_a=rwkv_k_a, rwkv_r_k=rwkv_r_k,
             rwkv_gn_g=rwkv_gn_g, rwkv_gn_b=rwkv_gn_b, gla_gk_up=gla_gk_up, gla_gk_b=gla_gk_b,
             gla_gn_g=gla_gn_g, hgrn_lb_logits=hgrn_lb_logits, hgrn_gn_g=hgrn_gn_g, lru_conv_w=lru_conv_w,
             lru_conv_b=lru_conv_b, lru_w_a=lru_w_a, lru_b_a=lru_b_a, lru_w_x=lru_w_x, lru_b_x=lru_b_x,
             lru_lambda=lru_lambda, final_g=final_g)
    wts = _prep_weights(p)
    bp = x_prompt.shape[0]
    bs = x_sample.shape[0]
    pad_rows = (-bp) % 8
    c_all = jnp.concatenate([c_prompt, jnp.zeros((pad_rows, D_MODEL), f32), c_sample], axis=0)
    mod = _ada(c_all, w_ada, b_ada)
    mod_p = mod[:, 0:bp]
    mod_s = mod[:, bp + pad_rows:bp + pad_rows + bs]
    y_p, st_p = _trunk_seq(x_prompt, mod_p, wts)
    states = (state_rwkv_shift, state_rwkv_wkv, state_gla, state_hgrn, state_lru_conv, state_lru_h)
    y_s, st_s = _trunk_tok(x_sample, mod_s, states, wts)
    return (y_p, y_s) + st_p + st_s
```

```python
import functools

import jax
import jax.numpy as jnp
from jax import lax
from jax.experimental import pallas as pl
from jax.experimental.pallas import tpu as pltpu

f32 = jnp.float32
bf16 = jnp.bfloat16

D_MODEL = 2048
DEPTH = 4
W_BRANCH = 512
EPS = 1e-6

RWKV_H = 8
RWKV_HEAD = 64
RWKV_LORA = 64
RWKV_PROJ = 3 * W_BRANCH + 2 * RWKV_LORA
RWKV_GN_EPS = 64e-5

GLA_H = 4
GLA_DK = 64
GLA_DV = 128
GLA_KW = GLA_H * GLA_DK
GLA_GATE_RANK = 16
GLA_GATE_NORM = 16.0

HGRN_H = 4
HGRN_DK = 128
HGRN_DV = 128
LB_FLOOR = 1e-30

LRU_BLOCKS = 8
LRU_BLOCK = 64
LRU_C = 8.0
CONV_W = 4

CHUNK = 32

C_R, C_K, C_V = 0, 512, 1024
C_GLA_V = 1536
C_HQ, C_HF, C_HI = 2048, 2560, 3072
C_LRU = 3584
C_GATE = 4096
C_GLA_Q, C_GLA_K = 6144, 6400
C_RWKV_LO = 6656
C_GLA_LO = 6784
PROJ_P = 6912

LANE = 128
MXU_TILE = 256
VMEM_LIMIT = 56 * 1024 * 1024

TM_IN = 1024
TN_IN = 1152
TM_OUT = 256
LB_CHUNK = 256
LB_RWKV = 128
RWKV_GROUP = 8
TOK_GROUP = 16

HIGHEST = lax.Precision.HIGHEST


def _cparams(sem):
    return pltpu.CompilerParams(dimension_semantics=sem, vmem_limit_bytes=VMEM_LIMIT)


def _dotf(a, b):
    return jnp.dot(a, b, preferred_element_type=f32, precision=HIGHEST)


def _prec(a):
    return HIGHEST if a.dtype == f32 else None


def _dot_nt(a, b):
    return lax.dot_general(a, b, (((1,), (1,)), ((), ())), preferred_element_type=f32, precision=_prec(a))


def _dot_tn(a, b):
    return lax.dot_general(a, b, (((0,), (0,)), ((), ())), preferred_element_type=f32, precision=_prec(a))


def _split(x, pieces):
    out = []
    for i in range(pieces):
        part = x.astype(bf16)
        out.append(part)
        if i + 1 < pieces:
            x = x - part.astype(f32)
    return out


def _half_dot(x, ones):
    d = functools.partial(jnp.dot, preferred_element_type=f32)
    kh = ones.shape[0]
    if x.shape[1] == kh:
        return d(x, ones)
    return jnp.concatenate([d(x[:, 0:kh], ones), d(x[:, kh:2 * kh], ones)], axis=1)


def _seg_sum(x, ones, pieces=2):
    return sum(_half_dot(part, ones) for part in _split(x, pieces))


def _ones_seg(ones, x, pieces=2):
    d = functools.partial(jnp.dot, preferred_element_type=f32)
    return sum(d(ones, part) for part in _split(x, pieces))


def _sigmoid(x):
    return jax.nn.sigmoid(x)


def _softplus(x):
    return jnp.maximum(x, 0.0) + jnp.log1p(jnp.exp(-jnp.abs(x)))


def _log_sigmoid(x):
    return -_softplus(-x)


def _ada_kernel(c_ref, w_ref, b_ref, o_ref):
    c = c_ref[...]
    s = (c * _sigmoid(c)).astype(bf16)
    o_ref[...] = jnp.dot(s, w_ref[...].astype(bf16), preferred_element_type=f32) + b_ref[...]


def _ada(c_all, w_ada, b_ada):
    rows = c_all.shape[0]
    tn = 512
    n = w_ada.shape[2]
    return pl.pallas_call(
        _ada_kernel,
        grid=(DEPTH, n // tn),
        in_specs=[
            pl.BlockSpec((rows, D_MODEL), lambda l, j: (0, 0)),
            pl.BlockSpec((None, D_MODEL, tn), lambda l, j: (l, 0, j)),
            pl.BlockSpec((None, 1, tn), lambda l, j: (l, 0, j)),
        ],
        out_specs=pl.BlockSpec((None, rows, tn), lambda l, j: (l, 0, j)),
        out_shape=jax.ShapeDtypeStruct((DEPTH, rows, n), f32),
        compiler_params=_cparams(("parallel", "parallel")),
        name="ada_mod",
    )(c_all, w_ada, b_ada.reshape(DEPTH, 1, n))


def _inproj_kernel(x_ref, g_ref, sc_ref, sh_ref, w_ref, o_ref, h_ref):
    @pl.when(pl.program_id(1) == 0)
    def _():
        x = x_ref[...]
        ms = jnp.mean(x * x, axis=-1, keepdims=True)
        h = x * lax.rsqrt(ms + EPS) * g_ref[...]
        h = h * (1.0 + sc_ref[...]) + sh_ref[...]
        h_ref[...] = h.astype(bf16)

    o_ref[...] = _dot_nt(h_ref[...], w_ref[...])


def _mod_spec(l, which, per_seq_tiles, tm):
    if per_seq_tiles is None:
        return pl.BlockSpec((None, tm, D_MODEL), lambda i, *_: (l, 0, which))
    return pl.BlockSpec((None, 1, D_MODEL), lambda i, *_: (i // per_seq_tiles * 3 + which, 0, 0))


def _inproj(x2, norm_g3, mod, w_in_p, l, per_seq_tiles, tm):
    t = x2.shape[0]
    tn = TN_IN
    return pl.pallas_call(
        _inproj_kernel,
        grid=(t // tm, PROJ_P // tn),
        in_specs=[
            pl.BlockSpec((tm, D_MODEL), lambda i, j: (i, 0)),
            pl.BlockSpec((None, 1, D_MODEL), lambda i, j: (l, 0, 0)),
            _mod_spec(l, 1, per_seq_tiles, tm),
            _mod_spec(l, 0, per_seq_tiles, tm),
            pl.BlockSpec((None, tn, D_MODEL), lambda i, j: (l, j, 0)),
        ],
        out_specs=pl.BlockSpec((tm, tn), lambda i, j: (i, j)),
        out_shape=jax.ShapeDtypeStruct((t, PROJ_P), f32),
        scratch_shapes=[pltpu.VMEM((tm, D_MODEL), bf16)],
        compiler_params=_cparams(("parallel", "arbitrary")),
        name="in_proj",
    )(x2, norm_g3, mod, mod, w_in_p)


def _outproj_kernel(oa_ref, ob_ref, oc_ref, od_ref, pz_ref, x_ref, gate_ref, w_ref, fg_ref, o_ref, *, final):
    z = pz_ref[...]
    o = jnp.concatenate([oa_ref[...], ob_ref[...], oc_ref[...], od_ref[...]], axis=1)
    o = o * (z * _sigmoid(z))
    y = jnp.dot(o.astype(bf16), w_ref[...], preferred_element_type=f32)
    xn = x_ref[...] + gate_ref[...] * y
    if final:
        ms = jnp.mean(xn * xn, axis=-1, keepdims=True)
        xn = xn * lax.rsqrt(ms + EPS) * fg_ref[...]
    o_ref[...] = xn


def _outproj(outs, proj2, x2, mod, w_out_b, final_g2, l, per_seq_tiles, tm, final):
    t = x2.shape[0]
    mix_spec = pl.BlockSpec((tm, W_BRANCH), lambda i: (i, 0))
    return pl.pallas_call(
        functools.partial(_outproj_kernel, final=final),
        grid=(t // tm,),
        in_specs=[
            mix_spec, mix_spec, mix_spec, mix_spec,
            pl.BlockSpec((tm, D_MODEL), lambda i: (i, C_GATE // D_MODEL)),
            pl.BlockSpec((tm, D_MODEL), lambda i: (i, 0)),
            _mod_spec(l, 2, per_seq_tiles, tm),
            pl.BlockSpec((None, D_MODEL, D_MODEL), lambda i: (l, 0, 0)),
            pl.BlockSpec((1, D_MODEL), lambda i: (0, 0)),
        ],
        out_specs=pl.BlockSpec((tm, D_MODEL), lambda i: (i, 0)),
        out_shape=jax.ShapeDtypeStruct((t, D_MODEL), f32),
        compiler_params=_cparams(("parallel",)),
        name="out_proj",
    )(*outs, proj2, x2, mod, w_out_b, final_g2)


def _rwkv_prologue(r, k, v, lo, pr, pk, pv, plo, mu_rkv, mu_lo, wup, pvec, bo, sub_block=0):
    w0, a0, k_k, k_a, r_k = (pvec[i:i + 1, :] for i in range(5))
    xr = r + (pr - r) * mu_rkv[:, 0:512]
    xk = k + (pk - k) * mu_rkv[:, 512:1024]
    xv = v + (pv - v) * mu_rkv[:, 1024:1536]
    xlo = lo + (plo - lo) * mu_lo
    lane = lax.broadcasted_iota(jnp.int32, xlo.shape, 1)
    act = jnp.where(lane < RWKV_LORA, jnp.tanh(xlo), xlo)
    act_hi, act_lo = _split(act, 2)
    d = functools.partial(jnp.dot, preferred_element_type=f32)
    up = d(act_hi, wup[0]) + d(act_hi, wup[1]) + d(act_lo, wup[0])
    w_raw = -_softplus(-(w0 + up[:, 0:512])) - 0.5
    logw = -jnp.exp(w_raw)
    a = _sigmoid(a0 + up[:, 512:1024])
    kk = xk * k_k
    kk = kk / jnp.maximum(jnp.sqrt(_seg_sum(kk * kk, bo)), 1e-12)
    kh = xk * (1.0 + (a - 1.0) * k_a)
    alp = kk * a
    ar = _seg_sum(alp * xr, bo, pieces=1)
    out = dict(vv=xv, xr=xr, kr=_seg_sum(kh * xr, bo, pieces=1),
               bonus=_seg_sum(xr * kh * r_k, bo, pieces=1) * xv)
    if not sub_block:
        ew = jnp.exp(logw)
        out.update(kap=kk, alp=alp, kh=kh, ew=ew, wr=ew * xr - ar * kk)
        return out
    rowi = lax.broadcasted_iota(jnp.int32, (logw.shape[0], 1), 0) & (sub_block - 1)
    lc = logw
    step = 1
    while step < sub_block:
        lc = lc + jnp.where(rowi >= step, pltpu.roll(lc, step, 0), 0.0)
        step *= 2
    gam, ginv = jnp.exp(lc), jnp.exp(-lc)
    kap = kk * jnp.exp(lc - logw)
    out.update(kap=kap, alp=alp * ginv, kh=kh * ginv, gam=gam, wr=gam * xr - ar * kap)
    return out


_RWKV_STEP_KEYS = ("kap", "alp", "kh", "vv", "wr", "kr")


def _rwkv_step(s, kap, alp, kh, vv, wr, kr, bo, idt):
    s_new, o_row = [], []
    for c0 in range(0, W_BRANCH, MXU_TILE):
        sl = slice(c0, c0 + MXU_TILE)
        s_h, idt_h, v_h = s[:, sl], idt[:, sl], vv[:, sl]
        s_b = s_h.astype(bf16)
        x = jnp.concatenate([s_b * kap[:, sl].astype(bf16), s_b * wr[:, sl].astype(bf16),
                             idt_h.astype(bf16) * v_h.astype(bf16)], axis=0)
        red = jnp.dot(x, bo, preferred_element_type=f32)
        s_new.append(s_h - red[0:64] * alp[:, sl] + red[128:192] * kh[:, sl])
        o_row.append(jnp.sum(red[64:128] * idt_h, axis=0, keepdims=True) + v_h * kr[:, sl])
    return jnp.concatenate(s_new, axis=1), jnp.concatenate(o_row, axis=1)


def _rwkv_epilogue(o, bonus, pvec, bo):
    gn_g, gn_b = pvec[5:6, :], pvec[6:7, :]
    mu = _seg_sum(o, bo, pieces=1) * (1.0 / RWKV_HEAD)
    d = o - mu
    var = _seg_sum(d * d, bo, pieces=1) * (1.0 / RWKV_HEAD)
    return d * lax.rsqrt(var + RWKV_GN_EPS) * gn_g + gn_b + bonus


def _rwkv_seq_kernel(r_ref, k_ref, v_ref, lo_ref, prkv_ref, plo_ref, sin_ref, mu_rkv_ref, mu_lo_ref, wup_ref,
                     pvec_ref, bo_ref, idt_ref, o_ref, sout_ref,
                     s_sc, crkv_sc, clo_sc, kap_sc, gam_sc, alp_sc, kh_sc, vv_sc, wr_sc, kr_sc, oraw_sc,
                     bonus_sc, *, g_seqs, lb):
    tb = pl.program_id(1)

    @pl.when(tb == 0)
    def _():
        s_sc[...] = sin_ref[...]
        crkv_sc[...] = prkv_ref[...]
        clo_sc[...] = plo_ref[...]

    bo = bo_ref[...]
    idt = idt_ref[...]
    pvec = pvec_ref[...]
    step_sc = dict(kap=kap_sc, alp=alp_sc, kh=kh_sc, vv=vv_sc, wr=wr_sc, kr=kr_sc)

    row0 = lax.broadcasted_iota(jnp.int32, (lb, 1), 0) == 0
    for g in range(g_seqs):
        cur = [r_ref[g], k_ref[g], v_ref[g]]
        lo = lo_ref[g]
        carry = crkv_sc[g]
        prev = [jnp.where(row0, carry[:, i * 512:(i + 1) * 512], pltpu.roll(c, 1, 0)) for i, c in enumerate(cur)]
        plo = jnp.where(row0, clo_sc[g], pltpu.roll(lo, 1, 0))
        res = _rwkv_prologue(cur[0], cur[1], cur[2], lo, prev[0], prev[1], prev[2], plo,
                             mu_rkv_ref[...], mu_lo_ref[...], wup_ref[...], pvec, bo, sub_block=RWKV_GROUP)
        for key in _RWKV_STEP_KEYS:
            step_sc[key][g] = res[key]
        gam_sc[g] = res["gam"]
        bonus_sc[g] = res["bonus"]
        for i, c in enumerate(cur):
            crkv_sc[g, :, i * 512:(i + 1) * 512] = c[lb - 1:lb, :]
        clo_sc[g] = lo[lb - 1:lb, :]

    def group(i, carry):
        t0 = pl.multiple_of(i * RWKV_GROUP, RWKV_GROUP)
        states = [s_sc[g] for g in range(g_seqs)]
        for j in range(RWKV_GROUP):
            for g in range(g_seqs):
                rows = [step_sc[key][g, pl.ds(t0 + j, 1), :] for key in _RWKV_STEP_KEYS]
                states[g], o_row = _rwkv_step(states[g], *rows, bo, idt)
                oraw_sc[g, pl.ds(t0 + j, 1), :] = o_row
        for g in range(g_seqs):
            s_sc[g] = states[g] * gam_sc[g, pl.ds(t0 + RWKV_GROUP - 1, 1), :]
        return carry

    lax.fori_loop(0, lb // RWKV_GROUP, group, 0)

    for g in range(g_seqs):
        o_ref[g] = _rwkv_epilogue(oraw_sc[g], bonus_sc[g], pvec, bo)

    @pl.when(tb == pl.num_programs(1) - 1)
    def _():
        sout_ref[...] = s_sc[...]


_RWKV_TOK_KEYS = ("kap", "ew", "alp", "kh", "vv", "xr")


def _rwkv_tok_kernel(r_ref, k_ref, v_ref, lo_ref, prkv_ref, plo_ref, sin_ref, mu_rkv_ref, mu_lo_ref, wup_ref,
                     pvec_ref, bo_ref, idt_ref, *rest):
    del idt_ref
    o_ref, sout_ref = rest[-10:-8]
    kap_t, ew_t, alp_t, kh_t, v_t, r_t, o_t, bonus_sc = rest[-8:]
    h = pl.program_id(0)
    bo = bo_ref[...]
    pvec = pvec_ref[...]

    @pl.when(h == 0)
    def _():
        prkv = prkv_ref[...]
        res = _rwkv_prologue(r_ref[...], k_ref[...], v_ref[...], lo_ref[...],
                             prkv[:, 0:512], prkv[:, 512:1024], prkv[:, 1024:1536], plo_ref[...],
                             mu_rkv_ref[...], mu_lo_ref[...], wup_ref[...], pvec, bo)
        for key, dst in zip(_RWKV_TOK_KEYS, (kap_t, ew_t, alp_t, kh_t, v_t, r_t)):
            dst[...] = res[key].T
        bonus_sc[...] = res["bonus"]

    row0 = pl.multiple_of(h * RWKV_HEAD, RWKV_HEAD)
    keys = pl.ds(row0, RWKV_HEAD)
    kap, ew, alp, kh, rr = kap_t[keys, :], ew_t[keys, :], alp_t[keys, :], kh_t[keys, :], r_t[keys, :]

    def value_row(v, carry):
        s = sin_ref[v]
        sk = jnp.sum(s * kap, axis=0, keepdims=True)
        s_new = s * ew - sk * alp + v_t[pl.ds(row0 + v, 1), :] * kh
        sout_ref[v] = s_new
        o_t[pl.ds(row0 + v, 1), :] = jnp.sum(s_new * rr, axis=0, keepdims=True)
        return carry

    lax.fori_loop(0, RWKV_HEAD, value_row, 0, unroll=4)

    @pl.when(h == pl.num_programs(0) - 1)
    def _():
        o_ref[...] = _rwkv_epilogue(o_t[...].T, bonus_sc[...], pvec, bo)


def _rwkv_consts():
    i = jnp.arange(W_BRANCH)
    j = jnp.arange(MXU_TILE)
    bo = (j[:, None] // RWKV_HEAD == j[None, :] // RWKV_HEAD).astype(bf16)
    idt = (jnp.arange(RWKV_HEAD)[:, None] == (i[None, :] % RWKV_HEAD)).astype(f32)
    return bo, idt


def _rwkv_weight_specs(l):
    def cs(shape):
        return pl.BlockSpec((None,) + shape, lambda *_: (l,) + (0,) * len(shape))

    def const(shape):
        return pl.BlockSpec(shape, lambda *_: (0,) * len(shape))

    return [cs((1, 1536)), cs((1, LANE)), cs((2, LANE, 1024)), cs((8, W_BRANCH)),
            const((MXU_TILE, MXU_TILE)), const((RWKV_HEAD, W_BRANCH))]


def _rwkv_seq(proj3, prev_rkv, prev_lo, s_in, wts, l, lb):
    b, seq_len, _ = proj3.shape
    g = b
    nt = seq_len // lb

    def col(width, off):
        return pl.BlockSpec((g, lb, width), lambda i, t: (i, t, off // width))

    tok_sc = pltpu.VMEM((g, lb, W_BRANCH), f32)
    out = pl.pallas_call(
        functools.partial(_rwkv_seq_kernel, g_seqs=g, lb=lb),
        grid=(b // g, nt),
        in_specs=[
            col(512, C_R), col(512, C_K), col(512, C_V), col(LANE, C_RWKV_LO),
            pl.BlockSpec((g, 1, 1536), lambda i, t: (i, 0, 0)),
            pl.BlockSpec((g, 1, LANE), lambda i, t: (i, 0, 0)),
            pl.BlockSpec((g, RWKV_HEAD, W_BRANCH), lambda i, t: (i, 0, 0)),
        ] + _rwkv_weight_specs(l),
        out_specs=[
            pl.BlockSpec((g, lb, W_BRANCH), lambda i, t: (i, t, 0)),
            pl.BlockSpec((g, RWKV_HEAD, W_BRANCH), lambda i, t: (i, 0, 0)),
        ],
        out_shape=[
            jax.ShapeDtypeStruct((b, seq_len, W_BRANCH), f32),
            jax.ShapeDtypeStruct((b, RWKV_HEAD, W_BRANCH), f32),
        ],
        scratch_shapes=[
            pltpu.VMEM((g, RWKV_HEAD, W_BRANCH), f32),
            pltpu.VMEM((g, 1, 1536), f32),
            pltpu.VMEM((g, 1, LANE), f32),
        ] + [tok_sc] * 9,
        compiler_params=_cparams(("parallel", "arbitrary")),
        name="rwkv_seq",
    )(proj3, proj3, proj3, proj3, prev_rkv, prev_lo, s_in, *wts)
    return out


def _rwkv_tok(proj2, prev_rkv, prev_lo, s_all, acc, wts, l):
    b = proj2.shape[0]
    assert b == LANE, "the single-token RWKV-7 kernel keeps exactly one lane tile of sequences"

    def col(width, off):
        return pl.BlockSpec((b, width), lambda i: (0, off // width))

    state_spec = pl.BlockSpec((None, None, RWKV_HEAD, RWKV_HEAD, b), lambda i: (l, i, 0, 0, 0))
    in_specs = [
        col(512, C_R), col(512, C_K), col(512, C_V), col(LANE, C_RWKV_LO),
        pl.BlockSpec((None, b, 1536), lambda i: (l, 0, 0)),
        pl.BlockSpec((None, b, LANE), lambda i: (l, 0, 0)),
        state_spec,
    ] + _rwkv_weight_specs(l)
    args = [proj2, proj2, proj2, proj2, prev_rkv, prev_lo, s_all, *wts]
    aliases = {}
    if acc is not None:
        in_specs.append(pl.BlockSpec(memory_space=pl.ANY))
        aliases = {len(args): 1}
        args.append(acc)
    feat_sc = pltpu.VMEM((W_BRANCH, b), f32)
    return pl.pallas_call(
        _rwkv_tok_kernel,
        grid=(RWKV_H,),
        in_specs=in_specs,
        out_specs=[pl.BlockSpec((b, W_BRANCH), lambda i: (0, 0)), state_spec],
        out_shape=[
            jax.ShapeDtypeStruct((b, W_BRANCH), f32),
            jax.ShapeDtypeStruct(s_all.shape, f32),
        ],
        scratch_shapes=[feat_sc] * 7 + [pltpu.VMEM((b, W_BRANCH), f32)],
        input_output_aliases=aliases,
        compiler_params=_cparams(("arbitrary",)),
        name="rwkv_tok",
    )(*args)


def _gla_inputs(q_ref, k_ref, v_ref, lo_ref, gup_ref, gb_ref):
    q = q_ref[...] * (GLA_DK ** -0.5)
    z = _dotf(lo_ref[...], gup_ref[...]) + gb_ref[...]
    g = _log_sigmoid(z) * (1.0 / GLA_GATE_NORM)
    return q, k_ref[...], v_ref[...], g


def _hgrn_lb(logits, l):
    m = jnp.max(logits, axis=0, keepdims=True)
    e = jnp.exp(logits - m)
    sm = e / jnp.sum(e, axis=0, keepdims=True)
    lb = jnp.zeros_like(sm[0:1, :])
    for i in range(1, l + 1):
        lb = lb + sm[i:i + 1, :]
    return lb


def _hgrn_inputs(q_ref, f_ref, i_ref, logits_ref, l):
    lb = _hgrn_lb(logits_ref[...], l)
    f_lo = f_ref[...]
    logf = jnp.log(jnp.maximum(lb, LB_FLOOR) + (1.0 - lb) * _sigmoid(f_lo))
    k = (1.0 - lb) * _sigmoid(-f_lo)
    return q_ref[...], k, i_ref[...], logf


def _chunk_core(q, k, v, g, lt_ref, esum_ref, vmask_ref, gmask_ref, bon_ref, gn_ref, st_sc, kpad, bpad, lb,
                heads_per_group):
    bc = _ones_seg(lt_ref[...], g)
    nc = lb // CHUNK
    f = q.shape[1]
    kpad[...] = k.reshape(nc, CHUNK, f)
    bpad[...] = bc.reshape(nc, CHUNK, f)
    q3 = q.reshape(nc, CHUNK, f)
    bc3 = bc.reshape(nc, CHUNK, f)
    esum = esum_ref[...]

    lane_j = lax.broadcasted_iota(jnp.int32, (1, 1, LANE), 2) & (CHUNK - 1)
    att3 = None
    for r0 in range(0, CHUNK, 8):
        rows = CHUNK - r0
        q_s, bc_s = q3[:, r0:, :], bc3[:, r0:, :]
        rowc8 = lax.broadcasted_iota(jnp.int32, (1, 8, 1), 1) + r0
        acc = jnp.zeros((nc, rows, LANE), f32)
        for j in range(r0, r0 + 8):
            kj, bj = kpad[:, j:j + 1, :], bpad[:, j:j + 1, :]
            z = q_s * kj * jnp.exp(bc_s - bj)
            head = jnp.where(rowc8 >= j, z[:, 0:8, :], 0.0)
            z = head if rows == 8 else jnp.concatenate([head, z[:, 8:, :]], axis=1)
            a = jnp.dot(z.reshape(nc * rows, f).astype(bf16), esum, preferred_element_type=f32)
            acc = acc + jnp.where(lane_j == j, a.reshape(nc, rows, LANE), 0.0)
        if r0:
            acc = jnp.concatenate([jnp.zeros((nc, r0, LANE), f32), acc], axis=1)
        att3 = acc if att3 is None else att3 + acc
    n_heads = LANE // CHUNK
    v3 = v.reshape(nc, CHUNK, W_BRANCH)
    vbd = jnp.concatenate([v3] * n_heads, axis=1) * vmask_ref[...]
    o = lax.dot_general(att3.astype(bf16), vbd.astype(bf16), (((2,), (1,)), ((0,), (0,))),
                        preferred_element_type=f32).reshape(lb, W_BRANCH)

    ng = f // LANE
    vw = W_BRANCH // ng
    sts = [st_sc[gi] for gi in range(ng)]
    outs = []
    for c in range(nc):
        sl = slice(c * CHUNK, (c + 1) * CHUNK)
        bcc = bc[sl]
        blast = bcc[CHUNK - 1:CHUNK, :]
        dec = jnp.exp(blast)
        qe = (q[sl] * jnp.exp(bcc)).astype(bf16)
        ke = (k[sl] * jnp.exp(blast - bcc)).astype(bf16)
        vb = v[sl].astype(bf16)
        parts = []
        for gi in range(ng):
            kl = slice(gi * LANE, (gi + 1) * LANE)
            vl = slice(gi * vw, (gi + 1) * vw)
            parts.append(_dot_nt(qe[:, kl], sts[gi].astype(bf16)))
            upd = _dot_tn(vb[:, vl], ke[:, kl])
            if heads_per_group > 1:
                upd = upd * gmask_ref[...]
            sts[gi] = sts[gi] * dec[:, kl] + upd
        outs.append(o[sl] + jnp.concatenate(parts, axis=1))
    for gi in range(ng):
        st_sc[gi] = sts[gi]
    o = jnp.concatenate(outs, axis=0)
    ms = _seg_sum(o * o, bon_ref[...], pieces=1) * (1.0 / LANE)
    return o * lax.rsqrt(ms + EPS) * gn_ref[...]


def _chunk_seq_kernel(*refs, kind, l, lb):
    n_in = 6 if kind == "gla" else 4
    ins = refs[:n_in]
    lt_ref, esum_ref, vmask_ref, gmask_ref, bon_ref, gn_ref, o_ref, stout_ref, st_sc, kpad, bpad = refs[n_in:]
    tb = pl.program_id(1)

    @pl.when(tb == 0)
    def _():
        st_sc[...] = jnp.zeros_like(st_sc)

    if kind == "gla":
        q, k, v, g = _gla_inputs(*ins)
        heads_per_group = LANE // GLA_DK
    else:
        q, k, v, g = _hgrn_inputs(*ins, l)
        heads_per_group = LANE // HGRN_DK
    o_ref[...] = _chunk_core(q, k, v, g, lt_ref, esum_ref, vmask_ref, gmask_ref, bon_ref, gn_ref, st_sc, kpad,
                             bpad, lb, heads_per_group)

    @pl.when(tb == pl.num_programs(1) - 1)
    def _():
        stout_ref[...] = st_sc[...]


def _chunk_consts(h, dk, lb):
    f = h * dk
    i = jnp.arange(lb)
    lt = ((i[:, None] // CHUNK == i[None, :] // CHUNK) & (i[:, None] >= i[None, :])).astype(bf16)
    fi = jnp.arange(f)
    li = jnp.arange(LANE)
    oi = jnp.arange(W_BRANCH)
    vw = W_BRANCH // (f // LANE)
    esum = (fi[:, None] // dk == li[None, :] // CHUNK).astype(bf16)
    vmask = (li[:, None] // CHUNK == oi[None, :] // LANE).astype(f32)
    gmask = (jnp.arange(vw)[:, None] // LANE == li[None, :] // dk).astype(f32)
    ti = jnp.arange(MXU_TILE)
    bon = (ti[:, None] // LANE == ti[None, :] // LANE).astype(bf16)
    return lt, esum, vmask, gmask, bon


def _chunk_seq(kind, proj3, extra, gn3, l, lb):
    b, seq_len, _ = proj3.shape
    h, dk = (GLA_H, GLA_DK) if kind == "gla" else (HGRN_H, HGRN_DK)
    f = h * dk
    nt = seq_len // lb

    def col(width, off):
        return pl.BlockSpec((None, lb, width), lambda i, t: (i, t, off // width))

    def const(shape):
        return pl.BlockSpec(shape, lambda *_: (0,) * len(shape))

    def layer(shape):
        return pl.BlockSpec((None,) + shape, lambda *_: (l,) + (0,) * len(shape))

    if kind == "gla":
        gup, gb = extra
        in_specs = [col(GLA_KW, C_GLA_Q), col(GLA_KW, C_GLA_K), col(W_BRANCH, C_GLA_V), col(LANE, C_GLA_LO),
                    layer((LANE, GLA_KW)), layer((1, GLA_KW))]
        args = [proj3, proj3, proj3, proj3, gup, gb]
    else:
        (logits,) = extra
        in_specs = [col(W_BRANCH, C_HQ), col(W_BRANCH, C_HF), col(W_BRANCH, C_HI), const((DEPTH, W_BRANCH))]
        args = [proj3, proj3, proj3, logits]
    consts = _chunk_consts(h, dk, lb)
    in_specs += [const(c.shape) for c in consts] + [layer((1, W_BRANCH))]
    ng = f // LANE
    st_shape = (ng, W_BRANCH // ng, LANE)
    return pl.pallas_call(
        functools.partial(_chunk_seq_kernel, kind=kind, l=l, lb=lb),
        grid=(b, nt),
        in_specs=in_specs,
        out_specs=[
            pl.BlockSpec((None, lb, W_BRANCH), lambda i, t: (i, t, 0)),
            pl.BlockSpec((None,) + st_shape, lambda i, t: (i, 0, 0, 0)),
        ],
        out_shape=[
            jax.ShapeDtypeStruct((b, seq_len, W_BRANCH), f32),
            jax.ShapeDtypeStruct((b,) + st_shape, f32),
        ],
        scratch_shapes=[
            pltpu.VMEM(st_shape, f32),
            pltpu.VMEM((lb // CHUNK, CHUNK, f), f32),
            pltpu.VMEM((lb // CHUNK, CHUNK, f), f32),
        ],
        compiler_params=_cparams(("parallel", "arbitrary")),
        name=kind + "_seq",
    )(*args, *consts, gn3)


def _state_from_transposed(st, h, dk):
    b, ng = st.shape[0], st.shape[1]
    hpg = h // ng
    st = st.reshape(b, ng, hpg, LANE, hpg, dk)
    diag = jnp.stack([st[:, :, i, :, i, :] for i in range(hpg)], axis=2)
    return jnp.swapaxes(diag.reshape(b, h, LANE, dk), 2, 3)


def _key_columns(x, g, dk, odd):
    xt = jnp.concatenate([x, jnp.zeros((LANE - g, LANE), f32)], axis=0).T
    if dk == LANE:
        return xt
    return jnp.where(odd, xt[dk:2 * dk], xt[0:dk])


def _chunk_tok_kernel(*refs, kind, l, g, has_acc):
    ins, (o_ref, sout_ref) = refs[:-2], refs[-2:]
    if has_acc:
        ins = ins[:-1]
    odd = (pl.program_id(1) % 2) == 1
    if kind == "gla":
        q_ref, k_ref, v_ref, lo_ref, gup_ref, gb_ref, s_ref, gn_ref = ins
        q = q_ref[...] * (GLA_DK ** -0.5)
        k = k_ref[...]
        z = _dotf(lo_ref[...], gup_ref[...]) + gb_ref[...]
        dec = jnp.exp(_log_sigmoid(z) * (1.0 / GLA_GATE_NORM))
        v = v_ref[...]
        dk = GLA_DK
    else:
        q_ref, f_ref, i_ref, logits_ref, s_ref, gn_ref = ins
        q, k, v, logf = _hgrn_inputs(q_ref, f_ref, i_ref, logits_ref, l)
        dec = jnp.exp(logf)
        dk = HGRN_DK
    qt, kt, dt = (_key_columns(x, g, dk, odd) for x in (q, k, dec))
    o_rows = []
    for b in range(g):
        qcol, kcol, dcol = (jnp.broadcast_to(t[:, b:b + 1], (dk, LANE)) for t in (qt, kt, dt))
        s_new = s_ref[b] * dcol + kcol * v[b:b + 1, :]
        sout_ref[b] = s_new
        o_rows.append(jnp.sum(qcol * s_new, axis=0, keepdims=True))
    o = jnp.concatenate(o_rows, axis=0)
    ms = jnp.mean(o * o, axis=-1, keepdims=True)
    o_ref[...] = o * lax.rsqrt(ms + EPS) * gn_ref[...]


def _chunk_tok(kind, proj2, extra, gn3, s_all, acc, l, g):
    b = proj2.shape[0]
    h, dk = (GLA_H, GLA_DK) if kind == "gla" else (HGRN_H, HGRN_DK)

    def head_col(off):
        per = LANE // dk
        return pl.BlockSpec((g, LANE), lambda i, j: (i, off // LANE + j // per))

    if kind == "gla":
        gup, gb = extra
        in_specs = [head_col(C_GLA_Q), head_col(C_GLA_K),
                    pl.BlockSpec((g, LANE), lambda i, j: (i, C_GLA_V // LANE + j)),
                    pl.BlockSpec((g, LANE), lambda i, j: (i, C_GLA_LO // LANE)),
                    pl.BlockSpec((None, LANE, LANE), lambda i, j: (l, 0, j // 2)),
                    pl.BlockSpec((None, 1, LANE), lambda i, j: (l, 0, j // 2))]
        args = [proj2, proj2, proj2, proj2, gup, gb]
    else:
        (logits,) = extra
        in_specs = [head_col(C_HQ), head_col(C_HF), head_col(C_HI),
                    pl.BlockSpec((DEPTH, LANE), lambda i, j: (0, j))]
        args = [proj2, proj2, proj2, logits]
    state_spec = pl.BlockSpec((None, g, None, dk, LANE), lambda i, j: (l, i, j, 0, 0))
    in_specs += [state_spec, pl.BlockSpec((None, 1, LANE), lambda i, j: (l, 0, j))]
    args += [s_all, gn3]
    aliases = {}
    if acc is not None:
        in_specs.append(pl.BlockSpec(memory_space=pl.ANY))
        aliases = {len(args): 1}
        args.append(acc)
    return pl.pallas_call(
        functools.partial(_chunk_tok_kernel, kind=kind, l=l, g=g, has_acc=acc is not None),
        grid=(b // g, h),
        in_specs=in_specs,
        out_specs=[pl.BlockSpec((g, LANE), lambda i, j: (i, j)), state_spec],
        out_shape=[
            jax.ShapeDtypeStruct((b, W_BRANCH), f32),
            jax.ShapeDtypeStruct(s_all.shape, f32),
        ],
        input_output_aliases=aliases,
        compiler_params=_cparams(("parallel", "parallel")),
        name=kind + "_tok",
    )(*args)


def _lru_gate_dot(y_hi, y_lo, w_ref, col0):
    d = functools.partial(jnp.dot, preferred_element_type=f32)
    halves = []
    for h0 in range(0, W_BRANCH, MXU_TILE):
        w_hi = w_ref[0, h0:h0 + MXU_TILE, col0 + h0:col0 + h0 + MXU_TILE]
        w_lo = w_ref[1, h0:h0 + MXU_TILE, col0 + h0:col0 + h0 + MXU_TILE]
        a, b = y_hi[:, h0:h0 + MXU_TILE], y_lo[:, h0:h0 + MXU_TILE]
        halves.append(d(a, w_hi) + d(a, w_lo) + d(b, w_hi))
    return jnp.concatenate(halves, axis=1)


def _lru_gates(y, w_ref, pvec):
    b_a, b_x, lam = pvec[1:2, :], pvec[2:3, :], pvec[3:4, :]
    y_hi, y_lo = _split(y, 2)
    r = _sigmoid(_lru_gate_dot(y_hi, y_lo, w_ref, 0) + b_a)
    ig = _sigmoid(_lru_gate_dot(y_hi, y_lo, w_ref, W_BRANCH) + b_x)
    log_a = -LRU_C * r * _softplus(-lam)
    a = jnp.exp(log_a)
    one_m_a2 = -jnp.tanh(log_a) * (jnp.exp(2.0 * log_a) + 1.0)
    b = jnp.sqrt(one_m_a2) * (ig * y)
    return a, b


def _lru_seq_kernel(x_ref, cw_ref, pvec_ref, wbd_ref, o_ref, xpad, hcar, *, lb):
    tb = pl.program_id(1)

    @pl.when(tb == 0)
    def _():
        xpad[0:8, :] = jnp.zeros((8, W_BRANCH), f32)
        hcar[...] = jnp.zeros_like(hcar)

    x = x_ref[...]
    xpad[8:8 + lb, :] = x
    cw = cw_ref[...]
    pvec = pvec_ref[...]
    y = pvec[0:1, :] + x * cw[3:4, :]
    for j in range(CONV_W - 1):
        y = y + xpad[5 + j:5 + j + lb, :] * cw[j:j + 1, :]
    a, b = _lru_gates(y, wbd_ref, pvec)
    row = lax.broadcasted_iota(jnp.int32, (lb, 1), 0)
    s = 1
    while s < lb:
        m = row >= s
        b = jnp.where(m, a * pltpu.roll(b, s, 0) + b, b)
        a = jnp.where(m, a * pltpu.roll(a, s, 0), a)
        s *= 2
    h = a * hcar[...] + b
    o_ref[...] = h
    hcar[...] = h[lb - 1:lb, :]
    xpad[5:8, :] = x[lb - 3:lb, :]


def _lru_seq(proj3, cw, pvec, wbd, l, lb):
    b, seq_len, _ = proj3.shape

    def layer(shape):
        return pl.BlockSpec((None,) + shape, lambda *_: (l,) + (0,) * len(shape))

    return pl.pallas_call(
        functools.partial(_lru_seq_kernel, lb=lb),
        grid=(b, seq_len // lb),
        in_specs=[
            pl.BlockSpec((None, lb, W_BRANCH), lambda i, t: (i, t, C_LRU // W_BRANCH)),
            layer((CONV_W, W_BRANCH)), layer((8, W_BRANCH)), layer((2, W_BRANCH, 2 * W_BRANCH)),
        ],
        out_specs=pl.BlockSpec((None, lb, W_BRANCH), lambda i, t: (i, t, 0)),
        out_shape=jax.ShapeDtypeStruct((b, seq_len, W_BRANCH), f32),
        scratch_shapes=[
            pltpu.VMEM((8 + lb, W_BRANCH), f32),
            pltpu.VMEM((1, W_BRANCH), f32),
        ],
        compiler_params=_cparams(("parallel", "arbitrary")),
        name="lru_seq",
    )(proj3, cw, pvec, wbd)


def _lru_tok_kernel(x_ref, b0_ref, b1_ref, b2_ref, h0_ref, cw_ref, pvec_ref, wbd_ref, o_ref):
    cw = cw_ref[...]
    pvec = pvec_ref[...]
    y = (pvec[0:1, :] + b0_ref[...] * cw[0:1, :] + b1_ref[...] * cw[1:2, :] + b2_ref[...] * cw[2:3, :]
         + x_ref[...] * cw[3:4, :])
    a, b = _lru_gates(y, wbd_ref, pvec)
    o_ref[...] = a * h0_ref[...] + b


def _lru_tok(proj2, bufs, h0, cw, pvec, wbd, l):
    b = proj2.shape[0]

    def layer(shape):
        return pl.BlockSpec((None,) + shape, lambda *_: (l,) + (0,) * len(shape))

    row = pl.BlockSpec((b, W_BRANCH), lambda i: (0, 0))
    return pl.pallas_call(
        _lru_tok_kernel,
        grid=(1,),
        in_specs=[pl.BlockSpec((b, W_BRANCH), lambda i: (0, C_LRU // W_BRANCH)), row, row, row, row,
                  layer((CONV_W, W_BRANCH)), layer((8, W_BRANCH)), layer((2, W_BRANCH, 2 * W_BRANCH))],
        out_specs=row,
        out_shape=jax.ShapeDtypeStruct((b, W_BRANCH), f32),
        compiler_params=_cparams(("arbitrary",)),
        name="lru_tok",
    )(proj2, *bufs, h0, cw, pvec, wbd)


def _prep_weights(p):
    w = jnp.swapaxes(p["w_in"], 1, 2)
    row_groups = [(0, 1536), (2176, 2688), (2704, 6800), (1664, 2176), (1536, 1664), (2688, 2704)]
    pad = jnp.zeros((DEPTH, LANE - GLA_GATE_RANK, D_MODEL), bf16)
    w_in_p = jnp.concatenate([w[:, a:b].astype(bf16) for a, b in row_groups] + [pad], axis=1)
    zl = jnp.zeros((DEPTH, RWKV_LORA, W_BRANCH), f32)
    wup = jnp.concatenate([jnp.concatenate([p["rwkv_w_up"], zl], axis=2),
                           jnp.concatenate([zl, p["rwkv_a_up"]], axis=2)], axis=1)
    wup_hi = wup.astype(bf16)
    wup = jnp.stack([wup_hi, (wup - wup_hi.astype(f32)).astype(bf16)], axis=1)
    zrow = jnp.zeros((DEPTH, W_BRANCH), f32)
    rwkv_vec = jnp.stack([p["rwkv_w0"], p["rwkv_a0"], p["rwkv_k_k"], p["rwkv_k_a"],
                          p["rwkv_r_k"].reshape(DEPTH, W_BRANCH), p["rwkv_gn_g"], p["rwkv_gn_b"], zrow], axis=1)
    mu = p["rwkv_mu"]
    gup = jnp.concatenate([p["gla_gk_up"], jnp.zeros((DEPTH, LANE - GLA_GATE_RANK, GLA_KW), f32)], axis=1)
    eye = jnp.eye(LRU_BLOCKS, dtype=f32)

    def bd(wb):
        return jnp.einsum("lhij,hg->lhigj", wb, eye).reshape(DEPTH, W_BRANCH, W_BRANCH)

    lru_vec = jnp.stack([p["lru_conv_b"], p["lru_b_a"], p["lru_b_x"], p["lru_lambda"],
                         zrow, zrow, zrow, zrow], axis=1)
    lru_w = jnp.concatenate([bd(p["lru_w_a"]), bd(p["lru_w_x"])], axis=2)
    lru_w_hi = lru_w.astype(bf16)
    lru_w_lo = (lru_w - lru_w_hi.astype(f32)).astype(bf16)
    return dict(
        w_in_p=w_in_p,
        w_out_b=p["w_out"].astype(bf16),
        norm_g3=p["norm_g"].reshape(DEPTH, 1, D_MODEL),
        final_g2=p["final_g"].reshape(1, D_MODEL),
        rwkv=(mu[:, None, 0:1536], mu[:, None, 1536:1664], wup, rwkv_vec) + _rwkv_consts(),
        gla=(gup, p["gla_gk_b"].reshape(DEPTH, 1, GLA_KW)),
        gla_gn=p["gla_gn_g"].reshape(DEPTH, 1, W_BRANCH),
        hgrn=(p["hgrn_lb_logits"],),
        hgrn_gn=p["hgrn_gn_g"].reshape(DEPTH, 1, W_BRANCH),
        lru=(p["lru_conv_w"], lru_vec, jnp.stack([lru_w_hi, lru_w_lo], axis=1)),
    )


def _shift_state(proj_last):
    return jnp.concatenate([proj_last[..., 0:1536], proj_last[..., C_RWKV_LO:C_RWKV_LO + LANE]], axis=-1)


def _rwkv_state_out(s):
    lead = s.shape[:-2]
    return jnp.swapaxes(s.reshape(lead + (RWKV_HEAD, RWKV_H, RWKV_HEAD)), -3, -2)


def _trunk_seq(x, mod, wts):
    b, seq_len, _ = x.shape
    t = b * seq_len
    lb = min(LB_CHUNK, seq_len)
    tm = min(TM_IN, seq_len)
    per_seq = seq_len // tm
    mod3 = mod.reshape(DEPTH * b * 3, 1, D_MODEL)
    x2 = x.reshape(t, D_MODEL)
    zeros = functools.partial(jnp.zeros, dtype=f32)
    new = []
    for l in range(DEPTH):
        mod_l = mod3[l * b * 3:(l + 1) * b * 3]
        proj2 = _inproj(x2, wts["norm_g3"], mod_l, wts["w_in_p"], l, per_seq, tm)
        proj3 = proj2.reshape(b, seq_len, PROJ_P)
        o_a, s_wkv = _rwkv_seq(proj3, zeros((b, 1, 1536)), zeros((b, 1, LANE)),
                               zeros((b, RWKV_HEAD, W_BRANCH)), wts["rwkv"], l, min(LB_RWKV, seq_len))
        o_b, st_gla = _chunk_seq("gla", proj3, wts["gla"], wts["gla_gn"], l, lb)
        o_c, st_hgrn = _chunk_seq("hgrn", proj3, wts["hgrn"], wts["hgrn_gn"], l, lb)
        o_d = _lru_seq(proj3, *wts["lru"], l, lb)
        outs = [o.reshape(t, W_BRANCH) for o in (o_a, o_b, o_c, o_d)]
        tmo = min(TM_OUT, seq_len)
        x2 = _outproj(outs, proj2, x2, mod_l, wts["w_out_b"], wts["final_g2"], l, seq_len // tmo, tmo,
                      final=(l == DEPTH - 1))
        last = proj3[:, seq_len - 1]
        new.append((
            _shift_state(last),
            _rwkv_state_out(s_wkv),
            _state_from_transposed(st_gla, GLA_H, GLA_DK),
            _state_from_transposed(st_hgrn, HGRN_H, HGRN_DK),
            proj3[:, seq_len - (CONV_W - 1):, C_LRU:C_LRU + W_BRANCH],
            o_d[:, seq_len - 1],
        ))
    return x2.reshape(b, seq_len, D_MODEL), tuple(jnp.stack([n[i] for n in new], axis=0) for i in range(6))


def _trunk_tok(x, mod, states, wts):
    b = x.shape[0]
    s_shift, s_wkv, s_gla, s_hgrn, s_conv, s_h = states
    x2 = x.reshape(b, D_MODEL)
    s_wkv_t = jnp.transpose(s_wkv, (0, 2, 3, 4, 1))
    prev_rkv, prev_lo = s_shift[:, :, 0:1536], s_shift[:, :, 1536:1664]
    n_wkv = n_gla = n_hgrn = None
    new = []
    for l in range(DEPTH):
        proj2 = _inproj(x2, wts["norm_g3"], mod, wts["w_in_p"], l, None, b)
        o_a, n_wkv = _rwkv_tok(proj2, prev_rkv, prev_lo, s_wkv_t, n_wkv, wts["rwkv"], l)
        o_b, n_gla = _chunk_tok("gla", proj2, wts["gla"], wts["gla_gn"], s_gla, n_gla, l, TOK_GROUP)
        o_c, n_hgrn = _chunk_tok("hgrn", proj2, wts["hgrn"], wts["hgrn_gn"], s_hgrn, n_hgrn, l, TOK_GROUP)
        conv = s_conv[l]
        o_d = _lru_tok(proj2, [conv[:, 0], conv[:, 1], conv[:, 2]], s_h[l], *wts["lru"], l)
        x2 = _outproj([o_a, o_b, o_c, o_d], proj2, x2, mod, wts["w_out_b"], wts["final_g2"], l, None, b,
                      final=(l == DEPTH - 1))
        n_conv = jnp.concatenate([conv[:, 1:], proj2[:, None, C_LRU:C_LRU + W_BRANCH]], axis=1)
        new.append((_shift_state(proj2), n_conv, o_d))
    n_shift, n_conv, n_h = (jnp.stack([n[i] for n in new], axis=0) for i in range(3))
    n_wkv = jnp.transpose(n_wkv, (0, 4, 1, 2, 3))
    return x2.reshape(b, 1, D_MODEL), (n_shift, n_wkv, n_gla, n_hgrn, n_conv, n_h)


def kernel(x_prompt, x_sample, c_prompt, c_sample, state_rwkv_shift, state_rwkv_wkv, state_gla, state_hgrn, state_lru_conv, state_lru_h, norm_g, w_ada, b_ada, w_in, w_out, rwkv_mu, rwkv_w0, rwkv_w_up, rwkv_a0, rwkv_a_up, rwkv_k_k, rwkv_k_a, rwkv_r_k, rwkv_gn_g, rwkv_gn_b, gla_gk_up, gla_gk_b, gla_gn_g, hgrn_lb_logits, hgrn_gn_g, lru_conv_w, lru_conv_b, lru_w_a, lru_b_a, lru_w_x, lru_b_x, lru_lambda, final_g):
    p = dict(norm_g=norm_g, w_in=w_in, w_out=w_out, rwkv_mu=rwkv_mu, rwkv_w0=rwkv_w0, rwkv_w_up=rwkv_w_up,
             rwkv_a0=rwkv_a0, rwkv_a_up=rwkv_a_up, rwkv_k_k=rwkv_k_k, rwkv_k_a=rwkv_k_a, rwkv_r_k=rwkv_r_k,
             rwkv_gn_g=rwkv_gn_g, rwkv_gn_b=rwkv_gn_b, gla_gk_up=gla_gk_up, gla_gk_b=gla_gk_b,
             gla_gn_g=gla_gn_g, hgrn_lb_logits=hgrn_lb_logits, hgrn_gn_g=hgrn_gn_g, lru_conv_w=lru_conv_w,
             lru_conv_b=lru_conv_b, lru_w_a=lru_w_a, lru_b_a=lru_b_a, lru_w_x=lru_w_x, lru_b_x=lru_b_x,
             lru_lambda=lru_lambda, final_g=final_g)
    wts = _prep_weights(p)
    bp = x_prompt.shape[0]
    bs = x_sample.shape[0]
    pad_rows = (-bp) % 8
    c_all = jnp.concatenate([c_prompt, jnp.zeros((pad_rows, D_MODEL), f32), c_sample], axis=0)
    mod = _ada(c_all, w_ada, b_ada)
    mod_p = mod[:, 0:bp]
    mod_s = mod[:, bp + pad_rows:bp + pad_rows + bs]
    y_p, st_p = _trunk_seq(x_prompt, mod_p, wts)
    states = (state_rwkv_shift, state_rwkv_wkv, state_gla, state_hgrn, state_lru_conv, state_lru_h)
    y_s, st_s = _trunk_tok(x_sample, mod_s, states, wts)
    return (y_p, y_s) + st_p + st_s
```

```python
import functools

import jax
import jax.numpy as jnp
from jax import lax
from jax.experimental import pallas as pl
from jax.experimental.pallas import tpu as pltpu

f32 = jnp.float32
bf16 = jnp.bfloat16

D_MODEL = 2048
DEPTH = 4
W_BRANCH = 512
EPS = 1e-6

RWKV_H = 8
RWKV_HEAD = 64
RWKV_LORA = 64
RWKV_PROJ = 3 * W_BRANCH + 2 * RWKV_LORA
RWKV_GN_EPS = 64e-5

GLA_H = 4
GLA_DK = 64
GLA_DV = 128
GLA_KW = GLA_H * GLA_DK
GLA_GATE_RANK = 16
GLA_GATE_NORM = 16.0

HGRN_H = 4
HGRN_DK = 128
HGRN_DV = 128
LB_FLOOR = 1e-30

LRU_BLOCKS = 8
LRU_BLOCK = 64
LRU_C = 8.0
CONV_W = 4

CHUNK = 32

C_R, C_K, C_V = 0, 512, 1024
C_GLA_V = 1536
C_HQ, C_HF, C_HI = 2048, 2560, 3072
C_LRU = 3584
C_GATE = 4096
C_GLA_Q, C_GLA_K = 6144, 6400
C_RWKV_LO = 6656
C_GLA_LO = 6784
PROJ_P = 6912

LANE = 128
MXU_TILE = 256
VMEM_LIMIT = 56 * 1024 * 1024

TM_IN = 1024
TN_IN = 1152
TM_OUT = 256
LB_CHUNK = 256
LB_RWKV = 128
RWKV_GROUP = 8
TOK_GROUP = 16

HIGHEST = lax.Precision.HIGHEST


def _cparams(sem):
    return pltpu.CompilerParams(dimension_semantics=sem, vmem_limit_bytes=VMEM_LIMIT)


def _dotf(a, b):
    return jnp.dot(a, b, preferred_element_type=f32, precision=HIGHEST)


def _prec(a):
    return HIGHEST if a.dtype == f32 else None


def _dot_nt(a, b):
    return lax.dot_general(a, b, (((1,), (1,)), ((), ())), preferred_element_type=f32, precision=_prec(a))


def _dot_tn(a, b):
    return lax.dot_general(a, b, (((0,), (0,)), ((), ())), preferred_element_type=f32, precision=_prec(a))


def _split(x, pieces):
    out = []
    for i in range(pieces):
        part = x.astype(bf16)
        out.append(part)
        if i + 1 < pieces:
            x = x - part.astype(f32)
    return out


def _half_dot(x, ones):
    d = functools.partial(jnp.dot, preferred_element_type=f32)
    kh = ones.shape[0]
    if x.shape[1] == kh:
        return d(x, ones)
    return jnp.concatenate([d(x[:, 0:kh], ones), d(x[:, kh:2 * kh], ones)], axis=1)


def _seg_sum(x, ones, pieces=2):
    return sum(_half_dot(part, ones) for part in _split(x, pieces))


def _ones_seg(ones, x, pieces=2):
    d = functools.partial(jnp.dot, preferred_element_type=f32)
    return sum(d(ones, part) for part in _split(x, pieces))


def _sigmoid(x):
    return jax.nn.sigmoid(x)


def _softplus(x):
    return jnp.maximum(x, 0.0) + jnp.log1p(jnp.exp(-jnp.abs(x)))


def _log_sigmoid(x):
    return -_softplus(-x)


def _ada_kernel(c_ref, w_ref, b_ref, o_ref):
    c = c_ref[...]
    s = (c * _sigmoid(c)).astype(bf16)
    o_ref[...] = jnp.dot(s, w_ref[...].astype(bf16), preferred_element_type=f32) + b_ref[...]


def _ada(c_all, w_ada, b_ada):
    rows = c_all.shape[0]
    tn = 512
    n = w_ada.shape[2]
    return pl.pallas_call(
        _ada_kernel,
        grid=(DEPTH, n // tn),
        in_specs=[
            pl.BlockSpec((rows, D_MODEL), lambda l, j: (0, 0)),
            pl.BlockSpec((None, D_MODEL, tn), lambda l, j: (l, 0, j)),
            pl.BlockSpec((None, 1, tn), lambda l, j: (l, 0, j)),
        ],
        out_specs=pl.BlockSpec((None, rows, tn), lambda l, j: (l, 0, j)),
        out_shape=jax.ShapeDtypeStruct((DEPTH, rows, n), f32),
        compiler_params=_cparams(("parallel", "parallel")),
        name="ada_mod",
    )(c_all, w_ada, b_ada.reshape(DEPTH, 1, n))


def _norm_mod(x, g, scale, shift):
    ms = jnp.mean(x * x, axis=-1, keepdims=True)
    h = x * lax.rsqrt(ms + EPS) * g
    return (h * (1.0 + scale) + shift).astype(bf16)


def _inproj_kernel(*refs, normed):
    if normed:
        h_ref, w_ref, o_ref = refs
    else:
        x_ref, g_ref, sc_ref, sh_ref, w_ref, o_ref, h_ref = refs

        @pl.when(pl.program_id(1) == 0)
        def _():
            h_ref[...] = _norm_mod(x_ref[...], g_ref[...], sc_ref[...], sh_ref[...])

    o_ref[...] = _dot_nt(h_ref[...], w_ref[...])


def _mod_spec(l, which, per_seq_tiles, tm):
    if per_seq_tiles is None:
        return pl.BlockSpec((None, tm, D_MODEL), lambda i, *_: (l, 0, which))
    return pl.BlockSpec((None, 1, D_MODEL), lambda i, *_: (i // per_seq_tiles * 3 + which, 0, 0))


def _inproj(x2, norm_g3, mod, w_in_p, l, per_seq_tiles, tm, normed):
    t = x2.shape[0]
    tn = TN_IN
    row_spec = pl.BlockSpec((tm, D_MODEL), lambda i, j: (i, 0))
    w_spec = pl.BlockSpec((None, tn, D_MODEL), lambda i, j: (l, j, 0))
    if normed:
        in_specs, args, scratch = [row_spec, w_spec], (x2, w_in_p), []
    else:
        in_specs = [row_spec, pl.BlockSpec((None, 1, D_MODEL), lambda i, j: (l, 0, 0)),
                    _mod_spec(l, 1, per_seq_tiles, tm), _mod_spec(l, 0, per_seq_tiles, tm), w_spec]
        args, scratch = (x2, norm_g3, mod, mod, w_in_p), [pltpu.VMEM((tm, D_MODEL), bf16)]
    return pl.pallas_call(
        functools.partial(_inproj_kernel, normed=normed),
        grid=(t // tm, PROJ_P // tn),
        in_specs=in_specs,
        out_specs=pl.BlockSpec((tm, tn), lambda i, j: (i, j)),
        out_shape=jax.ShapeDtypeStruct((t, PROJ_P), f32),
        scratch_shapes=scratch,
        compiler_params=_cparams(("parallel", "arbitrary")),
        name="in_proj",
    )(*args)


def _outproj_kernel(oa_ref, ob_ref, oc_ref, od_ref, pz_ref, x_ref, gate_ref, w_ref, g_ref, *rest, final):
    z = pz_ref[...]
    o = jnp.concatenate([oa_ref[...], ob_ref[...], oc_ref[...], od_ref[...]], axis=1)
    o = o * (z * _sigmoid(z))
    y = jnp.dot(o.astype(bf16), w_ref[...], preferred_element_type=f32)
    xn = x_ref[...] + gate_ref[...] * y
    if final:
        (o_ref,) = rest
        ms = jnp.mean(xn * xn, axis=-1, keepdims=True)
        o_ref[...] = xn * lax.rsqrt(ms + EPS) * g_ref[...]
    else:
        sc_ref, sh_ref, o_ref, h_ref = rest
        o_ref[...] = xn
        h_ref[...] = _norm_mod(xn, g_ref[...], sc_ref[...], sh_ref[...])


def _outproj(outs, proj2, x2, mod, mod_next, w_out_b, norm_g3, final_g2, l, per_seq_tiles, tm):
    t = x2.shape[0]
    final = l == DEPTH - 1
    mix_spec = pl.BlockSpec((tm, W_BRANCH), lambda i: (i, 0))
    row_spec = pl.BlockSpec((tm, D_MODEL), lambda i: (i, 0))
    in_specs = [
        mix_spec, mix_spec, mix_spec, mix_spec,
        pl.BlockSpec((tm, D_MODEL), lambda i: (i, C_GATE // D_MODEL)),
        row_spec,
        _mod_spec(l, 2, per_seq_tiles, tm),
        pl.BlockSpec((None, D_MODEL, D_MODEL), lambda i: (l, 0, 0)),
    ]
    args = [*outs, proj2, x2, mod, w_out_b]
    if final:
        in_specs.append(pl.BlockSpec((1, D_MODEL), lambda i: (0, 0)))
        args.append(final_g2)
        out_specs, out_shape = row_spec, jax.ShapeDtypeStruct((t, D_MODEL), f32)
    else:
        in_specs += [pl.BlockSpec((None, 1, D_MODEL), lambda i: (l + 1, 0, 0)),
                     _mod_spec(l + 1, 1, per_seq_tiles, tm), _mod_spec(l + 1, 0, per_seq_tiles, tm)]
        args += [norm_g3, mod_next, mod_next]
        out_specs = [row_spec, row_spec]
        out_shape = [jax.ShapeDtypeStruct((t, D_MODEL), f32), jax.ShapeDtypeStruct((t, D_MODEL), bf16)]
    return pl.pallas_call(
        functools.partial(_outproj_kernel, final=final),
        grid=(t // tm,),
        in_specs=in_specs,
        out_specs=out_specs,
        out_shape=out_shape,
        compiler_params=_cparams(("parallel",)),
        name="out_proj",
    )(*args)


def _rwkv_prologue(r, k, v, lo, pr, pk, pv, plo, mu_rkv, mu_lo, wup, pvec, bo, sub_block=0):
    w0, a0, k_k, k_a, r_k = (pvec[i:i + 1, :] for i in range(5))
    xr = r + (pr - r) * mu_rkv[:, 0:512]
    xk = k + (pk - k) * mu_rkv[:, 512:1024]
    xv = v + (pv - v) * mu_rkv[:, 1024:1536]
    xlo = lo + (plo - lo) * mu_lo
    lane = lax.broadcasted_iota(jnp.int32, xlo.shape, 1)
    act = jnp.where(lane < RWKV_LORA, jnp.tanh(xlo), xlo)
    act_hi, act_lo = _split(act, 2)
    d = functools.partial(jnp.dot, preferred_element_type=f32)
    up = d(act_hi, wup[0]) + d(act_hi, wup[1]) + d(act_lo, wup[0])
    w_raw = -_softplus(-(w0 + up[:, 0:512])) - 0.5
    logw = -jnp.exp(w_raw)
    a = _sigmoid(a0 + up[:, 512:1024])
    kk = xk * k_k
    kk = kk / jnp.maximum(jnp.sqrt(_seg_sum(kk * kk, bo)), 1e-12)
    kh = xk * (1.0 + (a - 1.0) * k_a)
    alp = kk * a
    ar = _seg_sum(alp * xr, bo, pieces=1)
    out = dict(vv=xv, xr=xr, kr=_seg_sum(kh * xr, bo, pieces=1),
               bonus=_seg_sum(xr * kh * r_k, bo, pieces=1) * xv)
    if not sub_block:
        ew = jnp.exp(logw)
        out.update(kap=kk, alp=alp, kh=kh, ew=ew, wr=ew * xr - ar * kk)
        return out
    rowi = lax.broadcasted_iota(jnp.int32, (logw.shape[0], 1), 0) & (sub_block - 1)
    lc = logw
    step = 1
    while step < sub_block:
        lc = lc + jnp.where(rowi >= step, pltpu.roll(lc, step, 0), 0.0)
        step *= 2
    gam, ginv = jnp.exp(lc), jnp.exp(-lc)
    kap = kk * jnp.exp(lc - logw)
    out.update(kap=kap, alp=alp * ginv, kh=kh * ginv, gam=gam, wr=gam * xr - ar * kap)
    return out


_RWKV_STEP_KEYS = ("kap", "alp", "kh", "vv", "wr", "kr")


def _rwkv_step(s, kap, alp, kh, vv, wr, kr, bo, idt):
    s_new, o_row = [], []
    for c0 in range(0, W_BRANCH, MXU_TILE):
        sl = slice(c0, c0 + MXU_TILE)
        s_h, idt_h, v_h = s[:, sl], idt[:, sl], vv[:, sl]
        s_b = s_h.astype(bf16)
        x = jnp.concatenate([s_b * kap[:, sl].astype(bf16), s_b * wr[:, sl].astype(bf16),
                             idt_h.astype(bf16) * v_h.astype(bf16)], axis=0)
        red = jnp.dot(x, bo, preferred_element_type=f32)
        s_new.append(s_h - red[0:64] * alp[:, sl] + red[128:192] * kh[:, sl])
        o_row.append(jnp.sum(red[64:128] * idt_h, axis=0, keepdims=True) + v_h * kr[:, sl])
    return jnp.concatenate(s_new, axis=1), jnp.concatenate(o_row, axis=1)


def _rwkv_epilogue(o, bonus, pvec, bo):
    gn_g, gn_b = pvec[5:6, :], pvec[6:7, :]
    mu = _seg_sum(o, bo, pieces=1) * (1.0 / RWKV_HEAD)
    d = o - mu
    var = _seg_sum(d * d, bo, pieces=1) * (1.0 / RWKV_HEAD)
    return d * lax.rsqrt(var + RWKV_GN_EPS) * gn_g + gn_b + bonus


def _rwkv_seq_kernel(r_ref, k_ref, v_ref, lo_ref, prkv_ref, plo_ref, sin_ref, mu_rkv_ref, mu_lo_ref, wup_ref,
                     pvec_ref, bo_ref, idt_ref, o_ref, sout_ref,
                     s_sc, crkv_sc, clo_sc, kap_sc, gam_sc, alp_sc, kh_sc, vv_sc, wr_sc, kr_sc, oraw_sc,
                     bonus_sc, *, g_seqs, lb):
    tb = pl.program_id(1)

    @pl.when(tb == 0)
    def _():
        s_sc[...] = sin_ref[...]
        crkv_sc[...] = prkv_ref[...]
        clo_sc[...] = plo_ref[...]

    bo = bo_ref[...]
    idt = idt_ref[...]
    pvec = pvec_ref[...]
    step_sc = dict(kap=kap_sc, alp=alp_sc, kh=kh_sc, vv=vv_sc, wr=wr_sc, kr=kr_sc)

    row0 = lax.broadcasted_iota(jnp.int32, (lb, 1), 0) == 0
    for g in range(g_seqs):
        cur = [r_ref[g], k_ref[g], v_ref[g]]
        lo = lo_ref[g]
        carry = crkv_sc[g]
        prev = [jnp.where(row0, carry[:, i * 512:(i + 1) * 512], pltpu.roll(c, 1, 0)) for i, c in enumerate(cur)]
        plo = jnp.where(row0, clo_sc[g], pltpu.roll(lo, 1, 0))
        res = _rwkv_prologue(cur[0], cur[1], cur[2], lo, prev[0], prev[1], prev[2], plo,
                             mu_rkv_ref[...], mu_lo_ref[...], wup_ref[...], pvec, bo, sub_block=RWKV_GROUP)
        for key in _RWKV_STEP_KEYS:
            step_sc[key][g] = res[key]
        gam_sc[g] = res["gam"]
        bonus_sc[g] = res["bonus"]
        for i, c in enumerate(cur):
            crkv_sc[g, :, i * 512:(i + 1) * 512] = c[lb - 1:lb, :]
        clo_sc[g] = lo[lb - 1:lb, :]

    def group(i, carry):
        t0 = pl.multiple_of(i * RWKV_GROUP, RWKV_GROUP)
        states = [s_sc[g] for g in range(g_seqs)]
        for j in range(RWKV_GROUP):
            for g in range(g_seqs):
                rows = [step_sc[key][g, pl.ds(t0 + j, 1), :] for key in _RWKV_STEP_KEYS]
                states[g], o_row = _rwkv_step(states[g], *rows, bo, idt)
                oraw_sc[g, pl.ds(t0 + j, 1), :] = o_row
        for g in range(g_seqs):
            s_sc[g] = states[g] * gam_sc[g, pl.ds(t0 + RWKV_GROUP - 1, 1), :]
        return carry

    lax.fori_loop(0, lb // RWKV_GROUP, group, 0)

    for g in range(g_seqs):
        o_ref[g] = _rwkv_epilogue(oraw_sc[g], bonus_sc[g], pvec, bo)

    @pl.when(tb == pl.num_programs(1) - 1)
    def _():
        sout_ref[...] = s_sc[...]


_RWKV_TOK_KEYS = ("kap", "ew", "alp", "kh", "vv", "xr")


def _rwkv_tok_kernel(r_ref, k_ref, v_ref, lo_ref, prkv_ref, plo_ref, sin_ref, mu_rkv_ref, mu_lo_ref, wup_ref,
                     pvec_ref, bo_ref, idt_ref, *rest):
    del idt_ref
    o_ref, sout_ref = rest[-10:-8]
    kap_t, ew_t, alp_t, kh_t, v_t, r_t, o_t, bonus_sc = rest[-8:]
    h = pl.program_id(0)
    bo = bo_ref[...]
    pvec = pvec_ref[...]

    @pl.when(h == 0)
    def _():
        prkv = prkv_ref[...]
        res = _rwkv_prologue(r_ref[...], k_ref[...], v_ref[...], lo_ref[...],
                             prkv[:, 0:512], prkv[:, 512:1024], prkv[:, 1024:1536], plo_ref[...],
                             mu_rkv_ref[...], mu_lo_ref[...], wup_ref[...], pvec, bo)
        for key, dst in zip(_RWKV_TOK_KEYS, (kap_t, ew_t, alp_t, kh_t, v_t, r_t)):
            dst[...] = res[key].T
        bonus_sc[...] = res["bonus"]

    row0 = pl.multiple_of(h * RWKV_HEAD, RWKV_HEAD)
    keys = pl.ds(row0, RWKV_HEAD)
    kap, ew, alp, kh, rr = kap_t[keys, :], ew_t[keys, :], alp_t[keys, :], kh_t[keys, :], r_t[keys, :]

    def value_row(v, carry):
        s = sin_ref[v]
        sk = jnp.sum(s * kap, axis=0, keepdims=True)
        s_new = s * ew - sk * alp + v_t[pl.ds(row0 + v, 1), :] * kh
        sout_ref[v] = s_new
        o_t[pl.ds(row0 + v, 1), :] = jnp.sum(s_new * rr, axis=0, keepdims=True)
        return carry

    lax.fori_loop(0, RWKV_HEAD, value_row, 0, unroll=4)

    @pl.when(h == pl.num_programs(0) - 1)
    def _():
        o_ref[...] = _rwkv_epilogue(o_t[...].T, bonus_sc[...], pvec, bo)


def _rwkv_consts():
    i = jnp.arange(W_BRANCH)
    j = jnp.arange(MXU_TILE)
    bo = (j[:, None] // RWKV_HEAD == j[None, :] // RWKV_HEAD).astype(bf16)
    idt = (jnp.arange(RWKV_HEAD)[:, None] == (i[None, :] % RWKV_HEAD)).astype(f32)
    return bo, idt


def _rwkv_weight_specs(l):
    def cs(shape):
        return pl.BlockSpec((None,) + shape, lambda *_: (l,) + (0,) * len(shape))

    def const(shape):
        return pl.BlockSpec(shape, lambda *_: (0,) * len(shape))

    return [cs((1, 1536)), cs((1, LANE)), cs((2, LANE, 1024)), cs((8, W_BRANCH)),
            const((MXU_TILE, MXU_TILE)), const((RWKV_HEAD, W_BRANCH))]


def _rwkv_seq(proj3, prev_rkv, prev_lo, s_in, wts, l, lb):
    b, seq_len, _ = proj3.shape
    g = b
    nt = seq_len // lb

    def col(width, off):
        return pl.BlockSpec((g, lb, width), lambda i, t: (i, t, off // width))

    tok_sc = pltpu.VMEM((g, lb, W_BRANCH), f32)
    out = pl.pallas_call(
        functools.partial(_rwkv_seq_kernel, g_seqs=g, lb=lb),
        grid=(b // g, nt),
        in_specs=[
            col(512, C_R), col(512, C_K), col(512, C_V), col(LANE, C_RWKV_LO),
            pl.BlockSpec((g, 1, 1536), lambda i, t: (i, 0, 0)),
            pl.BlockSpec((g, 1, LANE), lambda i, t: (i, 0, 0)),
            pl.BlockSpec((g, RWKV_HEAD, W_BRANCH), lambda i, t: (i, 0, 0)),
        ] + _rwkv_weight_specs(l),
        out_specs=[
            pl.BlockSpec((g, lb, W_BRANCH), lambda i, t: (i, t, 0)),
            pl.BlockSpec((g, RWKV_HEAD, W_BRANCH), lambda i, t: (i, 0, 0)),
        ],
        out_shape=[
            jax.ShapeDtypeStruct((b, seq_len, W_BRANCH), f32),
            jax.ShapeDtypeStruct((b, RWKV_HEAD, W_BRANCH), f32),
        ],
        scratch_shapes=[
            pltpu.VMEM((g, RWKV_HEAD, W_BRANCH), f32),
            pltpu.VMEM((g, 1, 1536), f32),
            pltpu.VMEM((g, 1, LANE), f32),
        ] + [tok_sc] * 9,
        compiler_params=_cparams(("parallel", "arbitrary")),
        name="rwkv_seq",
    )(proj3, proj3, proj3, proj3, prev_rkv, prev_lo, s_in, *wts)
    return out


def _rwkv_tok(proj2, prev_rkv, prev_lo, s_all, acc, wts, l):
    b = proj2.shape[0]
    assert b == LANE, "the single-token RWKV-7 kernel keeps exactly one lane tile of sequences"

    def col(width, off):
        return pl.BlockSpec((b, width), lambda i: (0, off // width))

    state_spec = pl.BlockSpec((None, None, RWKV_HEAD, RWKV_HEAD, b), lambda i: (l, i, 0, 0, 0))
    in_specs = [
        col(512, C_R), col(512, C_K), col(512, C_V), col(LANE, C_RWKV_LO),
        pl.BlockSpec((None, b, 1536), lambda i: (l, 0, 0)),
        pl.BlockSpec((None, b, LANE), lambda i: (l, 0, 0)),
        state_spec,
    ] + _rwkv_weight_specs(l)
    args = [proj2, proj2, proj2, proj2, prev_rkv, prev_lo, s_all, *wts]
    aliases = {}
    if acc is not None:
        in_specs.append(pl.BlockSpec(memory_space=pl.ANY))
        aliases = {len(args): 1}
        args.append(acc)
    feat_sc = pltpu.VMEM((W_BRANCH, b), f32)
    return pl.pallas_call(
        _rwkv_tok_kernel,
        grid=(RWKV_H,),
        in_specs=in_specs,
        out_specs=[pl.BlockSpec((b, W_BRANCH), lambda i: (0, 0)), state_spec],
        out_shape=[
            jax.ShapeDtypeStruct((b, W_BRANCH), f32),
            jax.ShapeDtypeStruct(s_all.shape, f32),
        ],
        scratch_shapes=[feat_sc] * 7 + [pltpu.VMEM((b, W_BRANCH), f32)],
        input_output_aliases=aliases,
        compiler_params=_cparams(("arbitrary",)),
        name="rwkv_tok",
    )(*args)


def _gla_inputs(q_ref, k_ref, v_ref, lo_ref, gup_ref, gb_ref):
    q = q_ref[...] * (GLA_DK ** -0.5)
    z = _dotf(lo_ref[...], gup_ref[...]) + gb_ref[...]
    g = _log_sigmoid(z) * (1.0 / GLA_GATE_NORM)
    return q, k_ref[...], v_ref[...], g


def _hgrn_lb(logits, l):
    m = jnp.max(logits, axis=0, keepdims=True)
    e = jnp.exp(logits - m)
    sm = e / jnp.sum(e, axis=0, keepdims=True)
    lb = jnp.zeros_like(sm[0:1, :])
    for i in range(1, l + 1):
        lb = lb + sm[i:i + 1, :]
    return lb


def _hgrn_inputs(q_ref, f_ref, i_ref, logits_ref, l):
    lb = _hgrn_lb(logits_ref[...], l)
    f_lo = f_ref[...]
    logf = jnp.log(jnp.maximum(lb, LB_FLOOR) + (1.0 - lb) * _sigmoid(f_lo))
    k = (1.0 - lb) * _sigmoid(-f_lo)
    return q_ref[...], k, i_ref[...], logf


def _chunk_core(q, k, v, g, lt_ref, esum_ref, vmask_ref, gmask_ref, bon_ref, gn_ref, st_sc, kpad, bpad, lb,
                heads_per_group):
    bc = _ones_seg(lt_ref[...], g)
    nc = lb // CHUNK
    f = q.shape[1]
    kpad[...] = k.reshape(nc, CHUNK, f)
    bpad[...] = bc.reshape(nc, CHUNK, f)
    q3 = q.reshape(nc, CHUNK, f)
    bc3 = bc.reshape(nc, CHUNK, f)
    esum = esum_ref[...]

    lane_j = lax.broadcasted_iota(jnp.int32, (1, 1, LANE), 2) & (CHUNK - 1)
    att3 = None
    for r0 in range(0, CHUNK, 8):
        rows = CHUNK - r0
        q_s, bc_s = q3[:, r0:, :], bc3[:, r0:, :]
        rowc8 = lax.broadcasted_iota(jnp.int32, (1, 8, 1), 1) + r0
        acc = jnp.zeros((nc, rows, LANE), f32)
        for j in range(r0, r0 + 8):
            kj, bj = kpad[:, j:j + 1, :], bpad[:, j:j + 1, :]
            z = q_s * kj * jnp.exp(bc_s - bj)
            head = jnp.where(rowc8 >= j, z[:, 0:8, :], 0.0)
            z = head if rows == 8 else jnp.concatenate([head, z[:, 8:, :]], axis=1)
            a = jnp.dot(z.reshape(nc * rows, f).astype(bf16), esum, preferred_element_type=f32)
            acc = acc + jnp.where(lane_j == j, a.reshape(nc, rows, LANE), 0.0)
        if r0:
            acc = jnp.concatenate([jnp.zeros((nc, r0, LANE), f32), acc], axis=1)
        att3 = acc if att3 is None else att3 + acc
    n_heads = LANE // CHUNK
    v3 = v.reshape(nc, CHUNK, W_BRANCH)
    vbd = jnp.concatenate([v3] * n_heads, axis=1) * vmask_ref[...]
    o = lax.dot_general(att3.astype(bf16), vbd.astype(bf16), (((2,), (1,)), ((0,), (0,))),
                        preferred_element_type=f32).reshape(lb, W_BRANCH)

    ng = f // LANE
    vw = W_BRANCH // ng
    sts = [st_sc[gi] for gi in range(ng)]
    outs = []
    for c in range(nc):
        sl = slice(c * CHUNK, (c + 1) * CHUNK)
        bcc = bc[sl]
        blast = bcc[CHUNK - 1:CHUNK, :]
        dec = jnp.exp(blast)
        qe = (q[sl] * jnp.exp(bcc)).astype(bf16)
        ke = (k[sl] * jnp.exp(blast - bcc)).astype(bf16)
        vb = v[sl].astype(bf16)
        parts = []
        for gi in range(ng):
            kl = slice(gi * LANE, (gi + 1) * LANE)
            vl = slice(gi * vw, (gi + 1) * vw)
            parts.append(_dot_nt(qe[:, kl], sts[gi].astype(bf16)))
            upd = _dot_tn(vb[:, vl], ke[:, kl])
            if heads_per_group > 1:
                upd = upd * gmask_ref[...]
            sts[gi] = sts[gi] * dec[:, kl] + upd
        outs.append(o[sl] + jnp.concatenate(parts, axis=1))
    for gi in range(ng):
        st_sc[gi] = sts[gi]
    o = jnp.concatenate(outs, axis=0)
    ms = _seg_sum(o * o, bon_ref[...], pieces=1) * (1.0 / LANE)
    return o * lax.rsqrt(ms + EPS) * gn_ref[...]


def _chunk_seq_kernel(*refs, kind, l, lb):
    n_in = 6 if kind == "gla" else 4
    ins = refs[:n_in]
    lt_ref, esum_ref, vmask_ref, gmask_ref, bon_ref, gn_ref, o_ref, stout_ref, st_sc, kpad, bpad = refs[n_in:]
    tb = pl.program_id(1)

    @pl.when(tb == 0)
    def _():
        st_sc[...] = jnp.zeros_like(st_sc)

    if kind == "gla":
        q, k, v, g = _gla_inputs(*ins)
        heads_per_group = LANE // GLA_DK
    else:
        q, k, v, g = _hgrn_inputs(*ins, l)
        heads_per_group = LANE // HGRN_DK
    o_ref[...] = _chunk_core(q, k, v, g, lt_ref, esum_ref, vmask_ref, gmask_ref, bon_ref, gn_ref, st_sc, kpad,
                             bpad, lb, heads_per_group)

    @pl.when(tb == pl.num_programs(1) - 1)
    def _():
        stout_ref[...] = st_sc[...]


def _chunk_consts(h, dk, lb):
    f = h * dk
    i = jnp.arange(lb)
    lt = ((i[:, None] // CHUNK == i[None, :] // CHUNK) & (i[:, None] >= i[None, :])).astype(bf16)
    fi = jnp.arange(f)
    li = jnp.arange(LANE)
    oi = jnp.arange(W_BRANCH)
    vw = W_BRANCH // (f // LANE)
    esum = (fi[:, None] // dk == li[None, :] // CHUNK).astype(bf16)
    vmask = (li[:, None] // CHUNK == oi[None, :] // LANE).astype(f32)
    gmask = (jnp.arange(vw)[:, None] // LANE == li[None, :] // dk).astype(f32)
    ti = jnp.arange(MXU_TILE)
    bon = (ti[:, None] // LANE == ti[None, :] // LANE).astype(bf16)
    return lt, esum, vmask, gmask, bon


def _chunk_seq(kind, proj3, extra, gn3, l, lb):
    b, seq_len, _ = proj3.shape
    h, dk = (GLA_H, GLA_DK) if kind == "gla" else (HGRN_H, HGRN_DK)
    f = h * dk
    nt = seq_len // lb

    def col(width, off):
        return pl.BlockSpec((None, lb, width), lambda i, t: (i, t, off // width))

    def const(shape):
        return pl.BlockSpec(shape, lambda *_: (0,) * len(shape))

    def layer(shape):
        return pl.BlockSpec((None,) + shape, lambda *_: (l,) + (0,) * len(shape))

    if kind == "gla":
        gup, gb = extra
        in_specs = [col(GLA_KW, C_GLA_Q), col(GLA_KW, C_GLA_K), col(W_BRANCH, C_GLA_V), col(LANE, C_GLA_LO),
                    layer((LANE, GLA_KW)), layer((1, GLA_KW))]
        args = [proj3, proj3, proj3, proj3, gup, gb]
    else:
        (logits,) = extra
        in_specs = [col(W_BRANCH, C_HQ), col(W_BRANCH, C_HF), col(W_BRANCH, C_HI), const((DEPTH, W_BRANCH))]
        args = [proj3, proj3, proj3, logits]
    consts = _chunk_consts(h, dk, lb)
    in_specs += [const(c.shape) for c in consts] + [layer((1, W_BRANCH))]
    ng = f // LANE
    st_shape = (ng, W_BRANCH // ng, LANE)
    return pl.pallas_call(
        functools.partial(_chunk_seq_kernel, kind=kind, l=l, lb=lb),
        grid=(b, nt),
        in_specs=in_specs,
        out_specs=[
            pl.BlockSpec((None, lb, W_BRANCH), lambda i, t: (i, t, 0)),
            pl.BlockSpec((None,) + st_shape, lambda i, t: (i, 0, 0, 0)),
        ],
        out_shape=[
            jax.ShapeDtypeStruct((b, seq_len, W_BRANCH), f32),
            jax.ShapeDtypeStruct((b,) + st_shape, f32),
        ],
        scratch_shapes=[
            pltpu.VMEM(st_shape, f32),
            pltpu.VMEM((lb // CHUNK, CHUNK, f), f32),
            pltpu.VMEM((lb // CHUNK, CHUNK, f), f32),
        ],
        compiler_params=_cparams(("parallel", "arbitrary")),
        name=kind + "_seq",
    )(*args, *consts, gn3)


def _state_from_transposed(st, h, dk):
    b, ng = st.shape[0], st.shape[1]
    hpg = h // ng
    st = st.reshape(b, ng, hpg, LANE, hpg, dk)
    diag = jnp.stack([st[:, :, i, :, i, :] for i in range(hpg)], axis=2)
    return jnp.swapaxes(diag.reshape(b, h, LANE, dk), 2, 3)


def _key_columns(x, g, dk, odd):
    xt = jnp.concatenate([x, jnp.zeros((LANE - g, LANE), f32)], axis=0).T
    if dk == LANE:
        return xt
    return jnp.where(odd, xt[dk:2 * dk], xt[0:dk])


def _chunk_tok_kernel(*refs, kind, l, g, has_acc):
    ins, (o_ref, sout_ref) = refs[:-2], refs[-2:]
    if has_acc:
        ins = ins[:-1]
    odd = (pl.program_id(1) % 2) == 1
    if kind == "gla":
        q_ref, k_ref, v_ref, lo_ref, gup_ref, gb_ref, s_ref, gn_ref = ins
        q = q_ref[...] * (GLA_DK ** -0.5)
        k = k_ref[...]
        z = _dotf(lo_ref[...], gup_ref[...]) + gb_ref[...]
        dec = jnp.exp(_log_sigmoid(z) * (1.0 / GLA_GATE_NORM))
        v = v_ref[...]
        dk = GLA_DK
    else:
        q_ref, f_ref, i_ref, logits_ref, s_ref, gn_ref = ins
        q, k, v, logf = _hgrn_inputs(q_ref, f_ref, i_ref, logits_ref, l)
        dec = jnp.exp(logf)
        dk = HGRN_DK
    qt, kt, dt = (_key_columns(x, g, dk, odd) for x in (q, k, dec))
    o_rows = []
    for b in range(g):
        qcol, kcol, dcol = (jnp.broadcast_to(t[:, b:b + 1], (dk, LANE)) for t in (qt, kt, dt))
        s_new = s_ref[b] * dcol + kcol * v[b:b + 1, :]
        sout_ref[b] = s_new
        o_rows.append(jnp.sum(qcol * s_new, axis=0, keepdims=True))
    o = jnp.concatenate(o_rows, axis=0)
    ms = jnp.mean(o * o, axis=-1, keepdims=True)
    o_ref[...] = o * lax.rsqrt(ms + EPS) * gn_ref[...]


def _chunk_tok(kind, proj2, extra, gn3, s_all, acc, l, g):
    b = proj2.shape[0]
    h, dk = (GLA_H, GLA_DK) if kind == "gla" else (HGRN_H, HGRN_DK)

    def head_col(off):
        per = LANE // dk
        return pl.BlockSpec((g, LANE), lambda i, j: (i, off // LANE + j // per))

    if kind == "gla":
        gup, gb = extra
        in_specs = [head_col(C_GLA_Q), head_col(C_GLA_K),
                    pl.BlockSpec((g, LANE), lambda i, j: (i, C_GLA_V // LANE + j)),
                    pl.BlockSpec((g, LANE), lambda i, j: (i, C_GLA_LO // LANE)),
                    pl.BlockSpec((None, LANE, LANE), lambda i, j: (l, 0, j // 2)),
                    pl.BlockSpec((None, 1, LANE), lambda i, j: (l, 0, j // 2))]
        args = [proj2, proj2, proj2, proj2, gup, gb]
    else:
        (logits,) = extra
        in_specs = [head_col(C_HQ), head_col(C_HF), head_col(C_HI),
                    pl.BlockSpec((DEPTH, LANE), lambda i, j: (0, j))]
        args = [proj2, proj2, proj2, logits]
    state_spec = pl.BlockSpec((None, g, None, dk, LANE), lambda i, j: (l, i, j, 0, 0))
    in_specs += [state_spec, pl.BlockSpec((None, 1, LANE), lambda i, j: (l, 0, j))]
    args += [s_all, gn3]
    aliases = {}
    if acc is not None:
        in_specs.append(pl.BlockSpec(memory_space=pl.ANY))
        aliases = {len(args): 1}
        args.append(acc)
    return pl.pallas_call(
        functools.partial(_chunk_tok_kernel, kind=kind, l=l, g=g, has_acc=acc is not None),
        grid=(b // g, h),
        in_specs=in_specs,
        out_specs=[pl.BlockSpec((g, LANE), lambda i, j: (i, j)), state_spec],
        out_shape=[
            jax.ShapeDtypeStruct((b, W_BRANCH), f32),
            jax.ShapeDtypeStruct(s_all.shape, f32),
        ],
        input_output_aliases=aliases,
        compiler_params=_cparams(("parallel", "parallel")),
        name=kind + "_tok",
    )(*args)


def _lru_gate_dot(y_hi, y_lo, w_ref, col0):
    d = functools.partial(jnp.dot, preferred_element_type=f32)
    halves = []
    for h0 in range(0, W_BRANCH, MXU_TILE):
        w_hi = w_ref[0, h0:h0 + MXU_TILE, col0 + h0:col0 + h0 + MXU_TILE]
        w_lo = w_ref[1, h0:h0 + MXU_TILE, col0 + h0:col0 + h0 + MXU_TILE]
        a, b = y_hi[:, h0:h0 + MXU_TILE], y_lo[:, h0:h0 + MXU_TILE]
        halves.append(d(a, w_hi) + d(a, w_lo) + d(b, w_hi))
    return jnp.concatenate(halves, axis=1)


def _lru_gates(y, w_ref, pvec):
    b_a, b_x, lam = pvec[1:2, :], pvec[2:3, :], pvec[3:4, :]
    y_hi, y_lo = _split(y, 2)
    r = _sigmoid(_lru_gate_dot(y_hi, y_lo, w_ref, 0) + b_a)
    ig = _sigmoid(_lru_gate_dot(y_hi, y_lo, w_ref, W_BRANCH) + b_x)
    log_a = -LRU_C * r * _softplus(-lam)
    a = jnp.exp(log_a)
    one_m_a2 = -jnp.tanh(log_a) * (jnp.exp(2.0 * log_a) + 1.0)
    b = jnp.sqrt(one_m_a2) * (ig * y)
    return a, b


def _lru_seq_kernel(x_ref, cw_ref, pvec_ref, wbd_ref, o_ref, xpad, hcar, *, lb):
    tb = pl.program_id(1)

    @pl.when(tb == 0)
    def _():
        xpad[0:8, :] = jnp.zeros((8, W_BRANCH), f32)
        hcar[...] = jnp.zeros_like(hcar)

    x = x_ref[...]
    xpad[8:8 + lb, :] = x
    cw = cw_ref[...]
    pvec = pvec_ref[...]
    y = pvec[0:1, :] + x * cw[3:4, :]
    for j in range(CONV_W - 1):
        y = y + xpad[5 + j:5 + j + lb, :] * cw[j:j + 1, :]
    a, b = _lru_gates(y, wbd_ref, pvec)
    row = lax.broadcasted_iota(jnp.int32, (lb, 1), 0)
    s = 1
    while s < lb:
        m = row >= s
        b = jnp.where(m, a * pltpu.roll(b, s, 0) + b, b)
        a = jnp.where(m, a * pltpu.roll(a, s, 0), a)
        s *= 2
    h = a * hcar[...] + b
    o_ref[...] = h
    hcar[...] = h[lb - 1:lb, :]
    xpad[5:8, :] = x[lb - 3:lb, :]


def _lru_seq(proj3, cw, pvec, wbd, l, lb):
    b, seq_len, _ = proj3.shape

    def layer(shape):
        return pl.BlockSpec((None,) + shape, lambda *_: (l,) + (0,) * len(shape))

    return pl.pallas_call(
        functools.partial(_lru_seq_kernel, lb=lb),
        grid=(b, seq_len // lb),
        in_specs=[
            pl.BlockSpec((None, lb, W_BRANCH), lambda i, t: (i, t, C_LRU // W_BRANCH)),
            layer((CONV_W, W_BRANCH)), layer((8, W_BRANCH)), layer((2, W_BRANCH, 2 * W_BRANCH)),
        ],
        out_specs=pl.BlockSpec((None, lb, W_BRANCH), lambda i, t: (i, t, 0)),
        out_shape=jax.ShapeDtypeStruct((b, seq_len, W_BRANCH), f32),
        scratch_shapes=[
            pltpu.VMEM((8 + lb, W_BRANCH), f32),
            pltpu.VMEM((1, W_BRANCH), f32),
        ],
        compiler_params=_cparams(("parallel", "arbitrary")),
        name="lru_seq",
    )(proj3, cw, pvec, wbd)


def _lru_tok_kernel(x_ref, b0_ref, b1_ref, b2_ref, h0_ref, cw_ref, pvec_ref, wbd_ref, o_ref):
    cw = cw_ref[...]
    pvec = pvec_ref[...]
    y = (pvec[0:1, :] + b0_ref[...] * cw[0:1, :] + b1_ref[...] * cw[1:2, :] + b2_ref[...] * cw[2:3, :]
         + x_ref[...] * cw[3:4, :])
    a, b = _lru_gates(y, wbd_ref, pvec)
    o_ref[...] = a * h0_ref[...] + b


def _lru_tok(proj2, bufs, h0, cw, pvec, wbd, l):
    b = proj2.shape[0]

    def layer(shape):
        return pl.BlockSpec((None,) + shape, lambda *_: (l,) + (0,) * len(shape))

    row = pl.BlockSpec((b, W_BRANCH), lambda i: (0, 0))
    return pl.pallas_call(
        _lru_tok_kernel,
        grid=(1,),
        in_specs=[pl.BlockSpec((b, W_BRANCH), lambda i: (0, C_LRU // W_BRANCH)), row, row, row, row,
                  layer((CONV_W, W_BRANCH)), layer((8, W_BRANCH)), layer((2, W_BRANCH, 2 * W_BRANCH))],
        out_specs=row,
        out_shape=jax.ShapeDtypeStruct((b, W_BRANCH), f32),
        compiler_params=_cparams(("arbitrary",)),
        name="lru_tok",
    )(proj2, *bufs, h0, cw, pvec, wbd)


def _prep_weights(p):
    w = jnp.swapaxes(p["w_in"], 1, 2)
    row_groups = [(0, 1536), (2176, 2688), (2704, 6800), (1664, 2176), (1536, 1664), (2688, 2704)]
    pad = jnp.zeros((DEPTH, LANE - GLA_GATE_RANK, D_MODEL), bf16)
    w_in_p = jnp.concatenate([w[:, a:b].astype(bf16) for a, b in row_groups] + [pad], axis=1)
    zl = jnp.zeros((DEPTH, RWKV_LORA, W_BRANCH), f32)
    wup = jnp.concatenate([jnp.concatenate([p["rwkv_w_up"], zl], axis=2),
                           jnp.concatenate([zl, p["rwkv_a_up"]], axis=2)], axis=1)
    wup_hi = wup.astype(bf16)
    wup = jnp.stack([wup_hi, (wup - wup_hi.astype(f32)).astype(bf16)], axis=1)
    zrow = jnp.zeros((DEPTH, W_BRANCH), f32)
    rwkv_vec = jnp.stack([p["rwkv_w0"], p["rwkv_a0"], p["rwkv_k_k"], p["rwkv_k_a"],
                          p["rwkv_r_k"].reshape(DEPTH, W_BRANCH), p["rwkv_gn_g"], p["rwkv_gn_b"], zrow], axis=1)
    mu = p["rwkv_mu"]
    gup = jnp.concatenate([p["gla_gk_up"], jnp.zeros((DEPTH, LANE - GLA_GATE_RANK, GLA_KW), f32)], axis=1)
    eye = jnp.eye(LRU_BLOCKS, dtype=f32)

    def bd(wb):
        return jnp.einsum("lhij,hg->lhigj", wb, eye).reshape(DEPTH, W_BRANCH, W_BRANCH)

    lru_vec = jnp.stack([p["lru_conv_b"], p["lru_b_a"], p["lru_b_x"], p["lru_lambda"],
                         zrow, zrow, zrow, zrow], axis=1)
    lru_w = jnp.concatenate([bd(p["lru_w_a"]), bd(p["lru_w_x"])], axis=2)
    lru_w_hi = lru_w.astype(bf16)
    lru_w_lo = (lru_w - lru_w_hi.astype(f32)).astype(bf16)
    return dict(
        w_in_p=w_in_p,
        w_out_b=p["w_out"].astype(bf16),
        norm_g3=p["norm_g"].reshape(DEPTH, 1, D_MODEL),
        final_g2=p["final_g"].reshape(1, D_MODEL),
        rwkv=(mu[:, None, 0:1536], mu[:, None, 1536:1664], wup, rwkv_vec) + _rwkv_consts(),
        gla=(gup, p["gla_gk_b"].reshape(DEPTH, 1, GLA_KW)),
        gla_gn=p["gla_gn_g"].reshape(DEPTH, 1, W_BRANCH),
        hgrn=(p["hgrn_lb_logits"],),
        hgrn_gn=p["hgrn_gn_g"].reshape(DEPTH, 1, W_BRANCH),
        lru=(p["lru_conv_w"], lru_vec, jnp.stack([lru_w_hi, lru_w_lo], axis=1)),
    )


def _shift_state(proj_last):
    return jnp.concatenate([proj_last[..., 0:1536], proj_last[..., C_RWKV_LO:C_RWKV_LO + LANE]], axis=-1)


def _rwkv_state_out(s):
    lead = s.shape[:-2]
    return jnp.swapaxes(s.reshape(lead + (RWKV_HEAD, RWKV_H, RWKV_HEAD)), -3, -2)


def _trunk_seq(x, mod, wts):
    b, seq_len, _ = x.shape
    t = b * seq_len
    lb = min(LB_CHUNK, seq_len)
    tm = min(TM_IN, seq_len)
    per_seq = seq_len // tm
    mod3 = mod.reshape(DEPTH * b * 3, 1, D_MODEL)
    x2 = x.reshape(t, D_MODEL)
    zeros = functools.partial(jnp.zeros, dtype=f32)
    new = []
    h2 = None
    for l in range(DEPTH):
        mod_l = mod3[l * b * 3:(l + 1) * b * 3]
        mod_next = mod3[(l + 1) * b * 3:(l + 2) * b * 3]
        proj2 = _inproj(x2 if l == 0 else h2, wts["norm_g3"], mod_l, wts["w_in_p"], l, per_seq, tm, normed=l > 0)
        proj3 = proj2.reshape(b, seq_len, PROJ_P)
        o_a, s_wkv = _rwkv_seq(proj3, zeros((b, 1, 1536)), zeros((b, 1, LANE)),
                               zeros((b, RWKV_HEAD, W_BRANCH)), wts["rwkv"], l, min(LB_RWKV, seq_len))
        o_b, st_gla = _chunk_seq("gla", proj3, wts["gla"], wts["gla_gn"], l, lb)
        o_c, st_hgrn = _chunk_seq("hgrn", proj3, wts["hgrn"], wts["hgrn_gn"], l, lb)
        o_d = _lru_seq(proj3, *wts["lru"], l, lb)
        outs = [o.reshape(t, W_BRANCH) for o in (o_a, o_b, o_c, o_d)]
        tmo = min(TM_OUT, seq_len)
        res = _outproj(outs, proj2, x2, mod_l, mod_next, wts["w_out_b"], wts["norm_g3"], wts["final_g2"], l,
                       seq_len // tmo, tmo)
        x2, h2 = (res, None) if l == DEPTH - 1 else res
        last = proj3[:, seq_len - 1]
        new.append((
            _shift_state(last),
            _rwkv_state_out(s_wkv),
            _state_from_transposed(st_gla, GLA_H, GLA_DK),
            _state_from_transposed(st_hgrn, HGRN_H, HGRN_DK),
            proj3[:, seq_len - (CONV_W - 1):, C_LRU:C_LRU + W_BRANCH],
            o_d[:, seq_len - 1],
        ))
    return x2.reshape(b, seq_len, D_MODEL), tuple(jnp.stack([n[i] for n in new], axis=0) for i in range(6))


def _trunk_tok(x, mod, states, wts):
    b = x.shape[0]
    s_shift, s_wkv, s_gla, s_hgrn, s_conv, s_h = states
    x2 = x.reshape(b, D_MODEL)
    s_wkv_t = jnp.transpose(s_wkv, (0, 2, 3, 4, 1))
    prev_rkv, prev_lo = s_shift[:, :, 0:1536], s_shift[:, :, 1536:1664]
    n_wkv = n_gla = n_hgrn = None
    new = []
    h2 = None
    for l in range(DEPTH):
        proj2 = _inproj(x2 if l == 0 else h2, wts["norm_g3"], mod, wts["w_in_p"], l, None, b, normed=l > 0)
        o_a, n_wkv = _rwkv_tok(proj2, prev_rkv, prev_lo, s_wkv_t, n_wkv, wts["rwkv"], l)
        o_b, n_gla = _chunk_tok("gla", proj2, wts["gla"], wts["gla_gn"], s_gla, n_gla, l, TOK_GROUP)
        o_c, n_hgrn = _chunk_tok("hgrn", proj2, wts["hgrn"], wts["hgrn_gn"], s_hgrn, n_hgrn, l, TOK_GROUP)
        conv = s_conv[l]
        o_d = _lru_tok(proj2, [conv[:, 0], conv[:, 1], conv[:, 2]], s_h[l], *wts["lru"], l)
        res = _outproj([o_a, o_b, o_c, o_d], proj2, x2, mod, mod, wts["w_out_b"], wts["norm_g3"],
                       wts["final_g2"], l, None, b)
        x2, h2 = (res, None) if l == DEPTH - 1 else res
        n_conv = jnp.concatenate([conv[:, 1:], proj2[:, None, C_LRU:C_LRU + W_BRANCH]], axis=1)
        new.append((_shift_state(proj2), n_conv, o_d))
    n_shift, n_conv, n_h = (jnp.stack([n[i] for n in new], axis=0) for i in range(3))
    n_wkv = jnp.transpose(n_wkv, (0, 4, 1, 2, 3))
    return x2.reshape(b, 1, D_MODEL), (n_shift, n_wkv, n_gla, n_hgrn, n_conv, n_h)


def kernel(x_prompt, x_sample, c_prompt, c_sample, state_rwkv_shift, state_rwkv_wkv, state_gla, state_hgrn, state_lru_conv, state_lru_h, norm_g, w_ada, b_ada, w_in, w_out, rwkv_mu, rwkv_w0, rwkv_w_up, rwkv_a0, rwkv_a_up, rwkv_k_k, rwkv_k_a, rwkv_r_k, rwkv_gn_g, rwkv_gn_b, gla_gk_up, gla_gk_b, gla_gn_g, hgrn_lb_logits, hgrn_gn_g, lru_conv_w, lru_conv_b, lru_w_a, lru_b_a, lru_w_x, lru_b_x, lru_lambda, final_g):
    p = dict(norm_g=norm_g, w_in=w_in, w_out=w_out, rwkv_mu=rwkv_mu, rwkv_w0=rwkv_w0, rwkv_w_up=rwkv_w_up,
             rwkv_a0=rwkv_a0, rwkv_a_up=rwkv_a_up, rwkv_k_k=rwkv_k_k, rwkv_k_a=rwkv_k_a, rwkv_r_k=rwkv_r_k,
             rwkv_gn_g=rwkv_gn_g, rwkv_gn_b=rwkv_gn_b, gla_gk_up=gla_gk_up, gla_gk_b=gla_gk_b,
             gla_gn_g=gla_gn_g, hgrn_lb_logits=hgrn_lb_logits, hgrn_gn_g=hgrn_gn_g, lru_conv_w=lru_conv_w,
             lru_conv_b=lru_conv_b, lru_w_a=lru_w_a, lru_b_a=lru_b_a, lru_w_x=lru_w_x, lru_b_x=lru_b_x,
             lru_lambda=lru_lambda, final_g=final_g)
    wts = _prep_weights(p)
    bp = x_prompt.shape[0]
    bs = x_sample.shape[0]
    pad_rows = (-bp) % 8
    c_all = jnp.concatenate([c_prompt, jnp.zeros((pad_rows, D_MODEL), f32), c_sample], axis=0)
    mod = _ada(c_all, w_ada, b_ada)
    mod_p = mod[:, 0:bp]
    mod_s = mod[:, bp + pad_rows:bp + pad_rows + bs]
    y_p, st_p = _trunk_seq(x_prompt, mod_p, wts)
    states = (state_rwkv_shift, state_rwkv_wkv, state_gla, state_hgrn, state_lru_conv, state_lru_h)
    y_s, st_s = _trunk_tok(x_sample, mod_s, states, wts)
    return (y_p, y_s) + st_p + st_s
```

```python
import functools

import jax
import jax.numpy as jnp
from jax import lax
from jax.experimental import pallas as pl
from jax.experimental.pallas import tpu as pltpu

f32 = jnp.float32
bf16 = jnp.bfloat16

D_MODEL = 2048
DEPTH = 4
W_BRANCH = 512
EPS = 1e-6

RWKV_H = 8
RWKV_HEAD = 64
RWKV_LORA = 64
RWKV_PROJ = 3 * W_BRANCH + 2 * RWKV_LORA
RWKV_GN_EPS = 64e-5

GLA_H = 4
GLA_DK = 64
GLA_DV = 128
GLA_KW = GLA_H * GLA_DK
GLA_GATE_RANK = 16
GLA_GATE_NORM = 16.0

HGRN_H = 4
HGRN_DK = 128
HGRN_DV = 128
LB_FLOOR = 1e-30

LRU_BLOCKS = 8
LRU_BLOCK = 64
LRU_C = 8.0
CONV_W = 4

CHUNK = 32

C_R, C_K, C_V = 0, 512, 1024
C_GLA_V = 1536
C_HQ, C_HF, C_HI = 2048, 2560, 3072
C_LRU = 3584
C_GATE = 4096
C_GLA_Q, C_GLA_K = 6144, 6400
C_RWKV_LO = 6656
C_GLA_LO = 6784
PROJ_P = 6912

LANE = 128
MXU_TILE = 256
VMEM_LIMIT = 56 * 1024 * 1024

TM_IN = 1024
TM_IN_NORMED = 2048
TN_IN = 1152
TM_OUT = 256
LB_CHUNK = 256
LB_RWKV = 128
RWKV_GROUP = 16
TOK_GROUP = 16

HIGHEST = lax.Precision.HIGHEST


def _cparams(sem):
    return pltpu.CompilerParams(dimension_semantics=sem, vmem_limit_bytes=VMEM_LIMIT)


def _dotf(a, b):
    return jnp.dot(a, b, preferred_element_type=f32, precision=HIGHEST)


def _prec(a):
    return HIGHEST if a.dtype == f32 else None


def _dot_nt(a, b):
    return lax.dot_general(a, b, (((1,), (1,)), ((), ())), preferred_element_type=f32, precision=_prec(a))


def _dot_tn(a, b):
    return lax.dot_general(a, b, (((0,), (0,)), ((), ())), preferred_element_type=f32, precision=_prec(a))


def _split(x, pieces):
    out = []
    for i in range(pieces):
        part = x.astype(bf16)
        out.append(part)
        if i + 1 < pieces:
            x = x - part.astype(f32)
    return out


def _half_dot(x, ones):
    d = functools.partial(jnp.dot, preferred_element_type=f32)
    kh = ones.shape[0]
    if x.shape[1] == kh:
        return d(x, ones)
    return jnp.concatenate([d(x[:, 0:kh], ones), d(x[:, kh:2 * kh], ones)], axis=1)


def _seg_sum(x, ones, pieces=2):
    return sum(_half_dot(part, ones) for part in _split(x, pieces))


def _ones_seg(ones, x, pieces=2):
    d = functools.partial(jnp.dot, preferred_element_type=f32)
    return sum(d(ones, part) for part in _split(x, pieces))


def _sigmoid(x):
    return jax.nn.sigmoid(x)


def _softplus(x):
    return jnp.maximum(x, 0.0) + jnp.log1p(jnp.exp(-jnp.abs(x)))


def _log_sigmoid(x):
    return -_softplus(-x)


def _ada_kernel(c_ref, w_ref, b_ref, o_ref):
    c = c_ref[...]
    s = (c * _sigmoid(c)).astype(bf16)
    o_ref[...] = jnp.dot(s, w_ref[...].astype(bf16), preferred_element_type=f32) + b_ref[...]


def _ada(c_all, w_ada, b_ada):
    rows = c_all.shape[0]
    tn = 512
    n = w_ada.shape[2]
    return pl.pallas_call(
        _ada_kernel,
        grid=(DEPTH, n // tn),
        in_specs=[
            pl.BlockSpec((rows, D_MODEL), lambda l, j: (0, 0)),
            pl.BlockSpec((None, D_MODEL, tn), lambda l, j: (l, 0, j)),
            pl.BlockSpec((None, 1, tn), lambda l, j: (l, 0, j)),
        ],
        out_specs=pl.BlockSpec((None, rows, tn), lambda l, j: (l, 0, j)),
        out_shape=jax.ShapeDtypeStruct((DEPTH, rows, n), f32),
        compiler_params=_cparams(("parallel", "parallel")),
        name="ada_mod",
    )(c_all, w_ada, b_ada.reshape(DEPTH, 1, n))


def _norm_mod(x, g, scale, shift):
    ms = jnp.mean(x * x, axis=-1, keepdims=True)
    h = x * lax.rsqrt(ms + EPS) * g
    return (h * (1.0 + scale) + shift).astype(bf16)


def _inproj_kernel(*refs, normed):
    if normed:
        h_ref, w_ref, o_ref = refs
    else:
        x_ref, g_ref, sc_ref, sh_ref, w_ref, o_ref, h_ref = refs

        @pl.when(pl.program_id(1) == 0)
        def _():
            h_ref[...] = _norm_mod(x_ref[...], g_ref[...], sc_ref[...], sh_ref[...])

    o_ref[...] = _dot_nt(h_ref[...], w_ref[...])


def _mod_spec(l, which, per_seq_tiles, tm):
    if per_seq_tiles is None:
        return pl.BlockSpec((None, tm, D_MODEL), lambda i, *_: (l, 0, which))
    return pl.BlockSpec((None, 1, D_MODEL), lambda i, *_: (i // per_seq_tiles * 3 + which, 0, 0))


def _inproj(x2, norm_g3, mod, w_in_p, l, per_seq_tiles, tm, normed):
    t = x2.shape[0]
    tn = TN_IN
    row_spec = pl.BlockSpec((tm, D_MODEL), lambda i, j: (i, 0))
    w_spec = pl.BlockSpec((None, tn, D_MODEL), lambda i, j: (l, j, 0))
    if normed:
        in_specs, args, scratch = [row_spec, w_spec], (x2, w_in_p), []
    else:
        in_specs = [row_spec, pl.BlockSpec((None, 1, D_MODEL), lambda i, j: (l, 0, 0)),
                    _mod_spec(l, 1, per_seq_tiles, tm), _mod_spec(l, 0, per_seq_tiles, tm), w_spec]
        args, scratch = (x2, norm_g3, mod, mod, w_in_p), [pltpu.VMEM((tm, D_MODEL), bf16)]
    return pl.pallas_call(
        functools.partial(_inproj_kernel, normed=normed),
        grid=(t // tm, PROJ_P // tn),
        in_specs=in_specs,
        out_specs=pl.BlockSpec((tm, tn), lambda i, j: (i, j)),
        out_shape=jax.ShapeDtypeStruct((t, PROJ_P), f32),
        scratch_shapes=scratch,
        compiler_params=_cparams(("parallel", "arbitrary")),
        name="in_proj",
    )(*args)


def _outproj_kernel(oa_ref, ob_ref, oc_ref, od_ref, pz_ref, x_ref, gate_ref, w_ref, g_ref, *rest, final):
    z = pz_ref[...]
    o = jnp.concatenate([oa_ref[...], ob_ref[...], oc_ref[...], od_ref[...]], axis=1)
    o = o * (z * _sigmoid(z))
    y = jnp.dot(o.astype(bf16), w_ref[...], preferred_element_type=f32)
    xn = x_ref[...] + gate_ref[...] * y
    if final:
        (o_ref,) = rest
        ms = jnp.mean(xn * xn, axis=-1, keepdims=True)
        o_ref[...] = xn * lax.rsqrt(ms + EPS) * g_ref[...]
    else:
        sc_ref, sh_ref, o_ref, h_ref = rest
        o_ref[...] = xn
        h_ref[...] = _norm_mod(xn, g_ref[...], sc_ref[...], sh_ref[...])


def _outproj(outs, proj2, x2, mod, mod_next, w_out_b, norm_g3, final_g2, l, per_seq_tiles, tm):
    t = x2.shape[0]
    final = l == DEPTH - 1
    mix_spec = pl.BlockSpec((tm, W_BRANCH), lambda i: (i, 0))
    row_spec = pl.BlockSpec((tm, D_MODEL), lambda i: (i, 0))
    in_specs = [
        mix_spec, mix_spec, mix_spec, mix_spec,
        pl.BlockSpec((tm, D_MODEL), lambda i: (i, C_GATE // D_MODEL)),
        row_spec,
        _mod_spec(l, 2, per_seq_tiles, tm),
        pl.BlockSpec((None, D_MODEL, D_MODEL), lambda i: (l, 0, 0)),
    ]
    args = [*outs, proj2, x2, mod, w_out_b]
    if final:
        in_specs.append(pl.BlockSpec((1, D_MODEL), lambda i: (0, 0)))
        args.append(final_g2)
        out_specs, out_shape = row_spec, jax.ShapeDtypeStruct((t, D_MODEL), f32)
    else:
        in_specs += [pl.BlockSpec((None, 1, D_MODEL), lambda i: (l + 1, 0, 0)),
                     _mod_spec(l + 1, 1, per_seq_tiles, tm), _mod_spec(l + 1, 0, per_seq_tiles, tm)]
        args += [norm_g3, mod_next, mod_next]
        out_specs = [row_spec, row_spec]
        out_shape = [jax.ShapeDtypeStruct((t, D_MODEL), f32), jax.ShapeDtypeStruct((t, D_MODEL), bf16)]
    return pl.pallas_call(
        functools.partial(_outproj_kernel, final=final),
        grid=(t // tm,),
        in_specs=in_specs,
        out_specs=out_specs,
        out_shape=out_shape,
        compiler_params=_cparams(("parallel",)),
        name="out_proj",
    )(*args)


def _rwkv_prologue(r, k, v, lo, pr, pk, pv, plo, mu_rkv, mu_lo, wup, pvec, bo, sub_block=0):
    w0, a0, k_k, k_a, r_k = (pvec[i:i + 1, :] for i in range(5))
    xr = r + (pr - r) * mu_rkv[:, 0:512]
    xk = k + (pk - k) * mu_rkv[:, 512:1024]
    xv = v + (pv - v) * mu_rkv[:, 1024:1536]
    xlo = lo + (plo - lo) * mu_lo
    lane = lax.broadcasted_iota(jnp.int32, xlo.shape, 1)
    act = jnp.where(lane < RWKV_LORA, jnp.tanh(xlo), xlo)
    act_hi, act_lo = _split(act, 2)
    d = functools.partial(jnp.dot, preferred_element_type=f32)
    up = d(act_hi, wup[0]) + d(act_hi, wup[1]) + d(act_lo, wup[0])
    w_raw = -_softplus(-(w0 + up[:, 0:512])) - 0.5
    logw = -jnp.exp(w_raw)
    a = _sigmoid(a0 + up[:, 512:1024])
    kk = xk * k_k
    kk = kk / jnp.maximum(jnp.sqrt(_seg_sum(kk * kk, bo)), 1e-12)
    kh = xk * (1.0 + (a - 1.0) * k_a)
    alp = kk * a
    ar = _seg_sum(alp * xr, bo, pieces=1)
    out = dict(vv=xv, xr=xr, kr=_seg_sum(kh * xr, bo, pieces=1),
               bonus=_seg_sum(xr * kh * r_k, bo, pieces=1) * xv)
    if not sub_block:
        ew = jnp.exp(logw)
        out.update(kap=kk, alp=alp, kh=kh, ew=ew, wr=ew * xr - ar * kk)
        return out
    rowi = lax.broadcasted_iota(jnp.int32, (logw.shape[0], 1), 0) & (sub_block - 1)
    lc = logw
    step = 1
    while step < sub_block:
        lc = lc + jnp.where(rowi >= step, pltpu.roll(lc, step, 0), 0.0)
        step *= 2
    gam, ginv = jnp.exp(lc), jnp.exp(-lc)
    kap = kk * jnp.exp(lc - logw)
    out.update(kap=kap, alp=alp * ginv, kh=kh * ginv, gam=gam, wr=gam * xr - ar * kap)
    return out


_RWKV_STEP_KEYS = ("kap", "alp", "kh", "vv", "wr", "kr")


def _rwkv_step(s, kap, alp, kh, vv, wr, kr, bo, idt):
    s_new, o_row = [], []
    for c0 in range(0, W_BRANCH, MXU_TILE):
        sl = slice(c0, c0 + MXU_TILE)
        s_h, idt_h, v_h = s[:, sl], idt[:, sl], vv[:, sl]
        s_b = s_h.astype(bf16)
        x = jnp.concatenate([s_b * kap[:, sl].astype(bf16), s_b * wr[:, sl].astype(bf16),
                             idt_h.astype(bf16) * v_h.astype(bf16)], axis=0)
        red = jnp.dot(x, bo, preferred_element_type=f32)
        s_new.append(s_h - red[0:64] * alp[:, sl] + red[128:192] * kh[:, sl])
        o_row.append(jnp.sum(red[64:128] * idt_h, axis=0, keepdims=True) + v_h * kr[:, sl])
    return jnp.concatenate(s_new, axis=1), jnp.concatenate(o_row, axis=1)


def _rwkv_epilogue(o, bonus, pvec, bo):
    gn_g, gn_b = pvec[5:6, :], pvec[6:7, :]
    mu = _seg_sum(o, bo, pieces=1) * (1.0 / RWKV_HEAD)
    d = o - mu
    var = _seg_sum(d * d, bo, pieces=1) * (1.0 / RWKV_HEAD)
    return d * lax.rsqrt(var + RWKV_GN_EPS) * gn_g + gn_b + bonus


def _rwkv_seq_kernel(r_ref, k_ref, v_ref, lo_ref, prkv_ref, plo_ref, sin_ref, mu_rkv_ref, mu_lo_ref, wup_ref,
                     pvec_ref, bo_ref, idt_ref, o_ref, sout_ref,
                     s_sc, crkv_sc, clo_sc, kap_sc, gam_sc, alp_sc, kh_sc, vv_sc, wr_sc, kr_sc, oraw_sc,
                     bonus_sc, *, g_seqs, lb):
    tb = pl.program_id(1)

    @pl.when(tb == 0)
    def _():
        s_sc[...] = sin_ref[...]
        crkv_sc[...] = prkv_ref[...]
        clo_sc[...] = plo_ref[...]

    bo = bo_ref[...]
    idt = idt_ref[...]
    pvec = pvec_ref[...]
    step_sc = dict(kap=kap_sc, alp=alp_sc, kh=kh_sc, vv=vv_sc, wr=wr_sc, kr=kr_sc)

    row0 = lax.broadcasted_iota(jnp.int32, (lb, 1), 0) == 0
    for g in range(g_seqs):
        cur = [r_ref[g], k_ref[g], v_ref[g]]
        lo = lo_ref[g]
        carry = crkv_sc[g]
        prev = [jnp.where(row0, carry[:, i * 512:(i + 1) * 512], pltpu.roll(c, 1, 0)) for i, c in enumerate(cur)]
        plo = jnp.where(row0, clo_sc[g], pltpu.roll(lo, 1, 0))
        res = _rwkv_prologue(cur[0], cur[1], cur[2], lo, prev[0], prev[1], prev[2], plo,
                             mu_rkv_ref[...], mu_lo_ref[...], wup_ref[...], pvec, bo, sub_block=RWKV_GROUP)
        for key in _RWKV_STEP_KEYS:
            step_sc[key][g] = res[key]
        gam_sc[g] = res["gam"]
        bonus_sc[g] = res["bonus"]
        for i, c in enumerate(cur):
            crkv_sc[g, :, i * 512:(i + 1) * 512] = c[lb - 1:lb, :]
        clo_sc[g] = lo[lb - 1:lb, :]

    def group(i, carry):
        t0 = pl.multiple_of(i * RWKV_GROUP, RWKV_GROUP)
        states = [s_sc[g] for g in range(g_seqs)]
        for j in range(RWKV_GROUP):
            for g in range(g_seqs):
                rows = [step_sc[key][g, pl.ds(t0 + j, 1), :] for key in _RWKV_STEP_KEYS]
                states[g], o_row = _rwkv_step(states[g], *rows, bo, idt)
                oraw_sc[g, pl.ds(t0 + j, 1), :] = o_row
        for g in range(g_seqs):
            s_sc[g] = states[g] * gam_sc[g, pl.ds(t0 + RWKV_GROUP - 1, 1), :]
        return carry

    lax.fori_loop(0, lb // RWKV_GROUP, group, 0)

    for g in range(g_seqs):
        o_ref[g] = _rwkv_epilogue(oraw_sc[g], bonus_sc[g], pvec, bo)

    @pl.when(tb == pl.num_programs(1) - 1)
    def _():
        sout_ref[...] = s_sc[...]


_RWKV_TOK_KEYS = ("kap", "ew", "alp", "kh", "vv", "xr")


def _rwkv_tok_kernel(r_ref, k_ref, v_ref, lo_ref, prkv_ref, plo_ref, sin_ref, mu_rkv_ref, mu_lo_ref, wup_ref,
                     pvec_ref, bo_ref, idt_ref, *rest):
    del idt_ref
    o_ref, sout_ref = rest[-10:-8]
    kap_t, ew_t, alp_t, kh_t, v_t, r_t, o_t, bonus_sc = rest[-8:]
    h = pl.program_id(0)
    bo = bo_ref[...]
    pvec = pvec_ref[...]

    @pl.when(h == 0)
    def _():
        prkv = prkv_ref[...]
        res = _rwkv_prologue(r_ref[...], k_ref[...], v_ref[...], lo_ref[...],
                             prkv[:, 0:512], prkv[:, 512:1024], prkv[:, 1024:1536], plo_ref[...],
                             mu_rkv_ref[...], mu_lo_ref[...], wup_ref[...], pvec, bo)
        for key, dst in zip(_RWKV_TOK_KEYS, (kap_t, ew_t, alp_t, kh_t, v_t, r_t)):
            dst[...] = res[key].T
        bonus_sc[...] = res["bonus"]

    row0 = pl.multiple_of(h * RWKV_HEAD, RWKV_HEAD)
    keys = pl.ds(row0, RWKV_HEAD)
    kap, ew, alp, kh, rr = kap_t[keys, :], ew_t[keys, :], alp_t[keys, :], kh_t[keys, :], r_t[keys, :]

    def value_row(v, carry):
        s = sin_ref[v]
        sk = jnp.sum(s * kap, axis=0, keepdims=True)
        s_new = s * ew - sk * alp + v_t[pl.ds(row0 + v, 1), :] * kh
        sout_ref[v] = s_new
        o_t[pl.ds(row0 + v, 1), :] = jnp.sum(s_new * rr, axis=0, keepdims=True)
        return carry

    lax.fori_loop(0, RWKV_HEAD, value_row, 0, unroll=4)

    @pl.when(h == pl.num_programs(0) - 1)
    def _():
        o_ref[...] = _rwkv_epilogue(o_t[...].T, bonus_sc[...], pvec, bo)


def _rwkv_consts():
    i = jnp.arange(W_BRANCH)
    j = jnp.arange(MXU_TILE)
    bo = (j[:, None] // RWKV_HEAD == j[None, :] // RWKV_HEAD).astype(bf16)
    idt = (jnp.arange(RWKV_HEAD)[:, None] == (i[None, :] % RWKV_HEAD)).astype(f32)
    return bo, idt


def _rwkv_weight_specs(l):
    def cs(shape):
        return pl.BlockSpec((None,) + shape, lambda *_: (l,) + (0,) * len(shape))

    def const(shape):
        return pl.BlockSpec(shape, lambda *_: (0,) * len(shape))

    return [cs((1, 1536)), cs((1, LANE)), cs((2, LANE, 1024)), cs((8, W_BRANCH)),
            const((MXU_TILE, MXU_TILE)), const((RWKV_HEAD, W_BRANCH))]


def _rwkv_seq(proj3, prev_rkv, prev_lo, s_in, wts, l, lb):
    b, seq_len, _ = proj3.shape
    g = b
    nt = seq_len // lb

    def col(width, off):
        return pl.BlockSpec((g, lb, width), lambda i, t: (i, t, off // width))

    tok_sc = pltpu.VMEM((g, lb, W_BRANCH), f32)
    out = pl.pallas_call(
        functools.partial(_rwkv_seq_kernel, g_seqs=g, lb=lb),
        grid=(b // g, nt),
        in_specs=[
            col(512, C_R), col(512, C_K), col(512, C_V), col(LANE, C_RWKV_LO),
            pl.BlockSpec((g, 1, 1536), lambda i, t: (i, 0, 0)),
            pl.BlockSpec((g, 1, LANE), lambda i, t: (i, 0, 0)),
            pl.BlockSpec((g, RWKV_HEAD, W_BRANCH), lambda i, t: (i, 0, 0)),
        ] + _rwkv_weight_specs(l),
        out_specs=[
            pl.BlockSpec((g, lb, W_BRANCH), lambda i, t: (i, t, 0)),
            pl.BlockSpec((g, RWKV_HEAD, W_BRANCH), lambda i, t: (i, 0, 0)),
        ],
        out_shape=[
            jax.ShapeDtypeStruct((b, seq_len, W_BRANCH), f32),
            jax.ShapeDtypeStruct((b, RWKV_HEAD, W_BRANCH), f32),
        ],
        scratch_shapes=[
            pltpu.VMEM((g, RWKV_HEAD, W_BRANCH), f32),
            pltpu.VMEM((g, 1, 1536), f32),
            pltpu.VMEM((g, 1, LANE), f32),
        ] + [tok_sc] * 9,
        compiler_params=_cparams(("parallel", "arbitrary")),
        name="rwkv_seq",
    )(proj3, proj3, proj3, proj3, prev_rkv, prev_lo, s_in, *wts)
    return out


def _rwkv_tok(proj2, prev_rkv, prev_lo, s_all, acc, wts, l):
    b = proj2.shape[0]
    assert b == LANE, "the single-token RWKV-7 kernel keeps exactly one lane tile of sequences"

    def col(width, off):
        return pl.BlockSpec((b, width), lambda i: (0, off // width))

    state_spec = pl.BlockSpec((None, None, RWKV_HEAD, RWKV_HEAD, b), lambda i: (l, i, 0, 0, 0))
    in_specs = [
        col(512, C_R), col(512, C_K), col(512, C_V), col(LANE, C_RWKV_LO),
        pl.BlockSpec((None, b, 1536), lambda i: (l, 0, 0)),
        pl.BlockSpec((None, b, LANE), lambda i: (l, 0, 0)),
        state_spec,
    ] + _rwkv_weight_specs(l)
    args = [proj2, proj2, proj2, proj2, prev_rkv, prev_lo, s_all, *wts]
    aliases = {}
    if acc is not None:
        in_specs.append(pl.BlockSpec(memory_space=pl.ANY))
        aliases = {len(args): 1}
        args.append(acc)
    feat_sc = pltpu.VMEM((W_BRANCH, b), f32)
    return pl.pallas_call(
        _rwkv_tok_kernel,
        grid=(RWKV_H,),
        in_specs=in_specs,
        out_specs=[pl.BlockSpec((b, W_BRANCH), lambda i: (0, 0)), state_spec],
        out_shape=[
            jax.ShapeDtypeStruct((b, W_BRANCH), f32),
            jax.ShapeDtypeStruct(s_all.shape, f32),
        ],
        scratch_shapes=[feat_sc] * 7 + [pltpu.VMEM((b, W_BRANCH), f32)],
        input_output_aliases=aliases,
        compiler_params=_cparams(("arbitrary",)),
        name="rwkv_tok",
    )(*args)


def _gla_inputs(q_ref, k_ref, v_ref, lo_ref, gup_ref, gb_ref):
    q = q_ref[...] * (GLA_DK ** -0.5)
    z = _dotf(lo_ref[...], gup_ref[...]) + gb_ref[...]
    g = _log_sigmoid(z) * (1.0 / GLA_GATE_NORM)
    return q, k_ref[...], v_ref[...], g


def _hgrn_lb(logits, l):
    m = jnp.max(logits, axis=0, keepdims=True)
    e = jnp.exp(logits - m)
    sm = e / jnp.sum(e, axis=0, keepdims=True)
    lb = jnp.zeros_like(sm[0:1, :])
    for i in range(1, l + 1):
        lb = lb + sm[i:i + 1, :]
    return lb


def _hgrn_inputs(q_ref, f_ref, i_ref, logits_ref, l):
    lb = _hgrn_lb(logits_ref[...], l)
    f_lo = f_ref[...]
    logf = jnp.log(jnp.maximum(lb, LB_FLOOR) + (1.0 - lb) * _sigmoid(f_lo))
    k = (1.0 - lb) * _sigmoid(-f_lo)
    return q_ref[...], k, i_ref[...], logf


def _chunk_core(q, k, v, g, lt_ref, esum_ref, vmask_ref, gmask_ref, bon_ref, gn_ref, st_sc, kpad, bpad, lb,
                heads_per_group):
    bc = _ones_seg(lt_ref[...], g)
    nc = lb // CHUNK
    f = q.shape[1]
    kpad[...] = k.reshape(nc, CHUNK, f)
    bpad[...] = bc.reshape(nc, CHUNK, f)
    q3 = q.reshape(nc, CHUNK, f)
    bc3 = bc.reshape(nc, CHUNK, f)
    esum = esum_ref[...]

    lane_j = lax.broadcasted_iota(jnp.int32, (1, 1, LANE), 2) & (CHUNK - 1)
    att3 = None
    for r0 in range(0, CHUNK, 8):
        rows = CHUNK - r0
        q_s, bc_s = q3[:, r0:, :], bc3[:, r0:, :]
        rowc8 = lax.broadcasted_iota(jnp.int32, (1, 8, 1), 1) + r0
        acc = jnp.zeros((nc, rows, LANE), f32)
        for j in range(r0, r0 + 8):
            kj, bj = kpad[:, j:j + 1, :], bpad[:, j:j + 1, :]
            z = q_s * kj * jnp.exp(bc_s - bj)
            head = jnp.where(rowc8 >= j, z[:, 0:8, :], 0.0)
            z = head if rows == 8 else jnp.concatenate([head, z[:, 8:, :]], axis=1)
            a = jnp.dot(z.reshape(nc * rows, f).astype(bf16), esum, preferred_element_type=f32)
            acc = acc + jnp.where(lane_j == j, a.reshape(nc, rows, LANE), 0.0)
        if r0:
            acc = jnp.concatenate([jnp.zeros((nc, r0, LANE), f32), acc], axis=1)
        att3 = acc if att3 is None else att3 + acc
    n_heads = LANE // CHUNK
    v3 = v.reshape(nc, CHUNK, W_BRANCH)
    vbd = jnp.concatenate([v3] * n_heads, axis=1) * vmask_ref[...]
    o = lax.dot_general(att3.astype(bf16), vbd.astype(bf16), (((2,), (1,)), ((0,), (0,))),
                        preferred_element_type=f32).reshape(lb, W_BRANCH)

    ng = f // LANE
    vw = W_BRANCH // ng
    sts = [st_sc[gi] for gi in range(ng)]
    outs = []
    for c in range(nc):
        sl = slice(c * CHUNK, (c + 1) * CHUNK)
        bcc = bc[sl]
        blast = bcc[CHUNK - 1:CHUNK, :]
        dec = jnp.exp(blast)
        qe = (q[sl] * jnp.exp(bcc)).astype(bf16)
        ke = (k[sl] * jnp.exp(blast - bcc)).astype(bf16)
        vb = v[sl].astype(bf16)
        parts = []
        for gi in range(ng):
            kl = slice(gi * LANE, (gi + 1) * LANE)
            vl = slice(gi * vw, (gi + 1) * vw)
            parts.append(_dot_nt(qe[:, kl], sts[gi].astype(bf16)))
            upd = _dot_tn(vb[:, vl], ke[:, kl])
            if heads_per_group > 1:
                upd = upd * gmask_ref[...]
            sts[gi] = sts[gi] * dec[:, kl] + upd
        outs.append(o[sl] + jnp.concatenate(parts, axis=1))
    for gi in range(ng):
        st_sc[gi] = sts[gi]
    o = jnp.concatenate(outs, axis=0)
    ms = _seg_sum(o * o, bon_ref[...], pieces=1) * (1.0 / LANE)
    return o * lax.rsqrt(ms + EPS) * gn_ref[...]


def _chunk_seq_kernel(*refs, kind, l, lb):
    n_in = 6 if kind == "gla" else 4
    ins = refs[:n_in]
    lt_ref, esum_ref, vmask_ref, gmask_ref, bon_ref, gn_ref, o_ref, stout_ref, st_sc, kpad, bpad = refs[n_in:]
    tb = pl.program_id(1)

    @pl.when(tb == 0)
    def _():
        st_sc[...] = jnp.zeros_like(st_sc)

    if kind == "gla":
        q, k, v, g = _gla_inputs(*ins)
        heads_per_group = LANE // GLA_DK
    else:
        q, k, v, g = _hgrn_inputs(*ins, l)
        heads_per_group = LANE // HGRN_DK
    o_ref[...] = _chunk_core(q, k, v, g, lt_ref, esum_ref, vmask_ref, gmask_ref, bon_ref, gn_ref, st_sc, kpad,
                             bpad, lb, heads_per_group)

    @pl.when(tb == pl.num_programs(1) - 1)
    def _():
        stout_ref[...] = st_sc[...]


def _chunk_consts(h, dk, lb):
    f = h * dk
    i = jnp.arange(lb)
    lt = ((i[:, None] // CHUNK == i[None, :] // CHUNK) & (i[:, None] >= i[None, :])).astype(bf16)
    fi = jnp.arange(f)
    li = jnp.arange(LANE)
    oi = jnp.arange(W_BRANCH)
    vw = W_BRANCH // (f // LANE)
    esum = (fi[:, None] // dk == li[None, :] // CHUNK).astype(bf16)
    vmask = (li[:, None] // CHUNK == oi[None, :] // LANE).astype(f32)
    gmask = (jnp.arange(vw)[:, None] // LANE == li[None, :] // dk).astype(f32)
    ti = jnp.arange(MXU_TILE)
    bon = (ti[:, None] // LANE == ti[None, :] // LANE).astype(bf16)
    return lt, esum, vmask, gmask, bon


def _chunk_seq(kind, proj3, extra, gn3, l, lb):
    b, seq_len, _ = proj3.shape
    h, dk = (GLA_H, GLA_DK) if kind == "gla" else (HGRN_H, HGRN_DK)
    f = h * dk
    nt = seq_len // lb

    def col(width, off):
        return pl.BlockSpec((None, lb, width), lambda i, t: (i, t, off // width))

    def const(shape):
        return pl.BlockSpec(shape, lambda *_: (0,) * len(shape))

    def layer(shape):
        return pl.BlockSpec((None,) + shape, lambda *_: (l,) + (0,) * len(shape))

    if kind == "gla":
        gup, gb = extra
        in_specs = [col(GLA_KW, C_GLA_Q), col(GLA_KW, C_GLA_K), col(W_BRANCH, C_GLA_V), col(LANE, C_GLA_LO),
                    layer((LANE, GLA_KW)), layer((1, GLA_KW))]
        args = [proj3, proj3, proj3, proj3, gup, gb]
    else:
        (logits,) = extra
        in_specs = [col(W_BRANCH, C_HQ), col(W_BRANCH, C_HF), col(W_BRANCH, C_HI), const((DEPTH, W_BRANCH))]
        args = [proj3, proj3, proj3, logits]
    consts = _chunk_consts(h, dk, lb)
    in_specs += [const(c.shape) for c in consts] + [layer((1, W_BRANCH))]
    ng = f // LANE
    st_shape = (ng, W_BRANCH // ng, LANE)
    return pl.pallas_call(
        functools.partial(_chunk_seq_kernel, kind=kind, l=l, lb=lb),
        grid=(b, nt),
        in_specs=in_specs,
        out_specs=[
            pl.BlockSpec((None, lb, W_BRANCH), lambda i, t: (i, t, 0)),
            pl.BlockSpec((None,) + st_shape, lambda i, t: (i, 0, 0, 0)),
        ],
        out_shape=[
            jax.ShapeDtypeStruct((b, seq_len, W_BRANCH), f32),
            jax.ShapeDtypeStruct((b,) + st_shape, f32),
        ],
        scratch_shapes=[
            pltpu.VMEM(st_shape, f32),
            pltpu.VMEM((lb // CHUNK, CHUNK, f), f32),
            pltpu.VMEM((lb // CHUNK, CHUNK, f), f32),
        ],
        compiler_params=_cparams(("parallel", "arbitrary")),
        name=kind + "_seq",
    )(*args, *consts, gn3)


def _state_from_transposed(st, h, dk):
    b, ng = st.shape[0], st.shape[1]
    hpg = h // ng
    st = st.reshape(b, ng, hpg, LANE, hpg, dk)
    diag = jnp.stack([st[:, :, i, :, i, :] for i in range(hpg)], axis=2)
    return jnp.swapaxes(diag.reshape(b, h, LANE, dk), 2, 3)


def _key_columns(x, g, dk, odd):
    xt = jnp.concatenate([x, jnp.zeros((LANE - g, LANE), f32)], axis=0).T
    if dk == LANE:
        return xt
    return jnp.where(odd, xt[dk:2 * dk], xt[0:dk])


def _chunk_tok_kernel(*refs, kind, l, g, has_acc):
    ins, (o_ref, sout_ref) = refs[:-2], refs[-2:]
    if has_acc:
        ins = ins[:-1]
    odd = (pl.program_id(1) % 2) == 1
    if kind == "gla":
        q_ref, k_ref, v_ref, lo_ref, gup_ref, gb_ref, s_ref, gn_ref = ins
        q = q_ref[...] * (GLA_DK ** -0.5)
        k = k_ref[...]
        z = _dotf(lo_ref[...], gup_ref[...]) + gb_ref[...]
        dec = jnp.exp(_log_sigmoid(z) * (1.0 / GLA_GATE_NORM))
        v = v_ref[...]
        dk = GLA_DK
    else:
        q_ref, f_ref, i_ref, logits_ref, s_ref, gn_ref = ins
        q, k, v, logf = _hgrn_inputs(q_ref, f_ref, i_ref, logits_ref, l)
        dec = jnp.exp(logf)
        dk = HGRN_DK
    qt, kt, dt = (_key_columns(x, g, dk, odd) for x in (q, k, dec))
    o_rows = []
    for b in range(g):
        qcol, kcol, dcol = (jnp.broadcast_to(t[:, b:b + 1], (dk, LANE)) for t in (qt, kt, dt))
        s_new = s_ref[b] * dcol + kcol * v[b:b + 1, :]
        sout_ref[b] = s_new
        o_rows.append(jnp.sum(qcol * s_new, axis=0, keepdims=True))
    o = jnp.concatenate(o_rows, axis=0)
    ms = jnp.mean(o * o, axis=-1, keepdims=True)
    o_ref[...] = o * lax.rsqrt(ms + EPS) * gn_ref[...]


def _chunk_tok(kind, proj2, extra, gn3, s_all, acc, l, g):
    b = proj2.shape[0]
    h, dk = (GLA_H, GLA_DK) if kind == "gla" else (HGRN_H, HGRN_DK)

    def head_col(off):
        per = LANE // dk
        return pl.BlockSpec((g, LANE), lambda i, j: (i, off // LANE + j // per))

    if kind == "gla":
        gup, gb = extra
        in_specs = [head_col(C_GLA_Q), head_col(C_GLA_K),
                    pl.BlockSpec((g, LANE), lambda i, j: (i, C_GLA_V // LANE + j)),
                    pl.BlockSpec((g, LANE), lambda i, j: (i, C_GLA_LO // LANE)),
                    pl.BlockSpec((None, LANE, LANE), lambda i, j: (l, 0, j // 2)),
                    pl.BlockSpec((None, 1, LANE), lambda i, j: (l, 0, j // 2))]
        args = [proj2, proj2, proj2, proj2, gup, gb]
    else:
        (logits,) = extra
        in_specs = [head_col(C_HQ), head_col(C_HF), head_col(C_HI),
                    pl.BlockSpec((DEPTH, LANE), lambda i, j: (0, j))]
        args = [proj2, proj2, proj2, logits]
    state_spec = pl.BlockSpec((None, g, None, dk, LANE), lambda i, j: (l, i, j, 0, 0))
    in_specs += [state_spec, pl.BlockSpec((None, 1, LANE), lambda i, j: (l, 0, j))]
    args += [s_all, gn3]
    aliases = {}
    if acc is not None:
        in_specs.append(pl.BlockSpec(memory_space=pl.ANY))
        aliases = {len(args): 1}
        args.append(acc)
    return pl.pallas_call(
        functools.partial(_chunk_tok_kernel, kind=kind, l=l, g=g, has_acc=acc is not None),
        grid=(b // g, h),
        in_specs=in_specs,
        out_specs=[pl.BlockSpec((g, LANE), lambda i, j: (i, j)), state_spec],
        out_shape=[
            jax.ShapeDtypeStruct((b, W_BRANCH), f32),
            jax.ShapeDtypeStruct(s_all.shape, f32),
        ],
        input_output_aliases=aliases,
        compiler_params=_cparams(("parallel", "parallel")),
        name=kind + "_tok",
    )(*args)


def _lru_gate_dot(y_hi, y_lo, w_ref, col0):
    d = functools.partial(jnp.dot, preferred_element_type=f32)
    halves = []
    for h0 in range(0, W_BRANCH, MXU_TILE):
        w_hi = w_ref[0, h0:h0 + MXU_TILE, col0 + h0:col0 + h0 + MXU_TILE]
        w_lo = w_ref[1, h0:h0 + MXU_TILE, col0 + h0:col0 + h0 + MXU_TILE]
        a, b = y_hi[:, h0:h0 + MXU_TILE], y_lo[:, h0:h0 + MXU_TILE]
        halves.append(d(a, w_hi) + d(a, w_lo) + d(b, w_hi))
    return jnp.concatenate(halves, axis=1)


def _lru_gates(y, w_ref, pvec):
    b_a, b_x, lam = pvec[1:2, :], pvec[2:3, :], pvec[3:4, :]
    y_hi, y_lo = _split(y, 2)
    r = _sigmoid(_lru_gate_dot(y_hi, y_lo, w_ref, 0) + b_a)
    ig = _sigmoid(_lru_gate_dot(y_hi, y_lo, w_ref, W_BRANCH) + b_x)
    log_a = -LRU_C * r * _softplus(-lam)
    a = jnp.exp(log_a)
    one_m_a2 = -jnp.tanh(log_a) * (jnp.exp(2.0 * log_a) + 1.0)
    b = jnp.sqrt(one_m_a2) * (ig * y)
    return a, b


def _lru_seq_kernel(x_ref, cw_ref, pvec_ref, wbd_ref, o_ref, xpad, hcar, *, lb):
    tb = pl.program_id(1)

    @pl.when(tb == 0)
    def _():
        xpad[0:8, :] = jnp.zeros((8, W_BRANCH), f32)
        hcar[...] = jnp.zeros_like(hcar)

    x = x_ref[...]
    xpad[8:8 + lb, :] = x
    cw = cw_ref[...]
    pvec = pvec_ref[...]
    y = pvec[0:1, :] + x * cw[3:4, :]
    for j in range(CONV_W - 1):
        y = y + xpad[5 + j:5 + j + lb, :] * cw[j:j + 1, :]
    a, b = _lru_gates(y, wbd_ref, pvec)
    row = lax.broadcasted_iota(jnp.int32, (lb, 1), 0)
    s = 1
    while s < lb:
        m = row >= s
        b = jnp.where(m, a * pltpu.roll(b, s, 0) + b, b)
        a = jnp.where(m, a * pltpu.roll(a, s, 0), a)
        s *= 2
    h = a * hcar[...] + b
    o_ref[...] = h
    hcar[...] = h[lb - 1:lb, :]
    xpad[5:8, :] = x[lb - 3:lb, :]


def _lru_seq(proj3, cw, pvec, wbd, l, lb):
    b, seq_len, _ = proj3.shape

    def layer(shape):
        return pl.BlockSpec((None,) + shape, lambda *_: (l,) + (0,) * len(shape))

    return pl.pallas_call(
        functools.partial(_lru_seq_kernel, lb=lb),
        grid=(b, seq_len // lb),
        in_specs=[
            pl.BlockSpec((None, lb, W_BRANCH), lambda i, t: (i, t, C_LRU // W_BRANCH)),
            layer((CONV_W, W_BRANCH)), layer((8, W_BRANCH)), layer((2, W_BRANCH, 2 * W_BRANCH)),
        ],
        out_specs=pl.BlockSpec((None, lb, W_BRANCH), lambda i, t: (i, t, 0)),
        out_shape=jax.ShapeDtypeStruct((b, seq_len, W_BRANCH), f32),
        scratch_shapes=[
            pltpu.VMEM((8 + lb, W_BRANCH), f32),
            pltpu.VMEM((1, W_BRANCH), f32),
        ],
        compiler_params=_cparams(("parallel", "arbitrary")),
        name="lru_seq",
    )(proj3, cw, pvec, wbd)


def _lru_tok_kernel(x_ref, b0_ref, b1_ref, b2_ref, h0_ref, cw_ref, pvec_ref, wbd_ref, o_ref):
    cw = cw_ref[...]
    pvec = pvec_ref[...]
    y = (pvec[0:1, :] + b0_ref[...] * cw[0:1, :] + b1_ref[...] * cw[1:2, :] + b2_ref[...] * cw[2:3, :]
         + x_ref[...] * cw[3:4, :])
    a, b = _lru_gates(y, wbd_ref, pvec)
    o_ref[...] = a * h0_ref[...] + b


def _lru_tok(proj2, bufs, h0, cw, pvec, wbd, l):
    b = proj2.shape[0]

    def layer(shape):
        return pl.BlockSpec((None,) + shape, lambda *_: (l,) + (0,) * len(shape))

    row = pl.BlockSpec((b, W_BRANCH), lambda i: (0, 0))
    return pl.pallas_call(
        _lru_tok_kernel,
        grid=(1,),
        in_specs=[pl.BlockSpec((b, W_BRANCH), lambda i: (0, C_LRU // W_BRANCH)), row, row, row, row,
                  layer((CONV_W, W_BRANCH)), layer((8, W_BRANCH)), layer((2, W_BRANCH, 2 * W_BRANCH))],
        out_specs=row,
        out_shape=jax.ShapeDtypeStruct((b, W_BRANCH), f32),
        compiler_params=_cparams(("arbitrary",)),
        name="lru_tok",
    )(proj2, *bufs, h0, cw, pvec, wbd)


def _prep_weights(p):
    w = jnp.swapaxes(p["w_in"], 1, 2)
    row_groups = [(0, 1536), (2176, 2688), (2704, 6800), (1664, 2176), (1536, 1664), (2688, 2704)]
    pad = jnp.zeros((DEPTH, LANE - GLA_GATE_RANK, D_MODEL), bf16)
    w_in_p = jnp.concatenate([w[:, a:b].astype(bf16) for a, b in row_groups] + [pad], axis=1)
    zl = jnp.zeros((DEPTH, RWKV_LORA, W_BRANCH), f32)
    wup = jnp.concatenate([jnp.concatenate([p["rwkv_w_up"], zl], axis=2),
                           jnp.concatenate([zl, p["rwkv_a_up"]], axis=2)], axis=1)
    wup_hi = wup.astype(bf16)
    wup = jnp.stack([wup_hi, (wup - wup_hi.astype(f32)).astype(bf16)], axis=1)
    zrow = jnp.zeros((DEPTH, W_BRANCH), f32)
    rwkv_vec = jnp.stack([p["rwkv_w0"], p["rwkv_a0"], p["rwkv_k_k"], p["rwkv_k_a"],
                          p["rwkv_r_k"].reshape(DEPTH, W_BRANCH), p["rwkv_gn_g"], p["rwkv_gn_b"], zrow], axis=1)
    mu = p["rwkv_mu"]
    gup = jnp.concatenate([p["gla_gk_up"], jnp.zeros((DEPTH, LANE - GLA_GATE_RANK, GLA_KW), f32)], axis=1)
    eye = jnp.eye(LRU_BLOCKS, dtype=f32)

    def bd(wb):
        return jnp.einsum("lhij,hg->lhigj", wb, eye).reshape(DEPTH, W_BRANCH, W_BRANCH)

    lru_vec = jnp.stack([p["lru_conv_b"], p["lru_b_a"], p["lru_b_x"], p["lru_lambda"],
                         zrow, zrow, zrow, zrow], axis=1)
    lru_w = jnp.concatenate([bd(p["lru_w_a"]), bd(p["lru_w_x"])], axis=2)
    lru_w_hi = lru_w.astype(bf16)
    lru_w_lo = (lru_w - lru_w_hi.astype(f32)).astype(bf16)
    return dict(
        w_in_p=w_in_p,
        w_out_b=p["w_out"].astype(bf16),
        norm_g3=p["norm_g"].reshape(DEPTH, 1, D_MODEL),
        final_g2=p["final_g"].reshape(1, D_MODEL),
        rwkv=(mu[:, None, 0:1536], mu[:, None, 1536:1664], wup, rwkv_vec) + _rwkv_consts(),
        gla=(gup, p["gla_gk_b"].reshape(DEPTH, 1, GLA_KW)),
        gla_gn=p["gla_gn_g"].reshape(DEPTH, 1, W_BRANCH),
        hgrn=(p["hgrn_lb_logits"],),
        hgrn_gn=p["hgrn_gn_g"].reshape(DEPTH, 1, W_BRANCH),
        lru=(p["lru_conv_w"], lru_vec, jnp.stack([lru_w_hi, lru_w_lo], axis=1)),
    )


def _shift_state(proj_last):
    return jnp.concatenate([proj_last[..., 0:1536], proj_last[..., C_RWKV_LO:C_RWKV_LO + LANE]], axis=-1)


def _rwkv_state_out(s):
    lead = s.shape[:-2]
    return jnp.swapaxes(s.reshape(lead + (RWKV_HEAD, RWKV_H, RWKV_HEAD)), -3, -2)


def _trunk_seq(x, mod, wts):
    b, seq_len, _ = x.shape
    t = b * seq_len
    lb = min(LB_CHUNK, seq_len)
    tm = min(TM_IN, seq_len)
    per_seq = seq_len // tm
    mod3 = mod.reshape(DEPTH * b * 3, 1, D_MODEL)
    x2 = x.reshape(t, D_MODEL)
    zeros = functools.partial(jnp.zeros, dtype=f32)
    new = []
    h2 = None
    for l in range(DEPTH):
        mod_l = mod3[l * b * 3:(l + 1) * b * 3]
        mod_next = mod3[(l + 1) * b * 3:(l + 2) * b * 3]
        if l == 0:
            proj2 = _inproj(x2, wts["norm_g3"], mod_l, wts["w_in_p"], l, per_seq, tm, normed=False)
        else:
            proj2 = _inproj(h2, None, None, wts["w_in_p"], l, None, min(TM_IN_NORMED, t), normed=True)
        proj3 = proj2.reshape(b, seq_len, PROJ_P)
        o_a, s_wkv = _rwkv_seq(proj3, zeros((b, 1, 1536)), zeros((b, 1, LANE)),
                               zeros((b, RWKV_HEAD, W_BRANCH)), wts["rwkv"], l, min(LB_RWKV, seq_len))
        o_b, st_gla = _chunk_seq("gla", proj3, wts["gla"], wts["gla_gn"], l, lb)
        o_c, st_hgrn = _chunk_seq("hgrn", proj3, wts["hgrn"], wts["hgrn_gn"], l, lb)
        o_d = _lru_seq(proj3, *wts["lru"], l, lb)
        outs = [o.reshape(t, W_BRANCH) for o in (o_a, o_b, o_c, o_d)]
        tmo = min(TM_OUT, seq_len)
        res = _outproj(outs, proj2, x2, mod_l, mod_next, wts["w_out_b"], wts["norm_g3"], wts["final_g2"], l,
                       seq_len // tmo, tmo)
        x2, h2 = (res, None) if l == DEPTH - 1 else res
        last = proj3[:, seq_len - 1]
        new.append((
            _shift_state(last),
            _rwkv_state_out(s_wkv),
            _state_from_transposed(st_gla, GLA_H, GLA_DK),
            _state_from_transposed(st_hgrn, HGRN_H, HGRN_DK),
            proj3[:, seq_len - (CONV_W - 1):, C_LRU:C_LRU + W_BRANCH],
            o_d[:, seq_len - 1],
        ))
    return x2.reshape(b, seq_len, D_MODEL), tuple(jnp.stack([n[i] for n in new], axis=0) for i in range(6))


def _trunk_tok(x, mod, states, wts):
    b = x.shape[0]
    s_shift, s_wkv, s_gla, s_hgrn, s_conv, s_h = states
    x2 = x.reshape(b, D_MODEL)
    s_wkv_t = jnp.transpose(s_wkv, (0, 2, 3, 4, 1))
    prev_rkv, prev_lo = s_shift[:, :, 0:1536], s_shift[:, :, 1536:1664]
    n_wkv = n_gla = n_hgrn = None
    new = []
    h2 = None
    for l in range(DEPTH):
        proj2 = _inproj(x2 if l == 0 else h2, wts["norm_g3"], mod, wts["w_in_p"], l, None, b, normed=l > 0)
        o_a, n_wkv = _rwkv_tok(proj2, prev_rkv, prev_lo, s_wkv_t, n_wkv, wts["rwkv"], l)
        o_b, n_gla = _chunk_tok("gla", proj2, wts["gla"], wts["gla_gn"], s_gla, n_gla, l, TOK_GROUP)
        o_c, n_hgrn = _chunk_tok("hgrn", proj2, wts["hgrn"], wts["hgrn_gn"], s_hgrn, n_hgrn, l, TOK_GROUP)
        conv = s_conv[l]
        o_d = _lru_tok(proj2, [conv[:, 0], conv[:, 1], conv[:, 2]], s_h[l], *wts["lru"], l)
        res = _outproj([o_a, o_b, o_c, o_d], proj2, x2, mod, mod, wts["w_out_b"], wts["norm_g3"],
                       wts["final_g2"], l, None, b)
        x2, h2 = (res, None) if l == DEPTH - 1 else res
        n_conv = jnp.concatenate([conv[:, 1:], proj2[:, None, C_LRU:C_LRU + W_BRANCH]], axis=1)
        new.append((_shift_state(proj2), n_conv, o_d))
    n_shift, n_conv, n_h = (jnp.stack([n[i] for n in new], axis=0) for i in range(3))
    n_wkv = jnp.transpose(n_wkv, (0, 4, 1, 2, 3))
    return x2.reshape(b, 1, D_MODEL), (n_shift, n_wkv, n_gla, n_hgrn, n_conv, n_h)


def kernel(x_prompt, x_sample, c_prompt, c_sample, state_rwkv_shift, state_rwkv_wkv, state_gla, state_hgrn, state_lru_conv, state_lru_h, norm_g, w_ada, b_ada, w_in, w_out, rwkv_mu, rwkv_w0, rwkv_w_up, rwkv_a0, rwkv_a_up, rwkv_k_k, rwkv_k_a, rwkv_r_k, rwkv_gn_g, rwkv_gn_b, gla_gk_up, gla_gk_b, gla_gn_g, hgrn_lb_logits, hgrn_gn_g, lru_conv_w, lru_conv_b, lru_w_a, lru_b_a, lru_w_x, lru_b_x, lru_lambda, final_g):
    p = dict(norm_g=norm_g, w_in=w_in, w_out=w_out, rwkv_mu=rwkv_mu, rwkv_w0=rwkv_w0, rwkv_w_up=rwkv_w_up,
             rwkv_a0=rwkv_a0, rwkv_a_up=rwkv_a_up, rwkv_k_k=rwkv_k_k, rwkv_k_a=rwkv_k_a, rwkv_r_k=rwkv_r_k,
             rwkv_gn_g=rwkv_gn_g, rwkv_gn_b=rwkv_gn_b, gla_gk_up=gla_gk_up, gla_gk_b=gla_gk_b,
             gla_gn_g=gla_gn_g, hgrn_lb_logits=hgrn_lb_logits, hgrn_gn_g=hgrn_gn_g, lru_conv_w=lru_conv_w,
             lru_conv_b=lru_conv_b, lru_w_a=lru_w_a, lru_b_a=lru_b_a, lru_w_x=lru_w_x, lru_b_x=lru_b_x,
             lru_lambda=lru_lambda, final_g=final_g)
    wts = _prep_weights(p)
    bp = x_prompt.shape[0]
    bs = x_sample.shape[0]
    pad_rows = (-bp) % 8
    c_all = jnp.concatenate([c_prompt, jnp.zeros((pad_rows, D_MODEL), f32), c_sample], axis=0)
    mod = _ada(c_all, w_ada, b_ada)
    mod_p = mod[:, 0:bp]
    mod_s = mod[:, bp + pad_rows:bp + pad_rows + bs]
    y_p, st_p = _trunk_seq(x_prompt, mod_p, wts)
    states = (state_rwkv_shift, state_rwkv_wkv, state_gla, state_hgrn, state_lru_conv, state_lru_h)
    y_s, st_s = _trunk_tok(x_sample, mod_s, states, wts)
    return (y_p, y_s) + st_p + st_s
```

```python
import functools

import jax
import jax.numpy as jnp
from jax import lax
from jax.experimental import pallas as pl
from jax.experimental.pallas import tpu as pltpu

f32 = jnp.float32
bf16 = jnp.bfloat16

D_MODEL = 2048
DEPTH = 4
W_BRANCH = 512
EPS = 1e-6

RWKV_H = 8
RWKV_HEAD = 64
RWKV_LORA = 64
RWKV_PROJ = 3 * W_BRANCH + 2 * RWKV_LORA
RWKV_GN_EPS = 64e-5

GLA_H = 4
GLA_DK = 64
GLA_DV = 128
GLA_KW = GLA_H * GLA_DK
GLA_GATE_RANK = 16
GLA_GATE_NORM = 16.0

HGRN_H = 4
HGRN_DK = 128
HGRN_DV = 128
LB_FLOOR = 1e-30

LRU_BLOCKS = 8
LRU_BLOCK = 64
LRU_C = 8.0
CONV_W = 4

CHUNK = 32

C_R, C_K, C_V = 0, 512, 1024
C_GLA_V = 1536
C_HQ, C_HF, C_HI = 2048, 2560, 3072
C_LRU = 3584
C_GATE = 4096
C_GLA_Q, C_GLA_K = 6144, 6400
C_RWKV_LO = 6656
C_GLA_LO = 6784
PROJ_P = 6912

LANE = 128
MXU_TILE = 256
VMEM_LIMIT = 56 * 1024 * 1024

TM_IN = 1024
TM_IN_NORMED = 2048
TN_IN = 1152
TM_OUT = 256
LB_CHUNK = 256
LB_RWKV = 128
RWKV_GROUP = 16
TOK_GROUP = 32

HIGHEST = lax.Precision.HIGHEST


def _cparams(sem):
    return pltpu.CompilerParams(dimension_semantics=sem, vmem_limit_bytes=VMEM_LIMIT)


def _dotf(a, b):
    return jnp.dot(a, b, preferred_element_type=f32, precision=HIGHEST)


def _prec(a):
    return HIGHEST if a.dtype == f32 else None


def _dot_nt(a, b):
    return lax.dot_general(a, b, (((1,), (1,)), ((), ())), preferred_element_type=f32, precision=_prec(a))


def _dot_tn(a, b):
    return lax.dot_general(a, b, (((0,), (0,)), ((), ())), preferred_element_type=f32, precision=_prec(a))


def _split(x, pieces):
    out = []
    for i in range(pieces):
        part = x.astype(bf16)
        out.append(part)
        if i + 1 < pieces:
            x = x - part.astype(f32)
    return out


def _half_dot(x, ones):
    d = functools.partial(jnp.dot, preferred_element_type=f32)
    kh = ones.shape[0]
    if x.shape[1] == kh:
        return d(x, ones)
    return jnp.concatenate([d(x[:, 0:kh], ones), d(x[:, kh:2 * kh], ones)], axis=1)


def _seg_sum(x, ones, pieces=2):
    return sum(_half_dot(part, ones) for part in _split(x, pieces))


def _ones_seg(ones, x, pieces=2):
    d = functools.partial(jnp.dot, preferred_element_type=f32)
    return sum(d(ones, part) for part in _split(x, pieces))


def _sigmoid(x):
    return jax.nn.sigmoid(x)


def _softplus(x):
    return jnp.maximum(x, 0.0) + jnp.log1p(jnp.exp(-jnp.abs(x)))


def _log_sigmoid(x):
    return -_softplus(-x)


def _ada_kernel(c_ref, w_ref, b_ref, o_ref):
    c = c_ref[...]
    s = (c * _sigmoid(c)).astype(bf16)
    o_ref[...] = jnp.dot(s, w_ref[...].astype(bf16), preferred_element_type=f32) + b_ref[...]


def _ada(c_all, w_ada, b_ada):
    rows = c_all.shape[0]
    tn = 1024
    n = w_ada.shape[2]
    return pl.pallas_call(
        _ada_kernel,
        grid=(DEPTH, n // tn),
        in_specs=[
            pl.BlockSpec((rows, D_MODEL), lambda l, j: (0, 0)),
            pl.BlockSpec((None, D_MODEL, tn), lambda l, j: (l, 0, j)),
            pl.BlockSpec((None, 1, tn), lambda l, j: (l, 0, j)),
        ],
        out_specs=pl.BlockSpec((None, rows, tn), lambda l, j: (l, 0, j)),
        out_shape=jax.ShapeDtypeStruct((DEPTH, rows, n), f32),
        compiler_params=_cparams(("parallel", "parallel")),
        name="ada_mod",
    )(c_all, w_ada, b_ada.reshape(DEPTH, 1, n))


def _norm_mod(x, g, scale, shift):
    ms = jnp.mean(x * x, axis=-1, keepdims=True)
    h = x * lax.rsqrt(ms + EPS) * g
    return (h * (1.0 + scale) + shift).astype(bf16)


def _inproj_kernel(*refs, normed):
    if normed:
        h_ref, w_ref, o_ref = refs
    else:
        x_ref, g_ref, sc_ref, sh_ref, w_ref, o_ref, h_ref = refs

        @pl.when(pl.program_id(1) == 0)
        def _():
            h_ref[...] = _norm_mod(x_ref[...], g_ref[...], sc_ref[...], sh_ref[...])

    o_ref[...] = _dot_nt(h_ref[...], w_ref[...])


def _mod_spec(l, which, per_seq_tiles, tm):
    if per_seq_tiles is None:
        return pl.BlockSpec((None, tm, D_MODEL), lambda i, *_: (l, 0, which))
    return pl.BlockSpec((None, 1, D_MODEL), lambda i, *_: (i // per_seq_tiles * 3 + which, 0, 0))


def _inproj(x2, norm_g3, mod, w_in_p, l, per_seq_tiles, tm, normed):
    t = x2.shape[0]
    tn = TN_IN
    row_spec = pl.BlockSpec((tm, D_MODEL), lambda i, j: (i, 0))
    w_spec = pl.BlockSpec((None, tn, D_MODEL), lambda i, j: (l, j, 0))
    if normed:
        in_specs, args, scratch = [row_spec, w_spec], (x2, w_in_p), []
    else:
        in_specs = [row_spec, pl.BlockSpec((None, 1, D_MODEL), lambda i, j: (l, 0, 0)),
                    _mod_spec(l, 1, per_seq_tiles, tm), _mod_spec(l, 0, per_seq_tiles, tm), w_spec]
        args, scratch = (x2, norm_g3, mod, mod, w_in_p), [pltpu.VMEM((tm, D_MODEL), bf16)]
    return pl.pallas_call(
        functools.partial(_inproj_kernel, normed=normed),
        grid=(t // tm, PROJ_P // tn),
        in_specs=in_specs,
        out_specs=pl.BlockSpec((tm, tn), lambda i, j: (i, j)),
        out_shape=jax.ShapeDtypeStruct((t, PROJ_P), f32),
        scratch_shapes=scratch,
        compiler_params=_cparams(("parallel", "arbitrary")),
        name="in_proj",
    )(*args)


def _outproj_kernel(oa_ref, ob_ref, oc_ref, od_ref, pz_ref, x_ref, gate_ref, w_ref, g_ref, *rest, final):
    z = pz_ref[...]
    o = jnp.concatenate([oa_ref[...], ob_ref[...], oc_ref[...], od_ref[...]], axis=1)
    o = o * (z * _sigmoid(z))
    y = jnp.dot(o.astype(bf16), w_ref[...], preferred_element_type=f32)
    xn = x_ref[...] + gate_ref[...] * y
    if final:
        (o_ref,) = rest
        ms = jnp.mean(xn * xn, axis=-1, keepdims=True)
        o_ref[...] = xn * lax.rsqrt(ms + EPS) * g_ref[...]
    else:
        sc_ref, sh_ref, o_ref, h_ref = rest
        o_ref[...] = xn
        h_ref[...] = _norm_mod(xn, g_ref[...], sc_ref[...], sh_ref[...])


def _outproj(outs, proj2, x2, mod, mod_next, w_out_b, norm_g3, final_g2, l, per_seq_tiles, tm):
    t = x2.shape[0]
    final = l == DEPTH - 1
    mix_spec = pl.BlockSpec((tm, W_BRANCH), lambda i: (i, 0))
    row_spec = pl.BlockSpec((tm, D_MODEL), lambda i: (i, 0))
    in_specs = [
        mix_spec, mix_spec, mix_spec, mix_spec,
        pl.BlockSpec((tm, D_MODEL), lambda i: (i, C_GATE // D_MODEL)),
        row_spec,
        _mod_spec(l, 2, per_seq_tiles, tm),
        pl.BlockSpec((None, D_MODEL, D_MODEL), lambda i: (l, 0, 0)),
    ]
    args = [*outs, proj2, x2, mod, w_out_b]
    if final:
        in_specs.append(pl.BlockSpec((1, D_MODEL), lambda i: (0, 0)))
        args.append(final_g2)
        out_specs, out_shape = row_spec, jax.ShapeDtypeStruct((t, D_MODEL), f32)
    else:
        in_specs += [pl.BlockSpec((None, 1, D_MODEL), lambda i: (l + 1, 0, 0)),
                     _mod_spec(l + 1, 1, per_seq_tiles, tm), _mod_spec(l + 1, 0, per_seq_tiles, tm)]
        args += [norm_g3, mod_next, mod_next]
        out_specs = [row_spec, row_spec]
        out_shape = [jax.ShapeDtypeStruct((t, D_MODEL), f32), jax.ShapeDtypeStruct((t, D_MODEL), bf16)]
    return pl.pallas_call(
        functools.partial(_outproj_kernel, final=final),
        grid=(t // tm,),
        in_specs=in_specs,
        out_specs=out_specs,
        out_shape=out_shape,
        compiler_params=_cparams(("parallel",)),
        name="out_proj",
    )(*args)


def _rwkv_prologue(r, k, v, lo, pr, pk, pv, plo, mu_rkv, mu_lo, wup, pvec, bo, sub_block=0):
    w0, a0, k_k, k_a, r_k = (pvec[i:i + 1, :] for i in range(5))
    xr = r + (pr - r) * mu_rkv[:, 0:512]
    xk = k + (pk - k) * mu_rkv[:, 512:1024]
    xv = v + (pv - v) * mu_rkv[:, 1024:1536]
    xlo = lo + (plo - lo) * mu_lo
    lane = lax.broadcasted_iota(jnp.int32, xlo.shape, 1)
    act = jnp.where(lane < RWKV_LORA, jnp.tanh(xlo), xlo)
    act_hi, act_lo = _split(act, 2)
    d = functools.partial(jnp.dot, preferred_element_type=f32)
    up = d(act_hi, wup[0]) + d(act_hi, wup[1]) + d(act_lo, wup[0])
    w_raw = -_softplus(-(w0 + up[:, 0:512])) - 0.5
    logw = -jnp.exp(w_raw)
    a = _sigmoid(a0 + up[:, 512:1024])
    kk = xk * k_k
    kk = kk / jnp.maximum(jnp.sqrt(_seg_sum(kk * kk, bo)), 1e-12)
    kh = xk * (1.0 + (a - 1.0) * k_a)
    alp = kk * a
    ar = _seg_sum(alp * xr, bo, pieces=1)
    out = dict(vv=xv, xr=xr, kr=_seg_sum(kh * xr, bo, pieces=1),
               bonus=_seg_sum(xr * kh * r_k, bo, pieces=1) * xv)
    if not sub_block:
        ew = jnp.exp(logw)
        out.update(kap=kk, alp=alp, kh=kh, ew=ew, wr=ew * xr - ar * kk)
        return out
    rowi = lax.broadcasted_iota(jnp.int32, (logw.shape[0], 1), 0) & (sub_block - 1)
    lc = logw
    step = 1
    while step < sub_block:
        lc = lc + jnp.where(rowi >= step, pltpu.roll(lc, step, 0), 0.0)
        step *= 2
    gam, ginv = jnp.exp(lc), jnp.exp(-lc)
    kap = kk * jnp.exp(lc - logw)
    out.update(kap=kap, alp=alp * ginv, kh=kh * ginv, gam=gam, wr=gam * xr - ar * kap)
    return out


_RWKV_STEP_KEYS = ("kap", "alp", "kh", "vv", "wr", "kr")


def _rwkv_step(s, kap, alp, kh, vv, wr, kr, bo, idt):
    s_new, o_row = [], []
    for c0 in range(0, W_BRANCH, MXU_TILE):
        sl = slice(c0, c0 + MXU_TILE)
        s_h, idt_h, v_h = s[:, sl], idt[:, sl], vv[:, sl]
        s_b = s_h.astype(bf16)
        x = jnp.concatenate([s_b * kap[:, sl].astype(bf16), s_b * wr[:, sl].astype(bf16),
                             idt_h.astype(bf16) * v_h.astype(bf16)], axis=0)
        red = jnp.dot(x, bo, preferred_element_type=f32)
        s_new.append(s_h - red[0:64] * alp[:, sl] + red[128:192] * kh[:, sl])
        o_row.append(jnp.sum(red[64:128] * idt_h, axis=0, keepdims=True) + v_h * kr[:, sl])
    return jnp.concatenate(s_new, axis=1), jnp.concatenate(o_row, axis=1)


def _rwkv_epilogue(o, bonus, pvec, bo):
    gn_g, gn_b = pvec[5:6, :], pvec[6:7, :]
    mu = _seg_sum(o, bo, pieces=1) * (1.0 / RWKV_HEAD)
    d = o - mu
    var = _seg_sum(d * d, bo, pieces=1) * (1.0 / RWKV_HEAD)
    return d * lax.rsqrt(var + RWKV_GN_EPS) * gn_g + gn_b + bonus


def _rwkv_seq_kernel(r_ref, k_ref, v_ref, lo_ref, prkv_ref, plo_ref, sin_ref, mu_rkv_ref, mu_lo_ref, wup_ref,
                     pvec_ref, bo_ref, idt_ref, o_ref, sout_ref,
                     s_sc, crkv_sc, clo_sc, kap_sc, gam_sc, alp_sc, kh_sc, vv_sc, wr_sc, kr_sc, oraw_sc,
                     bonus_sc, *, g_seqs, lb):
    tb = pl.program_id(1)

    @pl.when(tb == 0)
    def _():
        s_sc[...] = sin_ref[...]
        crkv_sc[...] = prkv_ref[...]
        clo_sc[...] = plo_ref[...]

    bo = bo_ref[...]
    idt = idt_ref[...]
    pvec = pvec_ref[...]
    step_sc = dict(kap=kap_sc, alp=alp_sc, kh=kh_sc, vv=vv_sc, wr=wr_sc, kr=kr_sc)

    row0 = lax.broadcasted_iota(jnp.int32, (lb, 1), 0) == 0
    for g in range(g_seqs):
        cur = [r_ref[g], k_ref[g], v_ref[g]]
        lo = lo_ref[g]
        carry = crkv_sc[g]
        prev = [jnp.where(row0, carry[:, i * 512:(i + 1) * 512], pltpu.roll(c, 1, 0)) for i, c in enumerate(cur)]
        plo = jnp.where(row0, clo_sc[g], pltpu.roll(lo, 1, 0))
        res = _rwkv_prologue(cur[0], cur[1], cur[2], lo, prev[0], prev[1], prev[2], plo,
                             mu_rkv_ref[...], mu_lo_ref[...], wup_ref[...], pvec, bo, sub_block=RWKV_GROUP)
        for key in _RWKV_STEP_KEYS:
            step_sc[key][g] = res[key]
        gam_sc[g] = res["gam"]
        bonus_sc[g] = res["bonus"]
        for i, c in enumerate(cur):
            crkv_sc[g, :, i * 512:(i + 1) * 512] = c[lb - 1:lb, :]
        clo_sc[g] = lo[lb - 1:lb, :]

    def group(i, carry):
        t0 = pl.multiple_of(i * RWKV_GROUP, RWKV_GROUP)
        states = [s_sc[g] for g in range(g_seqs)]
        for j in range(RWKV_GROUP):
            for g in range(g_seqs):
                rows = [step_sc[key][g, pl.ds(t0 + j, 1), :] for key in _RWKV_STEP_KEYS]
                states[g], o_row = _rwkv_step(states[g], *rows, bo, idt)
                oraw_sc[g, pl.ds(t0 + j, 1), :] = o_row
        for g in range(g_seqs):
            s_sc[g] = states[g] * gam_sc[g, pl.ds(t0 + RWKV_GROUP - 1, 1), :]
        return carry

    lax.fori_loop(0, lb // RWKV_GROUP, group, 0)

    for g in range(g_seqs):
        o_ref[g] = _rwkv_epilogue(oraw_sc[g], bonus_sc[g], pvec, bo)

    @pl.when(tb == pl.num_programs(1) - 1)
    def _():
        sout_ref[...] = s_sc[...]


_RWKV_TOK_KEYS = ("kap", "ew", "alp", "kh", "vv", "xr")


def _rwkv_tok_kernel(r_ref, k_ref, v_ref, lo_ref, prkv_ref, plo_ref, sin_ref, mu_rkv_ref, mu_lo_ref, wup_ref,
                     pvec_ref, bo_ref, idt_ref, *rest):
    del idt_ref
    o_ref, sout_ref = rest[-10:-8]
    kap_t, ew_t, alp_t, kh_t, v_t, r_t, o_t, bonus_sc = rest[-8:]
    h = pl.program_id(0)
    bo = bo_ref[...]
    pvec = pvec_ref[...]

    @pl.when(h == 0)
    def _():
        prkv = prkv_ref[...]
        res = _rwkv_prologue(r_ref[...], k_ref[...], v_ref[...], lo_ref[...],
                             prkv[:, 0:512], prkv[:, 512:1024], prkv[:, 1024:1536], plo_ref[...],
                             mu_rkv_ref[...], mu_lo_ref[...], wup_ref[...], pvec, bo)
        for key, dst in zip(_RWKV_TOK_KEYS, (kap_t, ew_t, alp_t, kh_t, v_t, r_t)):
            dst[...] = res[key].T
        bonus_sc[...] = res["bonus"]

    row0 = pl.multiple_of(h * RWKV_HEAD, RWKV_HEAD)
    keys = pl.ds(row0, RWKV_HEAD)
    kap, ew, alp, kh, rr = kap_t[keys, :], ew_t[keys, :], alp_t[keys, :], kh_t[keys, :], r_t[keys, :]

    def value_row(v, carry):
        s = sin_ref[v]
        sk = jnp.sum(s * kap, axis=0, keepdims=True)
        s_new = s * ew - sk * alp + v_t[pl.ds(row0 + v, 1), :] * kh
        sout_ref[v] = s_new
        o_t[pl.ds(row0 + v, 1), :] = jnp.sum(s_new * rr, axis=0, keepdims=True)
        return carry

    lax.fori_loop(0, RWKV_HEAD, value_row, 0, unroll=4)

    @pl.when(h == pl.num_programs(0) - 1)
    def _():
        o_ref[...] = _rwkv_epilogue(o_t[...].T, bonus_sc[...], pvec, bo)


def _rwkv_consts():
    i = jnp.arange(W_BRANCH)
    j = jnp.arange(MXU_TILE)
    bo = (j[:, None] // RWKV_HEAD == j[None, :] // RWKV_HEAD).astype(bf16)
    idt = (jnp.arange(RWKV_HEAD)[:, None] == (i[None, :] % RWKV_HEAD)).astype(f32)
    return bo, idt


def _rwkv_weight_specs(l):
    def cs(shape):
        return pl.BlockSpec((None,) + shape, lambda *_: (l,) + (0,) * len(shape))

    def const(shape):
        return pl.BlockSpec(shape, lambda *_: (0,) * len(shape))

    return [cs((1, 1536)), cs((1, LANE)), cs((2, LANE, 1024)), cs((8, W_BRANCH)),
            const((MXU_TILE, MXU_TILE)), const((RWKV_HEAD, W_BRANCH))]


def _rwkv_seq(proj3, prev_rkv, prev_lo, s_in, wts, l, lb):
    b, seq_len, _ = proj3.shape
    g = b
    nt = seq_len // lb

    def col(width, off):
        return pl.BlockSpec((g, lb, width), lambda i, t: (i, t, off // width))

    tok_sc = pltpu.VMEM((g, lb, W_BRANCH), f32)
    out = pl.pallas_call(
        functools.partial(_rwkv_seq_kernel, g_seqs=g, lb=lb),
        grid=(b // g, nt),
        in_specs=[
            col(512, C_R), col(512, C_K), col(512, C_V), col(LANE, C_RWKV_LO),
            pl.BlockSpec((g, 1, 1536), lambda i, t: (i, 0, 0)),
            pl.BlockSpec((g, 1, LANE), lambda i, t: (i, 0, 0)),
            pl.BlockSpec((g, RWKV_HEAD, W_BRANCH), lambda i, t: (i, 0, 0)),
        ] + _rwkv_weight_specs(l),
        out_specs=[
            pl.BlockSpec((g, lb, W_BRANCH), lambda i, t: (i, t, 0)),
            pl.BlockSpec((g, RWKV_HEAD, W_BRANCH), lambda i, t: (i, 0, 0)),
        ],
        out_shape=[
            jax.ShapeDtypeStruct((b, seq_len, W_BRANCH), f32),
            jax.ShapeDtypeStruct((b, RWKV_HEAD, W_BRANCH), f32),
        ],
        scratch_shapes=[
            pltpu.VMEM((g, RWKV_HEAD, W_BRANCH), f32),
            pltpu.VMEM((g, 1, 1536), f32),
            pltpu.VMEM((g, 1, LANE), f32),
        ] + [tok_sc] * 9,
        compiler_params=_cparams(("parallel", "arbitrary")),
        name="rwkv_seq",
    )(proj3, proj3, proj3, proj3, prev_rkv, prev_lo, s_in, *wts)
    return out


def _rwkv_tok(proj2, prev_rkv, prev_lo, s_all, acc, wts, l):
    b = proj2.shape[0]
    assert b == LANE, "the single-token RWKV-7 kernel keeps exactly one lane tile of sequences"

    def col(width, off):
        return pl.BlockSpec((b, width), lambda i: (0, off // width))

    state_spec = pl.BlockSpec((None, None, RWKV_HEAD, RWKV_HEAD, b), lambda i: (l, i, 0, 0, 0))
    in_specs = [
        col(512, C_R), col(512, C_K), col(512, C_V), col(LANE, C_RWKV_LO),
        pl.BlockSpec((None, b, 1536), lambda i: (l, 0, 0)),
        pl.BlockSpec((None, b, LANE), lambda i: (l, 0, 0)),
        state_spec,
    ] + _rwkv_weight_specs(l)
    args = [proj2, proj2, proj2, proj2, prev_rkv, prev_lo, s_all, *wts]
    aliases = {}
    if acc is not None:
        in_specs.append(pl.BlockSpec(memory_space=pl.ANY))
        aliases = {len(args): 1}
        args.append(acc)
    feat_sc = pltpu.VMEM((W_BRANCH, b), f32)
    return pl.pallas_call(
        _rwkv_tok_kernel,
        grid=(RWKV_H,),
        in_specs=in_specs,
        out_specs=[pl.BlockSpec((b, W_BRANCH), lambda i: (0, 0)), state_spec],
        out_shape=[
            jax.ShapeDtypeStruct((b, W_BRANCH), f32),
            jax.ShapeDtypeStruct(s_all.shape, f32),
        ],
        scratch_shapes=[feat_sc] * 7 + [pltpu.VMEM((b, W_BRANCH), f32)],
        input_output_aliases=aliases,
        compiler_params=_cparams(("arbitrary",)),
        name="rwkv_tok",
    )(*args)


def _gla_inputs(q_ref, k_ref, v_ref, lo_ref, gup_ref, gb_ref):
    q = q_ref[...] * (GLA_DK ** -0.5)
    z = _dotf(lo_ref[...], gup_ref[...]) + gb_ref[...]
    g = _log_sigmoid(z) * (1.0 / GLA_GATE_NORM)
    return q, k_ref[...], v_ref[...], g


def _hgrn_lb(logits, l):
    m = jnp.max(logits, axis=0, keepdims=True)
    e = jnp.exp(logits - m)
    sm = e / jnp.sum(e, axis=0, keepdims=True)
    lb = jnp.zeros_like(sm[0:1, :])
    for i in range(1, l + 1):
        lb = lb + sm[i:i + 1, :]
    return lb


def _hgrn_inputs(q_ref, f_ref, i_ref, logits_ref, l):
    lb = _hgrn_lb(logits_ref[...], l)
    f_lo = f_ref[...]
    logf = jnp.log(jnp.maximum(lb, LB_FLOOR) + (1.0 - lb) * _sigmoid(f_lo))
    k = (1.0 - lb) * _sigmoid(-f_lo)
    return q_ref[...], k, i_ref[...], logf


def _chunk_core(q, k, v, g, lt_ref, esum_ref, vmask_ref, gmask_ref, bon_ref, gn_ref, st_sc, kpad, bpad, lb,
                heads_per_group):
    bc = _ones_seg(lt_ref[...], g)
    nc = lb // CHUNK
    f = q.shape[1]
    kpad[...] = k.reshape(nc, CHUNK, f)
    bpad[...] = bc.reshape(nc, CHUNK, f)
    q3 = q.reshape(nc, CHUNK, f)
    bc3 = bc.reshape(nc, CHUNK, f)
    esum = esum_ref[...]

    lane_j = lax.broadcasted_iota(jnp.int32, (1, 1, LANE), 2) & (CHUNK - 1)
    att3 = None
    for r0 in range(0, CHUNK, 8):
        rows = CHUNK - r0
        q_s, bc_s = q3[:, r0:, :], bc3[:, r0:, :]
        rowc8 = lax.broadcasted_iota(jnp.int32, (1, 8, 1), 1) + r0
        acc = jnp.zeros((nc, rows, LANE), f32)
        for j in range(r0, r0 + 8):
            kj, bj = kpad[:, j:j + 1, :], bpad[:, j:j + 1, :]
            z = q_s * kj * jnp.exp(bc_s - bj)
            head = jnp.where(rowc8 >= j, z[:, 0:8, :], 0.0)
            z = head if rows == 8 else jnp.concatenate([head, z[:, 8:, :]], axis=1)
            a = jnp.dot(z.reshape(nc * rows, f).astype(bf16), esum, preferred_element_type=f32)
            acc = acc + jnp.where(lane_j == j, a.reshape(nc, rows, LANE), 0.0)
        if r0:
            acc = jnp.concatenate([jnp.zeros((nc, r0, LANE), f32), acc], axis=1)
        att3 = acc if att3 is None else att3 + acc
    n_heads = LANE // CHUNK
    v3 = v.reshape(nc, CHUNK, W_BRANCH)
    vbd = jnp.concatenate([v3] * n_heads, axis=1) * vmask_ref[...]
    o = lax.dot_general(att3.astype(bf16), vbd.astype(bf16), (((2,), (1,)), ((0,), (0,))),
                        preferred_element_type=f32).reshape(lb, W_BRANCH)

    ng = f // LANE
    vw = W_BRANCH // ng
    sts = [st_sc[gi] for gi in range(ng)]
    outs = []
    for c in range(nc):
        sl = slice(c * CHUNK, (c + 1) * CHUNK)
        bcc = bc[sl]
        blast = bcc[CHUNK - 1:CHUNK, :]
        dec = jnp.exp(blast)
        qe = (q[sl] * jnp.exp(bcc)).astype(bf16)
        ke = (k[sl] * jnp.exp(blast - bcc)).astype(bf16)
        vb = v[sl].astype(bf16)
        parts = []
        for gi in range(ng):
            kl = slice(gi * LANE, (gi + 1) * LANE)
            vl = slice(gi * vw, (gi + 1) * vw)
            parts.append(_dot_nt(qe[:, kl], sts[gi].astype(bf16)))
            upd = _dot_tn(vb[:, vl], ke[:, kl])
            if heads_per_group > 1:
                upd = upd * gmask_ref[...]
            sts[gi] = sts[gi] * dec[:, kl] + upd
        outs.append(o[sl] + jnp.concatenate(parts, axis=1))
    for gi in range(ng):
        st_sc[gi] = sts[gi]
    o = jnp.concatenate(outs, axis=0)
    ms = _seg_sum(o * o, bon_ref[...], pieces=1) * (1.0 / LANE)
    return o * lax.rsqrt(ms + EPS) * gn_ref[...]


def _chunk_seq_kernel(*refs, kind, l, lb):
    n_in = 6 if kind == "gla" else 4
    ins = refs[:n_in]
    lt_ref, esum_ref, vmask_ref, gmask_ref, bon_ref, gn_ref, o_ref, stout_ref, st_sc, kpad, bpad = refs[n_in:]
    tb = pl.program_id(1)

    @pl.when(tb == 0)
    def _():
        st_sc[...] = jnp.zeros_like(st_sc)

    if kind == "gla":
        q, k, v, g = _gla_inputs(*ins)
        heads_per_group = LANE // GLA_DK
    else:
        q, k, v, g = _hgrn_inputs(*ins, l)
        heads_per_group = LANE // HGRN_DK
    o_ref[...] = _chunk_core(q, k, v, g, lt_ref, esum_ref, vmask_ref, gmask_ref, bon_ref, gn_ref, st_sc, kpad,
                             bpad, lb, heads_per_group)

    @pl.when(tb == pl.num_programs(1) - 1)
    def _():
        stout_ref[...] = st_sc[...]


def _chunk_consts(h, dk, lb):
    f = h * dk
    i = jnp.arange(lb)
    lt = ((i[:, None] // CHUNK == i[None, :] // CHUNK) & (i[:, None] >= i[None, :])).astype(bf16)
    fi = jnp.arange(f)
    li = jnp.arange(LANE)
    oi = jnp.arange(W_BRANCH)
    vw = W_BRANCH // (f // LANE)
    esum = (fi[:, None] // dk == li[None, :] // CHUNK).astype(bf16)
    vmask = (li[:, None] // CHUNK == oi[None, :] // LANE).astype(f32)
    gmask = (jnp.arange(vw)[:, None] // LANE == li[None, :] // dk).astype(f32)
    ti = jnp.arange(MXU_TILE)
    bon = (ti[:, None] // LANE == ti[None, :] // LANE).astype(bf16)
    return lt, esum, vmask, gmask, bon


def _chunk_seq(kind, proj3, extra, gn3, l, lb):
    b, seq_len, _ = proj3.shape
    h, dk = (GLA_H, GLA_DK) if kind == "gla" else (HGRN_H, HGRN_DK)
    f = h * dk
    nt = seq_len // lb

    def col(width, off):
        return pl.BlockSpec((None, lb, width), lambda i, t: (i, t, off // width))

    def const(shape):
        return pl.BlockSpec(shape, lambda *_: (0,) * len(shape))

    def layer(shape):
        return pl.BlockSpec((None,) + shape, lambda *_: (l,) + (0,) * len(shape))

    if kind == "gla":
        gup, gb = extra
        in_specs = [col(GLA_KW, C_GLA_Q), col(GLA_KW, C_GLA_K), col(W_BRANCH, C_GLA_V), col(LANE, C_GLA_LO),
                    layer((LANE, GLA_KW)), layer((1, GLA_KW))]
        args = [proj3, proj3, proj3, proj3, gup, gb]
    else:
        (logits,) = extra
        in_specs = [col(W_BRANCH, C_HQ), col(W_BRANCH, C_HF), col(W_BRANCH, C_HI), const((DEPTH, W_BRANCH))]
        args = [proj3, proj3, proj3, logits]
    consts = _chunk_consts(h, dk, lb)
    in_specs += [const(c.shape) for c in consts] + [layer((1, W_BRANCH))]
    ng = f // LANE
    st_shape = (ng, W_BRANCH // ng, LANE)
    return pl.pallas_call(
        functools.partial(_chunk_seq_kernel, kind=kind, l=l, lb=lb),
        grid=(b, nt),
        in_specs=in_specs,
        out_specs=[
            pl.BlockSpec((None, lb, W_BRANCH), lambda i, t: (i, t, 0)),
            pl.BlockSpec((None,) + st_shape, lambda i, t: (i, 0, 0, 0)),
        ],
        out_shape=[
            jax.ShapeDtypeStruct((b, seq_len, W_BRANCH), f32),
            jax.ShapeDtypeStruct((b,) + st_shape, f32),
        ],
        scratch_shapes=[
            pltpu.VMEM(st_shape, f32),
            pltpu.VMEM((lb // CHUNK, CHUNK, f), f32),
            pltpu.VMEM((lb // CHUNK, CHUNK, f), f32),
        ],
        compiler_params=_cparams(("parallel", "arbitrary")),
        name=kind + "_seq",
    )(*args, *consts, gn3)


def _state_from_transposed(st, h, dk):
    b, ng = st.shape[0], st.shape[1]
    hpg = h // ng
    st = st.reshape(b, ng, hpg, LANE, hpg, dk)
    diag = jnp.stack([st[:, :, i, :, i, :] for i in range(hpg)], axis=2)
    return jnp.swapaxes(diag.reshape(b, h, LANE, dk), 2, 3)


def _key_columns(x, g, dk, odd):
    xt = jnp.concatenate([x, jnp.zeros((LANE - g, LANE), f32)], axis=0).T
    if dk == LANE:
        return xt
    return jnp.where(odd, xt[dk:2 * dk], xt[0:dk])


def _chunk_tok_kernel(*refs, kind, l, g, has_acc):
    ins, (o_ref, sout_ref) = refs[:-2], refs[-2:]
    if has_acc:
        ins = ins[:-1]
    odd = (pl.program_id(1) % 2) == 1
    if kind == "gla":
        q_ref, k_ref, v_ref, lo_ref, gup_ref, gb_ref, s_ref, gn_ref = ins
        q = q_ref[...] * (GLA_DK ** -0.5)
        k = k_ref[...]
        z = _dotf(lo_ref[...], gup_ref[...]) + gb_ref[...]
        dec = jnp.exp(_log_sigmoid(z) * (1.0 / GLA_GATE_NORM))
        v = v_ref[...]
        dk = GLA_DK
    else:
        q_ref, f_ref, i_ref, logits_ref, s_ref, gn_ref = ins
        q, k, v, logf = _hgrn_inputs(q_ref, f_ref, i_ref, logits_ref, l)
        dec = jnp.exp(logf)
        dk = HGRN_DK
    qt, kt, dt = (_key_columns(x, g, dk, odd) for x in (q, k, dec))
    o_rows = []
    for b in range(g):
        qcol, kcol, dcol = (jnp.broadcast_to(t[:, b:b + 1], (dk, LANE)) for t in (qt, kt, dt))
        s_new = s_ref[b] * dcol + kcol * v[b:b + 1, :]
        sout_ref[b] = s_new
        o_rows.append(jnp.sum(qcol * s_new, axis=0, keepdims=True))
    o = jnp.concatenate(o_rows, axis=0)
    ms = jnp.mean(o * o, axis=-1, keepdims=True)
    o_ref[...] = o * lax.rsqrt(ms + EPS) * gn_ref[...]


def _chunk_tok(kind, proj2, extra, gn3, s_all, acc, l, g):
    b = proj2.shape[0]
    h, dk = (GLA_H, GLA_DK) if kind == "gla" else (HGRN_H, HGRN_DK)

    def head_col(off):
        per = LANE // dk
        return pl.BlockSpec((g, LANE), lambda i, j: (i, off // LANE + j // per))

    if kind == "gla":
        gup, gb = extra
        in_specs = [head_col(C_GLA_Q), head_col(C_GLA_K),
                    pl.BlockSpec((g, LANE), lambda i, j: (i, C_GLA_V // LANE + j)),
                    pl.BlockSpec((g, LANE), lambda i, j: (i, C_GLA_LO // LANE)),
                    pl.BlockSpec((None, LANE, LANE), lambda i, j: (l, 0, j // 2)),
                    pl.BlockSpec((None, 1, LANE), lambda i, j: (l, 0, j // 2))]
        args = [proj2, proj2, proj2, proj2, gup, gb]
    else:
        (logits,) = extra
        in_specs = [head_col(C_HQ), head_col(C_HF), head_col(C_HI),
                    pl.BlockSpec((DEPTH, LANE), lambda i, j: (0, j))]
        args = [proj2, proj2, proj2, logits]
    state_spec = pl.BlockSpec((None, g, None, dk, LANE), lambda i, j: (l, i, j, 0, 0))
    in_specs += [state_spec, pl.BlockSpec((None, 1, LANE), lambda i, j: (l, 0, j))]
    args += [s_all, gn3]
    aliases = {}
    if acc is not None:
        in_specs.append(pl.BlockSpec(memory_space=pl.ANY))
        aliases = {len(args): 1}
        args.append(acc)
    return pl.pallas_call(
        functools.partial(_chunk_tok_kernel, kind=kind, l=l, g=g, has_acc=acc is not None),
        grid=(b // g, h),
        in_specs=in_specs,
        out_specs=[pl.BlockSpec((g, LANE), lambda i, j: (i, j)), state_spec],
        out_shape=[
            jax.ShapeDtypeStruct((b, W_BRANCH), f32),
            jax.ShapeDtypeStruct(s_all.shape, f32),
        ],
        input_output_aliases=aliases,
        compiler_params=_cparams(("parallel", "parallel")),
        name=kind + "_tok",
    )(*args)


def _lru_gate_dot(y_hi, y_lo, w_ref, col0):
    d = functools.partial(jnp.dot, preferred_element_type=f32)
    halves = []
    for h0 in range(0, W_BRANCH, MXU_TILE):
        w_hi = w_ref[0, h0:h0 + MXU_TILE, col0 + h0:col0 + h0 + MXU_TILE]
        w_lo = w_ref[1, h0:h0 + MXU_TILE, col0 + h0:col0 + h0 + MXU_TILE]
        a, b = y_hi[:, h0:h0 + MXU_TILE], y_lo[:, h0:h0 + MXU_TILE]
        halves.append(d(a, w_hi) + d(a, w_lo) + d(b, w_hi))
    return jnp.concatenate(halves, axis=1)


def _lru_gates(y, w_ref, pvec):
    b_a, b_x, lam = pvec[1:2, :], pvec[2:3, :], pvec[3:4, :]
    y_hi, y_lo = _split(y, 2)
    r = _sigmoid(_lru_gate_dot(y_hi, y_lo, w_ref, 0) + b_a)
    ig = _sigmoid(_lru_gate_dot(y_hi, y_lo, w_ref, W_BRANCH) + b_x)
    log_a = -LRU_C * r * _softplus(-lam)
    a = jnp.exp(log_a)
    one_m_a2 = -jnp.tanh(log_a) * (jnp.exp(2.0 * log_a) + 1.0)
    b = jnp.sqrt(one_m_a2) * (ig * y)
    return a, b


def _lru_seq_kernel(x_ref, cw_ref, pvec_ref, wbd_ref, o_ref, xpad, hcar, *, lb):
    tb = pl.program_id(1)

    @pl.when(tb == 0)
    def _():
        xpad[0:8, :] = jnp.zeros((8, W_BRANCH), f32)
        hcar[...] = jnp.zeros_like(hcar)

    x = x_ref[...]
    xpad[8:8 + lb, :] = x
    cw = cw_ref[...]
    pvec = pvec_ref[...]
    y = pvec[0:1, :] + x * cw[3:4, :]
    for j in range(CONV_W - 1):
        y = y + xpad[5 + j:5 + j + lb, :] * cw[j:j + 1, :]
    a, b = _lru_gates(y, wbd_ref, pvec)
    row = lax.broadcasted_iota(jnp.int32, (lb, 1), 0)
    s = 1
    while s < lb:
        m = row >= s
        b = jnp.where(m, a * pltpu.roll(b, s, 0) + b, b)
        a = jnp.where(m, a * pltpu.roll(a, s, 0), a)
        s *= 2
    h = a * hcar[...] + b
    o_ref[...] = h
    hcar[...] = h[lb - 1:lb, :]
    xpad[5:8, :] = x[lb - 3:lb, :]


def _lru_seq(proj3, cw, pvec, wbd, l, lb):
    b, seq_len, _ = proj3.shape

    def layer(shape):
        return pl.BlockSpec((None,) + shape, lambda *_: (l,) + (0,) * len(shape))

    return pl.pallas_call(
        functools.partial(_lru_seq_kernel, lb=lb),
        grid=(b, seq_len // lb),
        in_specs=[
            pl.BlockSpec((None, lb, W_BRANCH), lambda i, t: (i, t, C_LRU // W_BRANCH)),
            layer((CONV_W, W_BRANCH)), layer((8, W_BRANCH)), layer((2, W_BRANCH, 2 * W_BRANCH)),
        ],
        out_specs=pl.BlockSpec((None, lb, W_BRANCH), lambda i, t: (i, t, 0)),
        out_shape=jax.ShapeDtypeStruct((b, seq_len, W_BRANCH), f32),
        scratch_shapes=[
            pltpu.VMEM((8 + lb, W_BRANCH), f32),
            pltpu.VMEM((1, W_BRANCH), f32),
        ],
        compiler_params=_cparams(("parallel", "arbitrary")),
        name="lru_seq",
    )(proj3, cw, pvec, wbd)


def _lru_tok_kernel(x_ref, b0_ref, b1_ref, b2_ref, h0_ref, cw_ref, pvec_ref, wbd_ref, o_ref):
    cw = cw_ref[...]
    pvec = pvec_ref[...]
    y = (pvec[0:1, :] + b0_ref[...] * cw[0:1, :] + b1_ref[...] * cw[1:2, :] + b2_ref[...] * cw[2:3, :]
         + x_ref[...] * cw[3:4, :])
    a, b = _lru_gates(y, wbd_ref, pvec)
    o_ref[...] = a * h0_ref[...] + b


def _lru_tok(proj2, bufs, h0, cw, pvec, wbd, l):
    b = proj2.shape[0]

    def layer(shape):
        return pl.BlockSpec((None,) + shape, lambda *_: (l,) + (0,) * len(shape))

    row = pl.BlockSpec((b, W_BRANCH), lambda i: (0, 0))
    return pl.pallas_call(
        _lru_tok_kernel,
        grid=(1,),
        in_specs=[pl.BlockSpec((b, W_BRANCH), lambda i: (0, C_LRU // W_BRANCH)), row, row, row, row,
                  layer((CONV_W, W_BRANCH)), layer((8, W_BRANCH)), layer((2, W_BRANCH, 2 * W_BRANCH))],
        out_specs=row,
        out_shape=jax.ShapeDtypeStruct((b, W_BRANCH), f32),
        compiler_params=_cparams(("arbitrary",)),
        name="lru_tok",
    )(proj2, *bufs, h0, cw, pvec, wbd)


def _prep_weights(p):
    w = jnp.swapaxes(p["w_in"], 1, 2)
    row_groups = [(0, 1536), (2176, 2688), (2704, 6800), (1664, 2176), (1536, 1664), (2688, 2704)]
    pad = jnp.zeros((DEPTH, LANE - GLA_GATE_RANK, D_MODEL), bf16)
    w_in_p = jnp.concatenate([w[:, a:b].astype(bf16) for a, b in row_groups] + [pad], axis=1)
    zl = jnp.zeros((DEPTH, RWKV_LORA, W_BRANCH), f32)
    wup = jnp.concatenate([jnp.concatenate([p["rwkv_w_up"], zl], axis=2),
                           jnp.concatenate([zl, p["rwkv_a_up"]], axis=2)], axis=1)
    wup_hi = wup.astype(bf16)
    wup = jnp.stack([wup_hi, (wup - wup_hi.astype(f32)).astype(bf16)], axis=1)
    zrow = jnp.zeros((DEPTH, W_BRANCH), f32)
    rwkv_vec = jnp.stack([p["rwkv_w0"], p["rwkv_a0"], p["rwkv_k_k"], p["rwkv_k_a"],
                          p["rwkv_r_k"].reshape(DEPTH, W_BRANCH), p["rwkv_gn_g"], p["rwkv_gn_b"], zrow], axis=1)
    mu = p["rwkv_mu"]
    gup = jnp.concatenate([p["gla_gk_up"], jnp.zeros((DEPTH, LANE - GLA_GATE_RANK, GLA_KW), f32)], axis=1)
    eye = jnp.eye(LRU_BLOCKS, dtype=f32)

    def bd(wb):
        return jnp.einsum("lhij,hg->lhigj", wb, eye).reshape(DEPTH, W_BRANCH, W_BRANCH)

    lru_vec = jnp.stack([p["lru_conv_b"], p["lru_b_a"], p["lru_b_x"], p["lru_lambda"],
                         zrow, zrow, zrow, zrow], axis=1)
    lru_w = jnp.concatenate([bd(p["lru_w_a"]), bd(p["lru_w_x"])], axis=2)
    lru_w_hi = lru_w.astype(bf16)
    lru_w_lo = (lru_w - lru_w_hi.astype(f32)).astype(bf16)
    return dict(
        w_in_p=w_in_p,
        w_out_b=p["w_out"].astype(bf16),
        norm_g3=p["norm_g"].reshape(DEPTH, 1, D_MODEL),
        final_g2=p["final_g"].reshape(1, D_MODEL),
        rwkv=(mu[:, None, 0:1536], mu[:, None, 1536:1664], wup, rwkv_vec) + _rwkv_consts(),
        gla=(gup, p["gla_gk_b"].reshape(DEPTH, 1, GLA_KW)),
        gla_gn=p["gla_gn_g"].reshape(DEPTH, 1, W_BRANCH),
        hgrn=(p["hgrn_lb_logits"],),
        hgrn_gn=p["hgrn_gn_g"].reshape(DEPTH, 1, W_BRANCH),
        lru=(p["lru_conv_w"], lru_vec, jnp.stack([lru_w_hi, lru_w_lo], axis=1)),
    )


def _shift_state(proj_last):
    return jnp.concatenate([proj_last[..., 0:1536], proj_last[..., C_RWKV_LO:C_RWKV_LO + LANE]], axis=-1)


def _rwkv_state_out(s):
    lead = s.shape[:-2]
    return jnp.swapaxes(s.reshape(lead + (RWKV_HEAD, RWKV_H, RWKV_HEAD)), -3, -2)


def _trunk_seq(x, mod, wts):
    b, seq_len, _ = x.shape
    t = b * seq_len
    lb = min(LB_CHUNK, seq_len)
    tm = min(TM_IN, seq_len)
    per_seq = seq_len // tm
    mod3 = mod.reshape(DEPTH * b * 3, 1, D_MODEL)
    x2 = x.reshape(t, D_MODEL)
    zeros = functools.partial(jnp.zeros, dtype=f32)
    new = []
    h2 = None
    for l in range(DEPTH):
        mod_l = mod3[l * b * 3:(l + 1) * b * 3]
        mod_next = mod3[(l + 1) * b * 3:(l + 2) * b * 3]
        if l == 0:
            proj2 = _inproj(x2, wts["norm_g3"], mod_l, wts["w_in_p"], l, per_seq, tm, normed=False)
        else:
            proj2 = _inproj(h2, None, None, wts["w_in_p"], l, None, min(TM_IN_NORMED, t), normed=True)
        proj3 = proj2.reshape(b, seq_len, PROJ_P)
        o_a, s_wkv = _rwkv_seq(proj3, zeros((b, 1, 1536)), zeros((b, 1, LANE)),
                               zeros((b, RWKV_HEAD, W_BRANCH)), wts["rwkv"], l, min(LB_RWKV, seq_len))
        o_b, st_gla = _chunk_seq("gla", proj3, wts["gla"], wts["gla_gn"], l, lb)
        o_c, st_hgrn = _chunk_seq("hgrn", proj3, wts["hgrn"], wts["hgrn_gn"], l, lb)
        o_d = _lru_seq(proj3, *wts["lru"], l, lb)
        outs = [o.reshape(t, W_BRANCH) for o in (o_a, o_b, o_c, o_d)]
        tmo = min(TM_OUT, seq_len)
        res = _outproj(outs, proj2, x2, mod_l, mod_next, wts["w_out_b"], wts["norm_g3"], wts["final_g2"], l,
                       seq_len // tmo, tmo)
        x2, h2 = (res, None) if l == DEPTH - 1 else res
        last = proj3[:, seq_len - 1]
        new.append((
            _shift_state(last),
            _rwkv_state_out(s_wkv),
            _state_from_transposed(st_gla, GLA_H, GLA_DK),
            _state_from_transposed(st_hgrn, HGRN_H, HGRN_DK),
            proj3[:, seq_len - (CONV_W - 1):, C_LRU:C_LRU + W_BRANCH],
            o_d[:, seq_len - 1],
        ))
    return x2.reshape(b, seq_len, D_MODEL), tuple(jnp.stack([n[i] for n in new], axis=0) for i in range(6))


def _trunk_tok(x, mod, states, wts):
    b = x.shape[0]
    s_shift, s_wkv, s_gla, s_hgrn, s_conv, s_h = states
    x2 = x.reshape(b, D_MODEL)
    s_wkv_t = jnp.transpose(s_wkv, (0, 2, 3, 4, 1))
    prev_rkv, prev_lo = s_shift[:, :, 0:1536], s_shift[:, :, 1536:1664]
    n_wkv = n_gla = n_hgrn = None
    new = []
    h2 = None
    for l in range(DEPTH):
        proj2 = _inproj(x2 if l == 0 else h2, wts["norm_g3"], mod, wts["w_in_p"], l, None, b, normed=l > 0)
        o_a, n_wkv = _rwkv_tok(proj2, prev_rkv, prev_lo, s_wkv_t, n_wkv, wts["rwkv"], l)
        o_b, n_gla = _chunk_tok("gla", proj2, wts["gla"], wts["gla_gn"], s_gla, n_gla, l, TOK_GROUP)
        o_c, n_hgrn = _chunk_tok("hgrn", proj2, wts["hgrn"], wts["hgrn_gn"], s_hgrn, n_hgrn, l, TOK_GROUP)
        conv = s_conv[l]
        o_d = _lru_tok(proj2, [conv[:, 0], conv[:, 1], conv[:, 2]], s_h[l], *wts["lru"], l)
        res = _outproj([o_a, o_b, o_c, o_d], proj2, x2, mod, mod, wts["w_out_b"], wts["norm_g3"],
                       wts["final_g2"], l, None, b)
        x2, h2 = (res, None) if l == DEPTH - 1 else res
        n_conv = jnp.concatenate([conv[:, 1:], proj2[:, None, C_LRU:C_LRU + W_BRANCH]], axis=1)
        new.append((_shift_state(proj2), n_conv, o_d))
    n_shift, n_conv, n_h = (jnp.stack([n[i] for n in new], axis=0) for i in range(3))
    n_wkv = jnp.transpose(n_wkv, (0, 4, 1, 2, 3))
    return x2.reshape(b, 1, D_MODEL), (n_shift, n_wkv, n_gla, n_hgrn, n_conv, n_h)


def kernel(x_prompt, x_sample, c_prompt, c_sample, state_rwkv_shift, state_rwkv_wkv, state_gla, state_hgrn, state_lru_conv, state_lru_h, norm_g, w_ada, b_ada, w_in, w_out, rwkv_mu, rwkv_w0, rwkv_w_up, rwkv_a0, rwkv_a_up, rwkv_k_k, rwkv_k_a, rwkv_r_k, rwkv_gn_g, rwkv_gn_b, gla_gk_up, gla_gk_b, gla_gn_g, hgrn_lb_logits, hgrn_gn_g, lru_conv_w, lru_conv_b, lru_w_a, lru_b_a, lru_w_x, lru_b_x, lru_lambda, final_g):
    p = dict(norm_g=norm_g, w_in=w_in, w_out=w_out, rwkv_mu=rwkv_mu, rwkv_w0=rwkv_w0, rwkv_w_up=rwkv_w_up,
             rwkv_a0=rwkv_a0, rwkv_a_up=rwkv_a_up, rwkv_k_k=rwkv_k_k, rwkv_k_a=rwkv_k_a, rwkv_r_k=rwkv_r_k,
             rwkv_gn_g=rwkv_gn_g, rwkv_gn_b=rwkv_gn_b, gla_gk_up=gla_gk_up, gla_gk_b=gla_gk_b,
             gla_gn_g=gla_gn_g, hgrn_lb_logits=hgrn_lb_logits, hgrn_gn_g=hgrn_gn_g, lru_conv_w=lru_conv_w,
             lru_conv_b=lru_conv_b, lru_w_a=lru_w_a, lru_b_a=lru_b_a, lru_w_x=lru_w_x, lru_b_x=lru_b_x,
             lru_lambda=lru_lambda, final_g=final_g)
    wts = _prep_weights(p)
    bp = x_prompt.shape[0]
    bs = x_sample.shape[0]
    pad_rows = (-bp) % 8
    c_all = jnp.concatenate([c_prompt, jnp.zeros((pad_rows, D_MODEL), f32), c_sample], axis=0)
    mod = _ada(c_all, w_ada, b_ada)
    mod_p = mod[:, 0:bp]
    mod_s = mod[:, bp + pad_rows:bp + pad_rows + bs]
    y_p, st_p = _trunk_seq(x_prompt, mod_p, wts)
    states = (state_rwkv_shift, state_rwkv_wkv, state_gla, state_hgrn, state_lru_conv, state_lru_h)
    y_s, st_s = _trunk_tok(x_sample, mod_s, states, wts)
    return (y_p, y_s) + st_p + st_s
```

```python
import functools

import jax
import jax.numpy as jnp
from jax import lax
from jax.experimental import pallas as pl
from jax.experimental.pallas import tpu as pltpu

f32 = jnp.float32
bf16 = jnp.bfloat16

D_MODEL = 2048
DEPTH = 4
W_BRANCH = 512
EPS = 1e-6

RWKV_H = 8
RWKV_HEAD = 64
RWKV_LORA = 64
RWKV_PROJ = 3 * W_BRANCH + 2 * RWKV_LORA
RWKV_GN_EPS = 64e-5

GLA_H = 4
GLA_DK = 64
GLA_DV = 128
GLA_KW = GLA_H * GLA_DK
GLA_GATE_RANK = 16
GLA_GATE_NORM = 16.0

HGRN_H = 4
HGRN_DK = 128
HGRN_DV = 128
LB_FLOOR = 1e-30

LRU_BLOCKS = 8
LRU_BLOCK = 64
LRU_C = 8.0
CONV_W = 4

CHUNK = 32

C_R, C_K, C_V = 0, 512, 1024
C_GLA_V = 1536
C_HQ, C_HF, C_HI = 2048, 2560, 3072
C_LRU = 3584
C_GATE = 4096
C_GLA_Q, C_GLA_K = 6144, 6400
C_RWKV_LO = 6656
C_GLA_LO = 6784
PROJ_P = 6912

LANE = 128
MXU_TILE = 256
VMEM_LIMIT = 56 * 1024 * 1024

TM_IN = 1024
TM_IN_NORMED = 2048
TN_IN = 1152
TM_OUT = 256
LB_CHUNK = 256
LB_RWKV = 256
RWKV_GROUP = 16
TOK_GROUP = 64

HIGHEST = lax.Precision.HIGHEST


def _cparams(sem):
    return pltpu.CompilerParams(dimension_semantics=sem, vmem_limit_bytes=VMEM_LIMIT)


def _dotf(a, b):
    return jnp.dot(a, b, preferred_element_type=f32, precision=HIGHEST)


def _prec(a):
    return HIGHEST if a.dtype == f32 else None


def _dot_nt(a, b):
    return lax.dot_general(a, b, (((1,), (1,)), ((), ())), preferred_element_type=f32, precision=_prec(a))


def _dot_tn(a, b):
    return lax.dot_general(a, b, (((0,), (0,)), ((), ())), preferred_element_type=f32, precision=_prec(a))


def _split(x, pieces):
    out = []
    for i in range(pieces):
        part = x.astype(bf16)
        out.append(part)
        if i + 1 < pieces:
            x = x - part.astype(f32)
    return out


def _half_dot(x, ones):
    d = functools.partial(jnp.dot, preferred_element_type=f32)
    kh = ones.shape[0]
    if x.shape[1] == kh:
        return d(x, ones)
    return jnp.concatenate([d(x[:, 0:kh], ones), d(x[:, kh:2 * kh], ones)], axis=1)


def _seg_sum(x, ones, pieces=2):
    return sum(_half_dot(part, ones) for part in _split(x, pieces))


def _ones_seg(ones, x, pieces=2):
    d = functools.partial(jnp.dot, preferred_element_type=f32)
    return sum(d(ones, part) for part in _split(x, pieces))


def _sigmoid(x):
    return jax.nn.sigmoid(x)


def _softplus(x):
    return jnp.maximum(x, 0.0) + jnp.log1p(jnp.exp(-jnp.abs(x)))


def _log_sigmoid(x):
    return -_softplus(-x)


def _ada_kernel(c_ref, w_ref, b_ref, o_ref):
    c = c_ref[...]
    s = (c * _sigmoid(c)).astype(bf16)
    o_ref[...] = jnp.dot(s, w_ref[...].astype(bf16), preferred_element_type=f32) + b_ref[...]


def _ada(c_all, w_ada, b_ada):
    rows = c_all.shape[0]
    tn = 1024
    n = w_ada.shape[2]
    return pl.pallas_call(
        _ada_kernel,
        grid=(DEPTH, n // tn),
        in_specs=[
            pl.BlockSpec((rows, D_MODEL), lambda l, j: (0, 0)),
            pl.BlockSpec((None, D_MODEL, tn), lambda l, j: (l, 0, j)),
            pl.BlockSpec((None, 1, tn), lambda l, j: (l, 0, j)),
        ],
        out_specs=pl.BlockSpec((None, rows, tn), lambda l, j: (l, 0, j)),
        out_shape=jax.ShapeDtypeStruct((DEPTH, rows, n), f32),
        compiler_params=_cparams(("parallel", "parallel")),
        name="ada_mod",
    )(c_all, w_ada, b_ada.reshape(DEPTH, 1, n))


def _norm_mod(x, g, scale, shift):
    ms = jnp.mean(x * x, axis=-1, keepdims=True)
    h = x * lax.rsqrt(ms + EPS) * g
    return (h * (1.0 + scale) + shift).astype(bf16)


def _inproj_kernel(*refs, normed):
    if normed:
        h_ref, w_ref, o_ref = refs
    else:
        x_ref, g_ref, sc_ref, sh_ref, w_ref, o_ref, h_ref = refs

        @pl.when(pl.program_id(1) == 0)
        def _():
            h_ref[...] = _norm_mod(x_ref[...], g_ref[...], sc_ref[...], sh_ref[...])

    o_ref[...] = _dot_nt(h_ref[...], w_ref[...])


def _mod_spec(l, which, per_seq_tiles, tm):
    if per_seq_tiles is None:
        return pl.BlockSpec((None, tm, D_MODEL), lambda i, *_: (l, 0, which))
    return pl.BlockSpec((None, 1, D_MODEL), lambda i, *_: (i // per_seq_tiles * 3 + which, 0, 0))


def _inproj(x2, norm_g3, mod, w_in_p, l, per_seq_tiles, tm, normed):
    t = x2.shape[0]
    tn = TN_IN
    row_spec = pl.BlockSpec((tm, D_MODEL), lambda i, j: (i, 0))
    w_spec = pl.BlockSpec((None, tn, D_MODEL), lambda i, j: (l, j, 0))
    if normed:
        in_specs, args, scratch = [row_spec, w_spec], (x2, w_in_p), []
    else:
        in_specs = [row_spec, pl.BlockSpec((None, 1, D_MODEL), lambda i, j: (l, 0, 0)),
                    _mod_spec(l, 1, per_seq_tiles, tm), _mod_spec(l, 0, per_seq_tiles, tm), w_spec]
        args, scratch = (x2, norm_g3, mod, mod, w_in_p), [pltpu.VMEM((tm, D_MODEL), bf16)]
    return pl.pallas_call(
        functools.partial(_inproj_kernel, normed=normed),
        grid=(t // tm, PROJ_P // tn),
        in_specs=in_specs,
        out_specs=pl.BlockSpec((tm, tn), lambda i, j: (i, j)),
        out_shape=jax.ShapeDtypeStruct((t, PROJ_P), f32),
        scratch_shapes=scratch,
        compiler_params=_cparams(("parallel", "arbitrary")),
        name="in_proj",
    )(*args)


def _outproj_kernel(oa_ref, ob_ref, oc_ref, od_ref, pz_ref, x_ref, gate_ref, w_ref, g_ref, *rest, final):
    z = pz_ref[...]
    o = jnp.concatenate([oa_ref[...], ob_ref[...], oc_ref[...], od_ref[...]], axis=1)
    o = o * (z * _sigmoid(z))
    y = jnp.dot(o.astype(bf16), w_ref[...], preferred_element_type=f32)
    xn = x_ref[...] + gate_ref[...] * y
    if final:
        (o_ref,) = rest
        ms = jnp.mean(xn * xn, axis=-1, keepdims=True)
        o_ref[...] = xn * lax.rsqrt(ms + EPS) * g_ref[...]
    else:
        sc_ref, sh_ref, o_ref, h_ref = rest
        o_ref[...] = xn
        h_ref[...] = _norm_mod(xn, g_ref[...], sc_ref[...], sh_ref[...])


def _outproj(outs, proj2, x2, mod, mod_next, w_out_b, norm_g3, final_g2, l, per_seq_tiles, tm):
    t = x2.shape[0]
    final = l == DEPTH - 1
    mix_spec = pl.BlockSpec((tm, W_BRANCH), lambda i: (i, 0))
    row_spec = pl.BlockSpec((tm, D_MODEL), lambda i: (i, 0))
    in_specs = [
        mix_spec, mix_spec, mix_spec, mix_spec,
        pl.BlockSpec((tm, D_MODEL), lambda i: (i, C_GATE // D_MODEL)),
        row_spec,
        _mod_spec(l, 2, per_seq_tiles, tm),
        pl.BlockSpec((None, D_MODEL, D_MODEL), lambda i: (l, 0, 0)),
    ]
    args = [*outs, proj2, x2, mod, w_out_b]
    if final:
        in_specs.append(pl.BlockSpec((1, D_MODEL), lambda i: (0, 0)))
        args.append(final_g2)
        out_specs, out_shape = row_spec, jax.ShapeDtypeStruct((t, D_MODEL), f32)
    else:
        in_specs += [pl.BlockSpec((None, 1, D_MODEL), lambda i: (l + 1, 0, 0)),
                     _mod_spec(l + 1, 1, per_seq_tiles, tm), _mod_spec(l + 1, 0, per_seq_tiles, tm)]
        args += [norm_g3, mod_next, mod_next]
        out_specs = [row_spec, row_spec]
        out_shape = [jax.ShapeDtypeStruct((t, D_MODEL), f32), jax.ShapeDtypeStruct((t, D_MODEL), bf16)]
    return pl.pallas_call(
        functools.partial(_outproj_kernel, final=final),
        grid=(t // tm,),
        in_specs=in_specs,
        out_specs=out_specs,
        out_shape=out_shape,
        compiler_params=_cparams(("parallel",)),
        name="out_proj",
    )(*args)


def _rwkv_prologue(r, k, v, lo, pr, pk, pv, plo, mu_rkv, mu_lo, wup, pvec, bo, sub_block=0):
    w0, a0, k_k, k_a, r_k = (pvec[i:i + 1, :] for i in range(5))
    xr = r + (pr - r) * mu_rkv[:, 0:512]
    xk = k + (pk - k) * mu_rkv[:, 512:1024]
    xv = v + (pv - v) * mu_rkv[:, 1024:1536]
    xlo = lo + (plo - lo) * mu_lo
    lane = lax.broadcasted_iota(jnp.int32, xlo.shape, 1)
    act = jnp.where(lane < RWKV_LORA, jnp.tanh(xlo), xlo)
    act_hi, act_lo = _split(act, 2)
    d = functools.partial(jnp.dot, preferred_element_type=f32)
    up = d(act_hi, wup[0]) + d(act_hi, wup[1]) + d(act_lo, wup[0])
    w_raw = -_softplus(-(w0 + up[:, 0:512])) - 0.5
    logw = -jnp.exp(w_raw)
    a = _sigmoid(a0 + up[:, 512:1024])
    kk = xk * k_k
    kk = kk / jnp.maximum(jnp.sqrt(_seg_sum(kk * kk, bo)), 1e-12)
    kh = xk * (1.0 + (a - 1.0) * k_a)
    alp = kk * a
    ar = _seg_sum(alp * xr, bo, pieces=1)
    out = dict(vv=xv, xr=xr, kr=_seg_sum(kh * xr, bo, pieces=1),
               bonus=_seg_sum(xr * kh * r_k, bo, pieces=1) * xv)
    if not sub_block:
        ew = jnp.exp(logw)
        out.update(kap=kk, alp=alp, kh=kh, ew=ew, wr=ew * xr - ar * kk)
        return out
    rowi = lax.broadcasted_iota(jnp.int32, (logw.shape[0], 1), 0) & (sub_block - 1)
    lc = logw
    step = 1
    while step < sub_block:
        lc = lc + jnp.where(rowi >= step, pltpu.roll(lc, step, 0), 0.0)
        step *= 2
    gam, ginv = jnp.exp(lc), jnp.exp(-lc)
    kap = kk * jnp.exp(lc - logw)
    out.update(kap=kap, alp=alp * ginv, kh=kh * ginv, gam=gam, wr=gam * xr - ar * kap)
    return out


_RWKV_STEP_KEYS = ("kap", "alp", "kh", "vv", "wr", "kr")


def _rwkv_step(s, kap, alp, kh, vv, wr, kr, bo, idt):
    s_new, o_row = [], []
    for c0 in range(0, W_BRANCH, MXU_TILE):
        sl = slice(c0, c0 + MXU_TILE)
        s_h, idt_h, v_h = s[:, sl], idt[:, sl], vv[:, sl]
        s_b = s_h.astype(bf16)
        x = jnp.concatenate([s_b * kap[:, sl].astype(bf16), s_b * wr[:, sl].astype(bf16),
                             idt_h.astype(bf16) * v_h.astype(bf16)], axis=0)
        red = jnp.dot(x, bo, preferred_element_type=f32)
        s_new.append(s_h - red[0:64] * alp[:, sl] + red[128:192] * kh[:, sl])
        o_row.append(jnp.sum(red[64:128] * idt_h, axis=0, keepdims=True) + v_h * kr[:, sl])
    return jnp.concatenate(s_new, axis=1), jnp.concatenate(o_row, axis=1)


def _rwkv_epilogue(o, bonus, pvec, bo):
    gn_g, gn_b = pvec[5:6, :], pvec[6:7, :]
    mu = _seg_sum(o, bo, pieces=1) * (1.0 / RWKV_HEAD)
    d = o - mu
    var = _seg_sum(d * d, bo, pieces=1) * (1.0 / RWKV_HEAD)
    return d * lax.rsqrt(var + RWKV_GN_EPS) * gn_g + gn_b + bonus


def _rwkv_seq_kernel(r_ref, k_ref, v_ref, lo_ref, prkv_ref, plo_ref, sin_ref, mu_rkv_ref, mu_lo_ref, wup_ref,
                     pvec_ref, bo_ref, idt_ref, o_ref, sout_ref,
                     s_sc, crkv_sc, clo_sc, kap_sc, gam_sc, alp_sc, kh_sc, vv_sc, wr_sc, kr_sc, oraw_sc,
                     bonus_sc, *, g_seqs, lb):
    tb = pl.program_id(1)

    @pl.when(tb == 0)
    def _():
        s_sc[...] = sin_ref[...]
        crkv_sc[...] = prkv_ref[...]
        clo_sc[...] = plo_ref[...]

    bo = bo_ref[...]
    idt = idt_ref[...]
    pvec = pvec_ref[...]
    step_sc = dict(kap=kap_sc, alp=alp_sc, kh=kh_sc, vv=vv_sc, wr=wr_sc, kr=kr_sc)

    row0 = lax.broadcasted_iota(jnp.int32, (lb, 1), 0) == 0
    for g in range(g_seqs):
        cur = [r_ref[g], k_ref[g], v_ref[g]]
        lo = lo_ref[g]
        carry = crkv_sc[g]
        prev = [jnp.where(row0, carry[:, i * 512:(i + 1) * 512], pltpu.roll(c, 1, 0)) for i, c in enumerate(cur)]
        plo = jnp.where(row0, clo_sc[g], pltpu.roll(lo, 1, 0))
        res = _rwkv_prologue(cur[0], cur[1], cur[2], lo, prev[0], prev[1], prev[2], plo,
                             mu_rkv_ref[...], mu_lo_ref[...], wup_ref[...], pvec, bo, sub_block=RWKV_GROUP)
        for key in _RWKV_STEP_KEYS:
            step_sc[key][g] = res[key]
        gam_sc[g] = res["gam"]
        bonus_sc[g] = res["bonus"]
        for i, c in enumerate(cur):
            crkv_sc[g, :, i * 512:(i + 1) * 512] = c[lb - 1:lb, :]
        clo_sc[g] = lo[lb - 1:lb, :]

    def group(i, carry):
        t0 = pl.multiple_of(i * RWKV_GROUP, RWKV_GROUP)
        states = [s_sc[g] for g in range(g_seqs)]
        for j in range(RWKV_GROUP):
            for g in range(g_seqs):
                rows = [step_sc[key][g, pl.ds(t0 + j, 1), :] for key in _RWKV_STEP_KEYS]
                states[g], o_row = _rwkv_step(states[g], *rows, bo, idt)
                oraw_sc[g, pl.ds(t0 + j, 1), :] = o_row
        for g in range(g_seqs):
            s_sc[g] = states[g] * gam_sc[g, pl.ds(t0 + RWKV_GROUP - 1, 1), :]
        return carry

    lax.fori_loop(0, lb // RWKV_GROUP, group, 0)

    for g in range(g_seqs):
        o_ref[g] = _rwkv_epilogue(oraw_sc[g], bonus_sc[g], pvec, bo)

    @pl.when(tb == pl.num_programs(1) - 1)
    def _():
        sout_ref[...] = s_sc[...]


_RWKV_TOK_KEYS = ("kap", "ew", "alp", "kh", "vv", "xr")


def _rwkv_tok_kernel(r_ref, k_ref, v_ref, lo_ref, prkv_ref, plo_ref, sin_ref, mu_rkv_ref, mu_lo_ref, wup_ref,
                     pvec_ref, bo_ref, idt_ref, *rest):
    del idt_ref
    o_ref, sout_ref = rest[-10:-8]
    kap_t, ew_t, alp_t, kh_t, v_t, r_t, o_t, bonus_sc = rest[-8:]
    h = pl.program_id(0)
    bo = bo_ref[...]
    pvec = pvec_ref[...]

    @pl.when(h == 0)
    def _():
        prkv = prkv_ref[...]
        res = _rwkv_prologue(r_ref[...], k_ref[...], v_ref[...], lo_ref[...],
                             prkv[:, 0:512], prkv[:, 512:1024], prkv[:, 1024:1536], plo_ref[...],
                             mu_rkv_ref[...], mu_lo_ref[...], wup_ref[...], pvec, bo)
        for key, dst in zip(_RWKV_TOK_KEYS, (kap_t, ew_t, alp_t, kh_t, v_t, r_t)):
            dst[...] = res[key].T
        bonus_sc[...] = res["bonus"]

    row0 = pl.multiple_of(h * RWKV_HEAD, RWKV_HEAD)
    keys = pl.ds(row0, RWKV_HEAD)
    kap, ew, alp, kh, rr = kap_t[keys, :], ew_t[keys, :], alp_t[keys, :], kh_t[keys, :], r_t[keys, :]

    def value_row(v, carry):
        s = sin_ref[v]
        sk = jnp.sum(s * kap, axis=0, keepdims=True)
        s_new = s * ew - sk * alp + v_t[pl.ds(row0 + v, 1), :] * kh
        sout_ref[v] = s_new
        o_t[pl.ds(row0 + v, 1), :] = jnp.sum(s_new * rr, axis=0, keepdims=True)
        return carry

    lax.fori_loop(0, RWKV_HEAD, value_row, 0, unroll=4)

    @pl.when(h == pl.num_programs(0) - 1)
    def _():
        o_ref[...] = _rwkv_epilogue(o_t[...].T, bonus_sc[...], pvec, bo)


def _rwkv_consts():
    i = jnp.arange(W_BRANCH)
    j = jnp.arange(MXU_TILE)
    bo = (j[:, None] // RWKV_HEAD == j[None, :] // RWKV_HEAD).astype(bf16)
    idt = (jnp.arange(RWKV_HEAD)[:, None] == (i[None, :] % RWKV_HEAD)).astype(f32)
    return bo, idt


def _rwkv_weight_specs(l):
    def cs(shape):
        return pl.BlockSpec((None,) + shape, lambda *_: (l,) + (0,) * len(shape))

    def const(shape):
        return pl.BlockSpec(shape, lambda *_: (0,) * len(shape))

    return [cs((1, 1536)), cs((1, LANE)), cs((2, LANE, 1024)), cs((8, W_BRANCH)),
            const((MXU_TILE, MXU_TILE)), const((RWKV_HEAD, W_BRANCH))]


def _rwkv_seq(proj3, prev_rkv, prev_lo, s_in, wts, l, lb):
    b, seq_len, _ = proj3.shape
    g = b
    nt = seq_len // lb

    def col(width, off):
        return pl.BlockSpec((g, lb, width), lambda i, t: (i, t, off // width))

    tok_sc = pltpu.VMEM((g, lb, W_BRANCH), f32)
    out = pl.pallas_call(
        functools.partial(_rwkv_seq_kernel, g_seqs=g, lb=lb),
        grid=(b // g, nt),
        in_specs=[
            col(512, C_R), col(512, C_K), col(512, C_V), col(LANE, C_RWKV_LO),
            pl.BlockSpec((g, 1, 1536), lambda i, t: (i, 0, 0)),
            pl.BlockSpec((g, 1, LANE), lambda i, t: (i, 0, 0)),
            pl.BlockSpec((g, RWKV_HEAD, W_BRANCH), lambda i, t: (i, 0, 0)),
        ] + _rwkv_weight_specs(l),
        out_specs=[
            pl.BlockSpec((g, lb, W_BRANCH), lambda i, t: (i, t, 0)),
            pl.BlockSpec((g, RWKV_HEAD, W_BRANCH), lambda i, t: (i, 0, 0)),
        ],
        out_shape=[
            jax.ShapeDtypeStruct((b, seq_len, W_BRANCH), f32),
            jax.ShapeDtypeStruct((b, RWKV_HEAD, W_BRANCH), f32),
        ],
        scratch_shapes=[
            pltpu.VMEM((g, RWKV_HEAD, W_BRANCH), f32),
            pltpu.VMEM((g, 1, 1536), f32),
            pltpu.VMEM((g, 1, LANE), f32),
        ] + [tok_sc] * 9,
        compiler_params=_cparams(("parallel", "arbitrary")),
        name="rwkv_seq",
    )(proj3, proj3, proj3, proj3, prev_rkv, prev_lo, s_in, *wts)
    return out


def _rwkv_tok(proj2, prev_rkv, prev_lo, s_all, acc, wts, l):
    b = proj2.shape[0]
    assert b == LANE, "the single-token RWKV-7 kernel keeps exactly one lane tile of sequences"

    def col(width, off):
        return pl.BlockSpec((b, width), lambda i: (0, off // width))

    state_spec = pl.BlockSpec((None, None, RWKV_HEAD, RWKV_HEAD, b), lambda i: (l, i, 0, 0, 0))
    in_specs = [
        col(512, C_R), col(512, C_K), col(512, C_V), col(LANE, C_RWKV_LO),
        pl.BlockSpec((None, b, 1536), lambda i: (l, 0, 0)),
        pl.BlockSpec((None, b, LANE), lambda i: (l, 0, 0)),
        state_spec,
    ] + _rwkv_weight_specs(l)
    args = [proj2, proj2, proj2, proj2, prev_rkv, prev_lo, s_all, *wts]
    aliases = {}
    if acc is not None:
        in_specs.append(pl.BlockSpec(memory_space=pl.ANY))
        aliases = {len(args): 1}
        args.append(acc)
    feat_sc = pltpu.VMEM((W_BRANCH, b), f32)
    return pl.pallas_call(
        _rwkv_tok_kernel,
        grid=(RWKV_H,),
        in_specs=in_specs,
        out_specs=[pl.BlockSpec((b, W_BRANCH), lambda i: (0, 0)), state_spec],
        out_shape=[
            jax.ShapeDtypeStruct((b, W_BRANCH), f32),
            jax.ShapeDtypeStruct(s_all.shape, f32),
        ],
        scratch_shapes=[feat_sc] * 7 + [pltpu.VMEM((b, W_BRANCH), f32)],
        input_output_aliases=aliases,
        compiler_params=_cparams(("arbitrary",)),
        name="rwkv_tok",
    )(*args)


def _gla_inputs(q_ref, k_ref, v_ref, lo_ref, gup_ref, gb_ref):
    q = q_ref[...] * (GLA_DK ** -0.5)
    z = _dotf(lo_ref[...], gup_ref[...]) + gb_ref[...]
    g = _log_sigmoid(z) * (1.0 / GLA_GATE_NORM)
    return q, k_ref[...], v_ref[...], g


def _hgrn_lb(logits, l):
    m = jnp.max(logits, axis=0, keepdims=True)
    e = jnp.exp(logits - m)
    sm = e / jnp.sum(e, axis=0, keepdims=True)
    lb = jnp.zeros_like(sm[0:1, :])
    for i in range(1, l + 1):
        lb = lb + sm[i:i + 1, :]
    return lb


def _hgrn_inputs(q_ref, f_ref, i_ref, logits_ref, l):
    lb = _hgrn_lb(logits_ref[...], l)
    f_lo = f_ref[...]
    logf = jnp.log(jnp.maximum(lb, LB_FLOOR) + (1.0 - lb) * _sigmoid(f_lo))
    k = (1.0 - lb) * _sigmoid(-f_lo)
    return q_ref[...], k, i_ref[...], logf


def _chunk_core(q, k, v, g, lt_ref, esum_ref, vmask_ref, gmask_ref, bon_ref, gn_ref, st_sc, kpad, bpad, lb,
                heads_per_group):
    bc = _ones_seg(lt_ref[...], g)
    nc = lb // CHUNK
    f = q.shape[1]
    kpad[...] = k.reshape(nc, CHUNK, f)
    bpad[...] = bc.reshape(nc, CHUNK, f)
    q3 = q.reshape(nc, CHUNK, f)
    bc3 = bc.reshape(nc, CHUNK, f)
    esum = esum_ref[...]

    lane_j = lax.broadcasted_iota(jnp.int32, (1, 1, LANE), 2) & (CHUNK - 1)
    att3 = None
    for r0 in range(0, CHUNK, 8):
        rows = CHUNK - r0
        q_s, bc_s = q3[:, r0:, :], bc3[:, r0:, :]
        rowc8 = lax.broadcasted_iota(jnp.int32, (1, 8, 1), 1) + r0
        acc = jnp.zeros((nc, rows, LANE), f32)
        for j in range(r0, r0 + 8):
            kj, bj = kpad[:, j:j + 1, :], bpad[:, j:j + 1, :]
            z = q_s * kj * jnp.exp(bc_s - bj)
            head = jnp.where(rowc8 >= j, z[:, 0:8, :], 0.0)
            z = head if rows == 8 else jnp.concatenate([head, z[:, 8:, :]], axis=1)
            a = jnp.dot(z.reshape(nc * rows, f).astype(bf16), esum, preferred_element_type=f32)
            acc = acc + jnp.where(lane_j == j, a.reshape(nc, rows, LANE), 0.0)
        if r0:
            acc = jnp.concatenate([jnp.zeros((nc, r0, LANE), f32), acc], axis=1)
        att3 = acc if att3 is None else att3 + acc
    n_heads = LANE // CHUNK
    v3 = v.reshape(nc, CHUNK, W_BRANCH)
    vbd = jnp.concatenate([v3] * n_heads, axis=1) * vmask_ref[...]
    o = lax.dot_general(att3.astype(bf16), vbd.astype(bf16), (((2,), (1,)), ((0,), (0,))),
                        preferred_element_type=f32).reshape(lb, W_BRANCH)

    ng = f // LANE
    vw = W_BRANCH // ng
    sts = [st_sc[gi] for gi in range(ng)]
    outs = []
    for c in range(nc):
        sl = slice(c * CHUNK, (c + 1) * CHUNK)
        bcc = bc[sl]
        blast = bcc[CHUNK - 1:CHUNK, :]
        dec = jnp.exp(blast)
        qe = (q[sl] * jnp.exp(bcc)).astype(bf16)
        ke = (k[sl] * jnp.exp(blast - bcc)).astype(bf16)
        vb = v[sl].astype(bf16)
        parts = []
        for gi in range(ng):
            kl = slice(gi * LANE, (gi + 1) * LANE)
            vl = slice(gi * vw, (gi + 1) * vw)
            parts.append(_dot_nt(qe[:, kl], sts[gi].astype(bf16)))
            upd = _dot_tn(vb[:, vl], ke[:, kl])
            if heads_per_group > 1:
                upd = upd * gmask_ref[...]
            sts[gi] = sts[gi] * dec[:, kl] + upd
        outs.append(o[sl] + jnp.concatenate(parts, axis=1))
    for gi in range(ng):
        st_sc[gi] = sts[gi]
    o = jnp.concatenate(outs, axis=0)
    ms = _seg_sum(o * o, bon_ref[...], pieces=1) * (1.0 / LANE)
    return o * lax.rsqrt(ms + EPS) * gn_ref[...]


def _chunk_seq_kernel(*refs, kind, l, lb):
    n_in = 6 if kind == "gla" else 4
    ins = refs[:n_in]
    lt_ref, esum_ref, vmask_ref, gmask_ref, bon_ref, gn_ref, o_ref, stout_ref, st_sc, kpad, bpad = refs[n_in:]
    tb = pl.program_id(1)

    @pl.when(tb == 0)
    def _():
        st_sc[...] = jnp.zeros_like(st_sc)

    if kind == "gla":
        q, k, v, g = _gla_inputs(*ins)
        heads_per_group = LANE // GLA_DK
    else:
        q, k, v, g = _hgrn_inputs(*ins, l)
        heads_per_group = LANE // HGRN_DK
    o_ref[...] = _chunk_core(q, k, v, g, lt_ref, esum_ref, vmask_ref, gmask_ref, bon_ref, gn_ref, st_sc, kpad,
                             bpad, lb, heads_per_group)

    @pl.when(tb == pl.num_programs(1) - 1)
    def _():
        stout_ref[...] = st_sc[...]


def _chunk_consts(h, dk, lb):
    f = h * dk
    i = jnp.arange(lb)
    lt = ((i[:, None] // CHUNK == i[None, :] // CHUNK) & (i[:, None] >= i[None, :])).astype(bf16)
    fi = jnp.arange(f)
    li = jnp.arange(LANE)
    oi = jnp.arange(W_BRANCH)
    vw = W_BRANCH // (f // LANE)
    esum = (fi[:, None] // dk == li[None, :] // CHUNK).astype(bf16)
    vmask = (li[:, None] // CHUNK == oi[None, :] // LANE).astype(f32)
    gmask = (jnp.arange(vw)[:, None] // LANE == li[None, :] // dk).astype(f32)
    ti = jnp.arange(MXU_TILE)
    bon = (ti[:, None] // LANE == ti[None, :] // LANE).astype(bf16)
    return lt, esum, vmask, gmask, bon


def _chunk_seq(kind, proj3, extra, gn3, l, lb):
    b, seq_len, _ = proj3.shape
    h, dk = (GLA_H, GLA_DK) if kind == "gla" else (HGRN_H, HGRN_DK)
    f = h * dk
    nt = seq_len // lb

    def col(width, off):
        return pl.BlockSpec((None, lb, width), lambda i, t: (i, t, off // width))

    def const(shape):
        return pl.BlockSpec(shape, lambda *_: (0,) * len(shape))

    def layer(shape):
        return pl.BlockSpec((None,) + shape, lambda *_: (l,) + (0,) * len(shape))

    if kind == "gla":
        gup, gb = extra
        in_specs = [col(GLA_KW, C_GLA_Q), col(GLA_KW, C_GLA_K), col(W_BRANCH, C_GLA_V), col(LANE, C_GLA_LO),
                    layer((LANE, GLA_KW)), layer((1, GLA_KW))]
        args = [proj3, proj3, proj3, proj3, gup, gb]
    else:
        (logits,) = extra
        in_specs = [col(W_BRANCH, C_HQ), col(W_BRANCH, C_HF), col(W_BRANCH, C_HI), const((DEPTH, W_BRANCH))]
        args = [proj3, proj3, proj3, logits]
    consts = _chunk_consts(h, dk, lb)
    in_specs += [const(c.shape) for c in consts] + [layer((1, W_BRANCH))]
    ng = f // LANE
    st_shape = (ng, W_BRANCH // ng, LANE)
    return pl.pallas_call(
        functools.partial(_chunk_seq_kernel, kind=kind, l=l, lb=lb),
        grid=(b, nt),
        in_specs=in_specs,
        out_specs=[
            pl.BlockSpec((None, lb, W_BRANCH), lambda i, t: (i, t, 0)),
            pl.BlockSpec((None,) + st_shape, lambda i, t: (i, 0, 0, 0)),
        ],
        out_shape=[
            jax.ShapeDtypeStruct((b, seq_len, W_BRANCH), f32),
            jax.ShapeDtypeStruct((b,) + st_shape, f32),
        ],
        scratch_shapes=[
            pltpu.VMEM(st_shape, f32),
            pltpu.VMEM((lb // CHUNK, CHUNK, f), f32),
            pltpu.VMEM((lb // CHUNK, CHUNK, f), f32),
        ],
        compiler_params=_cparams(("parallel", "arbitrary")),
        name=kind + "_seq",
    )(*args, *consts, gn3)


def _state_from_transposed(st, h, dk):
    b, ng = st.shape[0], st.shape[1]
    hpg = h // ng
    st = st.reshape(b, ng, hpg, LANE, hpg, dk)
    diag = jnp.stack([st[:, :, i, :, i, :] for i in range(hpg)], axis=2)
    return jnp.swapaxes(diag.reshape(b, h, LANE, dk), 2, 3)


def _key_columns(x, g, dk, odd):
    xt = jnp.concatenate([x, jnp.zeros((LANE - g, LANE), f32)], axis=0).T
    if dk == LANE:
        return xt
    return jnp.where(odd, xt[dk:2 * dk], xt[0:dk])


def _chunk_tok_kernel(*refs, kind, l, g, has_acc):
    ins, (o_ref, sout_ref) = refs[:-2], refs[-2:]
    if has_acc:
        ins = ins[:-1]
    odd = (pl.program_id(1) % 2) == 1
    if kind == "gla":
        q_ref, k_ref, v_ref, lo_ref, gup_ref, gb_ref, s_ref, gn_ref = ins
        q = q_ref[...] * (GLA_DK ** -0.5)
        k = k_ref[...]
        z = _dotf(lo_ref[...], gup_ref[...]) + gb_ref[...]
        dec = jnp.exp(_log_sigmoid(z) * (1.0 / GLA_GATE_NORM))
        v = v_ref[...]
        dk = GLA_DK
    else:
        q_ref, f_ref, i_ref, logits_ref, s_ref, gn_ref = ins
        q, k, v, logf = _hgrn_inputs(q_ref, f_ref, i_ref, logits_ref, l)
        dec = jnp.exp(logf)
        dk = HGRN_DK
    qt, kt, dt = (_key_columns(x, g, dk, odd) for x in (q, k, dec))
    o_rows = []
    for b in range(g):
        qcol, kcol, dcol = (jnp.broadcast_to(t[:, b:b + 1], (dk, LANE)) for t in (qt, kt, dt))
        s_new = s_ref[b] * dcol + kcol * v[b:b + 1, :]
        sout_ref[b] = s_new
        o_rows.append(jnp.sum(qcol * s_new, axis=0, keepdims=True))
    o = jnp.concatenate(o_rows, axis=0)
    ms = jnp.mean(o * o, axis=-1, keepdims=True)
    o_ref[...] = o * lax.rsqrt(ms + EPS) * gn_ref[...]


def _chunk_tok(kind, proj2, extra, gn3, s_all, acc, l, g):
    b = proj2.shape[0]
    h, dk = (GLA_H, GLA_DK) if kind == "gla" else (HGRN_H, HGRN_DK)

    def head_col(off):
        per = LANE // dk
        return pl.BlockSpec((g, LANE), lambda i, j: (i, off // LANE + j // per))

    if kind == "gla":
        gup, gb = extra
        in_specs = [head_col(C_GLA_Q), head_col(C_GLA_K),
                    pl.BlockSpec((g, LANE), lambda i, j: (i, C_GLA_V // LANE + j)),
                    pl.BlockSpec((g, LANE), lambda i, j: (i, C_GLA_LO // LANE)),
                    pl.BlockSpec((None, LANE, LANE), lambda i, j: (l, 0, j // 2)),
                    pl.BlockSpec((None, 1, LANE), lambda i, j: (l, 0, j // 2))]
        args = [proj2, proj2, proj2, proj2, gup, gb]
    else:
        (logits,) = extra
        in_specs = [head_col(C_HQ), head_col(C_HF), head_col(C_HI),
                    pl.BlockSpec((DEPTH, LANE), lambda i, j: (0, j))]
        args = [proj2, proj2, proj2, logits]
    state_spec = pl.BlockSpec((None, g, None, dk, LANE), lambda i, j: (l, i, j, 0, 0))
    in_specs += [state_spec, pl.BlockSpec((None, 1, LANE), lambda i, j: (l, 0, j))]
    args += [s_all, gn3]
    aliases = {}
    if acc is not None:
        in_specs.append(pl.BlockSpec(memory_space=pl.ANY))
        aliases = {len(args): 1}
        args.append(acc)
    return pl.pallas_call(
        functools.partial(_chunk_tok_kernel, kind=kind, l=l, g=g, has_acc=acc is not None),
        grid=(b // g, h),
        in_specs=in_specs,
        out_specs=[pl.BlockSpec((g, LANE), lambda i, j: (i, j)), state_spec],
        out_shape=[
            jax.ShapeDtypeStruct((b, W_BRANCH), f32),
            jax.ShapeDtypeStruct(s_all.shape, f32),
        ],
        input_output_aliases=aliases,
        compiler_params=_cparams(("parallel", "parallel")),
        name=kind + "_tok",
    )(*args)


def _lru_gate_dot(y_hi, y_lo, w_ref, col0):
    d = functools.partial(jnp.dot, preferred_element_type=f32)
    halves = []
    for h0 in range(0, W_BRANCH, MXU_TILE):
        w_hi = w_ref[0, h0:h0 + MXU_TILE, col0 + h0:col0 + h0 + MXU_TILE]
        w_lo = w_ref[1, h0:h0 + MXU_TILE, col0 + h0:col0 + h0 + MXU_TILE]
        a, b = y_hi[:, h0:h0 + MXU_TILE], y_lo[:, h0:h0 + MXU_TILE]
        halves.append(d(a, w_hi) + d(a, w_lo) + d(b, w_hi))
    return jnp.concatenate(halves, axis=1)


def _lru_gates(y, w_ref, pvec):
    b_a, b_x, lam = pvec[1:2, :], pvec[2:3, :], pvec[3:4, :]
    y_hi, y_lo = _split(y, 2)
    r = _sigmoid(_lru_gate_dot(y_hi, y_lo, w_ref, 0) + b_a)
    ig = _sigmoid(_lru_gate_dot(y_hi, y_lo, w_ref, W_BRANCH) + b_x)
    log_a = -LRU_C * r * _softplus(-lam)
    a = jnp.exp(log_a)
    one_m_a2 = -jnp.tanh(log_a) * (jnp.exp(2.0 * log_a) + 1.0)
    b = jnp.sqrt(one_m_a2) * (ig * y)
    return a, b


def _lru_seq_kernel(x_ref, cw_ref, pvec_ref, wbd_ref, o_ref, xpad, hcar, *, lb):
    tb = pl.program_id(1)

    @pl.when(tb == 0)
    def _():
        xpad[0:8, :] = jnp.zeros((8, W_BRANCH), f32)
        hcar[...] = jnp.zeros_like(hcar)

    x = x_ref[...]
    xpad[8:8 + lb, :] = x
    cw = cw_ref[...]
    pvec = pvec_ref[...]
    y = pvec[0:1, :] + x * cw[3:4, :]
    for j in range(CONV_W - 1):
        y = y + xpad[5 + j:5 + j + lb, :] * cw[j:j + 1, :]
    a, b = _lru_gates(y, wbd_ref, pvec)
    row = lax.broadcasted_iota(jnp.int32, (lb, 1), 0)
    s = 1
    while s < lb:
        m = row >= s
        b = jnp.where(m, a * pltpu.roll(b, s, 0) + b, b)
        a = jnp.where(m, a * pltpu.roll(a, s, 0), a)
        s *= 2
    h = a * hcar[...] + b
    o_ref[...] = h
    hcar[...] = h[lb - 1:lb, :]
    xpad[5:8, :] = x[lb - 3:lb, :]


def _lru_seq(proj3, cw, pvec, wbd, l, lb):
    b, seq_len, _ = proj3.shape

    def layer(shape):
        return pl.BlockSpec((None,) + shape, lambda *_: (l,) + (0,) * len(shape))

    return pl.pallas_call(
        functools.partial(_lru_seq_kernel, lb=lb),
        grid=(b, seq_len // lb),
        in_specs=[
            pl.BlockSpec((None, lb, W_BRANCH), lambda i, t: (i, t, C_LRU // W_BRANCH)),
            layer((CONV_W, W_BRANCH)), layer((8, W_BRANCH)), layer((2, W_BRANCH, 2 * W_BRANCH)),
        ],
        out_specs=pl.BlockSpec((None, lb, W_BRANCH), lambda i, t: (i, t, 0)),
        out_shape=jax.ShapeDtypeStruct((b, seq_len, W_BRANCH), f32),
        scratch_shapes=[
            pltpu.VMEM((8 + lb, W_BRANCH), f32),
            pltpu.VMEM((1, W_BRANCH), f32),
        ],
        compiler_params=_cparams(("parallel", "arbitrary")),
        name="lru_seq",
    )(proj3, cw, pvec, wbd)


def _lru_tok_kernel(x_ref, b0_ref, b1_ref, b2_ref, h0_ref, cw_ref, pvec_ref, wbd_ref, o_ref):
    cw = cw_ref[...]
    pvec = pvec_ref[...]
    y = (pvec[0:1, :] + b0_ref[...] * cw[0:1, :] + b1_ref[...] * cw[1:2, :] + b2_ref[...] * cw[2:3, :]
         + x_ref[...] * cw[3:4, :])
    a, b = _lru_gates(y, wbd_ref, pvec)
    o_ref[...] = a * h0_ref[...] + b


def _lru_tok(proj2, bufs, h0, cw, pvec, wbd, l):
    b = proj2.shape[0]

    def layer(shape):
        return pl.BlockSpec((None,) + shape, lambda *_: (l,) + (0,) * len(shape))

    row = pl.BlockSpec((b, W_BRANCH), lambda i: (0, 0))
    return pl.pallas_call(
        _lru_tok_kernel,
        grid=(1,),
        in_specs=[pl.BlockSpec((b, W_BRANCH), lambda i: (0, C_LRU // W_BRANCH)), row, row, row, row,
                  layer((CONV_W, W_BRANCH)), layer((8, W_BRANCH)), layer((2, W_BRANCH, 2 * W_BRANCH))],
        out_specs=row,
        out_shape=jax.ShapeDtypeStruct((b, W_BRANCH), f32),
        compiler_params=_cparams(("arbitrary",)),
        name="lru_tok",
    )(proj2, *bufs, h0, cw, pvec, wbd)


def _prep_weights(p):
    w = jnp.swapaxes(p["w_in"], 1, 2)
    row_groups = [(0, 1536), (2176, 2688), (2704, 6800), (1664, 2176), (1536, 1664), (2688, 2704)]
    pad = jnp.zeros((DEPTH, LANE - GLA_GATE_RANK, D_MODEL), bf16)
    w_in_p = jnp.concatenate([w[:, a:b].astype(bf16) for a, b in row_groups] + [pad], axis=1)
    zl = jnp.zeros((DEPTH, RWKV_LORA, W_BRANCH), f32)
    wup = jnp.concatenate([jnp.concatenate([p["rwkv_w_up"], zl], axis=2),
                           jnp.concatenate([zl, p["rwkv_a_up"]], axis=2)], axis=1)
    wup_hi = wup.astype(bf16)
    wup = jnp.stack([wup_hi, (wup - wup_hi.astype(f32)).astype(bf16)], axis=1)
    zrow = jnp.zeros((DEPTH, W_BRANCH), f32)
    rwkv_vec = jnp.stack([p["rwkv_w0"], p["rwkv_a0"], p["rwkv_k_k"], p["rwkv_k_a"],
                          p["rwkv_r_k"].reshape(DEPTH, W_BRANCH), p["rwkv_gn_g"], p["rwkv_gn_b"], zrow], axis=1)
    mu = p["rwkv_mu"]
    gup = jnp.concatenate([p["gla_gk_up"], jnp.zeros((DEPTH, LANE - GLA_GATE_RANK, GLA_KW), f32)], axis=1)
    eye = jnp.eye(LRU_BLOCKS, dtype=f32)

    def bd(wb):
        return jnp.einsum("lhij,hg->lhigj", wb, eye).reshape(DEPTH, W_BRANCH, W_BRANCH)

    lru_vec = jnp.stack([p["lru_conv_b"], p["lru_b_a"], p["lru_b_x"], p["lru_lambda"],
                         zrow, zrow, zrow, zrow], axis=1)
    lru_w = jnp.concatenate([bd(p["lru_w_a"]), bd(p["lru_w_x"])], axis=2)
    lru_w_hi = lru_w.astype(bf16)
    lru_w_lo = (lru_w - lru_w_hi.astype(f32)).astype(bf16)
    return dict(
        w_in_p=w_in_p,
        w_out_b=p["w_out"].astype(bf16),
        norm_g3=p["norm_g"].reshape(DEPTH, 1, D_MODEL),
        final_g2=p["final_g"].reshape(1, D_MODEL),
        rwkv=(mu[:, None, 0:1536], mu[:, None, 1536:1664], wup, rwkv_vec) + _rwkv_consts(),
        gla=(gup, p["gla_gk_b"].reshape(DEPTH, 1, GLA_KW)),
        gla_gn=p["gla_gn_g"].reshape(DEPTH, 1, W_BRANCH),
        hgrn=(p["hgrn_lb_logits"],),
        hgrn_gn=p["hgrn_gn_g"].reshape(DEPTH, 1, W_BRANCH),
        lru=(p["lru_conv_w"], lru_vec, jnp.stack([lru_w_hi, lru_w_lo], axis=1)),
    )


def _shift_state(proj_last):
    return jnp.concatenate([proj_last[..., 0:1536], proj_last[..., C_RWKV_LO:C_RWKV_LO + LANE]], axis=-1)


def _rwkv_state_out(s):
    lead = s.shape[:-2]
    return jnp.swapaxes(s.reshape(lead + (RWKV_HEAD, RWKV_H, RWKV_HEAD)), -3, -2)


def _trunk_seq(x, mod, wts):
    b, seq_len, _ = x.shape
    t = b * seq_len
    lb = min(LB_CHUNK, seq_len)
    tm = min(TM_IN, seq_len)
    per_seq = seq_len // tm
    mod3 = mod.reshape(DEPTH * b * 3, 1, D_MODEL)
    x2 = x.reshape(t, D_MODEL)
    zeros = functools.partial(jnp.zeros, dtype=f32)
    new = []
    h2 = None
    for l in range(DEPTH):
        mod_l = mod3[l * b * 3:(l + 1) * b * 3]
        mod_next = mod3[(l + 1) * b * 3:(l + 2) * b * 3]
        if l == 0:
            proj2 = _inproj(x2, wts["norm_g3"], mod_l, wts["w_in_p"], l, per_seq, tm, normed=False)
        else:
            proj2 = _inproj(h2, None, None, wts["w_in_p"], l, None, min(TM_IN_NORMED, t), normed=True)
        proj3 = proj2.reshape(b, seq_len, PROJ_P)
        o_a, s_wkv = _rwkv_seq(proj3, zeros((b, 1, 1536)), zeros((b, 1, LANE)),
                               zeros((b, RWKV_HEAD, W_BRANCH)), wts["rwkv"], l, min(LB_RWKV, seq_len))
        o_b, st_gla = _chunk_seq("gla", proj3, wts["gla"], wts["gla_gn"], l, lb)
        o_c, st_hgrn = _chunk_seq("hgrn", proj3, wts["hgrn"], wts["hgrn_gn"], l, lb)
        o_d = _lru_seq(proj3, *wts["lru"], l, lb)
        outs = [o.reshape(t, W_BRANCH) for o in (o_a, o_b, o_c, o_d)]
        tmo = min(TM_OUT, seq_len)
        res = _outproj(outs, proj2, x2, mod_l, mod_next, wts["w_out_b"], wts["norm_g3"], wts["final_g2"], l,
                       seq_len // tmo, tmo)
        x2, h2 = (res, None) if l == DEPTH - 1 else res
        last = proj3[:, seq_len - 1]
        new.append((
            _shift_state(last),
            _rwkv_state_out(s_wkv),
            _state_from_transposed(st_gla, GLA_H, GLA_DK),
            _state_from_transposed(st_hgrn, HGRN_H, HGRN_DK),
            proj3[:, seq_len - (CONV_W - 1):, C_LRU:C_LRU + W_BRANCH],
            o_d[:, seq_len - 1],
        ))
    return x2.reshape(b, seq_len, D_MODEL), tuple(jnp.stack([n[i] for n in new], axis=0) for i in range(6))


def _trunk_tok(x, mod, states, wts):
    b = x.shape[0]
    s_shift, s_wkv, s_gla, s_hgrn, s_conv, s_h = states
    x2 = x.reshape(b, D_MODEL)
    s_wkv_t = jnp.transpose(s_wkv, (0, 2, 3, 4, 1))
    prev_rkv, prev_lo = s_shift[:, :, 0:1536], s_shift[:, :, 1536:1664]
    n_wkv = n_gla = n_hgrn = None
    new = []
    h2 = None
    for l in range(DEPTH):
        proj2 = _inproj(x2 if l == 0 else h2, wts["norm_g3"], mod, wts["w_in_p"], l, None, b, normed=l > 0)
        o_a, n_wkv = _rwkv_tok(proj2, prev_rkv, prev_lo, s_wkv_t, n_wkv, wts["rwkv"], l)
        o_b, n_gla = _chunk_tok("gla", proj2, wts["gla"], wts["gla_gn"], s_gla, n_gla, l, TOK_GROUP)
        o_c, n_hgrn = _chunk_tok("hgrn", proj2, wts["hgrn"], wts["hgrn_gn"], s_hgrn, n_hgrn, l, TOK_GROUP)
        conv = s_conv[l]
        o_d = _lru_tok(proj2, [conv[:, 0], conv[:, 1], conv[:, 2]], s_h[l], *wts["lru"], l)
        res = _outproj([o_a, o_b, o_c, o_d], proj2, x2, mod, mod, wts["w_out_b"], wts["norm_g3"],
                       wts["final_g2"], l, None, b)
        x2, h2 = (res, None) if l == DEPTH - 1 else res
        n_conv = jnp.concatenate([conv[:, 1:], proj2[:, None, C_LRU:C_LRU + W_BRANCH]], axis=1)
        new.append((_shift_state(proj2), n_conv, o_d))
    n_shift, n_conv, n_h = (jnp.stack([n[i] for n in new], axis=0) for i in range(3))
    n_wkv = jnp.transpose(n_wkv, (0, 4, 1, 2, 3))
    return x2.reshape(b, 1, D_MODEL), (n_shift, n_wkv, n_gla, n_hgrn, n_conv, n_h)


def kernel(x_prompt, x_sample, c_prompt, c_sample, state_rwkv_shift, state_rwkv_wkv, state_gla, state_hgrn, state_lru_conv, state_lru_h, norm_g, w_ada, b_ada, w_in, w_out, rwkv_mu, rwkv_w0, rwkv_w_up, rwkv_a0, rwkv_a_up, rwkv_k_k, rwkv_k_a, rwkv_r_k, rwkv_gn_g, rwkv_gn_b, gla_gk_up, gla_gk_b, gla_gn_g, hgrn_lb_logits, hgrn_gn_g, lru_conv_w, lru_conv_b, lru_w_a, lru_b_a, lru_w_x, lru_b_x, lru_lambda, final_g):
    p = dict(norm_g=norm_g, w_in=w_in, w_out=w_out, rwkv_mu=rwkv_mu, rwkv_w0=rwkv_w0, rwkv_w_up=rwkv_w_up,
             rwkv_a0=rwkv_a0, rwkv_a_up=rwkv_a_up, rwkv_k_k=rwkv_k_k, rwkv_k_a=rwkv_k_a, rwkv_r_k=rwkv_r_k,
             rwkv_gn_g=rwkv_gn_g, rwkv_gn_b=rwkv_gn_b, gla_gk_up=gla_gk_up, gla_gk_b=gla_gk_b,
             gla_gn_g=gla_gn_g, hgrn_lb_logits=hgrn_lb_logits, hgrn_gn_g=hgrn_gn_g, lru_conv_w=lru_conv_w,
             lru_conv_b=lru_conv_b, lru_w_a=lru_w_a, lru_b_a=lru_b_a, lru_w_x=lru_w_x, lru_b_x=lru_b_x,
             lru_lambda=lru_lambda, final_g=final_g)
    wts = _prep_weights(p)
    bp = x_prompt.shape[0]
    bs = x_sample.shape[0]
    pad_rows = (-bp) % 8
    c_all = jnp.concatenate([c_prompt, jnp.zeros((pad_rows, D_MODEL), f32), c_sample], axis=0)
    mod = _ada(c_all, w_ada, b_ada)
    mod_p = mod[:, 0:bp]
    mod_s = mod[:, bp + pad_rows:bp + pad_rows + bs]
    y_p, st_p = _trunk_seq(x_prompt, mod_p, wts)
    states = (state_rwkv_shift, state_rwkv_wkv, state_gla, state_hgrn, state_lru_conv, state_lru_h)
    y_s, st_s = _trunk_tok(x_sample, mod_s, states, wts)
    return (y_p, y_s) + st_p + st_s
```

```python
import functools

import jax
import jax.numpy as jnp
from jax import lax
from jax.experimental import pallas as pl
from jax.experimental.pallas import tpu as pltpu

f32 = jnp.float32
bf16 = jnp.bfloat16

D_MODEL = 2048
DEPTH = 4
W_BRANCH = 512
EPS = 1e-6

RWKV_H = 8
RWKV_HEAD = 64
RWKV_LORA = 64
RWKV_PROJ = 3 * W_BRANCH + 2 * RWKV_LORA
RWKV_GN_EPS = 64e-5

GLA_H = 4
GLA_DK = 64
GLA_DV = 128
GLA_KW = GLA_H * GLA_DK
GLA_GATE_RANK = 16
GLA_GATE_NORM = 16.0

HGRN_H = 4
HGRN_DK = 128
HGRN_DV = 128
LB_FLOOR = 1e-30

LRU_BLOCKS = 8
LRU_BLOCK = 64
LRU_C = 8.0
CONV_W = 4

CHUNK = 32

C_R, C_K, C_V = 0, 512, 1024
C_GLA_V = 1536
C_HQ, C_HF, C_HI = 2048, 2560, 3072
C_LRU = 3584
C_GATE = 4096
C_GLA_Q, C_GLA_K = 6144, 6400
C_RWKV_LO = 6656
C_GLA_LO = 6784
PROJ_P = 6912

LANE = 128
MXU_TILE = 256
VMEM_LIMIT = 56 * 1024 * 1024

TM_IN = 1024
TM_IN_NORMED = 2048
TN_IN = 1152
TM_OUT = 512
LB_CHUNK = 256
LB_RWKV = 256
RWKV_GROUP = 16
TOK_GROUP = 64

HIGHEST = lax.Precision.HIGHEST


def _cparams(sem):
    return pltpu.CompilerParams(dimension_semantics=sem, vmem_limit_bytes=VMEM_LIMIT)


def _dotf(a, b):
    return jnp.dot(a, b, preferred_element_type=f32, precision=HIGHEST)


def _prec(a):
    return HIGHEST if a.dtype == f32 else None


def _dot_nt(a, b):
    return lax.dot_general(a, b, (((1,), (1,)), ((), ())), preferred_element_type=f32, precision=_prec(a))


def _dot_tn(a, b):
    return lax.dot_general(a, b, (((0,), (0,)), ((), ())), preferred_element_type=f32, precision=_prec(a))


def _split(x, pieces):
    out = []
    for i in range(pieces):
        part = x.astype(bf16)
        out.append(part)
        if i + 1 < pieces:
            x = x - part.astype(f32)
    return out


def _half_dot(x, ones):
    d = functools.partial(jnp.dot, preferred_element_type=f32)
    kh = ones.shape[0]
    if x.shape[1] == kh:
        return d(x, ones)
    return jnp.concatenate([d(x[:, 0:kh], ones), d(x[:, kh:2 * kh], ones)], axis=1)


def _seg_sum(x, ones, pieces=2):
    return sum(_half_dot(part, ones) for part in _split(x, pieces))


def _ones_seg(ones, x, pieces=2):
    d = functools.partial(jnp.dot, preferred_element_type=f32)
    return sum(d(ones, part) for part in _split(x, pieces))


def _sigmoid(x):
    return jax.nn.sigmoid(x)


def _softplus(x):
    return jnp.maximum(x, 0.0) + jnp.log1p(jnp.exp(-jnp.abs(x)))


def _log_sigmoid(x):
    return -_softplus(-x)


def _ada_kernel(c_ref, w_ref, b_ref, o_ref):
    c = c_ref[...]
    s = (c * _sigmoid(c)).astype(bf16)
    o_ref[...] = jnp.dot(s, w_ref[...].astype(bf16), preferred_element_type=f32) + b_ref[...]


def _ada(c_all, w_ada, b_ada):
    rows = c_all.shape[0]
    tn = 1024
    n = w_ada.shape[2]
    return pl.pallas_call(
        _ada_kernel,
        grid=(DEPTH, n // tn),
        in_specs=[
            pl.BlockSpec((rows, D_MODEL), lambda l, j: (0, 0)),
            pl.BlockSpec((None, D_MODEL, tn), lambda l, j: (l, 0, j)),
            pl.BlockSpec((None, 1, tn), lambda l, j: (l, 0, j)),
        ],
        out_specs=pl.BlockSpec((None, rows, tn), lambda l, j: (l, 0, j)),
        out_shape=jax.ShapeDtypeStruct((DEPTH, rows, n), f32),
        compiler_params=_cparams(("parallel", "parallel")),
        name="ada_mod",
    )(c_all, w_ada, b_ada.reshape(DEPTH, 1, n))


def _norm_mod(x, g, scale, shift):
    ms = jnp.mean(x * x, axis=-1, keepdims=True)
    h = x * lax.rsqrt(ms + EPS) * g
    return (h * (1.0 + scale) + shift).astype(bf16)


def _inproj_kernel(*refs, normed):
    if normed:
        h_ref, w_ref, o_ref = refs
    else:
        x_ref, g_ref, sc_ref, sh_ref, w_ref, o_ref, h_ref = refs

        @pl.when(pl.program_id(1) == 0)
        def _():
            h_ref[...] = _norm_mod(x_ref[...], g_ref[...], sc_ref[...], sh_ref[...])

    o_ref[...] = _dot_nt(h_ref[...], w_ref[...])


def _mod_spec(l, which, per_seq_tiles, tm):
    if per_seq_tiles is None:
        return pl.BlockSpec((None, tm, D_MODEL), lambda i, *_: (l, 0, which))
    return pl.BlockSpec((None, 1, D_MODEL), lambda i, *_: (i // per_seq_tiles * 3 + which, 0, 0))


def _inproj(x2, norm_g3, mod, w_in_p, l, per_seq_tiles, tm, normed):
    t = x2.shape[0]
    tn = TN_IN
    row_spec = pl.BlockSpec((tm, D_MODEL), lambda i, j: (i, 0))
    w_spec = pl.BlockSpec((None, tn, D_MODEL), lambda i, j: (l, j, 0))
    if normed:
        in_specs, args, scratch = [row_spec, w_spec], (x2, w_in_p), []
    else:
        in_specs = [row_spec, pl.BlockSpec((None, 1, D_MODEL), lambda i, j: (l, 0, 0)),
                    _mod_spec(l, 1, per_seq_tiles, tm), _mod_spec(l, 0, per_seq_tiles, tm), w_spec]
        args, scratch = (x2, norm_g3, mod, mod, w_in_p), [pltpu.VMEM((tm, D_MODEL), bf16)]
    return pl.pallas_call(
        functools.partial(_inproj_kernel, normed=normed),
        grid=(t // tm, PROJ_P // tn),
        in_specs=in_specs,
        out_specs=pl.BlockSpec((tm, tn), lambda i, j: (i, j)),
        out_shape=jax.ShapeDtypeStruct((t, PROJ_P), f32),
        scratch_shapes=scratch,
        compiler_params=_cparams(("parallel", "arbitrary")),
        name="in_proj",
    )(*args)


def _outproj_kernel(oa_ref, ob_ref, oc_ref, od_ref, pz_ref, x_ref, gate_ref, w_ref, g_ref, *rest, final):
    z = pz_ref[...]
    o = jnp.concatenate([oa_ref[...], ob_ref[...], oc_ref[...], od_ref[...]], axis=1)
    o = o * (z * _sigmoid(z))
    y = jnp.dot(o.astype(bf16), w_ref[...], preferred_element_type=f32)
    xn = x_ref[...] + gate_ref[...] * y
    if final:
        (o_ref,) = rest
        ms = jnp.mean(xn * xn, axis=-1, keepdims=True)
        o_ref[...] = xn * lax.rsqrt(ms + EPS) * g_ref[...]
    else:
        sc_ref, sh_ref, o_ref, h_ref = rest
        o_ref[...] = xn
        h_ref[...] = _norm_mod(xn, g_ref[...], sc_ref[...], sh_ref[...])


def _outproj(outs, proj2, x2, mod, mod_next, w_out_b, norm_g3, final_g2, l, per_seq_tiles, tm):
    t = x2.shape[0]
    final = l == DEPTH - 1
    mix_spec = pl.BlockSpec((tm, W_BRANCH), lambda i: (i, 0))
    row_spec = pl.BlockSpec((tm, D_MODEL), lambda i: (i, 0))
    in_specs = [
        mix_spec, mix_spec, mix_spec, mix_spec,
        pl.BlockSpec((tm, D_MODEL), lambda i: (i, C_GATE // D_MODEL)),
        row_spec,
        _mod_spec(l, 2, per_seq_tiles, tm),
        pl.BlockSpec((None, D_MODEL, D_MODEL), lambda i: (l, 0, 0), pipeline_mode=pl.Buffered(1)),
    ]
    args = [*outs, proj2, x2, mod, w_out_b]
    if final:
        in_specs.append(pl.BlockSpec((1, D_MODEL), lambda i: (0, 0)))
        args.append(final_g2)
        out_specs, out_shape = row_spec, jax.ShapeDtypeStruct((t, D_MODEL), f32)
    else:
        in_specs += [pl.BlockSpec((None, 1, D_MODEL), lambda i: (l + 1, 0, 0)),
                     _mod_spec(l + 1, 1, per_seq_tiles, tm), _mod_spec(l + 1, 0, per_seq_tiles, tm)]
        args += [norm_g3, mod_next, mod_next]
        out_specs = [row_spec, row_spec]
        out_shape = [jax.ShapeDtypeStruct((t, D_MODEL), f32), jax.ShapeDtypeStruct((t, D_MODEL), bf16)]
    return pl.pallas_call(
        functools.partial(_outproj_kernel, final=final),
        grid=(t // tm,),
        in_specs=in_specs,
        out_specs=out_specs,
        out_shape=out_shape,
        compiler_params=_cparams(("parallel",)),
        name="out_proj",
    )(*args)


def _rwkv_prologue(r, k, v, lo, pr, pk, pv, plo, mu_rkv, mu_lo, wup, pvec, bo, sub_block=0):
    w0, a0, k_k, k_a, r_k = (pvec[i:i + 1, :] for i in range(5))
    xr = r + (pr - r) * mu_rkv[:, 0:512]
    xk = k + (pk - k) * mu_rkv[:, 512:1024]
    xv = v + (pv - v) * mu_rkv[:, 1024:1536]
    xlo = lo + (plo - lo) * mu_lo
    lane = lax.broadcasted_iota(jnp.int32, xlo.shape, 1)
    act = jnp.where(lane < RWKV_LORA, jnp.tanh(xlo), xlo)
    act_hi, act_lo = _split(act, 2)
    d = functools.partial(jnp.dot, preferred_element_type=f32)
    up = d(act_hi, wup[0]) + d(act_hi, wup[1]) + d(act_lo, wup[0])
    w_raw = -_softplus(-(w0 + up[:, 0:512])) - 0.5
    logw = -jnp.exp(w_raw)
    a = _sigmoid(a0 + up[:, 512:1024])
    kk = xk * k_k
    kk = kk / jnp.maximum(jnp.sqrt(_seg_sum(kk * kk, bo)), 1e-12)
    kh = xk * (1.0 + (a - 1.0) * k_a)
    alp = kk * a
    ar = _seg_sum(alp * xr, bo, pieces=1)
    out = dict(vv=xv, xr=xr, kr=_seg_sum(kh * xr, bo, pieces=1),
               bonus=_seg_sum(xr * kh * r_k, bo, pieces=1) * xv)
    if not sub_block:
        ew = jnp.exp(logw)
        out.update(kap=kk, alp=alp, kh=kh, ew=ew, wr=ew * xr - ar * kk)
        return out
    rowi = lax.broadcasted_iota(jnp.int32, (logw.shape[0], 1), 0) & (sub_block - 1)
    lc = logw
    step = 1
    while step < sub_block:
        lc = lc + jnp.where(rowi >= step, pltpu.roll(lc, step, 0), 0.0)
        step *= 2
    gam, ginv = jnp.exp(lc), jnp.exp(-lc)
    kap = kk * jnp.exp(lc - logw)
    out.update(kap=kap, alp=alp * ginv, kh=kh * ginv, gam=gam, wr=gam * xr - ar * kap)
    return out


_RWKV_STEP_KEYS = ("kap", "alp", "kh", "vv", "wr", "kr")


def _rwkv_step(s, kap, alp, kh, vv, wr, kr, bo, idt):
    s_new, o_row = [], []
    for c0 in range(0, W_BRANCH, MXU_TILE):
        sl = slice(c0, c0 + MXU_TILE)
        s_h, idt_h, v_h = s[:, sl], idt[:, sl], vv[:, sl]
        s_b = s_h.astype(bf16)
        x = jnp.concatenate([s_b * kap[:, sl].astype(bf16), s_b * wr[:, sl].astype(bf16),
                             idt_h.astype(bf16) * v_h.astype(bf16)], axis=0)
        red = jnp.dot(x, bo, preferred_element_type=f32)
        s_new.append(s_h - red[0:64] * alp[:, sl] + red[128:192] * kh[:, sl])
        o_row.append(jnp.sum(red[64:128] * idt_h, axis=0, keepdims=True) + v_h * kr[:, sl])
    return jnp.concatenate(s_new, axis=1), jnp.concatenate(o_row, axis=1)


def _rwkv_epilogue(o, bonus, pvec, bo):
    gn_g, gn_b = pvec[5:6, :], pvec[6:7, :]
    mu = _seg_sum(o, bo, pieces=1) * (1.0 / RWKV_HEAD)
    d = o - mu
    var = _seg_sum(d * d, bo, pieces=1) * (1.0 / RWKV_HEAD)
    return d * lax.rsqrt(var + RWKV_GN_EPS) * gn_g + gn_b + bonus


def _rwkv_seq_kernel(r_ref, k_ref, v_ref, lo_ref, prkv_ref, plo_ref, sin_ref, mu_rkv_ref, mu_lo_ref, wup_ref,
                     pvec_ref, bo_ref, idt_ref, o_ref, sout_ref,
                     s_sc, crkv_sc, clo_sc, kap_sc, gam_sc, alp_sc, kh_sc, vv_sc, wr_sc, kr_sc, oraw_sc,
                     bonus_sc, *, g_seqs, lb):
    tb = pl.program_id(1)

    @pl.when(tb == 0)
    def _():
        s_sc[...] = sin_ref[...]
        crkv_sc[...] = prkv_ref[...]
        clo_sc[...] = plo_ref[...]

    bo = bo_ref[...]
    idt = idt_ref[...]
    pvec = pvec_ref[...]
    step_sc = dict(kap=kap_sc, alp=alp_sc, kh=kh_sc, vv=vv_sc, wr=wr_sc, kr=kr_sc)

    row0 = lax.broadcasted_iota(jnp.int32, (lb, 1), 0) == 0
    for g in range(g_seqs):
        cur = [r_ref[g], k_ref[g], v_ref[g]]
        lo = lo_ref[g]
        carry = crkv_sc[g]
        prev = [jnp.where(row0, carry[:, i * 512:(i + 1) * 512], pltpu.roll(c, 1, 0)) for i, c in enumerate(cur)]
        plo = jnp.where(row0, clo_sc[g], pltpu.roll(lo, 1, 0))
        res = _rwkv_prologue(cur[0], cur[1], cur[2], lo, prev[0], prev[1], prev[2], plo,
                             mu_rkv_ref[...], mu_lo_ref[...], wup_ref[...], pvec, bo, sub_block=RWKV_GROUP)
        for key in _RWKV_STEP_KEYS:
            step_sc[key][g] = res[key]
        gam_sc[g] = res["gam"]
        bonus_sc[g] = res["bonus"]
        for i, c in enumerate(cur):
            crkv_sc[g, :, i * 512:(i + 1) * 512] = c[lb - 1:lb, :]
        clo_sc[g] = lo[lb - 1:lb, :]

    def group(i, carry):
        t0 = pl.multiple_of(i * RWKV_GROUP, RWKV_GROUP)
        states = [s_sc[g] for g in range(g_seqs)]
        for j in range(RWKV_GROUP):
            for g in range(g_seqs):
                rows = [step_sc[key][g, pl.ds(t0 + j, 1), :] for key in _RWKV_STEP_KEYS]
                states[g], o_row = _rwkv_step(states[g], *rows, bo, idt)
                oraw_sc[g, pl.ds(t0 + j, 1), :] = o_row
        for g in range(g_seqs):
            s_sc[g] = states[g] * gam_sc[g, pl.ds(t0 + RWKV_GROUP - 1, 1), :]
        return carry

    lax.fori_loop(0, lb // RWKV_GROUP, group, 0)

    for g in range(g_seqs):
        o_ref[g] = _rwkv_epilogue(oraw_sc[g], bonus_sc[g], pvec, bo)

    @pl.when(tb == pl.num_programs(1) - 1)
    def _():
        sout_ref[...] = s_sc[...]


_RWKV_TOK_KEYS = ("kap", "ew", "alp", "kh", "vv", "xr")


def _rwkv_tok_kernel(r_ref, k_ref, v_ref, lo_ref, prkv_ref, plo_ref, sin_ref, mu_rkv_ref, mu_lo_ref, wup_ref,
                     pvec_ref, bo_ref, idt_ref, *rest):
    del idt_ref
    o_ref, sout_ref = rest[-10:-8]
    kap_t, ew_t, alp_t, kh_t, v_t, r_t, o_t, bonus_sc = rest[-8:]
    h = pl.program_id(0)
    bo = bo_ref[...]
    pvec = pvec_ref[...]

    @pl.when(h == 0)
    def _():
        prkv = prkv_ref[...]
        res = _rwkv_prologue(r_ref[...], k_ref[...], v_ref[...], lo_ref[...],
                             prkv[:, 0:512], prkv[:, 512:1024], prkv[:, 1024:1536], plo_ref[...],
                             mu_rkv_ref[...], mu_lo_ref[...], wup_ref[...], pvec, bo)
        for key, dst in zip(_RWKV_TOK_KEYS, (kap_t, ew_t, alp_t, kh_t, v_t, r_t)):
            dst[...] = res[key].T
        bonus_sc[...] = res["bonus"]

    row0 = pl.multiple_of(h * RWKV_HEAD, RWKV_HEAD)
    keys = pl.ds(row0, RWKV_HEAD)
    kap, ew, alp, kh, rr = kap_t[keys, :], ew_t[keys, :], alp_t[keys, :], kh_t[keys, :], r_t[keys, :]

    def value_row(v, carry):
        s = sin_ref[v]
        sk = jnp.sum(s * kap, axis=0, keepdims=True)
        s_new = s * ew - sk * alp + v_t[pl.ds(row0 + v, 1), :] * kh
        sout_ref[v] = s_new
        o_t[pl.ds(row0 + v, 1), :] = jnp.sum(s_new * rr, axis=0, keepdims=True)
        return carry

    lax.fori_loop(0, RWKV_HEAD, value_row, 0, unroll=4)

    @pl.when(h == pl.num_programs(0) - 1)
    def _():
        o_ref[...] = _rwkv_epilogue(o_t[...].T, bonus_sc[...], pvec, bo)


def _rwkv_consts():
    i = jnp.arange(W_BRANCH)
    j = jnp.arange(MXU_TILE)
    bo = (j[:, None] // RWKV_HEAD == j[None, :] // RWKV_HEAD).astype(bf16)
    idt = (jnp.arange(RWKV_HEAD)[:, None] == (i[None, :] % RWKV_HEAD)).astype(f32)
    return bo, idt


def _rwkv_weight_specs(l):
    def cs(shape):
        return pl.BlockSpec((None,) + shape, lambda *_: (l,) + (0,) * len(shape))

    def const(shape):
        return pl.BlockSpec(shape, lambda *_: (0,) * len(shape))

    return [cs((1, 1536)), cs((1, LANE)), cs((2, LANE, 1024)), cs((8, W_BRANCH)),
            const((MXU_TILE, MXU_TILE)), const((RWKV_HEAD, W_BRANCH))]


def _rwkv_seq(proj3, prev_rkv, prev_lo, s_in, wts, l, lb):
    b, seq_len, _ = proj3.shape
    g = b
    nt = seq_len // lb

    def col(width, off):
        return pl.BlockSpec((g, lb, width), lambda i, t: (i, t, off // width))

    tok_sc = pltpu.VMEM((g, lb, W_BRANCH), f32)
    out = pl.pallas_call(
        functools.partial(_rwkv_seq_kernel, g_seqs=g, lb=lb),
        grid=(b // g, nt),
        in_specs=[
            col(512, C_R), col(512, C_K), col(512, C_V), col(LANE, C_RWKV_LO),
            pl.BlockSpec((g, 1, 1536), lambda i, t: (i, 0, 0)),
            pl.BlockSpec((g, 1, LANE), lambda i, t: (i, 0, 0)),
            pl.BlockSpec((g, RWKV_HEAD, W_BRANCH), lambda i, t: (i, 0, 0)),
        ] + _rwkv_weight_specs(l),
        out_specs=[
            pl.BlockSpec((g, lb, W_BRANCH), lambda i, t: (i, t, 0)),
            pl.BlockSpec((g, RWKV_HEAD, W_BRANCH), lambda i, t: (i, 0, 0)),
        ],
        out_shape=[
            jax.ShapeDtypeStruct((b, seq_len, W_BRANCH), f32),
            jax.ShapeDtypeStruct((b, RWKV_HEAD, W_BRANCH), f32),
        ],
        scratch_shapes=[
            pltpu.VMEM((g, RWKV_HEAD, W_BRANCH), f32),
            pltpu.VMEM((g, 1, 1536), f32),
            pltpu.VMEM((g, 1, LANE), f32),
        ] + [tok_sc] * 9,
        compiler_params=_cparams(("parallel", "arbitrary")),
        name="rwkv_seq",
    )(proj3, proj3, proj3, proj3, prev_rkv, prev_lo, s_in, *wts)
    return out


def _rwkv_tok(proj2, prev_rkv, prev_lo, s_all, acc, wts, l):
    b = proj2.shape[0]
    assert b == LANE, "the single-token RWKV-7 kernel keeps exactly one lane tile of sequences"

    def col(width, off):
        return pl.BlockSpec((b, width), lambda i: (0, off // width))

    state_spec = pl.BlockSpec((None, None, RWKV_HEAD, RWKV_HEAD, b), lambda i: (l, i, 0, 0, 0))
    in_specs = [
        col(512, C_R), col(512, C_K), col(512, C_V), col(LANE, C_RWKV_LO),
        pl.BlockSpec((None, b, 1536), lambda i: (l, 0, 0)),
        pl.BlockSpec((None, b, LANE), lambda i: (l, 0, 0)),
        state_spec,
    ] + _rwkv_weight_specs(l)
    args = [proj2, proj2, proj2, proj2, prev_rkv, prev_lo, s_all, *wts]
    aliases = {}
    if acc is not None:
        in_specs.append(pl.BlockSpec(memory_space=pl.ANY))
        aliases = {len(args): 1}
        args.append(acc)
    feat_sc = pltpu.VMEM((W_BRANCH, b), f32)
    return pl.pallas_call(
        _rwkv_tok_kernel,
        grid=(RWKV_H,),
        in_specs=in_specs,
        out_specs=[pl.BlockSpec((b, W_BRANCH), lambda i: (0, 0)), state_spec],
        out_shape=[
            jax.ShapeDtypeStruct((b, W_BRANCH), f32),
            jax.ShapeDtypeStruct(s_all.shape, f32),
        ],
        scratch_shapes=[feat_sc] * 7 + [pltpu.VMEM((b, W_BRANCH), f32)],
        input_output_aliases=aliases,
        compiler_params=_cparams(("arbitrary",)),
        name="rwkv_tok",
    )(*args)


def _gla_inputs(q_ref, k_ref, v_ref, lo_ref, gup_ref, gb_ref):
    q = q_ref[...] * (GLA_DK ** -0.5)
    z = _dotf(lo_ref[...], gup_ref[...]) + gb_ref[...]
    g = _log_sigmoid(z) * (1.0 / GLA_GATE_NORM)
    return q, k_ref[...], v_ref[...], g


def _hgrn_lb(logits, l):
    m = jnp.max(logits, axis=0, keepdims=True)
    e = jnp.exp(logits - m)
    sm = e / jnp.sum(e, axis=0, keepdims=True)
    lb = jnp.zeros_like(sm[0:1, :])
    for i in range(1, l + 1):
        lb = lb + sm[i:i + 1, :]
    return lb


def _hgrn_inputs(q_ref, f_ref, i_ref, logits_ref, l):
    lb = _hgrn_lb(logits_ref[...], l)
    f_lo = f_ref[...]
    logf = jnp.log(jnp.maximum(lb, LB_FLOOR) + (1.0 - lb) * _sigmoid(f_lo))
    k = (1.0 - lb) * _sigmoid(-f_lo)
    return q_ref[...], k, i_ref[...], logf


def _chunk_core(q, k, v, g, lt_ref, esum_ref, vmask_ref, gmask_ref, bon_ref, gn_ref, st_sc, kpad, bpad, lb,
                heads_per_group):
    bc = _ones_seg(lt_ref[...], g)
    nc = lb // CHUNK
    f = q.shape[1]
    kpad[...] = k.reshape(nc, CHUNK, f)
    bpad[...] = bc.reshape(nc, CHUNK, f)
    q3 = q.reshape(nc, CHUNK, f)
    bc3 = bc.reshape(nc, CHUNK, f)
    esum = esum_ref[...]

    lane_j = lax.broadcasted_iota(jnp.int32, (1, 1, LANE), 2) & (CHUNK - 1)
    att3 = None
    for r0 in range(0, CHUNK, 8):
        rows = CHUNK - r0
        q_s, bc_s = q3[:, r0:, :], bc3[:, r0:, :]
        rowc8 = lax.broadcasted_iota(jnp.int32, (1, 8, 1), 1) + r0
        acc = jnp.zeros((nc, rows, LANE), f32)
        for j in range(r0, r0 + 8):
            kj, bj = kpad[:, j:j + 1, :], bpad[:, j:j + 1, :]
            z = q_s * kj * jnp.exp(bc_s - bj)
            head = jnp.where(rowc8 >= j, z[:, 0:8, :], 0.0)
            z = head if rows == 8 else jnp.concatenate([head, z[:, 8:, :]], axis=1)
            a = jnp.dot(z.reshape(nc * rows, f).astype(bf16), esum, preferred_element_type=f32)
            acc = acc + jnp.where(lane_j == j, a.reshape(nc, rows, LANE), 0.0)
        if r0:
            acc = jnp.concatenate([jnp.zeros((nc, r0, LANE), f32), acc], axis=1)
        att3 = acc if att3 is None else att3 + acc
    n_heads = LANE // CHUNK
    v3 = v.reshape(nc, CHUNK, W_BRANCH)
    vbd = jnp.concatenate([v3] * n_heads, axis=1) * vmask_ref[...]
    o = lax.dot_general(att3.astype(bf16), vbd.astype(bf16), (((2,), (1,)), ((0,), (0,))),
                        preferred_element_type=f32).reshape(lb, W_BRANCH)

    ng = f // LANE
    vw = W_BRANCH // ng
    sts = [st_sc[gi] for gi in range(ng)]
    outs = []
    for c in range(nc):
        sl = slice(c * CHUNK, (c + 1) * CHUNK)
        bcc = bc[sl]
        blast = bcc[CHUNK - 1:CHUNK, :]
        dec = jnp.exp(blast)
        qe = (q[sl] * jnp.exp(bcc)).astype(bf16)
        ke = (k[sl] * jnp.exp(blast - bcc)).astype(bf16)
        vb = v[sl].astype(bf16)
        parts = []
        for gi in range(ng):
            kl = slice(gi * LANE, (gi + 1) * LANE)
            vl = slice(gi * vw, (gi + 1) * vw)
            parts.append(_dot_nt(qe[:, kl], sts[gi].astype(bf16)))
            upd = _dot_tn(vb[:, vl], ke[:, kl])
            if heads_per_group > 1:
                upd = upd * gmask_ref[...]
            sts[gi] = sts[gi] * dec[:, kl] + upd
        outs.append(o[sl] + jnp.concatenate(parts, axis=1))
    for gi in range(ng):
        st_sc[gi] = sts[gi]
    o = jnp.concatenate(outs, axis=0)
    ms = _seg_sum(o * o, bon_ref[...], pieces=1) * (1.0 / LANE)
    return o * lax.rsqrt(ms + EPS) * gn_ref[...]


def _chunk_seq_kernel(*refs, kind, l, lb):
    n_in = 6 if kind == "gla" else 4
    ins = refs[:n_in]
    lt_ref, esum_ref, vmask_ref, gmask_ref, bon_ref, gn_ref, o_ref, stout_ref, st_sc, kpad, bpad = refs[n_in:]
    tb = pl.program_id(1)

    @pl.when(tb == 0)
    def _():
        st_sc[...] = jnp.zeros_like(st_sc)

    if kind == "gla":
        q, k, v, g = _gla_inputs(*ins)
        heads_per_group = LANE // GLA_DK
    else:
        q, k, v, g = _hgrn_inputs(*ins, l)
        heads_per_group = LANE // HGRN_DK
    o_ref[...] = _chunk_core(q, k, v, g, lt_ref, esum_ref, vmask_ref, gmask_ref, bon_ref, gn_ref, st_sc, kpad,
                             bpad, lb, heads_per_group)

    @pl.when(tb == pl.num_programs(1) - 1)
    def _():
        stout_ref[...] = st_sc[...]


def _chunk_consts(h, dk, lb):
    f = h * dk
    i = jnp.arange(lb)
    lt = ((i[:, None] // CHUNK == i[None, :] // CHUNK) & (i[:, None] >= i[None, :])).astype(bf16)
    fi = jnp.arange(f)
    li = jnp.arange(LANE)
    oi = jnp.arange(W_BRANCH)
    vw = W_BRANCH // (f // LANE)
    esum = (fi[:, None] // dk == li[None, :] // CHUNK).astype(bf16)
    vmask = (li[:, None] // CHUNK == oi[None, :] // LANE).astype(f32)
    gmask = (jnp.arange(vw)[:, None] // LANE == li[None, :] // dk).astype(f32)
    ti = jnp.arange(MXU_TILE)
    bon = (ti[:, None] // LANE == ti[None, :] // LANE).astype(bf16)
    return lt, esum, vmask, gmask, bon


def _chunk_seq(kind, proj3, extra, gn3, l, lb):
    b, seq_len, _ = proj3.shape
    h, dk = (GLA_H, GLA_DK) if kind == "gla" else (HGRN_H, HGRN_DK)
    f = h * dk
    nt = seq_len // lb

    def col(width, off):
        return pl.BlockSpec((None, lb, width), lambda i, t: (i, t, off // width))

    def const(shape):
        return pl.BlockSpec(shape, lambda *_: (0,) * len(shape))

    def layer(shape):
        return pl.BlockSpec((None,) + shape, lambda *_: (l,) + (0,) * len(shape))

    if kind == "gla":
        gup, gb = extra
        in_specs = [col(GLA_KW, C_GLA_Q), col(GLA_KW, C_GLA_K), col(W_BRANCH, C_GLA_V), col(LANE, C_GLA_LO),
                    layer((LANE, GLA_KW)), layer((1, GLA_KW))]
        args = [proj3, proj3, proj3, proj3, gup, gb]
    else:
        (logits,) = extra
        in_specs = [col(W_BRANCH, C_HQ), col(W_BRANCH, C_HF), col(W_BRANCH, C_HI), const((DEPTH, W_BRANCH))]
        args = [proj3, proj3, proj3, logits]
    consts = _chunk_consts(h, dk, lb)
    in_specs += [const(c.shape) for c in consts] + [layer((1, W_BRANCH))]
    ng = f // LANE
    st_shape = (ng, W_BRANCH // ng, LANE)
    return pl.pallas_call(
        functools.partial(_chunk_seq_kernel, kind=kind, l=l, lb=lb),
        grid=(b, nt),
        in_specs=in_specs,
        out_specs=[
            pl.BlockSpec((None, lb, W_BRANCH), lambda i, t: (i, t, 0)),
            pl.BlockSpec((None,) + st_shape, lambda i, t: (i, 0, 0, 0)),
        ],
        out_shape=[
            jax.ShapeDtypeStruct((b, seq_len, W_BRANCH), f32),
            jax.ShapeDtypeStruct((b,) + st_shape, f32),
        ],
        scratch_shapes=[
            pltpu.VMEM(st_shape, f32),
            pltpu.VMEM((lb // CHUNK, CHUNK, f), f32),
            pltpu.VMEM((lb // CHUNK, CHUNK, f), f32),
        ],
        compiler_params=_cparams(("parallel", "arbitrary")),
        name=kind + "_seq",
    )(*args, *consts, gn3)


def _state_from_transposed(st, h, dk):
    b, ng = st.shape[0], st.shape[1]
    hpg = h // ng
    st = st.reshape(b, ng, hpg, LANE, hpg, dk)
    diag = jnp.stack([st[:, :, i, :, i, :] for i in range(hpg)], axis=2)
    return jnp.swapaxes(diag.reshape(b, h, LANE, dk), 2, 3)


def _key_columns(x, g, dk, odd):
    xt = jnp.concatenate([x, jnp.zeros((LANE - g, LANE), f32)], axis=0).T
    if dk == LANE:
        return xt
    return jnp.where(odd, xt[dk:2 * dk], xt[0:dk])


def _chunk_tok_kernel(*refs, kind, l, g, has_acc):
    ins, (o_ref, sout_ref) = refs[:-2], refs[-2:]
    if has_acc:
        ins = ins[:-1]
    odd = (pl.program_id(1) % 2) == 1
    if kind == "gla":
        q_ref, k_ref, v_ref, lo_ref, gup_ref, gb_ref, s_ref, gn_ref = ins
        q = q_ref[...] * (GLA_DK ** -0.5)
        k = k_ref[...]
        z = _dotf(lo_ref[...], gup_ref[...]) + gb_ref[...]
        dec = jnp.exp(_log_sigmoid(z) * (1.0 / GLA_GATE_NORM))
        v = v_ref[...]
        dk = GLA_DK
    else:
        q_ref, f_ref, i_ref, logits_ref, s_ref, gn_ref = ins
        q, k, v, logf = _hgrn_inputs(q_ref, f_ref, i_ref, logits_ref, l)
        dec = jnp.exp(logf)
        dk = HGRN_DK
    qt, kt, dt = (_key_columns(x, g, dk, odd) for x in (q, k, dec))
    o_rows = []
    for b in range(g):
        qcol, kcol, dcol = (jnp.broadcast_to(t[:, b:b + 1], (dk, LANE)) for t in (qt, kt, dt))
        s_new = s_ref[b] * dcol + kcol * v[b:b + 1, :]
        sout_ref[b] = s_new
        o_rows.append(jnp.sum(qcol * s_new, axis=0, keepdims=True))
    o = jnp.concatenate(o_rows, axis=0)
    ms = jnp.mean(o * o, axis=-1, keepdims=True)
    o_ref[...] = o * lax.rsqrt(ms + EPS) * gn_ref[...]


def _chunk_tok(kind, proj2, extra, gn3, s_all, acc, l, g):
    b = proj2.shape[0]
    h, dk = (GLA_H, GLA_DK) if kind == "gla" else (HGRN_H, HGRN_DK)

    def head_col(off):
        per = LANE // dk
        return pl.BlockSpec((g, LANE), lambda i, j: (i, off // LANE + j // per))

    if kind == "gla":
        gup, gb = extra
        in_specs = [head_col(C_GLA_Q), head_col(C_GLA_K),
                    pl.BlockSpec((g, LANE), lambda i, j: (i, C_GLA_V // LANE + j)),
                    pl.BlockSpec((g, LANE), lambda i, j: (i, C_GLA_LO // LANE)),
                    pl.BlockSpec((None, LANE, LANE), lambda i, j: (l, 0, j // 2)),
                    pl.BlockSpec((None, 1, LANE), lambda i, j: (l, 0, j // 2))]
        args = [proj2, proj2, proj2, proj2, gup, gb]
    else:
        (logits,) = extra
        in_specs = [head_col(C_HQ), head_col(C_HF), head_col(C_HI),
                    pl.BlockSpec((DEPTH, LANE), lambda i, j: (0, j))]
        args = [proj2, proj2, proj2, logits]
    state_spec = pl.BlockSpec((None, g, None, dk, LANE), lambda i, j: (l, i, j, 0, 0))
    in_specs += [state_spec, pl.BlockSpec((None, 1, LANE), lambda i, j: (l, 0, j))]
    args += [s_all, gn3]
    aliases = {}
    if acc is not None:
        in_specs.append(pl.BlockSpec(memory_space=pl.ANY))
        aliases = {len(args): 1}
        args.append(acc)
    return pl.pallas_call(
        functools.partial(_chunk_tok_kernel, kind=kind, l=l, g=g, has_acc=acc is not None),
        grid=(b // g, h),
        in_specs=in_specs,
        out_specs=[pl.BlockSpec((g, LANE), lambda i, j: (i, j)), state_spec],
        out_shape=[
            jax.ShapeDtypeStruct((b, W_BRANCH), f32),
            jax.ShapeDtypeStruct(s_all.shape, f32),
        ],
        input_output_aliases=aliases,
        compiler_params=_cparams(("parallel", "parallel")),
        name=kind + "_tok",
    )(*args)


def _lru_gate_dot(y_hi, y_lo, w_ref, col0):
    d = functools.partial(jnp.dot, preferred_element_type=f32)
    halves = []
    for h0 in range(0, W_BRANCH, MXU_TILE):
        w_hi = w_ref[0, h0:h0 + MXU_TILE, col0 + h0:col0 + h0 + MXU_TILE]
        w_lo = w_ref[1, h0:h0 + MXU_TILE, col0 + h0:col0 + h0 + MXU_TILE]
        a, b = y_hi[:, h0:h0 + MXU_TILE], y_lo[:, h0:h0 + MXU_TILE]
        halves.append(d(a, w_hi) + d(a, w_lo) + d(b, w_hi))
    return jnp.concatenate(halves, axis=1)


def _lru_gates(y, w_ref, pvec):
    b_a, b_x, lam = pvec[1:2, :], pvec[2:3, :], pvec[3:4, :]
    y_hi, y_lo = _split(y, 2)
    r = _sigmoid(_lru_gate_dot(y_hi, y_lo, w_ref, 0) + b_a)
    ig = _sigmoid(_lru_gate_dot(y_hi, y_lo, w_ref, W_BRANCH) + b_x)
    log_a = -LRU_C * r * _softplus(-lam)
    a = jnp.exp(log_a)
    one_m_a2 = -jnp.tanh(log_a) * (jnp.exp(2.0 * log_a) + 1.0)
    b = jnp.sqrt(one_m_a2) * (ig * y)
    return a, b


def _lru_seq_kernel(x_ref, cw_ref, pvec_ref, wbd_ref, o_ref, xpad, hcar, *, lb):
    tb = pl.program_id(1)

    @pl.when(tb == 0)
    def _():
        xpad[0:8, :] = jnp.zeros((8, W_BRANCH), f32)
        hcar[...] = jnp.zeros_like(hcar)

    x = x_ref[...]
    xpad[8:8 + lb, :] = x
    cw = cw_ref[...]
    pvec = pvec_ref[...]
    y = pvec[0:1, :] + x * cw[3:4, :]
    for j in range(CONV_W - 1):
        y = y + xpad[5 + j:5 + j + lb, :] * cw[j:j + 1, :]
    a, b = _lru_gates(y, wbd_ref, pvec)
    row = lax.broadcasted_iota(jnp.int32, (lb, 1), 0)
    s = 1
    while s < lb:
        m = row >= s
        b = jnp.where(m, a * pltpu.roll(b, s, 0) + b, b)
        a = jnp.where(m, a * pltpu.roll(a, s, 0), a)
        s *= 2
    h = a * hcar[...] + b
    o_ref[...] = h
    hcar[...] = h[lb - 1:lb, :]
    xpad[5:8, :] = x[lb - 3:lb, :]


def _lru_seq(proj3, cw, pvec, wbd, l, lb):
    b, seq_len, _ = proj3.shape

    def layer(shape):
        return pl.BlockSpec((None,) + shape, lambda *_: (l,) + (0,) * len(shape))

    return pl.pallas_call(
        functools.partial(_lru_seq_kernel, lb=lb),
        grid=(b, seq_len // lb),
        in_specs=[
            pl.BlockSpec((None, lb, W_BRANCH), lambda i, t: (i, t, C_LRU // W_BRANCH)),
            layer((CONV_W, W_BRANCH)), layer((8, W_BRANCH)), layer((2, W_BRANCH, 2 * W_BRANCH)),
        ],
        out_specs=pl.BlockSpec((None, lb, W_BRANCH), lambda i, t: (i, t, 0)),
        out_shape=jax.ShapeDtypeStruct((b, seq_len, W_BRANCH), f32),
        scratch_shapes=[
            pltpu.VMEM((8 + lb, W_BRANCH), f32),
            pltpu.VMEM((1, W_BRANCH), f32),
        ],
        compiler_params=_cparams(("parallel", "arbitrary")),
        name="lru_seq",
    )(proj3, cw, pvec, wbd)


def _lru_tok_kernel(x_ref, b0_ref, b1_ref, b2_ref, h0_ref, cw_ref, pvec_ref, wbd_ref, o_ref):
    cw = cw_ref[...]
    pvec = pvec_ref[...]
    y = (pvec[0:1, :] + b0_ref[...] * cw[0:1, :] + b1_ref[...] * cw[1:2, :] + b2_ref[...] * cw[2:3, :]
         + x_ref[...] * cw[3:4, :])
    a, b = _lru_gates(y, wbd_ref, pvec)
    o_ref[...] = a * h0_ref[...] + b


def _lru_tok(proj2, bufs, h0, cw, pvec, wbd, l):
    b = proj2.shape[0]

    def layer(shape):
        return pl.BlockSpec((None,) + shape, lambda *_: (l,) + (0,) * len(shape))

    row = pl.BlockSpec((b, W_BRANCH), lambda i: (0, 0))
    return pl.pallas_call(
        _lru_tok_kernel,
        grid=(1,),
        in_specs=[pl.BlockSpec((b, W_BRANCH), lambda i: (0, C_LRU // W_BRANCH)), row, row, row, row,
                  layer((CONV_W, W_BRANCH)), layer((8, W_BRANCH)), layer((2, W_BRANCH, 2 * W_BRANCH))],
        out_specs=row,
        out_shape=jax.ShapeDtypeStruct((b, W_BRANCH), f32),
        compiler_params=_cparams(("arbitrary",)),
        name="lru_tok",
    )(proj2, *bufs, h0, cw, pvec, wbd)


def _prep_weights(p):
    w = jnp.swapaxes(p["w_in"], 1, 2)
    row_groups = [(0, 1536), (2176, 2688), (2704, 6800), (1664, 2176), (1536, 1664), (2688, 2704)]
    pad = jnp.zeros((DEPTH, LANE - GLA_GATE_RANK, D_MODEL), bf16)
    w_in_p = jnp.concatenate([w[:, a:b].astype(bf16) for a, b in row_groups] + [pad], axis=1)
    zl = jnp.zeros((DEPTH, RWKV_LORA, W_BRANCH), f32)
    wup = jnp.concatenate([jnp.concatenate([p["rwkv_w_up"], zl], axis=2),
                           jnp.concatenate([zl, p["rwkv_a_up"]], axis=2)], axis=1)
    wup_hi = wup.astype(bf16)
    wup = jnp.stack([wup_hi, (wup - wup_hi.astype(f32)).astype(bf16)], axis=1)
    zrow = jnp.zeros((DEPTH, W_BRANCH), f32)
    rwkv_vec = jnp.stack([p["rwkv_w0"], p["rwkv_a0"], p["rwkv_k_k"], p["rwkv_k_a"],
                          p["rwkv_r_k"].reshape(DEPTH, W_BRANCH), p["rwkv_gn_g"], p["rwkv_gn_b"], zrow], axis=1)
    mu = p["rwkv_mu"]
    gup = jnp.concatenate([p["gla_gk_up"], jnp.zeros((DEPTH, LANE - GLA_GATE_RANK, GLA_KW), f32)], axis=1)
    eye = jnp.eye(LRU_BLOCKS, dtype=f32)

    def bd(wb):
        return jnp.einsum("lhij,hg->lhigj", wb, eye).reshape(DEPTH, W_BRANCH, W_BRANCH)

    lru_vec = jnp.stack([p["lru_conv_b"], p["lru_b_a"], p["lru_b_x"], p["lru_lambda"],
                         zrow, zrow, zrow, zrow], axis=1)
    lru_w = jnp.concatenate([bd(p["lru_w_a"]), bd(p["lru_w_x"])], axis=2)
    lru_w_hi = lru_w.astype(bf16)
    lru_w_lo = (lru_w - lru_w_hi.astype(f32)).astype(bf16)
    return dict(
        w_in_p=w_in_p,
        w_out_b=p["w_out"].astype(bf16),
        norm_g3=p["norm_g"].reshape(DEPTH, 1, D_MODEL),
        final_g2=p["final_g"].reshape(1, D_MODEL),
        rwkv=(mu[:, None, 0:1536], mu[:, None, 1536:1664], wup, rwkv_vec) + _rwkv_consts(),
        gla=(gup, p["gla_gk_b"].reshape(DEPTH, 1, GLA_KW)),
        gla_gn=p["gla_gn_g"].reshape(DEPTH, 1, W_BRANCH),
        hgrn=(p["hgrn_lb_logits"],),
        hgrn_gn=p["hgrn_gn_g"].reshape(DEPTH, 1, W_BRANCH),
        lru=(p["lru_conv_w"], lru_vec, jnp.stack([lru_w_hi, lru_w_lo], axis=1)),
    )


def _shift_state(proj_last):
    return jnp.concatenate([proj_last[..., 0:1536], proj_last[..., C_RWKV_LO:C_RWKV_LO + LANE]], axis=-1)


def _rwkv_state_out(s):
    lead = s.shape[:-2]
    return jnp.swapaxes(s.reshape(lead + (RWKV_HEAD, RWKV_H, RWKV_HEAD)), -3, -2)


def _trunk_seq(x, mod, wts):
    b, seq_len, _ = x.shape
    t = b * seq_len
    lb = min(LB_CHUNK, seq_len)
    tm = min(TM_IN, seq_len)
    per_seq = seq_len // tm
    mod3 = mod.reshape(DEPTH * b * 3, 1, D_MODEL)
    x2 = x.reshape(t, D_MODEL)
    zeros = functools.partial(jnp.zeros, dtype=f32)
    new = []
    h2 = None
    for l in range(DEPTH):
        mod_l = mod3[l * b * 3:(l + 1) * b * 3]
        mod_next = mod3[(l + 1) * b * 3:(l + 2) * b * 3]
        if l == 0:
            proj2 = _inproj(x2, wts["norm_g3"], mod_l, wts["w_in_p"], l, per_seq, tm, normed=False)
        else:
            proj2 = _inproj(h2, None, None, wts["w_in_p"], l, None, min(TM_IN_NORMED, t), normed=True)
        proj3 = proj2.reshape(b, seq_len, PROJ_P)
        o_a, s_wkv = _rwkv_seq(proj3, zeros((b, 1, 1536)), zeros((b, 1, LANE)),
                               zeros((b, RWKV_HEAD, W_BRANCH)), wts["rwkv"], l, min(LB_RWKV, seq_len))
        o_b, st_gla = _chunk_seq("gla", proj3, wts["gla"], wts["gla_gn"], l, lb)
        o_c, st_hgrn = _chunk_seq("hgrn", proj3, wts["hgrn"], wts["hgrn_gn"], l, lb)
        o_d = _lru_seq(proj3, *wts["lru"], l, lb)
        outs = [o.reshape(t, W_BRANCH) for o in (o_a, o_b, o_c, o_d)]
        tmo = min(TM_OUT, seq_len)
        res = _outproj(outs, proj2, x2, mod_l, mod_next, wts["w_out_b"], wts["norm_g3"], wts["final_g2"], l,
                       seq_len // tmo, tmo)
        x2, h2 = (res, None) if l == DEPTH - 1 else res
        last = proj3[:, seq_len - 1]
        new.append((
            _shift_state(last),
            _rwkv_state_out(s_wkv),
            _state_from_transposed(st_gla, GLA_H, GLA_DK),
            _state_from_transposed(st_hgrn, HGRN_H, HGRN_DK),
            proj3[:, seq_len - (CONV_W - 1):, C_LRU:C_LRU + W_BRANCH],
            o_d[:, seq_len - 1],
        ))
    return x2.reshape(b, seq_len, D_MODEL), tuple(jnp.stack([n[i] for n in new], axis=0) for i in range(6))


def _trunk_tok(x, mod, states, wts):
    b = x.shape[0]
    s_shift, s_wkv, s_gla, s_hgrn, s_conv, s_h = states
    x2 = x.reshape(b, D_MODEL)
    s_wkv_t = jnp.transpose(s_wkv, (0, 2, 3, 4, 1))
    prev_rkv, prev_lo = s_shift[:, :, 0:1536], s_shift[:, :, 1536:1664]
    n_wkv = n_gla = n_hgrn = None
    new = []
    h2 = None
    for l in range(DEPTH):
        proj2 = _inproj(x2 if l == 0 else h2, wts["norm_g3"], mod, wts["w_in_p"], l, None, b, normed=l > 0)
        o_a, n_wkv = _rwkv_tok(proj2, prev_rkv, prev_lo, s_wkv_t, n_wkv, wts["rwkv"], l)
        o_b, n_gla = _chunk_tok("gla", proj2, wts["gla"], wts["gla_gn"], s_gla, n_gla, l, TOK_GROUP)
        o_c, n_hgrn = _chunk_tok("hgrn", proj2, wts["hgrn"], wts["hgrn_gn"], s_hgrn, n_hgrn, l, TOK_GROUP)
        conv = s_conv[l]
        o_d = _lru_tok(proj2, [conv[:, 0], conv[:, 1], conv[:, 2]], s_h[l], *wts["lru"], l)
        res = _outproj([o_a, o_b, o_c, o_d], proj2, x2, mod, mod, wts["w_out_b"], wts["norm_g3"],
                       wts["final_g2"], l, None, b)
        x2, h2 = (res, None) if l == DEPTH - 1 else res
        n_conv = jnp.concatenate([conv[:, 1:], proj2[:, None, C_LRU:C_LRU + W_BRANCH]], axis=1)
        new.append((_shift_state(proj2), n_conv, o_d))
    n_shift, n_conv, n_h = (jnp.stack([n[i] for n in new], axis=0) for i in range(3))
    n_wkv = jnp.transpose(n_wkv, (0, 4, 1, 2, 3))
    return x2.reshape(b, 1, D_MODEL), (n_shift, n_wkv, n_gla, n_hgrn, n_conv, n_h)


def kernel(x_prompt, x_sample, c_prompt, c_sample, state_rwkv_shift, state_rwkv_wkv, state_gla, state_hgrn, state_lru_conv, state_lru_h, norm_g, w_ada, b_ada, w_in, w_out, rwkv_mu, rwkv_w0, rwkv_w_up, rwkv_a0, rwkv_a_up, rwkv_k_k, rwkv_k_a, rwkv_r_k, rwkv_gn_g, rwkv_gn_b, gla_gk_up, gla_gk_b, gla_gn_g, hgrn_lb_logits, hgrn_gn_g, lru_conv_w, lru_conv_b, lru_w_a, lru_b_a, lru_w_x, lru_b_x, lru_lambda, final_g):
    p = dict(norm_g=norm_g, w_in=w_in, w_out=w_out, rwkv_mu=rwkv_mu, rwkv_w0=rwkv_w0, rwkv_w_up=rwkv_w_up,
             rwkv_a0=rwkv_a0, rwkv_a_up=rwkv_a_up, rwkv_k_k=rwkv_k_k, rwkv_k_a=rwkv_k_a, rwkv_r_k=rwkv_r_k,
             rwkv_gn_g=rwkv_gn_g, rwkv_gn_b=rwkv_gn_b, gla_gk_up=gla_gk_up, gla_gk_b=gla_gk_b,
             gla_gn_g=gla_gn_g, hgrn_lb_logits=hgrn_lb_logits, hgrn_gn_g=hgrn_gn_g, lru_conv_w=lru_conv_w,
             lru_conv_b=lru_conv_b, lru_w_a=lru_w_a, lru_b_a=lru_b_a, lru_w_x=lru_w_x, lru_b_x=lru_b_x,
             lru_lambda=lru_lambda, final_g=final_g)
    wts = _prep_weights(p)
    bp = x_prompt.shape[0]
    bs = x_sample.shape[0]
    pad_rows = (-bp) % 8
    c_all = jnp.concatenate([c_prompt, jnp.zeros((pad_rows, D_MODEL), f32), c_sample], axis=0)
    mod = _ada(c_all, w_ada, b_ada)
    mod_p = mod[:, 0:bp]
    mod_s = mod[:, bp + pad_rows:bp + pad_rows + bs]
    y_p, st_p = _trunk_seq(x_prompt, mod_p, wts)
    states = (state_rwkv_shift, state_rwkv_wkv, state_gla, state_hgrn, state_lru_conv, state_lru_h)
    y_s, st_s = _trunk_tok(x_sample, mod_s, states, wts)
    return (y_p, y_s) + st_p + st_s
```
